```python
import math
import jax
import jax.numpy as jnp
from jax import lax
import numpy as np

D_MODEL = 2048
BATCH = 4
SEQ = 4096
DEPTH = 2

GRID_W = 64
CTX_LEN = 256
N_EVEN = (DEPTH + 1) // 2
N_ODD = DEPTH // 2
LRU_WIDTH = D_MODEL // 2
LRU_HEADS = 8
LRU_BLOCK = LRU_WIDTH // LRU_HEADS
LRU_CONV = 4
LRU_CONV_LEFT = 2
LRU_C = 8.0
ATT_HEAD_DIM = 128
ATT_Q_HEADS = (D_MODEL // 2) // ATT_HEAD_DIM
ATT_KV_HEADS = 2
ATT_GROUP = ATT_Q_HEADS // ATT_KV_HEADS
ATT_Q_WIDTH = ATT_Q_HEADS * ATT_HEAD_DIM
ATT_KV_WIDTH = ATT_KV_HEADS * ATT_HEAD_DIM
WINDOW = 128
ATT_BLOCK = 128
ROPE_BASE = 10000.0
ROPE_FREQS = ATT_HEAD_DIM // 4
AB_IN = 2 * LRU_WIDTH + ATT_Q_WIDTH + 2 * ATT_KV_WIDTH
AB_OUT = LRU_WIDTH + ATT_Q_WIDTH
S5_WIDTH = D_MODEL // 4
S5_GROUP = 16
S5_GROUPS = S5_WIDTH // S5_GROUP
S5_STATE = 64
SC_WIDTH = D_MODEL - S5_WIDTH
SC_CONV = 3
SC_CONV_LEFT = 1
CD_IN = S5_WIDTH + 3 * SC_WIDTH
CD_OUT = S5_WIDTH + SC_WIDTH
FFN_HIDDEN = 4 * D_MODEL
ALPHA = (2.0 * DEPTH) ** 0.25
OUT_SCALE = (8.0 * DEPTH) ** -0.25
LN_EPS = 1e-5
NEG_INF = -1e30

kernel_name = 'hybrid_rglru_swa_s5_shortconv_dit'


def layer_norm(x, g, b):
    xf = x.astype(jnp.float32)
    mu = jnp.mean(xf, axis=-1, keepdims=True)
    var = jnp.mean(jnp.square(xf - mu), axis=-1, keepdims=True)
    return ((xf - mu) * lax.rsqrt(var + LN_EPS)).astype(x.dtype) * g + b


def dw_conv(x, w, b, left):
    k = w.shape[0]
    y = lax.conv_general_dilated(x, w[:, None, :].astype(x.dtype), window_strides=(1,),
                                 padding=[(left, k - 1 - left)],
                                 dimension_numbers=('NWC', 'WIO', 'NWC'),
                                 feature_group_count=x.shape[-1])
    return y + b


def sq_relu_mlp(h, w1, b1, w2, b2):
    return jnp.square(jax.nn.relu(h @ w1 + b1)) @ w2 + b2


def _lin_combine(l, r):
    a_l, b_l = l
    a_r, b_r = r
    return a_r * a_l, a_r * b_l + b_r


def linear_scan(a, b, h0, reverse):
    a_cum, b_cum = lax.associative_scan(_lin_combine, (a, b), axis=1, reverse=reverse)
    if h0 is None:
        return b_cum
    return a_cum * h0[:, None] + b_cum


def _cplx_combine(l, r):
    alr, ali, blr, bli = l
    arr, ari, brr, bri = r
    return (arr * alr - ari * ali, arr * ali + ari * alr,
            arr * blr - ari * bli + brr, arr * bli + ari * blr + bri)


def complex_scan(a_re, a_im, b_re, b_im, h0, reverse):
    ar, ai, br, bi = lax.associative_scan(_cplx_combine, (a_re, a_im, b_re, b_im), axis=1, reverse=reverse)
    if h0 is None:
        return br, bi
    h0r, h0i = h0[0][:, None], h0[1][:, None]
    return ar * h0r - ai * h0i + br, ar * h0i + ai * h0r + bi


def rglru_coeffs(u, w_a, b_a, w_x, b_x, lam):
    bsz, n, w = u.shape
    ub = u.reshape(bsz, n, LRU_HEADS, LRU_BLOCK)
    r = jax.nn.sigmoid(jnp.einsum('blhi,hij->blhj', ub, w_a).reshape(bsz, n, w) + b_a)
    ig = jax.nn.sigmoid(jnp.einsum('blhi,hij->blhj', ub, w_x).reshape(bsz, n, w) + b_x)
    log_a = (-LRU_C * r * jax.nn.softplus(-lam)).astype(jnp.float32)
    a = jnp.exp(log_a)
    b = jnp.sqrt(-jnp.expm1(2.0 * log_a)) * (ig * u).astype(jnp.float32)
    return a, b


def axial_rope_tables(rows):
    t = jnp.arange(rows * GRID_W)
    row = (t // GRID_W).astype(jnp.float32)
    col = (t % GRID_W).astype(jnp.float32)
    inv = ROPE_BASE ** (-jnp.arange(ROPE_FREQS, dtype=jnp.float32) / ROPE_FREQS)
    ang_r = row[:, None] * inv[None, :]
    ang_c = col[:, None] * inv[None, :]
    return (jnp.cos(ang_r), jnp.sin(ang_r), jnp.cos(ang_c), jnp.sin(ang_c))


def _rotate(x, cos, sin):
    x1, x2 = jnp.split(x, 2, axis=-1)
    cos = cos[None, :, None, :].astype(x.dtype)
    sin = sin[None, :, None, :].astype(x.dtype)
    return jnp.concatenate([x1 * cos - x2 * sin, x2 * cos + x1 * sin], axis=-1)


def apply_axial_rope(x, tabs):
    cos_r, sin_r, cos_c, sin_c = tabs
    xr, xc = jnp.split(x, 2, axis=-1)
    return jnp.concatenate([_rotate(xr, cos_r, sin_r), _rotate(xc, cos_c, sin_c)], axis=-1)


def windowed_attn(q, k, v, kc, vc, rope, sink):
    bsz, n, _ = q.shape
    lc = kc.shape[1]
    nb = n // ATT_BLOCK
    q = apply_axial_rope(q.reshape(bsz, n, ATT_Q_HEADS, ATT_HEAD_DIM), rope)
    k = apply_axial_rope(k.reshape(bsz, n, ATT_KV_HEADS, ATT_HEAD_DIM), rope)
    v = v.reshape(bsz, n, ATT_KV_HEADS, ATT_HEAD_DIM)
    kc = kc.reshape(bsz, lc, ATT_KV_HEADS, ATT_HEAD_DIM)
    vc = vc.reshape(bsz, lc, ATT_KV_HEADS, ATT_HEAD_DIM)
    qb = q.reshape(bsz, nb, ATT_BLOCK, ATT_KV_HEADS, ATT_GROUP, ATT_HEAD_DIM)

    def band(t):
        tp = jnp.pad(t, ((0, 0), (ATT_BLOCK, ATT_BLOCK), (0, 0), (0, 0)))
        tp = tp.reshape(bsz, nb + 2, ATT_BLOCK, ATT_KV_HEADS, ATT_HEAD_DIM)
        return jnp.concatenate([tp[:, :-2], tp[:, 1:-1], tp[:, 2:]], axis=2)

    kb, vb = band(k), band(v)
    scale = ATT_HEAD_DIM ** -0.5
    s_loc = jnp.einsum('bnqgrd,bnkgd->bngrqk', qb, kb).astype(jnp.float32) * scale
    blk = jnp.arange(nb)[:, None, None] * ATT_BLOCK
    qpos = blk + jnp.arange(ATT_BLOCK)[None, :, None]
    kpos = blk + jnp.arange(3 * ATT_BLOCK)[None, None, :] - ATT_BLOCK
    valid = (jnp.abs(qpos - kpos) <= WINDOW) & (kpos >= 0) & (kpos < n)
    s_loc = jnp.where(valid[None, :, None, None], s_loc, NEG_INF)
    s_ctx = jnp.einsum('bnqgrd,bkgd->bngrqk', qb, kc).astype(jnp.float32) * scale
    s_sink = jnp.broadcast_to(sink.reshape(ATT_KV_HEADS, ATT_GROUP, 1, 1).astype(jnp.float32),
                              s_loc.shape[:-1] + (1,))
    p = jax.nn.softmax(jnp.concatenate([s_loc, s_ctx, s_sink], axis=-1), axis=-1).astype(v.dtype)
    nk = 3 * ATT_BLOCK
    o = (jnp.einsum('bngrqk,bnkgd->bnqgrd', p[..., :nk], vb)
         + jnp.einsum('bngrqk,bkgd->bnqgrd', p[..., nk:nk + lc], vc))
    return o.reshape(bsz, n, ATT_Q_WIDTH)


def context_attn(qc, kc, vc, sink):
    bsz, lc, _ = qc.shape
    qc = qc.reshape(bsz, lc, ATT_KV_HEADS, ATT_GROUP, ATT_HEAD_DIM)
    kc = kc.reshape(bsz, lc, ATT_KV_HEADS, ATT_HEAD_DIM)
    vc = vc.reshape(bsz, lc, ATT_KV_HEADS, ATT_HEAD_DIM)
    s = jnp.einsum('bqgrd,bkgd->bgrqk', qc, kc).astype(jnp.float32) * ATT_HEAD_DIM ** -0.5
    s_sink = jnp.broadcast_to(sink.reshape(ATT_KV_HEADS, ATT_GROUP, 1, 1).astype(jnp.float32),
                              s.shape[:-1] + (1,))
    p = jax.nn.softmax(jnp.concatenate([s, s_sink], axis=-1), axis=-1).astype(vc.dtype)
    o = jnp.einsum('bgrqk,bkgd->bqgrd', p[..., :-1], vc)
    return o.reshape(bsz, lc, ATT_Q_WIDTH)


def mixer_rglru_swa(h, hc, rope, w_in, w_out, conv_w, conv_b, w_a, b_a, w_x, b_x, lam, sink, ctx_out):
    splits = [LRU_WIDTH, 2 * LRU_WIDTH, 2 * LRU_WIDTH + ATT_Q_WIDTH, 2 * LRU_WIDTH + ATT_Q_WIDTH + ATT_KV_WIDTH]
    u, gate, q, k, v = jnp.split(h @ w_in, splits, axis=-1)
    uc, gatec, qc, kc, vc = jnp.split(hc @ w_in, splits, axis=-1)
    u = dw_conv(u, conv_w, conv_b, LRU_CONV_LEFT)
    uc = dw_conv(uc, conv_w, conv_b, LRU_CONV_LEFT)
    h_lat, h_ctx = [], []
    for d, rev in enumerate((False, True)):
        a_c, b_c = rglru_coeffs(uc, w_a[d], b_a[d], w_x[d], b_x[d], lam[d])
        hs_c = linear_scan(a_c, b_c, None, rev)
        h0 = hs_c[:, 0] if rev else hs_c[:, -1]
        a_l, b_l = rglru_coeffs(u, w_a[d], b_a[d], w_x[d], b_x[d], lam[d])
        h_lat.append(linear_scan(a_l, b_l, h0, rev))
        h_ctx.append(hs_c)
    y_a = (h_lat[0] + h_lat[1]).astype(h.dtype) * jax.nn.gelu(gate)
    y_b = windowed_attn(q, k, v, kc, vc, rope, sink)
    out = jnp.concatenate([y_a, y_b], axis=-1) @ w_out
    if not ctx_out:
        return out, None
    y_ac = (h_ctx[0] + h_ctx[1]).astype(hc.dtype) * jax.nn.gelu(gatec)
    y_bc = context_attn(qc, kc, vc, sink)
    out_c = jnp.concatenate([y_ac, y_bc], axis=-1) @ w_out
    return out, out_c


def s5_discretize(log_dt, a_re, a_im, b_re, b_im):
    f32 = jnp.float32
    dt = jnp.exp(log_dt.astype(f32))[:, None]
    a_re = a_re.astype(f32)
    a_im = a_im.astype(f32)
    mag = jnp.exp(dt * a_re)
    ab_re = mag * jnp.cos(dt * a_im)
    ab_im = mag * jnp.sin(dt * a_im)
    den = a_re * a_re + a_im * a_im
    k_re = ((ab_re - 1.0) * a_re + ab_im * a_im) / den
    k_im = (ab_im * a_re - (ab_re - 1.0) * a_im) / den
    b_re = b_re.astype(f32)
    b_im = b_im.astype(f32)
    bb_re = k_re[..., None] * b_re - k_im[..., None] * b_im
    bb_im = k_re[..., None] * b_im + k_im[..., None] * b_re
    return ab_re, ab_im, bb_re, bb_im


def s5_states(u, disc, h0, reverse):
    ab_re, ab_im, bb_re, bb_im = disc
    bsz, n, _ = u.shape
    ug = u.reshape(bsz, n, S5_GROUPS, S5_GROUP).astype(jnp.float32)
    bu_re = jnp.einsum('blgc,gpc->blgp', ug, bb_re)
    bu_im = jnp.einsum('blgc,gpc->blgp', ug, bb_im)
    a_re = jnp.broadcast_to(ab_re, bu_re.shape)
    a_im = jnp.broadcast_to(ab_im, bu_re.shape)
    return complex_scan(a_re, a_im, bu_re, bu_im, h0, reverse)


def s5_readout(hs, c_re, c_im):
    h_re, h_im = hs
    return (jnp.einsum('gcp,blgp->blgc', c_re.astype(jnp.float32), h_re)
            - jnp.einsum('gcp,blgp->blgc', c_im.astype(jnp.float32), h_im))


def s5_glu_out(y, u, d_skip, w_glu, b_glu):
    bsz, n, _ = u.shape
    y = y.reshape(bsz, n, S5_WIDTH).astype(u.dtype) + d_skip * u
    z = jax.nn.gelu(y)
    return z * jax.nn.sigmoid(z @ w_glu + b_glu)


def mixer_s5_shortconv(h, hc, w_in, w_out, log_dt, a_re, a_im, b_re, b_im, c_re, c_im,
                       d_skip, w_glu, b_glu, conv_w, conv_b, ctx_out):
    splits = [S5_WIDTH, S5_WIDTH + SC_WIDTH, S5_WIDTH + 2 * SC_WIDTH]
    u, gb, gc, xin = jnp.split(h @ w_in, splits, axis=-1)
    if ctx_out:
        uc, gbc, gcc, xinc = jnp.split(hc @ w_in, splits, axis=-1)
    else:
        uc = hc @ w_in[:, :S5_WIDTH]
    y_lat, y_ctx = [], []
    for d, rev in enumerate((False, True)):
        disc = s5_discretize(log_dt[d], a_re[d], a_im[d], b_re[d], b_im[d])
        hs_c = s5_states(uc, disc, None, rev)
        idx = 0 if rev else -1
        h0 = (hs_c[0][:, idx], hs_c[1][:, idx])
        y_lat.append(s5_readout(s5_states(u, disc, h0, rev), c_re[d], c_im[d]))
        if ctx_out:
            y_ctx.append(s5_readout(hs_c, c_re[d], c_im[d]))
    y_c = s5_glu_out(y_lat[0] + y_lat[1], u, d_skip, w_glu, b_glu)
    y_d = gb * dw_conv(gc * xin, conv_w, conv_b, SC_CONV_LEFT)
    out = jnp.concatenate([y_c, y_d], axis=-1) @ w_out
    if not ctx_out:
        return out, None
    y_cc = s5_glu_out(y_ctx[0] + y_ctx[1], uc, d_skip, w_glu, b_glu)
    y_dc = gbc * dw_conv(gcc * xinc, conv_w, conv_b, SC_CONV_LEFT)
    out_c = jnp.concatenate([y_cc, y_dc], axis=-1) @ w_out
    return out, out_c


def setup_inputs(seed: int = 0) -> dict:
    key = jax.random.key(seed)
    ks = iter(jax.random.split(key, 40))
    f32 = jnp.float32

    def nrm(shape, scale):
        return jax.random.normal(next(ks), shape, f32) * scale

    d = D_MODEL
    ne, no = N_EVEN, N_ODD
    x = nrm((BATCH, SEQ, d), 1.0)
    c = nrm((BATCH, d), 1.0)
    ctx = nrm((BATCH, CTX_LEN, d), 1.0)
    c_ctx = nrm((d,), 1.0)
    mod_w = nrm((DEPTH, d, 6 * d), 0.5 * d ** -0.5)
    mod_b = nrm((DEPTH, 6 * d), 0.02)
    ln1_g = 1.0 + nrm((DEPTH, d), 0.05)
    ln1_b = nrm((DEPTH, d), 0.02)
    ln2_g = 1.0 + nrm((DEPTH, d), 0.05)
    ln2_b = nrm((DEPTH, d), 0.02)
    ffn_w1 = nrm((DEPTH, d, FFN_HIDDEN), d ** -0.5)
    ffn_b1 = nrm((DEPTH, FFN_HIDDEN), 0.02)
    ffn_w2 = nrm((DEPTH, FFN_HIDDEN, d), OUT_SCALE * FFN_HIDDEN ** -0.5)
    ffn_b2 = nrm((DEPTH, d), 0.02)
    ab_w_in = nrm((ne, d, AB_IN), d ** -0.5)
    ab_w_out = nrm((ne, AB_OUT, d), OUT_SCALE * AB_OUT ** -0.5)
    lru_conv_w = nrm((ne, LRU_CONV, LRU_WIDTH), LRU_CONV ** -0.5)
    lru_conv_b = nrm((ne, LRU_WIDTH), 0.02)
    lru_w_a = nrm((ne, 2, LRU_HEADS, LRU_BLOCK, LRU_BLOCK), LRU_BLOCK ** -0.5)
    lru_b_a = nrm((ne, 2, LRU_WIDTH), 0.02)
    lru_w_x = nrm((ne, 2, LRU_HEADS, LRU_BLOCK, LRU_BLOCK), LRU_BLOCK ** -0.5)
    lru_b_x = nrm((ne, 2, LRU_WIDTH), 0.02)
    a_pow = jax.random.uniform(next(ks), (ne, 2, LRU_WIDTH), f32, 0.9, 0.999)
    s = a_pow ** (1.0 / LRU_C)
    lru_lam = jnp.log(s) - jnp.log1p(-s)
    att_sink = nrm((ne, ATT_Q_HEADS), 0.5)
    cd_w_in = nrm((no, d, CD_IN), d ** -0.5)
    cd_w_out = nrm((no, CD_OUT, d), OUT_SCALE * CD_OUT ** -0.5)
    s5_log_dt = jax.random.uniform(next(ks), (no, 2, S5_GROUPS), f32, math.log(1e-3), math.log(1e-1))
    s5_a_re = -0.5 + nrm((no, 2, S5_GROUPS, S5_STATE), 0.01)
    s5_a_im = math.pi * jnp.arange(S5_STATE, dtype=f32) + nrm((no, 2, S5_GROUPS, S5_STATE), 0.01)
    s5_b_re = nrm((no, 2, S5_GROUPS, S5_STATE, S5_GROUP), (2.0 * S5_GROUP) ** -0.5)
    s5_b_im = nrm((no, 2, S5_GROUPS, S5_STATE, S5_GROUP), (2.0 * S5_GROUP) ** -0.5)
    s5_c_re = nrm((no, 2, S5_GROUPS, S5_GROUP, S5_STATE), (2.0 * S5_STATE) ** -0.5)
    s5_c_im = nrm((no, 2, S5_GROUPS, S5_GROUP, S5_STATE), (2.0 * S5_STATE) ** -0.5)
    s5_d = nrm((no, S5_WIDTH), 1.0)
    s5_w_glu = nrm((no, S5_WIDTH, S5_WIDTH), S5_WIDTH ** -0.5)
    s5_b_glu = nrm((no, S5_WIDTH), 0.02)
    sc_conv_w = nrm((no, SC_CONV, SC_WIDTH), SC_CONV ** -0.5)
    sc_conv_b = nrm((no, SC_WIDTH), 0.02)
    return {'x': x, 'c': c, 'ctx': ctx, 'c_ctx': c_ctx, 'mod_w': mod_w, 'mod_b': mod_b,
            'ln1_g': ln1_g, 'ln1_b': ln1_b, 'ln2_g': ln2_g, 'ln2_b': ln2_b,
            'ffn_w1': ffn_w1, 'ffn_b1': ffn_b1, 'ffn_w2': ffn_w2, 'ffn_b2': ffn_b2,
            'ab_w_in': ab_w_in, 'ab_w_out': ab_w_out, 'lru_conv_w': lru_conv_w, 'lru_conv_b': lru_conv_b,
            'lru_w_a': lru_w_a, 'lru_b_a': lru_b_a, 'lru_w_x': lru_w_x, 'lru_b_x': lru_b_x,
            'lru_lam': lru_lam, 'att_sink': att_sink, 'cd_w_in': cd_w_in, 'cd_w_out': cd_w_out,
            's5_log_dt': s5_log_dt, 's5_a_re': s5_a_re, 's5_a_im': s5_a_im, 's5_b_re': s5_b_re,
            's5_b_im': s5_b_im, 's5_c_re': s5_c_re, 's5_c_im': s5_c_im, 's5_d': s5_d,
            's5_w_glu': s5_w_glu, 's5_b_glu': s5_b_glu, 'sc_conv_w': sc_conv_w, 'sc_conv_b': sc_conv_b}


def reference(x, c, ctx, c_ctx, mod_w, mod_b, ln1_g, ln1_b, ln2_g, ln2_b, ffn_w1, ffn_b1, ffn_w2, ffn_b2,
              ab_w_in, ab_w_out, lru_conv_w, lru_conv_b, lru_w_a, lru_b_a, lru_w_x, lru_b_x, lru_lam, att_sink,
              cd_w_in, cd_w_out, s5_log_dt, s5_a_re, s5_a_im, s5_b_re, s5_b_im, s5_c_re, s5_c_im, s5_d,
              s5_w_glu, s5_b_glu, sc_conv_w, sc_conv_b):
    n = x.shape[1]
    rows = n // GRID_W
    rope = axial_rope_tables(rows)
    for i in range(DEPTH):
        last = i == DEPTH - 1
        j = i // 2
        m = jax.nn.silu(c) @ mod_w[i] + mod_b[i]
        mc = jax.nn.silu(c_ctx) @ mod_w[i] + mod_b[i]
        sh1, sc1, g1, sh2, sc2, g2 = jnp.split(m[:, None, :], 6, axis=-1)
        csh1, csc1, cg1, csh2, csc2, cg2 = jnp.split(mc, 6, axis=-1)
        hx = x * (1.0 + sc1) + sh1
        hc = ctx * (1.0 + csc1) + csh1
        if i % 2 == 0:
            y, yc = mixer_rglru_swa(hx, hc, rope, ab_w_in[j], ab_w_out[j], lru_conv_w[j], lru_conv_b[j],
                                    lru_w_a[j], lru_b_a[j], lru_w_x[j], lru_b_x[j], lru_lam[j], att_sink[j],
                                    not last)
        else:
            y, yc = mixer_s5_shortconv(hx, hc, cd_w_in[j], cd_w_out[j], s5_log_dt[j], s5_a_re[j], s5_a_im[j],
                                       s5_b_re[j], s5_b_im[j], s5_c_re[j], s5_c_im[j], s5_d[j],
                                       s5_w_glu[j], s5_b_glu[j], sc_conv_w[j], sc_conv_b[j], not last)
        x = layer_norm(ALPHA * x + g1 * y, ln1_g[i], ln1_b[i])
        f = sq_relu_mlp(x * (1.0 + sc2) + sh2, ffn_w1[i], ffn_b1[i], ffn_w2[i], ffn_b2[i])
        x = layer_norm(ALPHA * x + g2 * f, ln2_g[i], ln2_b[i])
        if not last:
            ctx = layer_norm(ALPHA * ctx + cg1 * yc, ln1_g[i], ln1_b[i])
            fc = sq_relu_mlp(ctx * (1.0 + csc2) + csh2, ffn_w1[i], ffn_b1[i], ffn_w2[i], ffn_b2[i])
            ctx = layer_norm(ALPHA * ctx + cg2 * fc, ln2_g[i], ln2_b[i])
    return x
```

```python
import functools
import math

import jax
import jax.numpy as jnp
from jax import lax
from jax.experimental import pallas as pl
from jax.experimental.pallas import tpu as pltpu

F32 = jnp.float32
BF16 = jnp.bfloat16

D_MODEL = 2048
DEPTH = 2
GRID_W = 64
LRU_WIDTH = D_MODEL // 2
LRU_HEADS = 8
LRU_BLOCK = LRU_WIDTH // LRU_HEADS
LRU_CONV = 4
LRU_CONV_LEFT = 2
LRU_C = 8.0
ATT_HEAD_DIM = 128
ATT_Q_HEADS = (D_MODEL // 2) // ATT_HEAD_DIM
ATT_KV_HEADS = 2
ATT_GROUP = ATT_Q_HEADS // ATT_KV_HEADS
ATT_Q_WIDTH = ATT_Q_HEADS * ATT_HEAD_DIM
ATT_KV_WIDTH = ATT_KV_HEADS * ATT_HEAD_DIM
WINDOW = 128
ATT_BLOCK = 128
ROPE_BASE = 10000.0
ROPE_FREQS = ATT_HEAD_DIM // 4
S5_WIDTH = D_MODEL // 4
S5_GROUP = 16
S5_GROUPS = S5_WIDTH // S5_GROUP
S5_STATE = 64
SC_WIDTH = D_MODEL - S5_WIDTH
SC_CONV = 3
FFN_HIDDEN = 4 * D_MODEL
ALPHA = (2.0 * DEPTH) ** 0.25
LN_EPS = 1e-5
NEG_INF = -1e30

LANES = 128
SUBLANES = 8
V7X_VMEM_BYTES = 64 * 1024 * 1024
V7X_VMEM_BUDGET = 56 * 1024 * 1024

S5_CHUNK = 16
S5_SLAB_GROUPS = LANES // S5_GROUP
S5_SLABS = S5_WIDTH // LANES
S5_SLAB_STATE = S5_SLAB_GROUPS * S5_STATE
S5_XK = S5_CHUNK * LANES
S5_HK = 4 * S5_SLAB_STATE
S5_POW_ROWS = 64


def _vmem_limit(nbytes):
    return int(min(V7X_VMEM_BUDGET, max(nbytes * 3 // 2, 16 * 1024 * 1024)))


def _resident(shape, index_map):
    return pl.BlockSpec(shape, index_map, pipeline_mode=pl.Buffered(1))


def _params(sem, nbytes):
    return pltpu.CompilerParams(dimension_semantics=sem, vmem_limit_bytes=_vmem_limit(nbytes))


def _dot(a, b):
    return jnp.dot(a, b, preferred_element_type=F32)


def _dot_nt(a, b):
    return lax.dot_general(a, b, (((1,), (1,)), ((), ())), preferred_element_type=F32)


def _layer_norm(v, g, b):
    mu = jnp.mean(v, axis=-1, keepdims=True)
    c = v - mu
    var = jnp.mean(c * c, axis=-1, keepdims=True)
    return c * lax.rsqrt(var + LN_EPS) * g + b


def _gelu_tanh(x):
    return 0.5 * x * (1.0 + jnp.tanh(math.sqrt(2.0 / math.pi) * (x + 0.044715 * (x * x * x))))


def _sigmoid(x):
    return 0.5 * (1.0 + jnp.tanh(0.5 * x))


def _mod_kernel(cc_ref, w_ref, b_ref, o_ref):
    a = cc_ref[...]
    a = (a * _sigmoid(a)).astype(BF16)
    o_ref[0] = _dot(a, w_ref[0].astype(BF16)) + b_ref[0]


def _modulation(cc, mod_w, mod_b):
    depth, d, n = mod_w.shape
    tn = 1024
    rows = cc.shape[0]
    return pl.pallas_call(
        _mod_kernel,
        grid=(depth, n // tn),
        in_specs=[pl.BlockSpec((rows, d), lambda l, j: (0, 0)),
                  pl.BlockSpec((1, d, tn), lambda l, j: (l, 0, j)),
                  pl.BlockSpec((1, 1, tn), lambda l, j: (l, 0, j))],
        out_specs=pl.BlockSpec((1, rows, tn), lambda l, j: (l, 0, j)),
        out_shape=jax.ShapeDtypeStruct((depth, rows, n), F32),
        compiler_params=_params(("arbitrary", "arbitrary"), 2 * d * tn * 4 + d * tn * 2),
        name="modulation",
    )(cc, mod_w, mod_b.reshape(depth, 1, n))


def _rope(x, cos, sin_signed, heads):
    lane = lax.broadcasted_iota(jnp.int32, (x.shape[0], ATT_HEAD_DIM), 1)
    first = (lane % (2 * ROPE_FREQS)) < ROPE_FREQS
    out = []
    for h in range(heads):
        xs = x[:, h * ATT_HEAD_DIM:(h + 1) * ATT_HEAD_DIM]
        swapped = jnp.where(first, pltpu.roll(xs, ATT_HEAD_DIM - ROPE_FREQS, 1), pltpu.roll(xs, ROPE_FREQS, 1))
        out.append(xs * cos + swapped * sin_signed)
    return out


def _inproj0_kernel(*refs, rope):
    if rope:
        (x_ref, sc_ref, sh_ref, cos_ref, sin_ref, wu_ref, wg_ref, wq_ref, wk_ref, wv_ref,
         u_ref, g_ref, q_ref, k_ref, v_ref) = refs
    else:
        (x_ref, sc_ref, sh_ref, wu_ref, wg_ref, wq_ref, wk_ref, wv_ref,
         u_ref, g_ref, q_ref, k_ref, v_ref) = refs
    h = (x_ref[0] * (1.0 + sc_ref[0]) + sh_ref[0]).astype(BF16)
    u = _dot(h, wu_ref[...])
    g = _dot(h, wg_ref[...])
    for hd in range(LRU_HEADS):
        u_ref[0, hd] = u[:, hd * LRU_BLOCK:(hd + 1) * LRU_BLOCK].astype(BF16)
        g_ref[0, hd] = g[:, hd * LRU_BLOCK:(hd + 1) * LRU_BLOCK].astype(BF16)
    q = _dot(h, wq_ref[...]) * (ATT_HEAD_DIM ** -0.5)
    k = _dot(h, wk_ref[...])
    if rope:
        cos = cos_ref[...]
        sin = sin_ref[...]
        for hd, piece in enumerate(_rope(q, cos, sin, ATT_Q_HEADS)):
            q_ref[0, :, hd * ATT_HEAD_DIM:(hd + 1) * ATT_HEAD_DIM] = piece.astype(BF16)
        for hd, piece in enumerate(_rope(k, cos, sin, ATT_KV_HEADS)):
            k_ref[0, :, hd * ATT_HEAD_DIM:(hd + 1) * ATT_HEAD_DIM] = piece.astype(BF16)
    else:
        q_ref[0] = q.astype(BF16)
        k_ref[0] = k.astype(BF16)
    v_ref[0] = _dot(h, wv_ref[...]).astype(BF16)


def _inproj0(x, sc, sh, w_in, rope_tabs, tm):
    bsz, n, d = x.shape
    rope = rope_tabs is not None
    o0 = LRU_WIDTH
    o1 = 2 * LRU_WIDTH
    o2 = o1 + ATT_Q_WIDTH
    o3 = o2 + ATT_KV_WIDTH
    ws = [w_in[:, :o0], w_in[:, o0:o1], w_in[:, o1:o2], w_in[:, o2:o3], w_in[:, o3:]]
    tok = lambda b, i: (b, i, 0)
    vec = lambda b, i: (b, 0, 0)
    const = lambda b, i: (0, 0)
    in_specs = [pl.BlockSpec((1, tm, d), tok), pl.BlockSpec((1, 1, d), vec), pl.BlockSpec((1, 1, d), vec)]
    args = [x, sc, sh]
    if rope:
        in_specs += [pl.BlockSpec((tm, ATT_HEAD_DIM), lambda b, i: (i, 0))] * 2
        args += list(rope_tabs)
    in_specs += [_resident(w.shape, const) for w in ws]
    args += ws
    head_major = jax.ShapeDtypeStruct((bsz, LRU_HEADS, n, LRU_BLOCK), BF16)
    head_spec = pl.BlockSpec((1, LRU_HEADS, tm, LRU_BLOCK), lambda b, i: (b, 0, i, 0))
    nbytes = 2 * tm * d * 4 + 2 * sum(w.size for w in ws) + 2 * 2 * tm * w_in.shape[1] + tm * w_in.shape[1] * 4
    return pl.pallas_call(
        functools.partial(_inproj0_kernel, rope=rope),
        grid=(bsz, n // tm),
        in_specs=in_specs,
        out_specs=[head_spec, head_spec,
                   pl.BlockSpec((1, tm, ATT_Q_WIDTH), tok),
                   pl.BlockSpec((1, tm, ATT_KV_WIDTH), tok),
                   pl.BlockSpec((1, tm, ATT_KV_WIDTH), tok)],
        out_shape=[head_major, head_major,
                   jax.ShapeDtypeStruct((bsz, n, ATT_Q_WIDTH), BF16),
                   jax.ShapeDtypeStruct((bsz, n, ATT_KV_WIDTH), BF16),
                   jax.ShapeDtypeStruct((bsz, n, ATT_KV_WIDTH), BF16)],
        compiler_params=_params(("arbitrary", "arbitrary"), nbytes),
        name="inproj0_rope" if rope else "inproj0_ctx",
    )(*args)


LRU_PAD = SUBLANES
LRU_TILE = 256


def _scan8(a, b, row, reverse):
    for k in (1, 2, 4):
        if reverse:
            keep = row < SUBLANES - k
            shift = SUBLANES - k
        else:
            keep = row >= k
            shift = k
        a_sh = jnp.where(keep, pltpu.roll(a, shift, 0), 1.0)
        b_sh = jnp.where(keep, pltpu.roll(b, shift, 0), 0.0)
        b = a * b_sh + b
        a = a * a_sh
    return a, b


def _lru_kernel(ul_ref, gl_ref, uc_ref, gc_ref, cw_ref, cb_ref, wg_ref, bg_ref, lam_ref,
                yl_ref, yc_ref, upad, a_s, b_s, *, n_lat, n_ctx):
    cw = cw_ref[...]
    cb = cb_ref[...]
    wg = wg_ref[0]
    bg = bg_ref[0]
    lam = lam_ref[0]
    neg = -lam
    softplus = jnp.maximum(neg, 0.0) + jnp.log1p(jnp.exp(-jnp.abs(neg)))

    def coefficients(src_ref, n_rows, row_off):
        upad[pl.ds(0, LRU_PAD), :] = jnp.zeros((LRU_PAD, LRU_BLOCK), F32)
        upad[pl.ds(LRU_PAD, n_rows), :] = src_ref[0, 0].astype(F32)
        upad[pl.ds(LRU_PAD + n_rows, LRU_PAD), :] = jnp.zeros((LRU_PAD, LRU_BLOCK), F32)

        def tile(i, carry):
            t0 = pl.multiple_of(i * LRU_TILE, LRU_TILE)
            xp = upad[pl.ds(t0, LRU_TILE + 2 * LRU_PAD), :]
            conv = cb
            for k in range(LRU_CONV):
                o = LRU_PAD - LRU_CONV_LEFT + k
                conv = conv + cw[k:k + 1, :] * xp[o:o + LRU_TILE, :]
            z = _dot(conv.astype(BF16), wg) + bg
            for d in range(2):
                r = _sigmoid(z[:, (2 * d) * LRU_BLOCK:(2 * d + 1) * LRU_BLOCK])
                ig = _sigmoid(z[:, (2 * d + 1) * LRU_BLOCK:(2 * d + 2) * LRU_BLOCK])
                log_a = -LRU_C * r * softplus[:, d * LRU_BLOCK:(d + 1) * LRU_BLOCK]
                a = jnp.exp(log_a)
                a_s[d, pl.ds(row_off + t0, LRU_TILE), :] = a
                b_s[d, pl.ds(row_off + t0, LRU_TILE), :] = jnp.sqrt((1.0 - a) * (1.0 + a)) * (ig * conv)
            return carry

        lax.fori_loop(0, n_rows // LRU_TILE, tile, 0)

    coefficients(uc_ref, n_ctx, 0)
    coefficients(ul_ref, n_lat, n_ctx)

    groups_ctx = n_ctx // SUBLANES
    groups = (n_ctx + n_lat) // SUBLANES
    row = lax.broadcasted_iota(jnp.int32, (SUBLANES, LRU_BLOCK), 0)

    def scan_step(i, carry):
        cf, cr = carry
        rf = pl.multiple_of(i * SUBLANES, SUBLANES)
        a, b = _scan8(a_s[0, pl.ds(rf, SUBLANES), :], b_s[0, pl.ds(rf, SUBLANES), :], row, False)
        hf = b + a * cf
        b_s[0, pl.ds(rf, SUBLANES), :] = hf
        cf = jnp.broadcast_to(hf[SUBLANES - 1:SUBLANES, :], hf.shape)
        gr = jnp.where(i < groups_ctx, groups_ctx - 1 - i, groups + groups_ctx - 1 - i)
        rr = pl.multiple_of(gr * SUBLANES, SUBLANES)
        a, b = _scan8(a_s[1, pl.ds(rr, SUBLANES), :], b_s[1, pl.ds(rr, SUBLANES), :], row, True)
        hr = b + a * cr
        b_s[1, pl.ds(rr, SUBLANES), :] = hr
        cr = jnp.broadcast_to(hr[0:1, :], hr.shape)
        return cf, cr

    zero = jnp.zeros((SUBLANES, LRU_BLOCK), F32)
    lax.fori_loop(0, groups, scan_step, (zero, zero), unroll=8)

    def emit(g_ref, y_ref, n_rows, row_off):
        def tile(i, carry):
            t0 = pl.multiple_of(i * LRU_TILE, LRU_TILE)
            h = b_s[0, pl.ds(row_off + t0, LRU_TILE), :] + b_s[1, pl.ds(row_off + t0, LRU_TILE), :]
            gate = g_ref[0, 0, pl.ds(t0, LRU_TILE), :].astype(F32)
            y_ref[0, 0, pl.ds(t0, LRU_TILE), :] = (h * _gelu_tanh(gate)).astype(BF16)
            return carry

        lax.fori_loop(0, n_rows // LRU_TILE, tile, 0)

    emit(gc_ref, yc_ref, n_ctx, 0)
    emit(gl_ref, yl_ref, n_lat, n_ctx)


def _rglru(u_lat, g_lat, u_ctx, g_ctx, conv_w, conv_b, w_gate, b_gate, lam):
    bsz, heads, n_lat, blk = u_lat.shape
    n_ctx = u_ctx.shape[2]
    seq = lambda n: pl.BlockSpec((1, 1, n, blk), lambda b, h: (b, h, 0, 0))
    per_head = lambda shape: pl.BlockSpec((1,) + shape, lambda b, h: (h, 0, 0))
    total = n_lat + n_ctx
    nbytes = 4 * total * blk * 4 + (n_lat + 2 * LRU_PAD) * blk * 4 + 2 * 3 * 2 * total * blk * 2
    return pl.pallas_call(
        functools.partial(_lru_kernel, n_lat=n_lat, n_ctx=n_ctx),
        grid=(bsz, heads),
        in_specs=[seq(n_lat), seq(n_lat), seq(n_ctx), seq(n_ctx),
                  pl.BlockSpec((LRU_CONV, blk), lambda b, h: (0, h)),
                  pl.BlockSpec((1, blk), lambda b, h: (0, h)),
                  per_head((blk, 4 * blk)), per_head((1, 4 * blk)), per_head((1, 2 * blk))],
        out_specs=[seq(n_lat), seq(n_ctx)],
        out_shape=[jax.ShapeDtypeStruct(u_lat.shape, BF16), jax.ShapeDtypeStruct(u_ctx.shape, BF16)],
        scratch_shapes=[pltpu.VMEM((n_lat + 2 * LRU_PAD, blk), F32),
                        pltpu.VMEM((2, total, blk), F32),
                        pltpu.VMEM((2, total, blk), F32)],
        compiler_params=_params(("arbitrary", "arbitrary"), nbytes),
        name="rglru",
    )(u_lat, g_lat, u_ctx, g_ctx, conv_w, conv_b, w_gate, b_gate, lam)


ATT_TQ = 256
ATT_BAND = 3 * ATT_BLOCK


def _softmax_pv(parts, sink_col):
    m = sink_col
    for s, _ in parts:
        m = jnp.maximum(m, jnp.max(s, axis=-1, keepdims=True))
    den = jnp.exp(sink_col - m)
    acc = None
    for s, v in parts:
        p = jnp.exp(s - m)
        den = den + jnp.sum(p, axis=-1, keepdims=True)
        pv = _dot(p.astype(BF16), v)
        acc = pv if acc is None else acc + pv
    return acc / den


def _sink_column(sink_ref, g, rows):
    cols = [jnp.broadcast_to(sink_ref[g * ATT_GROUP + r:g * ATT_GROUP + r + 1, 0:1], (rows, 1))
            for r in range(ATT_GROUP)]
    return jnp.concatenate(cols, axis=0)


def _attn_kernel(q_ref, k_ref, v_ref, kc_ref, vc_ref, sink_ref, o_ref, *, seq):
    tile = pl.program_id(1)
    blocks = ATT_TQ // ATT_BLOCK
    rows = ATT_GROUP * ATT_BLOCK
    qrow = lax.broadcasted_iota(jnp.int32, (rows, 1), 0) % ATT_BLOCK
    kcol = lax.broadcasted_iota(jnp.int32, (1, ATT_BAND), 1)
    for i in range(blocks):
        q0 = (tile * blocks + i) * ATT_BLOCK
        start = pl.multiple_of(jnp.clip(q0 - ATT_BLOCK, 0, seq - ATT_BAND), ATT_BLOCK)
        valid = jnp.abs((q0 + qrow) - (start + kcol)) <= WINDOW
        for g in range(ATT_KV_HEADS):
            heads = [q_ref[0, i * ATT_BLOCK:(i + 1) * ATT_BLOCK,
                           (g * ATT_GROUP + r) * ATT_HEAD_DIM:(g * ATT_GROUP + r + 1) * ATT_HEAD_DIM]
                     for r in range(ATT_GROUP)]
            qs = jnp.concatenate(heads, axis=0)
            kv = slice(g * ATT_HEAD_DIM, (g + 1) * ATT_HEAD_DIM)
            s_loc = jnp.where(valid, _dot_nt(qs, k_ref[0, pl.ds(start, ATT_BAND), kv]), NEG_INF)
            s_ctx = _dot_nt(qs, kc_ref[0, :, kv])
            o = _softmax_pv([(s_loc, v_ref[0, pl.ds(start, ATT_BAND), kv]), (s_ctx, vc_ref[0, :, kv])],
                            _sink_column(sink_ref, g, ATT_BLOCK))
            for r in range(ATT_GROUP):
                col = (g * ATT_GROUP + r) * ATT_HEAD_DIM
                o_ref[0, i * ATT_BLOCK:(i + 1) * ATT_BLOCK, col:col + ATT_HEAD_DIM] = (
                    o[r * ATT_BLOCK:(r + 1) * ATT_BLOCK].astype(BF16))


def _window_attention(q, k, v, kc, vc, sink_b):
    bsz, n, _ = q.shape
    lc = kc.shape[1]
    whole = lambda rows: pl.BlockSpec((1, rows, ATT_KV_WIDTH), lambda b, i: (b, 0, 0))
    nbytes = 2 * 2 * (2 * ATT_TQ * ATT_Q_WIDTH + 2 * n * ATT_KV_WIDTH + 2 * lc * ATT_KV_WIDTH) + 8 * 1024 * 1024
    return pl.pallas_call(
        functools.partial(_attn_kernel, seq=n),
        grid=(bsz, n // ATT_TQ),
        in_specs=[pl.BlockSpec((1, ATT_TQ, ATT_Q_WIDTH), lambda b, i: (b, i, 0)),
                  whole(n), whole(n), whole(lc), whole(lc),
                  pl.BlockSpec(sink_b.shape, lambda b, i: (0, 0))],
        out_specs=pl.BlockSpec((1, ATT_TQ, ATT_Q_WIDTH), lambda b, i: (b, i, 0)),
        out_shape=jax.ShapeDtypeStruct(q.shape, BF16),
        compiler_params=_params(("arbitrary", "arbitrary"), nbytes),
        name="window_attention",
    )(q, k, v, kc, vc, sink_b)


def _ctx_attn_kernel(q_ref, kc_ref, vc_ref, sink_ref, o_ref):
    lc = q_ref.shape[1]
    for g in range(ATT_KV_HEADS):
        heads = [q_ref[0, :, (g * ATT_GROUP + r) * ATT_HEAD_DIM:(g * ATT_GROUP + r + 1) * ATT_HEAD_DIM]
                 for r in range(ATT_GROUP)]
        qs = jnp.concatenate(heads, axis=0)
        kv = slice(g * ATT_HEAD_DIM, (g + 1) * ATT_HEAD_DIM)
        o = _softmax_pv([(_dot_nt(qs, kc_ref[0, :, kv]), vc_ref[0, :, kv])], _sink_column(sink_ref, g, lc))
        for r in range(ATT_GROUP):
            col = (g * ATT_GROUP + r) * ATT_HEAD_DIM
            o_ref[0, :, col:col + ATT_HEAD_DIM] = o[r * lc:(r + 1) * lc].astype(BF16)


def _context_attention(qc, kc, vc, sink_b):
    bsz, lc, _ = qc.shape
    kv_spec = pl.BlockSpec((1, lc, ATT_KV_WIDTH), lambda b: (b, 0, 0))
    q_spec = pl.BlockSpec((1, lc, ATT_Q_WIDTH), lambda b: (b, 0, 0))
    return pl.pallas_call(
        _ctx_attn_kernel,
        grid=(bsz,),
        in_specs=[q_spec, kv_spec, kv_spec, pl.BlockSpec(sink_b.shape, lambda b: (0, 0))],
        out_specs=q_spec,
        out_shape=jax.ShapeDtypeStruct(qc.shape, BF16),
        compiler_params=_params(("arbitrary",), 8 * 1024 * 1024),
        name="context_attention",
    )(qc, kc, vc, sink_b)


def _outproj0_kernel(ya_ref, yb_ref, x_ref, gate_ref, w_ref, lg_ref, lb_ref, o_ref):
    y = jnp.concatenate([ya_ref[0, h] for h in range(LRU_HEADS)] + [yb_ref[0]], axis=1)
    out = _dot(y, w_ref[...])
    o_ref[0] = _layer_norm(ALPHA * x_ref[0] + gate_ref[0] * out, lg_ref[...], lb_ref[...])


def _outproj0(ya, yb, x, gate, w_out, ln_g, ln_b, tm):
    bsz, n, d = x.shape
    tok = lambda b, i: (b, i, 0)
    vec = lambda b, i: (b, 0, 0)
    const = lambda b, i: (0, 0)
    nbytes = 2 * 2 * tm * d * 4 + w_out.size * 2 + 2 * 2 * tm * d * 2 + 2 * tm * d * 4
    return pl.pallas_call(
        _outproj0_kernel,
        grid=(bsz, n // tm),
        in_specs=[pl.BlockSpec((1, LRU_HEADS, tm, LRU_BLOCK), lambda b, i: (b, 0, i, 0)),
                  pl.BlockSpec((1, tm, ATT_Q_WIDTH), tok),
                  pl.BlockSpec((1, tm, d), tok),
                  pl.BlockSpec((1, 1, d), vec),
                  _resident(w_out.shape, const),
                  pl.BlockSpec((1, d), const), pl.BlockSpec((1, d), const)],
        out_specs=pl.BlockSpec((1, tm, d), tok),
        out_shape=jax.ShapeDtypeStruct(x.shape, F32),
        compiler_params=_params(("arbitrary", "arbitrary"), nbytes),
        name="outproj0_ln",
    )(ya, yb, x, gate, w_out, ln_g, ln_b)


def _ffn_kernel(x_ref, sc_ref, sh_ref, gate_ref, w1_ref, b1_ref, w2_ref, b2_ref, lg_ref, lb_ref,
                o_ref, h_s, acc_s):
    j = pl.program_id(2)

    @pl.when(j == 0)
    def _():
        h_s[...] = (x_ref[0] * (1.0 + sc_ref[0]) + sh_ref[0]).astype(BF16)

    a = jnp.maximum(_dot(h_s[...], w1_ref[...]) + b1_ref[...], 0.0)
    part = _dot((a * a).astype(BF16), w2_ref[...])

    @pl.when(j == 0)
    def _():
        acc_s[...] = part

    @pl.when(j > 0)
    def _():
        acc_s[...] += part

    @pl.when(j == pl.num_programs(2) - 1)
    def _():
        f = acc_s[...] + b2_ref[...]
        o_ref[0] = _layer_norm(ALPHA * x_ref[0] + gate_ref[0] * f, lg_ref[...], lb_ref[...])


def _ffn(x, sc, sh, gate, w1, b1, w2, b2, ln_g, ln_b, tm, th):
    bsz, n, d = x.shape
    hidden = w1.shape[1]
    tok = lambda b, i, j: (b, i, 0)
    vec = lambda b, i, j: (b, 0, 0)
    const = lambda b, i, j: (0, 0)
    nbytes = 2 * 2 * tm * d * 4 + 2 * 2 * 2 * d * th * 2 + tm * d * (2 + 4) + 2 * tm * th * 4 + tm * d * 4
    return pl.pallas_call(
        _ffn_kernel,
        grid=(bsz, n // tm, hidden // th),
        in_specs=[pl.BlockSpec((1, tm, d), tok),
                  pl.BlockSpec((1, 1, d), vec), pl.BlockSpec((1, 1, d), vec), pl.BlockSpec((1, 1, d), vec),
                  pl.BlockSpec((d, th), lambda b, i, j: (0, j)),
                  pl.BlockSpec((1, th), lambda b, i, j: (0, j)),
                  pl.BlockSpec((th, d), lambda b, i, j: (j, 0)),
                  pl.BlockSpec((1, d), const), pl.BlockSpec((1, d), const), pl.BlockSpec((1, d), const)],
        out_specs=pl.BlockSpec((1, tm, d), tok),
        out_shape=jax.ShapeDtypeStruct(x.shape, F32),
        scratch_shapes=[pltpu.VMEM((tm, d), BF16), pltpu.VMEM((tm, d), F32)],
        compiler_params=_params(("arbitrary", "arbitrary", "arbitrary"), nbytes),
        name="ffn_ln",
    )(x, sc, sh, gate, w1, b1, w2, b2, ln_g, ln_b)


def _inproj1_kernel(*refs, full):
    if full:
        x_ref, sc_ref, sh_ref, wu_ref, wb_ref, wc_ref, wx_ref, u_ref, gb_ref, p_ref = refs
    else:
        x_ref, sc_ref, sh_ref, wu_ref, u_ref = refs
    h = (x_ref[0] * (1.0 + sc_ref[0]) + sh_ref[0]).astype(BF16)
    u_ref[0] = _dot(h, wu_ref[...]).astype(BF16)
    if full:
        gb_ref[0] = _dot(h, wb_ref[...]).astype(BF16)
        p_ref[0] = (_dot(h, wc_ref[...]) * _dot(h, wx_ref[...])).astype(BF16)


def _inproj1(x, sc, sh, w_in, tm, full):
    bsz, n, d = x.shape
    o0 = S5_WIDTH
    o1 = o0 + SC_WIDTH
    o2 = o1 + SC_WIDTH
    ws = [w_in[:, :o0]] + ([w_in[:, o0:o1], w_in[:, o1:o2], w_in[:, o2:]] if full else [])
    tok = lambda b, i: (b, i, 0)
    vec = lambda b, i: (b, 0, 0)
    const = lambda b, i: (0, 0)
    widths = [S5_WIDTH] + ([SC_WIDTH, SC_WIDTH] if full else [])
    ncols = sum(w.shape[1] for w in ws)
    nbytes = 2 * tm * d * 4 + 2 * sum(w.size for w in ws) + 2 * 2 * tm * ncols + tm * ncols * 4
    outs = pl.pallas_call(
        functools.partial(_inproj1_kernel, full=full),
        grid=(bsz, n // tm),
        in_specs=[pl.BlockSpec((1, tm, d), tok), pl.BlockSpec((1, 1, d), vec), pl.BlockSpec((1, 1, d), vec)]
                 + [_resident(w.shape, const) for w in ws],
        out_specs=[pl.BlockSpec((1, tm, w), tok) for w in widths],
        out_shape=[jax.ShapeDtypeStruct((bsz, n, w), BF16) for w in widths],
        compiler_params=_params(("arbitrary", "arbitrary"), nbytes),
        name="inproj1" if full else "inproj1_ctx",
    )(x, sc, sh, *ws)
    return outs


def _s5_power_table(par_ref):
    rows = 2 * S5_POW_ROWS
    r = lax.broadcasted_iota(jnp.int32, (rows, S5_SLAB_STATE), 0)
    first = r < S5_POW_ROWS
    lag = (r % S5_POW_ROWS).astype(F32)
    pick = lambda i: jnp.where(first, par_ref[0, 0, i:i + 1, :], par_ref[0, 1, i:i + 1, :])
    dt = jnp.exp(pick(0))
    mag = jnp.exp(lag * dt * pick(1))
    ang = lag * dt * pick(2)
    return mag * jnp.cos(ang), mag * jnp.sin(ang)


def _s5_input_matrix(par_ref, bt_ref, d, pw_re, pw_im):
    a_re = par_ref[0, d, 1:2, :]
    a_im = par_ref[0, d, 2:3, :]
    ab_re = pw_re[d * S5_POW_ROWS + 1:d * S5_POW_ROWS + 2, :]
    ab_im = pw_im[d * S5_POW_ROWS + 1:d * S5_POW_ROWS + 2, :]
    den = a_re * a_re + a_im * a_im
    k_re = ((ab_re - 1.0) * a_re + ab_im * a_im) / den
    k_im = (ab_im * a_re - (ab_re - 1.0) * a_im) / den
    b_re = bt_ref[0, d, 0]
    b_im = bt_ref[0, d, 1]
    return k_re * b_re - k_im * b_im, k_re * b_im + k_im * b_re


def _cmul_row(x_re, x_im, p_re, p_im):
    return x_re * p_re - x_im * p_im, x_re * p_im + x_im * p_re


def _s5_state_prep_kernel(par_ref, bt_ref, wst_ref, a16_ref):
    pw_re, pw_im = _s5_power_table(par_ref)
    for d in range(2):
        bb_re, bb_im = _s5_input_matrix(par_ref, bt_ref, d, pw_re, pw_im)
        base = d * S5_POW_ROWS
        for lag in range(S5_CHUNK):
            e_re, e_im = _cmul_row(bb_re, bb_im, pw_re[base + lag:base + lag + 1, :],
                                   pw_im[base + lag:base + lag + 1, :])
            s = S5_CHUNK - 1 - lag if d == 0 else lag
            col = 2 * d * S5_SLAB_STATE
            wst_ref[0, s * LANES:(s + 1) * LANES, col:col + S5_SLAB_STATE] = e_re.astype(BF16)
            wst_ref[0, s * LANES:(s + 1) * LANES, col + S5_SLAB_STATE:col + 2 * S5_SLAB_STATE] = e_im.astype(BF16)
        row16 = base + S5_CHUNK
        a16_ref[0, :, 2 * d * S5_SLAB_STATE:(2 * d + 1) * S5_SLAB_STATE] = jnp.broadcast_to(
            pw_re[row16:row16 + 1, :], (SUBLANES, S5_SLAB_STATE))
        a16_ref[0, :, (2 * d + 1) * S5_SLAB_STATE:(2 * d + 2) * S5_SLAB_STATE] = jnp.broadcast_to(
            pw_im[row16:row16 + 1, :], (SUBLANES, S5_SLAB_STATE))


def _s5_output_prep_kernel(par_ref, bt_ref, c_ref, ct_ref, wbig_ref):
    pw_re, pw_im = _s5_power_table(par_ref)
    kern = []
    for d in range(2):
        bb_re, bb_im = _s5_input_matrix(par_ref, bt_ref, d, pw_re, pw_im)
        c_re = c_ref[0, d, 0]
        c_im = c_ref[0, d, 1]
        base = d * S5_POW_ROWS
        per_lag = []
        for lag in range(S5_CHUNK):
            e_re, e_im = _cmul_row(bb_re, bb_im, pw_re[base + lag:base + lag + 1, :],
                                   pw_im[base + lag:base + lag + 1, :])
            per_lag.append(
                lax.dot_general(e_re, c_re, (((1,), (1,)), ((), ())), precision=lax.Precision.HIGHEST,
                                preferred_element_type=F32)
                - lax.dot_general(e_im, c_im, (((1,), (1,)), ((), ())), precision=lax.Precision.HIGHEST,
                                  preferred_element_type=F32))
        kern.append(per_lag)
    for s in range(S5_CHUNK):
        for t in range(S5_CHUNK):
            if s < t:
                blk = kern[0][t - s]
            elif s > t:
                blk = kern[1][s - t]
            else:
                blk = kern[0][0] + kern[1][0]
            wbig_ref[0, s * LANES:(s + 1) * LANES, t * LANES:(t + 1) * LANES] = blk.astype(BF16)
    pt_re = pw_re.T
    pt_im = pw_im.T
    for d in range(2):
        ct_re = ct_ref[0, d, 0]
        ct_im = ct_ref[0, d, 1]
        for t in range(S5_CHUNK):
            lag = t + 1 if d == 0 else S5_CHUNK - t
            col = d * S5_POW_ROWS + lag
            p_re = pt_re[:, col:col + 1]
            p_im = pt_im[:, col:col + 1]
            g_re = ct_re * p_re - ct_im * p_im
            g_im = ct_re * p_im + ct_im * p_re
            r0 = S5_XK + 2 * d * S5_SLAB_STATE
            wbig_ref[0, r0:r0 + S5_SLAB_STATE, t * LANES:(t + 1) * LANES] = g_re.astype(BF16)
            wbig_ref[0, r0 + S5_SLAB_STATE:r0 + 2 * S5_SLAB_STATE, t * LANES:(t + 1) * LANES] = (-g_im).astype(BF16)


def _s5_operators(log_dt, a_re, a_im, b_re, b_im, c_re, c_im):
    gs = S5_SLAB_GROUPS
    eye = jnp.eye(gs, dtype=F32)

    def lanes(v):
        return v.reshape(2, S5_SLABS, gs * S5_STATE)

    par = jnp.stack([lanes(jnp.broadcast_to(log_dt[:, :, None], a_re.shape)), lanes(a_re), lanes(a_im)], axis=2)
    par = jnp.pad(par, ((0, 0), (0, 0), (0, SUBLANES - 3), (0, 0))).transpose(1, 0, 2, 3)

    def embed_bt(b):
        b = b.reshape(2, S5_SLABS, gs, S5_STATE, S5_GROUP)
        e = b.transpose(0, 1, 2, 4, 3)[:, :, :, :, None, :] * eye[None, None, :, None, :, None]
        return e.reshape(2, S5_SLABS, gs * S5_GROUP, gs * S5_STATE).transpose(1, 0, 2, 3)

    def embed_c(c):
        c = c.reshape(2, S5_SLABS, gs, S5_GROUP, S5_STATE)
        e = c[:, :, :, :, None, :] * eye[None, None, :, None, :, None]
        return e.reshape(2, S5_SLABS, gs * S5_GROUP, gs * S5_STATE).transpose(1, 0, 2, 3)

    bt = jnp.stack([embed_bt(b_re), embed_bt(b_im)], axis=2)
    cm = jnp.stack([embed_c(c_re), embed_c(c_im)], axis=2)
    ct = cm.transpose(0, 1, 2, 4, 3)

    slab5 = lambda shape: pl.BlockSpec((1,) + shape, lambda k: (k, 0, 0, 0, 0))
    par_spec = pl.BlockSpec((1, 2, SUBLANES, S5_SLAB_STATE), lambda k: (k, 0, 0, 0))
    bt_spec = slab5((2, 2, LANES, S5_SLAB_STATE))
    wst, a16 = pl.pallas_call(
        _s5_state_prep_kernel,
        grid=(S5_SLABS,),
        in_specs=[par_spec, bt_spec],
        out_specs=[pl.BlockSpec((1, S5_XK, S5_HK), lambda k: (k, 0, 0)),
                   pl.BlockSpec((1, SUBLANES, S5_HK), lambda k: (k, 0, 0))],
        out_shape=[jax.ShapeDtypeStruct((S5_SLABS, S5_XK, S5_HK), BF16),
                   jax.ShapeDtypeStruct((S5_SLABS, SUBLANES, S5_HK), F32)],
        compiler_params=_params(("arbitrary",), 2 * S5_XK * S5_HK * 2 + 8 * 1024 * 1024),
        name="s5_state_operator",
    )(par, bt)
    wbig = pl.pallas_call(
        _s5_output_prep_kernel,
        grid=(S5_SLABS,),
        in_specs=[par_spec, bt_spec, bt_spec, slab5((2, 2, S5_SLAB_STATE, LANES))],
        out_specs=pl.BlockSpec((1, S5_XK + S5_HK, S5_XK), lambda k: (k, 0, 0)),
        out_shape=jax.ShapeDtypeStruct((S5_SLABS, S5_XK + S5_HK, S5_XK), BF16),
        compiler_params=_params(("arbitrary",), 2 * (S5_XK + S5_HK) * S5_XK * 2 + 8 * 1024 * 1024),
        name="s5_output_operator",
    )(par, bt, cm, ct)
    return wst, a16, wbig


def _s5_states_kernel(*refs, n_ctx, n_lat):
    xc_refs = refs[:S5_CHUNK]
    xl_refs = refs[S5_CHUNK:2 * S5_CHUNK]
    wst_ref, a16_ref, h_ref, s_s, h_s = refs[2 * S5_CHUNK:]
    x = jnp.concatenate([jnp.concatenate([r[0] for r in xc_refs], axis=1),
                         jnp.concatenate([r[0] for r in xl_refs], axis=1)], axis=0)
    s_s[...] = _dot(x, wst_ref[0])
    p = S5_SLAB_STATE
    af_re = a16_ref[0, 0:1, 0:p]
    af_im = a16_ref[0, 0:1, p:2 * p]
    ar_re = a16_ref[0, 0:1, 2 * p:3 * p]
    ar_im = a16_ref[0, 0:1, 3 * p:4 * p]
    total = n_ctx + n_lat

    def step(i, carry):
        f_re, f_im, r_re, r_im = carry
        h_s[pl.ds(i, 1), 0:p] = f_re
        h_s[pl.ds(i, 1), p:2 * p] = f_im
        s_re = s_s[pl.ds(i, 1), 0:p]
        s_im = s_s[pl.ds(i, 1), p:2 * p]
        f_re, f_im = af_re * f_re - af_im * f_im + s_re, af_re * f_im + af_im * f_re + s_im
        j = jnp.where(i < n_ctx, n_ctx - 1 - i, total + n_ctx - 1 - i)
        h_s[pl.ds(j, 1), 2 * p:3 * p] = r_re
        h_s[pl.ds(j, 1), 3 * p:4 * p] = r_im
        s_re = s_s[pl.ds(j, 1), 2 * p:3 * p]
        s_im = s_s[pl.ds(j, 1), 3 * p:4 * p]
        r_re, r_im = ar_re * r_re - ar_im * r_im + s_re, ar_re * r_im + ar_im * r_re + s_im
        return f_re, f_im, r_re, r_im

    zero = jnp.zeros((1, p), F32)
    lax.fori_loop(0, total, step, (zero, zero, zero, zero), unroll=4)
    h_ref[0, 0] = h_s[pl.ds(n_ctx, n_lat), :].astype(BF16)


def _s5_states(uc_flat, ul_flat, wst, a16):
    bsz, n_ctx, _ = uc_flat.shape
    n_lat = ul_flat.shape[1]
    piece = lambda rows, s: pl.BlockSpec((1, rows, LANES), lambda b, k, s=s: (b, 0, s * S5_SLABS + k))
    in_specs = ([piece(n_ctx, s) for s in range(S5_CHUNK)] + [piece(n_lat, s) for s in range(S5_CHUNK)]
                + [pl.BlockSpec((1, S5_XK, S5_HK), lambda b, k: (k, 0, 0)),
                   pl.BlockSpec((1, SUBLANES, S5_HK), lambda b, k: (k, 0, 0))])
    total = n_ctx + n_lat
    nbytes = 2 * S5_XK * S5_HK * 2 + 2 * total * S5_XK * 2 + 3 * total * S5_HK * 4 + 2 * n_lat * S5_HK * 2
    return pl.pallas_call(
        functools.partial(_s5_states_kernel, n_ctx=n_ctx, n_lat=n_lat),
        grid=(bsz, S5_SLABS),
        in_specs=in_specs,
        out_specs=pl.BlockSpec((1, 1, n_lat, S5_HK), lambda b, k: (b, k, 0, 0)),
        out_shape=jax.ShapeDtypeStruct((bsz, S5_SLABS, n_lat, S5_HK), BF16),
        scratch_shapes=[pltpu.VMEM((total, S5_HK), F32), pltpu.VMEM((total, S5_HK), F32)],
        compiler_params=_params(("arbitrary", "arbitrary"), nbytes),
        name="s5_states",
    )(*([uc_flat] * S5_CHUNK + [ul_flat] * S5_CHUNK + [wst, a16]))


def _s5_readout_kernel(*refs):
    x_refs = refs[:S5_CHUNK]
    h_ref, w_ref, y_ref = refs[S5_CHUNK:]
    lhs = jnp.concatenate([r[0] for r in x_refs] + [h_ref[0, 0]], axis=1)
    y = _dot(lhs, w_ref[0])
    for t in range(S5_CHUNK):
        y_ref[t, 0] = y[:, t * LANES:(t + 1) * LANES].astype(BF16)


def _s5_readout(ul_flat, h_in, wbig):
    bsz, n_lat, _ = ul_flat.shape
    piece = lambda s: pl.BlockSpec((1, n_lat, LANES), lambda b, k, s=s: (b, 0, s * S5_SLABS + k))
    nbytes = 2 * (S5_XK + S5_HK) * S5_XK * 2 + 2 * n_lat * (S5_XK + S5_HK) * 2 * 2 + n_lat * S5_XK * 4 * 2
    return pl.pallas_call(
        _s5_readout_kernel,
        grid=(bsz, S5_SLABS),
        in_specs=[piece(s) for s in range(S5_CHUNK)]
                 + [pl.BlockSpec((1, 1, n_lat, S5_HK), lambda b, k: (b, k, 0, 0)),
                    pl.BlockSpec((1, S5_XK + S5_HK, S5_XK), lambda b, k: (k, 0, 0))],
        out_specs=pl.BlockSpec((S5_CHUNK, 1, n_lat, LANES), lambda b, k: (0, b, 0, k)),
        out_shape=jax.ShapeDtypeStruct((S5_CHUNK, bsz, n_lat, S5_WIDTH), BF16),
        compiler_params=_params(("arbitrary", "arbitrary"), nbytes),
        name="s5_readout",
    )(*([ul_flat] * S5_CHUNK + [h_in, wbig]))


SC_HALO = 16


def _outproj1_kernel(y_ref, u_ref, gb_ref, p_ref, pprev_ref, pnext_ref, x_ref, gate_ref,
                     dskip_ref, wglu_ref, bglu_ref, cw_ref, cb_ref, w_ref, lg_ref, lb_ref, o_ref):
    i = pl.program_id(1)
    tm = x_ref.shape[1]
    yc = y_ref[0].astype(F32) + dskip_ref[...] * u_ref[0].astype(F32)
    z = _gelu_tanh(yc)
    y_c = z * _sigmoid(_dot(z.astype(BF16), wglu_ref[...]) + bglu_ref[...])
    p = p_ref[0].astype(F32)
    row = lax.broadcasted_iota(jnp.int32, (tm, 1), 0)
    prev_row = jnp.where(i > 0, pprev_ref[0, SC_HALO - 1:SC_HALO, :].astype(F32), 0.0)
    next_row = jnp.where(i < pl.num_programs(1) - 1, pnext_ref[0, 0:1, :].astype(F32), 0.0)
    p_dn = jnp.where(row == 0, prev_row, pltpu.roll(p, 1, 0))
    p_up = jnp.where(row == tm - 1, next_row, pltpu.roll(p, tm - 1, 0))
    conv = cb_ref[...] + cw_ref[0:1, :] * p_dn + cw_ref[1:2, :] * p + cw_ref[2:3, :] * p_up
    y_d = gb_ref[0].astype(F32) * conv
    y = jnp.concatenate([y_c.astype(BF16), y_d.astype(BF16)], axis=1)
    out = _dot(y, w_ref[...])
    o_ref[0] = _layer_norm(ALPHA * x_ref[0] + gate_ref[0] * out, lg_ref[...], lb_ref[...])


def _outproj1(y, u, gb, p, x, gate, d_skip, w_glu, b_glu, conv_w, conv_b, w_out, ln_g, ln_b, tm):
    bsz, n, d = x.shape
    tok = lambda b, i: (b, i, 0)
    vec = lambda b, i: (b, 0, 0)
    const = lambda b, i: (0, 0)
    per = tm // SC_HALO
    last = n // SC_HALO - 1
    nbytes = (2 * 2 * tm * d * 4 + w_out.size * 2 + 2 * 2 * tm * (2 * S5_WIDTH + 2 * SC_WIDTH) * 2
              + 6 * tm * SC_WIDTH * 4 + 2 * tm * d * 4)
    return pl.pallas_call(
        _outproj1_kernel,
        grid=(bsz, n // tm),
        in_specs=[pl.BlockSpec((1, tm, S5_WIDTH), tok), pl.BlockSpec((1, tm, S5_WIDTH), tok),
                  pl.BlockSpec((1, tm, SC_WIDTH), tok), pl.BlockSpec((1, tm, SC_WIDTH), tok),
                  pl.BlockSpec((1, SC_HALO, SC_WIDTH), lambda b, i: (b, jnp.maximum(i * per - 1, 0), 0)),
                  pl.BlockSpec((1, SC_HALO, SC_WIDTH), lambda b, i: (b, jnp.minimum((i + 1) * per, last), 0)),
                  pl.BlockSpec((1, tm, d), tok), pl.BlockSpec((1, 1, d), vec),
                  pl.BlockSpec((1, S5_WIDTH), const), _resident(w_glu.shape, const),
                  pl.BlockSpec((1, S5_WIDTH), const),
                  pl.BlockSpec((SC_CONV, SC_WIDTH), const), pl.BlockSpec((1, SC_WIDTH), const),
                  _resident(w_out.shape, const),
                  pl.BlockSpec((1, d), const), pl.BlockSpec((1, d), const)],
        out_specs=pl.BlockSpec((1, tm, d), tok),
        out_shape=jax.ShapeDtypeStruct(x.shape, F32),
        compiler_params=_params(("arbitrary", "arbitrary"), nbytes),
        name="outproj1_ln",
    )(y, u, gb, p, p, p, x, gate, d_skip, w_glu, b_glu, conv_w, conv_b, w_out, ln_g, ln_b)


def _rope_tables(n):
    t = jnp.arange(n)
    row = (t // GRID_W).astype(F32)
    col = (t % GRID_W).astype(F32)
    inv = ROPE_BASE ** (-jnp.arange(ROPE_FREQS, dtype=F32) / ROPE_FREQS)
    ang_r = row[:, None] * inv[None, :]
    ang_c = col[:, None] * inv[None, :]
    cos = jnp.concatenate([jnp.cos(ang_r), jnp.cos(ang_r), jnp.cos(ang_c), jnp.cos(ang_c)], axis=1)
    sin = jnp.concatenate([-jnp.sin(ang_r), jnp.sin(ang_r), -jnp.sin(ang_c), jnp.sin(ang_c)], axis=1)
    return cos, sin


TM_LATENT = 512
TM_CONTEXT = 256
FFN_TH = 1024


def kernel(x, c, ctx, c_ctx, mod_w, mod_b, ln1_g, ln1_b, ln2_g, ln2_b, ffn_w1, ffn_b1, ffn_w2, ffn_b2,
           ab_w_in, ab_w_out, lru_conv_w, lru_conv_b, lru_w_a, lru_b_a, lru_w_x, lru_b_x, lru_lam, att_sink,
           cd_w_in, cd_w_out, s5_log_dt, s5_a_re, s5_a_im, s5_b_re, s5_b_im, s5_c_re, s5_c_im, s5_d,
           s5_w_glu, s5_b_glu, sc_conv_w, sc_conv_b):
    bsz, n, d = x.shape
    lc = ctx.shape[1]
    assert n % TM_LATENT == 0 and lc % TM_CONTEXT == 0 and n % (GRID_W) == 0

    pad_rows = SUBLANES - (bsz + 1) % SUBLANES if (bsz + 1) % SUBLANES else 0
    cc = jnp.concatenate([c, c_ctx[None, :], jnp.zeros((pad_rows, d), F32)], axis=0)
    mod = _modulation(cc, mod_w, mod_b)

    def mod_vectors(layer):
        m = mod[layer]
        lat = [m[:bsz, k * d:(k + 1) * d][:, None, :] for k in range(6)]
        con = [jnp.broadcast_to(m[bsz:bsz + 1, k * d:(k + 1) * d][None], (bsz, 1, d)) for k in range(6)]
        return lat, con

    row2 = lambda v: v.reshape(1, -1)
    rope_tabs = _rope_tables(n)

    for i in range(DEPTH):
        last = i == DEPTH - 1
        j = i // 2
        (sh1, sc1, g1, sh2, sc2, g2), (csh1, csc1, cg1, csh2, csc2, cg2) = mod_vectors(i)
        w1 = ffn_w1[i].astype(BF16)
        w2 = ffn_w2[i].astype(BF16)
        if i % 2 == 0:
            w_in = ab_w_in[j].astype(BF16)
            w_out = ab_w_out[j].astype(BF16)
            u, gate, q, k, v = _inproj0(x, sc1, sh1, w_in, rope_tabs, TM_LATENT)
            uc, gatec, qc, kc, vc = _inproj0(ctx, csc1, csh1, w_in, None, TM_CONTEXT)
            w_gate = jnp.concatenate([lru_w_a[j, 0], lru_w_x[j, 0], lru_w_a[j, 1], lru_w_x[j, 1]],
                                     axis=-1).astype(BF16)
            hb = lambda b: b.reshape(LRU_HEADS, 1, LRU_BLOCK)
            b_gate = jnp.concatenate([hb(lru_b_a[j, 0]), hb(lru_b_x[j, 0]), hb(lru_b_a[j, 1]), hb(lru_b_x[j, 1])],
                                     axis=-1)
            lam = jnp.concatenate([hb(lru_lam[j, 0]), hb(lru_lam[j, 1])], axis=-1)
            ya, yac = _rglru(u, gate, uc, gatec, lru_conv_w[j], row2(lru_conv_b[j]), w_gate, b_gate, lam)
            sink_b = jnp.broadcast_to(att_sink[j][:, None], (ATT_Q_HEADS, LANES))
            yb = _window_attention(q, k, v, kc, vc, sink_b)
            x = _outproj0(ya, yb, x, g1, w_out, row2(ln1_g[i]), row2(ln1_b[i]), TM_LATENT)
            if not last:
                ybc = _context_attention(qc, kc, vc, sink_b)
                ctx = _outproj0(yac, ybc, ctx, cg1, w_out, row2(ln1_g[i]), row2(ln1_b[i]), TM_CONTEXT)
        else:
            assert last
            w_in = cd_w_in[j].astype(BF16)
            w_out = cd_w_out[j].astype(BF16)
            u, gb, p = _inproj1(x, sc1, sh1, w_in, TM_LATENT, True)
            (uc,) = _inproj1(ctx, csc1, csh1, w_in, TM_CONTEXT, False)
            wst, a16, wbig = _s5_operators(s5_log_dt[j], s5_a_re[j], s5_a_im[j], s5_b_re[j], s5_b_im[j],
                                           s5_c_re[j], s5_c_im[j])
            flat = lambda a: a.reshape(bsz, a.shape[1] // S5_CHUNK, S5_CHUNK * S5_WIDTH)
            h_in = _s5_states(flat(uc), flat(u), wst, a16)
            y_chunks = _s5_readout(flat(u), h_in, wbig)
            y = y_chunks.transpose(1, 2, 0, 3).reshape(bsz, n, S5_WIDTH)
            x = _outproj1(y, u, gb, p, x, g1, row2(s5_d[j]), s5_w_glu[j].astype(BF16), row2(s5_b_glu[j]),
                          sc_conv_w[j], row2(sc_conv_b[j]), w_out, row2(ln1_g[i]), row2(ln1_b[i]), TM_LATENT)
        x = _ffn(x, sc2, sh2, g2, w1, row2(ffn_b1[i]), w2, row2(ffn_b2[i]), row2(ln2_g[i]), row2(ln2_b[i]),
                 TM_LATENT, FFN_TH)
        if not last:
            ctx = _ffn(ctx, csc2, csh2, cg2, w1, row2(ffn_b1[i]), w2, row2(ffn_b2[i]), row2(ln2_g[i]),
                       row2(ln2_b[i]), TM_CONTEXT, FFN_TH)
    return x
```

```python
import functools
import math

import jax
import jax.numpy as jnp
from jax import lax
from jax.experimental import pallas as pl
from jax.experimental.pallas import tpu as pltpu

F32 = jnp.float32
BF16 = jnp.bfloat16

D_MODEL = 2048
DEPTH = 2
GRID_W = 64
LRU_WIDTH = D_MODEL // 2
LRU_HEADS = 8
LRU_BLOCK = LRU_WIDTH // LRU_HEADS
LRU_CONV = 4
LRU_CONV_LEFT = 2
LRU_C = 8.0
ATT_HEAD_DIM = 128
ATT_Q_HEADS = (D_MODEL // 2) // ATT_HEAD_DIM
ATT_KV_HEADS = 2
ATT_GROUP = ATT_Q_HEADS // ATT_KV_HEADS
ATT_Q_WIDTH = ATT_Q_HEADS * ATT_HEAD_DIM
ATT_KV_WIDTH = ATT_KV_HEADS * ATT_HEAD_DIM
WINDOW = 128
ATT_BLOCK = 128
ROPE_BASE = 10000.0
ROPE_FREQS = ATT_HEAD_DIM // 4
S5_WIDTH = D_MODEL // 4
S5_GROUP = 16
S5_GROUPS = S5_WIDTH // S5_GROUP
S5_STATE = 64
SC_WIDTH = D_MODEL - S5_WIDTH
SC_CONV = 3
FFN_HIDDEN = 4 * D_MODEL
ALPHA = (2.0 * DEPTH) ** 0.25
LN_EPS = 1e-5
NEG_INF = -1e30

LANES = 128
SUBLANES = 8
V7X_VMEM_BYTES = 64 * 1024 * 1024
V7X_VMEM_BUDGET = 56 * 1024 * 1024

S5_CHUNK = 16
S5_SLAB_GROUPS = LANES // S5_GROUP
S5_SLABS = S5_WIDTH // LANES
S5_SLAB_STATE = S5_SLAB_GROUPS * S5_STATE
S5_XK = S5_CHUNK * LANES
S5_HK = 4 * S5_SLAB_STATE
S5_POW_ROWS = 64


def _vmem_limit(nbytes):
    return int(min(V7X_VMEM_BUDGET, max(nbytes * 3 // 2, 16 * 1024 * 1024)))


def _resident(shape, index_map):
    return pl.BlockSpec(shape, index_map, pipeline_mode=pl.Buffered(1))


def _params(sem, nbytes):
    return pltpu.CompilerParams(dimension_semantics=sem, vmem_limit_bytes=_vmem_limit(nbytes))


def _dot(a, b):
    return jnp.dot(a, b, preferred_element_type=F32)


def _dot_nt(a, b):
    return lax.dot_general(a, b, (((1,), (1,)), ((), ())), preferred_element_type=F32)


def _layer_norm(v, g, b):
    mu = jnp.mean(v, axis=-1, keepdims=True)
    c = v - mu
    var = jnp.mean(c * c, axis=-1, keepdims=True)
    return c * lax.rsqrt(var + LN_EPS) * g + b


def _gelu_tanh(x):
    return 0.5 * x * (1.0 + jnp.tanh(math.sqrt(2.0 / math.pi) * (x + 0.044715 * (x * x * x))))


def _sigmoid(x):
    return 0.5 * (1.0 + jnp.tanh(0.5 * x))


def _mod_kernel(cc_ref, w_ref, b_ref, o_ref):
    a = cc_ref[...]
    a = (a * _sigmoid(a)).astype(BF16)
    o_ref[0] = _dot(a, w_ref[0].astype(BF16)) + b_ref[0]


def _modulation(cc, mod_w, mod_b):
    depth, d, n = mod_w.shape
    tn = 1024
    rows = cc.shape[0]
    return pl.pallas_call(
        _mod_kernel,
        grid=(depth, n // tn),
        in_specs=[pl.BlockSpec((rows, d), lambda l, j: (0, 0)),
                  pl.BlockSpec((1, d, tn), lambda l, j: (l, 0, j)),
                  pl.BlockSpec((1, 1, tn), lambda l, j: (l, 0, j))],
        out_specs=pl.BlockSpec((1, rows, tn), lambda l, j: (l, 0, j)),
        out_shape=jax.ShapeDtypeStruct((depth, rows, n), F32),
        compiler_params=_params(("arbitrary", "arbitrary"), 2 * d * tn * 4 + d * tn * 2),
        name="modulation",
    )(cc, mod_w, mod_b.reshape(depth, 1, n))


def _rope(x, cos, sin_signed, heads):
    lane = lax.broadcasted_iota(jnp.int32, (x.shape[0], ATT_HEAD_DIM), 1)
    first = (lane % (2 * ROPE_FREQS)) < ROPE_FREQS
    out = []
    for h in range(heads):
        xs = x[:, h * ATT_HEAD_DIM:(h + 1) * ATT_HEAD_DIM]
        swapped = jnp.where(first, pltpu.roll(xs, ATT_HEAD_DIM - ROPE_FREQS, 1), pltpu.roll(xs, ROPE_FREQS, 1))
        out.append(xs * cos + swapped * sin_signed)
    return out


def _inproj0_kernel(*refs, rope):
    if rope:
        (x_ref, sc_ref, sh_ref, cos_ref, sin_ref, wu_ref, wg_ref, wq_ref, wk_ref, wv_ref,
         u_ref, g_ref, q_ref, k_ref, v_ref) = refs
    else:
        (x_ref, sc_ref, sh_ref, wu_ref, wg_ref, wq_ref, wk_ref, wv_ref,
         u_ref, g_ref, q_ref, k_ref, v_ref) = refs
    h = (x_ref[0] * (1.0 + sc_ref[0]) + sh_ref[0]).astype(BF16)
    u = _dot(h, wu_ref[...])
    g = _dot(h, wg_ref[...])
    for hd in range(LRU_HEADS):
        u_ref[0, hd] = u[:, hd * LRU_BLOCK:(hd + 1) * LRU_BLOCK].astype(BF16)
        g_ref[0, hd] = g[:, hd * LRU_BLOCK:(hd + 1) * LRU_BLOCK].astype(BF16)
    q = _dot(h, wq_ref[...]) * (ATT_HEAD_DIM ** -0.5)
    k = _dot(h, wk_ref[...])
    if rope:
        cos = cos_ref[...]
        sin = sin_ref[...]
        for hd, piece in enumerate(_rope(q, cos, sin, ATT_Q_HEADS)):
            q_ref[0, :, hd * ATT_HEAD_DIM:(hd + 1) * ATT_HEAD_DIM] = piece.astype(BF16)
        for hd, piece in enumerate(_rope(k, cos, sin, ATT_KV_HEADS)):
            k_ref[0, :, hd * ATT_HEAD_DIM:(hd + 1) * ATT_HEAD_DIM] = piece.astype(BF16)
    else:
        q_ref[0] = q.astype(BF16)
        k_ref[0] = k.astype(BF16)
    v_ref[0] = _dot(h, wv_ref[...]).astype(BF16)


def _inproj0(x, sc, sh, w_in, rope_tabs, tm):
    bsz, n, d = x.shape
    rope = rope_tabs is not None
    o0 = LRU_WIDTH
    o1 = 2 * LRU_WIDTH
    o2 = o1 + ATT_Q_WIDTH
    o3 = o2 + ATT_KV_WIDTH
    ws = [w_in[:, :o0], w_in[:, o0:o1], w_in[:, o1:o2], w_in[:, o2:o3], w_in[:, o3:]]
    tok = lambda b, i: (b, i, 0)
    vec = lambda b, i: (b, 0, 0)
    const = lambda b, i: (0, 0)
    in_specs = [pl.BlockSpec((1, tm, d), tok), pl.BlockSpec((1, 1, d), vec), pl.BlockSpec((1, 1, d), vec)]
    args = [x, sc, sh]
    if rope:
        in_specs += [pl.BlockSpec((tm, ATT_HEAD_DIM), lambda b, i: (i, 0))] * 2
        args += list(rope_tabs)
    in_specs += [_resident(w.shape, const) for w in ws]
    args += ws
    head_major = jax.ShapeDtypeStruct((bsz, LRU_HEADS, n, LRU_BLOCK), BF16)
    head_spec = pl.BlockSpec((1, LRU_HEADS, tm, LRU_BLOCK), lambda b, i: (b, 0, i, 0))
    nbytes = 2 * tm * d * 4 + 2 * sum(w.size for w in ws) + 2 * 2 * tm * w_in.shape[1] + tm * w_in.shape[1] * 4
    return pl.pallas_call(
        functools.partial(_inproj0_kernel, rope=rope),
        grid=(bsz, n // tm),
        in_specs=in_specs,
        out_specs=[head_spec, head_spec,
                   pl.BlockSpec((1, tm, ATT_Q_WIDTH), tok),
                   pl.BlockSpec((1, tm, ATT_KV_WIDTH), tok),
                   pl.BlockSpec((1, tm, ATT_KV_WIDTH), tok)],
        out_shape=[head_major, head_major,
                   jax.ShapeDtypeStruct((bsz, n, ATT_Q_WIDTH), BF16),
                   jax.ShapeDtypeStruct((bsz, n, ATT_KV_WIDTH), BF16),
                   jax.ShapeDtypeStruct((bsz, n, ATT_KV_WIDTH), BF16)],
        compiler_params=_params(("arbitrary", "arbitrary"), nbytes),
        name="inproj0_rope" if rope else "inproj0_ctx",
    )(*args)


LRU_PAD = SUBLANES
LRU_TILE = 256


def _scan8(a, b, row, reverse):
    for k in (1, 2, 4):
        if reverse:
            keep = row < SUBLANES - k
            shift = SUBLANES - k
        else:
            keep = row >= k
            shift = k
        a_sh = jnp.where(keep, pltpu.roll(a, shift, 0), 1.0)
        b_sh = jnp.where(keep, pltpu.roll(b, shift, 0), 0.0)
        b = a * b_sh + b
        a = a * a_sh
    return a, b


def _scan8_rows(a, b, row, reverse):
    out_a, out_b = [], []
    for g in range(a.shape[0] // SUBLANES):
        sl = slice(g * SUBLANES, (g + 1) * SUBLANES)
        ag, bg = _scan8(a[sl], b[sl], row, reverse)
        out_a.append(ag)
        out_b.append(bg)
    return jnp.concatenate(out_a, axis=0), jnp.concatenate(out_b, axis=0)


def _lru_kernel(ul_ref, gl_ref, uc_ref, gc_ref, cw_ref, cb_ref, wg_ref, bg_ref, lam_ref,
                yl_ref, yc_ref, upad, af_s, bf_s, ar_s, br_s, cin_s, *, n_lat, n_ctx):
    cw = cw_ref[...]
    cb = cb_ref[...]
    wg = wg_ref[0]
    bg = bg_ref[0]
    lam = lam_ref[0]
    neg = -lam
    softplus = jnp.maximum(neg, 0.0) + jnp.log1p(jnp.exp(-jnp.abs(neg)))
    half_rate = (-0.5 * LRU_C) * softplus
    row = lax.broadcasted_iota(jnp.int32, (SUBLANES, LRU_BLOCK), 0)
    chains = ((af_s, bf_s, False), (ar_s, br_s, True))

    def coefficients(src_ref, n_rows, row_off):
        upad[pl.ds(0, LRU_PAD), :] = jnp.zeros((LRU_PAD, LRU_BLOCK), F32)
        upad[pl.ds(LRU_PAD, n_rows), :] = src_ref[0, 0].astype(F32)
        upad[pl.ds(LRU_PAD + n_rows, LRU_PAD), :] = jnp.zeros((LRU_PAD, LRU_BLOCK), F32)

        def tile(i, carry):
            t0 = pl.multiple_of(i * LRU_TILE, LRU_TILE)
            xp = upad[pl.ds(t0, LRU_TILE + 2 * LRU_PAD), :]
            conv = cb
            for k in range(LRU_CONV):
                o = LRU_PAD - LRU_CONV_LEFT + k
                conv = conv + cw[k:k + 1, :] * xp[o:o + LRU_TILE, :]
            z = _dot(conv.astype(BF16), wg) + bg
            half_conv = 0.5 * conv
            for d, (a_ref, b_ref, reverse) in enumerate(chains):
                t_a = jnp.tanh(0.5 * z[:, (2 * d) * LRU_BLOCK:(2 * d + 1) * LRU_BLOCK])
                t_x = jnp.tanh(0.5 * z[:, (2 * d + 1) * LRU_BLOCK:(2 * d + 2) * LRU_BLOCK])
                rate = half_rate[:, d * LRU_BLOCK:(d + 1) * LRU_BLOCK]
                a = jnp.exp(rate + rate * t_a)
                gated = half_conv + half_conv * t_x
                b = jnp.sqrt((1.0 - a) * (1.0 + a)) * gated
                a, b = _scan8_rows(a, b, row, reverse)
                a_ref[pl.ds(row_off + t0, LRU_TILE), :] = a
                b_ref[pl.ds(row_off + t0, LRU_TILE), :] = b
            return carry

        lax.fori_loop(0, n_rows // LRU_TILE, tile, 0)

    coefficients(uc_ref, n_ctx, 0)
    coefficients(ul_ref, n_lat, n_ctx)

    groups_ctx = n_ctx // SUBLANES
    groups = (n_ctx + n_lat) // SUBLANES
    fa = af_s[pl.ds(SUBLANES - 1, groups, stride=SUBLANES), :]
    fb = bf_s[pl.ds(SUBLANES - 1, groups, stride=SUBLANES), :]
    ra = ar_s[pl.ds(0, groups, stride=SUBLANES), :]
    rb = br_s[pl.ds(0, groups, stride=SUBLANES), :]
    zero = jnp.zeros((SUBLANES, LRU_BLOCK), F32)
    carry = zero
    for v in range(groups // SUBLANES):
        sl = slice(v * SUBLANES, (v + 1) * SUBLANES)
        a, b = _scan8(fa[sl], fb[sl], row, False)
        incl = b + a * carry
        cin_s[0, sl, :] = jnp.where(row == 0, carry, pltpu.roll(incl, 1, 0))
        carry = jnp.broadcast_to(incl[SUBLANES - 1:SUBLANES, :], incl.shape)
    carry = zero
    order = list(range(groups_ctx // SUBLANES - 1, -1, -1)) + list(range(groups // SUBLANES - 1,
                                                                        groups_ctx // SUBLANES - 1, -1))
    for v in order:
        sl = slice(v * SUBLANES, (v + 1) * SUBLANES)
        a, b = _scan8(ra[sl], rb[sl], row, True)
        incl = b + a * carry
        cin_s[1, sl, :] = jnp.where(row == SUBLANES - 1, carry, pltpu.roll(incl, SUBLANES - 1, 0))
        carry = jnp.broadcast_to(incl[0:1, :], incl.shape)

    def emit(g_ref, y_ref, n_rows, row_off):
        tile_groups = LRU_TILE // SUBLANES

        def tile(i, carry):
            t0 = pl.multiple_of(i * LRU_TILE, LRU_TILE)
            g0 = pl.multiple_of(row_off // SUBLANES + i * tile_groups, tile_groups)
            rows = pl.ds(row_off + t0, LRU_TILE)
            a_f, b_f, a_r, b_r = af_s[rows, :], bf_s[rows, :], ar_s[rows, :], br_s[rows, :]
            pieces = []
            for g in range(tile_groups):
                sl = slice(g * SUBLANES, (g + 1) * SUBLANES)
                c_f = jnp.broadcast_to(cin_s[0, pl.ds(g0 + g, 1), :], (SUBLANES, LRU_BLOCK))
                c_r = jnp.broadcast_to(cin_s[1, pl.ds(g0 + g, 1), :], (SUBLANES, LRU_BLOCK))
                pieces.append((b_f[sl] + a_f[sl] * c_f) + (b_r[sl] + a_r[sl] * c_r))
            h = jnp.concatenate(pieces, axis=0)
            gate = g_ref[0, 0, pl.ds(t0, LRU_TILE), :].astype(F32)
            y_ref[0, 0, pl.ds(t0, LRU_TILE), :] = (h * _gelu_tanh(gate)).astype(BF16)
            return carry

        lax.fori_loop(0, n_rows // LRU_TILE, tile, 0)

    emit(gc_ref, yc_ref, n_ctx, 0)
    emit(gl_ref, yl_ref, n_lat, n_ctx)


def _rglru(u_lat, g_lat, u_ctx, g_ctx, conv_w, conv_b, w_gate, b_gate, lam):
    bsz, heads, n_lat, blk = u_lat.shape
    n_ctx = u_ctx.shape[2]
    total = n_lat + n_ctx
    assert n_ctx % (SUBLANES * SUBLANES) == 0 and n_lat % LRU_TILE == 0 and n_ctx % LRU_TILE == 0
    seq = lambda n: pl.BlockSpec((1, 1, n, blk), lambda b, h: (b, h, 0, 0))
    per_head = lambda shape: pl.BlockSpec((1,) + shape, lambda b, h: (h, 0, 0))
    nbytes =4 * total * blk * 4 + (n_lat + 2 * LRU_PAD) * blk * 4 + 2 * 3 * 2 * total * blk * 2
    return pl.pallas_call(
        functools.partial(_lru_kernel, n_lat=n_lat, n_ctx=n_ctx),
        grid=(bsz, heads),
        in_specs=[seq(n_lat), seq(n_lat), seq(n_ctx), seq(n_ctx),
                  pl.BlockSpec((LRU_CONV, blk), lambda b, h: (0, h)),
                  pl.BlockSpec((1, blk), lambda b, h: (0, h)),
                  per_head((blk, 4 * blk)), per_head((1, 4 * blk)), per_head((1, 2 * blk))],
        out_specs=[seq(n_lat), seq(n_ctx)],
        out_shape=[jax.ShapeDtypeStruct(u_lat.shape, BF16), jax.ShapeDtypeStruct(u_ctx.shape, BF16)],
        scratch_shapes=[pltpu.VMEM((n_lat + 2 * LRU_PAD, blk), F32)]
                       + [pltpu.VMEM((total, blk), F32)] * 4
                       + [pltpu.VMEM((2, total // SUBLANES, blk), F32)],
        compiler_params=_params(("arbitrary", "arbitrary"), nbytes),
        name="rglru",
    )(u_lat, g_lat, u_ctx, g_ctx, conv_w, conv_b, w_gate, b_gate, lam)


ATT_TQ = 256
ATT_BAND = 3 * ATT_BLOCK


def _dot_tn(a, b):
    return lax.dot_general(a, b, (((0,), (0,)), ((), ())), preferred_element_type=F32)


def _softmax_pv(parts, sink_row):
    m = sink_row
    for s, _ in parts:
        m = jnp.maximum(m, jnp.max(s, axis=0, keepdims=True))
    den = jnp.exp(sink_row - m)
    probs = []
    for s, _ in parts:
        p = jnp.exp(s - m)
        den = den + jnp.sum(p, axis=0, keepdims=True)
        probs.append(p)
    inv = 1.0 / den
    acc = None
    for p, (_, v) in zip(probs, parts):
        pv = _dot_tn((p * inv).astype(BF16), v)
        acc = pv if acc is None else acc + pv
    return acc


def _sink_rows(sink, reps):
    return jnp.repeat(sink.reshape(ATT_KV_HEADS, 1, ATT_GROUP), reps, axis=2)


def _attn_kernel(q_ref, k_ref, v_ref, kc_ref, vc_ref, sink_ref, o_ref, *, seq):
    tile = pl.program_id(1)
    blocks = ATT_TQ // ATT_BLOCK
    qcol = lax.broadcasted_iota(jnp.int32, (1, ATT_GROUP * ATT_BLOCK), 1) % ATT_BLOCK
    krow = lax.broadcasted_iota(jnp.int32, (ATT_BAND, 1), 0)
    for i in range(blocks):
        q0 = (tile * blocks + i) * ATT_BLOCK
        start = pl.multiple_of(jnp.clip(q0 - ATT_BLOCK, 0, seq - ATT_BAND), ATT_BLOCK)
        valid = jnp.abs((q0 + qcol) - (start + krow)) <= WINDOW
        for g in range(ATT_KV_HEADS):
            heads = [q_ref[0, i * ATT_BLOCK:(i + 1) * ATT_BLOCK,
                           (g * ATT_GROUP + r) * ATT_HEAD_DIM:(g * ATT_GROUP + r + 1) * ATT_HEAD_DIM]
                     for r in range(ATT_GROUP)]
            qs = jnp.concatenate(heads, axis=0)
            kv = slice(g * ATT_HEAD_DIM, (g + 1) * ATT_HEAD_DIM)
            s_loc = jnp.where(valid, _dot_nt(k_ref[0, pl.ds(start, ATT_BAND), kv], qs), NEG_INF)
            s_ctx = _dot_nt(kc_ref[0, :, kv], qs)
            o = _softmax_pv([(s_loc, v_ref[0, pl.ds(start, ATT_BAND), kv]), (s_ctx, vc_ref[0, :, kv])],
                            sink_ref[g])
            for r in range(ATT_GROUP):
                col = (g * ATT_GROUP + r) * ATT_HEAD_DIM
                o_ref[0, i * ATT_BLOCK:(i + 1) * ATT_BLOCK, col:col + ATT_HEAD_DIM] = (
                    o[r * ATT_BLOCK:(r + 1) * ATT_BLOCK].astype(BF16))


def _window_attention(q, k, v, kc, vc, sink):
    bsz, n, _ = q.shape
    sink_rows = _sink_rows(sink, ATT_BLOCK)
    lc = kc.shape[1]
    whole = lambda rows: pl.BlockSpec((1, rows, ATT_KV_WIDTH), lambda b, i: (b, 0, 0))
    nbytes = 2 * 2 * (2 * ATT_TQ * ATT_Q_WIDTH + 2 * n * ATT_KV_WIDTH + 2 * lc * ATT_KV_WIDTH) + 8 * 1024 * 1024
    return pl.pallas_call(
        functools.partial(_attn_kernel, seq=n),
        grid=(bsz, n // ATT_TQ),
        in_specs=[pl.BlockSpec((1, ATT_TQ, ATT_Q_WIDTH), lambda b, i: (b, i, 0)),
                  whole(n), whole(n), whole(lc), whole(lc),
                  pl.BlockSpec(sink_rows.shape, lambda b, i: (0, 0, 0))],
        out_specs=pl.BlockSpec((1, ATT_TQ, ATT_Q_WIDTH), lambda b, i: (b, i, 0)),
        out_shape=jax.ShapeDtypeStruct(q.shape, BF16),
        compiler_params=_params(("arbitrary", "arbitrary"), nbytes),
        name="window_attention",
    )(q, k, v, kc, vc, sink_rows)


def _ctx_attn_kernel(q_ref, kc_ref, vc_ref, sink_ref, o_ref):
    lc = q_ref.shape[1]
    for g in range(ATT_KV_HEADS):
        heads = [q_ref[0, :, (g * ATT_GROUP + r) * ATT_HEAD_DIM:(g * ATT_GROUP + r + 1) * ATT_HEAD_DIM]
                 for r in range(ATT_GROUP)]
        qs = jnp.concatenate(heads, axis=0)
        kv = slice(g * ATT_HEAD_DIM, (g + 1) * ATT_HEAD_DIM)
        o = _softmax_pv([(_dot_nt(kc_ref[0, :, kv], qs), vc_ref[0, :, kv])], sink_ref[g])
        for r in range(ATT_GROUP):
            col = (g * ATT_GROUP + r) * ATT_HEAD_DIM
            o_ref[0, :, col:col + ATT_HEAD_DIM] = o[r * lc:(r + 1) * lc].astype(BF16)


def _context_attention(qc, kc, vc, sink):
    bsz, lc, _ = qc.shape
    sink_rows = _sink_rows(sink, lc)
    kv_spec = pl.BlockSpec((1, lc, ATT_KV_WIDTH), lambda b: (b, 0, 0))
    q_spec = pl.BlockSpec((1, lc, ATT_Q_WIDTH), lambda b: (b, 0, 0))
    return pl.pallas_call(
        _ctx_attn_kernel,
        grid=(bsz,),
        in_specs=[q_spec, kv_spec, kv_spec, pl.BlockSpec(sink_rows.shape, lambda b: (0, 0, 0))],
        out_specs=q_spec,
        out_shape=jax.ShapeDtypeStruct(qc.shape, BF16),
        compiler_params=_params(("arbitrary",), 8 * 1024 * 1024),
        name="context_attention",
    )(qc, kc, vc, sink_rows)


def _outproj0_kernel(ya_ref, yb_ref, x_ref, gate_ref, w_ref, lg_ref, lb_ref, o_ref):
    y = jnp.concatenate([ya_ref[0, h] for h in range(LRU_HEADS)] + [yb_ref[0]], axis=1)
    out = _dot(y, w_ref[...])
    o_ref[0] = _layer_norm(ALPHA * x_ref[0] + gate_ref[0] * out, lg_ref[...], lb_ref[...])


def _outproj0(ya, yb, x, gate, w_out, ln_g, ln_b, tm):
    bsz, n, d = x.shape
    tok = lambda b, i: (b, i, 0)
    vec = lambda b, i: (b, 0, 0)
    const = lambda b, i: (0, 0)
    nbytes = 2 * 2 * tm * d * 4 + w_out.size * 2 + 2 * 2 * tm * d * 2 + 2 * tm * d * 4
    return pl.pallas_call(
        _outproj0_kernel,
        grid=(bsz, n // tm),
        in_specs=[pl.BlockSpec((1, LRU_HEADS, tm, LRU_BLOCK), lambda b, i: (b, 0, i, 0)),
                  pl.BlockSpec((1, tm, ATT_Q_WIDTH), tok),
                  pl.BlockSpec((1, tm, d), tok),
                  pl.BlockSpec((1, 1, d), vec),
                  _resident(w_out.shape, const),
                  pl.BlockSpec((1, d), const), pl.BlockSpec((1, d), const)],
        out_specs=pl.BlockSpec((1, tm, d), tok),
        out_shape=jax.ShapeDtypeStruct(x.shape, F32),
        compiler_params=_params(("arbitrary", "arbitrary"), nbytes),
        name="outproj0_ln",
    )(ya, yb, x, gate, w_out, ln_g, ln_b)


def _ffn_kernel(x_ref, sc_ref, sh_ref, gate_ref, w1_ref, b1_ref, w2_ref, b2_ref, lg_ref, lb_ref,
                o_ref, h_s, acc_s):
    j = pl.program_id(2)

    @pl.when(j == 0)
    def _():
        h_s[...] = (x_ref[0] * (1.0 + sc_ref[0]) + sh_ref[0]).astype(BF16)
        acc_s[...] = jnp.zeros(acc_s.shape, F32)

    a = jnp.maximum(_dot(h_s[...], w1_ref[...]) + b1_ref[...], 0.0)
    acc_s[...] += _dot((a * a).astype(BF16), w2_ref[...])

    @pl.when(j == pl.num_programs(2) - 1)
    def _():
        f = acc_s[...] + b2_ref[...]
        o_ref[0] = _layer_norm(ALPHA * x_ref[0] + gate_ref[0] * f, lg_ref[...], lb_ref[...])


def _ffn(x, sc, sh, gate, w1, b1, w2, b2, ln_g, ln_b, tm, th):
    bsz, n, d = x.shape
    hidden = w1.shape[1]
    tok = lambda b, i, j: (b, i, 0)
    vec = lambda b, i, j: (b, 0, 0)
    const = lambda b, i, j: (0, 0)
    nbytes = 2 * 2 * tm * d * 4 + 2 * 2 * 2 * d * th * 2 + tm * d * (2 + 4) + 2 * tm * th * 4 + tm * d * 4
    return pl.pallas_call(
        _ffn_kernel,
        grid=(bsz, n // tm, hidden // th),
        in_specs=[pl.BlockSpec((1, tm, d), tok),
                  pl.BlockSpec((1, 1, d), vec), pl.BlockSpec((1, 1, d), vec), pl.BlockSpec((1, 1, d), vec),
                  pl.BlockSpec((d, th), lambda b, i, j: (0, j)),
                  pl.BlockSpec((1, th), lambda b, i, j: (0, j)),
                  pl.BlockSpec((th, d), lambda b, i, j: (j, 0)),
                  pl.BlockSpec((1, d), const), pl.BlockSpec((1, d), const), pl.BlockSpec((1, d), const)],
        out_specs=pl.BlockSpec((1, tm, d), tok),
        out_shape=jax.ShapeDtypeStruct(x.shape, F32),
        scratch_shapes=[pltpu.VMEM((tm, d), BF16), pltpu.VMEM((tm, d), F32)],
        compiler_params=_params(("arbitrary", "arbitrary", "arbitrary"), nbytes),
        name="ffn_ln",
    )(x, sc, sh, gate, w1, b1, w2, b2, ln_g, ln_b)


def _inproj1_kernel(*refs, full):
    if full:
        x_ref, sc_ref, sh_ref, wu_ref, wb_ref, wc_ref, wx_ref, u_ref, gb_ref, p_ref = refs
    else:
        x_ref, sc_ref, sh_ref, wu_ref, u_ref = refs
    h = (x_ref[0] * (1.0 + sc_ref[0]) + sh_ref[0]).astype(BF16)
    u_ref[0] = _dot(h, wu_ref[...]).astype(BF16)
    if full:
        gb_ref[0] = _dot(h, wb_ref[...]).astype(BF16)
        p_ref[0] = (_dot(h, wc_ref[...]) * _dot(h, wx_ref[...])).astype(BF16)


def _inproj1(x, sc, sh, w_in, tm, full):
    bsz, n, d = x.shape
    o0 = S5_WIDTH
    o1 = o0 + SC_WIDTH
    o2 = o1 + SC_WIDTH
    ws = [w_in[:, :o0]] + ([w_in[:, o0:o1], w_in[:, o1:o2], w_in[:, o2:]] if full else [])
    tok = lambda b, i: (b, i, 0)
    vec = lambda b, i: (b, 0, 0)
    const = lambda b, i: (0, 0)
    widths = [S5_WIDTH] + ([SC_WIDTH, SC_WIDTH] if full else [])
    ncols = sum(w.shape[1] for w in ws)
    nbytes = 2 * tm * d * 4 + 2 * sum(w.size for w in ws) + 2 * 2 * tm * ncols + tm * ncols * 4
    outs = pl.pallas_call(
        functools.partial(_inproj1_kernel, full=full),
        grid=(bsz, n // tm),
        in_specs=[pl.BlockSpec((1, tm, d), tok), pl.BlockSpec((1, 1, d), vec), pl.BlockSpec((1, 1, d), vec)]
                 + [_resident(w.shape, const) for w in ws],
        out_specs=[pl.BlockSpec((1, tm, w), tok) for w in widths],
        out_shape=[jax.ShapeDtypeStruct((bsz, n, w), BF16) for w in widths],
        compiler_params=_params(("arbitrary", "arbitrary"), nbytes),
        name="inproj1" if full else "inproj1_ctx",
    )(x, sc, sh, *ws)
    return outs


def _s5_power_table(par_ref):
    rows = 2 * S5_POW_ROWS
    r = lax.broadcasted_iota(jnp.int32, (rows, S5_SLAB_STATE), 0)
    first = r < S5_POW_ROWS
    lag = (r % S5_POW_ROWS).astype(F32)
    pick = lambda i: jnp.where(first, par_ref[0, 0, i:i + 1, :], par_ref[0, 1, i:i + 1, :])
    dt = jnp.exp(pick(0))
    mag = jnp.exp(lag * dt * pick(1))
    ang = lag * dt * pick(2)
    return mag * jnp.cos(ang), mag * jnp.sin(ang)


def _s5_input_matrix(par_ref, bt_ref, d, pw_re, pw_im):
    a_re = par_ref[0, d, 1:2, :]
    a_im = par_ref[0, d, 2:3, :]
    ab_re = pw_re[d * S5_POW_ROWS + 1:d * S5_POW_ROWS + 2, :]
    ab_im = pw_im[d * S5_POW_ROWS + 1:d * S5_POW_ROWS + 2, :]
    den = a_re * a_re + a_im * a_im
    k_re = ((ab_re - 1.0) * a_re + ab_im * a_im) / den
    k_im = (ab_im * a_re - (ab_re - 1.0) * a_im) / den
    b_re = bt_ref[0, d, 0]
    b_im = bt_ref[0, d, 1]
    return k_re * b_re - k_im * b_im, k_re * b_im + k_im * b_re


def _cmul_row(x_re, x_im, p_re, p_im):
    return x_re * p_re - x_im * p_im, x_re * p_im + x_im * p_re


def _s5_state_prep_kernel(par_ref, bt_ref, wst_ref, a16_ref):
    pw_re, pw_im = _s5_power_table(par_ref)
    for d in range(2):
        bb_re, bb_im = _s5_input_matrix(par_ref, bt_ref, d, pw_re, pw_im)
        base = d * S5_POW_ROWS
        for lag in range(S5_CHUNK):
            e_re, e_im = _cmul_row(bb_re, bb_im, pw_re[base + lag:base + lag + 1, :],
                                   pw_im[base + lag:base + lag + 1, :])
            s = S5_CHUNK - 1 - lag if d == 0 else lag
            col = 2 * d * S5_SLAB_STATE
            wst_ref[0, s * LANES:(s + 1) * LANES, col:col + S5_SLAB_STATE] = e_re.astype(BF16)
            wst_ref[0, s * LANES:(s + 1) * LANES, col + S5_SLAB_STATE:col + 2 * S5_SLAB_STATE] = e_im.astype(BF16)
        row16 = base + S5_CHUNK
        a16_ref[0, :, 2 * d * S5_SLAB_STATE:(2 * d + 1) * S5_SLAB_STATE] = jnp.broadcast_to(
            pw_re[row16:row16 + 1, :], (SUBLANES, S5_SLAB_STATE))
        a16_ref[0, :, (2 * d + 1) * S5_SLAB_STATE:(2 * d + 2) * S5_SLAB_STATE] = jnp.broadcast_to(
            pw_im[row16:row16 + 1, :], (SUBLANES, S5_SLAB_STATE))


def _s5_output_prep_kernel(par_ref, bt_ref, c_ref, ct_ref, wbig_ref):
    pw_re, pw_im = _s5_power_table(par_ref)
    kern = []
    for d in range(2):
        bb_re, bb_im = _s5_input_matrix(par_ref, bt_ref, d, pw_re, pw_im)
        c_re = c_ref[0, d, 0]
        c_im = c_ref[0, d, 1]
        base = d * S5_POW_ROWS
        per_lag = []
        for lag in range(S5_CHUNK):
            e_re, e_im = _cmul_row(bb_re, bb_im, pw_re[base + lag:base + lag + 1, :],
                                   pw_im[base + lag:base + lag + 1, :])
            per_lag.append(
                lax.dot_general(e_re, c_re, (((1,), (1,)), ((), ())), precision=lax.Precision.HIGHEST,
                                preferred_element_type=F32)
                - lax.dot_general(e_im, c_im, (((1,), (1,)), ((), ())), precision=lax.Precision.HIGHEST,
                                  preferred_element_type=F32))
        kern.append(per_lag)
    for s in range(S5_CHUNK):
        for t in range(S5_CHUNK):
            if s < t:
                blk = kern[0][t - s]
            elif s > t:
                blk = kern[1][s - t]
            else:
                blk = kern[0][0] + kern[1][0]
            wbig_ref[0, s * LANES:(s + 1) * LANES, t * LANES:(t + 1) * LANES] = blk.astype(BF16)
    pt_re = pw_re.T
    pt_im = pw_im.T
    for d in range(2):
        ct_re = ct_ref[0, d, 0]
        ct_im = ct_ref[0, d, 1]
        for t in range(S5_CHUNK):
            lag = t + 1 if d == 0 else S5_CHUNK - t
            col = d * S5_POW_ROWS + lag
            p_re = pt_re[:, col:col + 1]
            p_im = pt_im[:, col:col + 1]
            g_re = ct_re * p_re - ct_im * p_im
            g_im = ct_re * p_im + ct_im * p_re
            r0 = S5_XK + 2 * d * S5_SLAB_STATE
            wbig_ref[0, r0:r0 + S5_SLAB_STATE, t * LANES:(t + 1) * LANES] = g_re.astype(BF16)
            wbig_ref[0, r0 + S5_SLAB_STATE:r0 + 2 * S5_SLAB_STATE, t * LANES:(t + 1) * LANES] = (-g_im).astype(BF16)


def _s5_operators(log_dt, a_re, a_im, b_re, b_im, c_re, c_im):
    gs = S5_SLAB_GROUPS
    eye = jnp.eye(gs, dtype=F32)

    def lanes(v):
        return v.reshape(2, S5_SLABS, gs * S5_STATE)

    par = jnp.stack([lanes(jnp.broadcast_to(log_dt[:, :, None], a_re.shape)), lanes(a_re), lanes(a_im)], axis=2)
    par = jnp.pad(par, ((0, 0), (0, 0), (0, SUBLANES - 3), (0, 0))).transpose(1, 0, 2, 3)

    def embed_bt(b):
        b = b.reshape(2, S5_SLABS, gs, S5_STATE, S5_GROUP)
        e = b.transpose(0, 1, 2, 4, 3)[:, :, :, :, None, :] * eye[None, None, :, None, :, None]
        return e.reshape(2, S5_SLABS, gs * S5_GROUP, gs * S5_STATE).transpose(1, 0, 2, 3)

    def embed_c(c):
        c = c.reshape(2, S5_SLABS, gs, S5_GROUP, S5_STATE)
        e = c[:, :, :, :, None, :] * eye[None, None, :, None, :, None]
        return e.reshape(2, S5_SLABS, gs * S5_GROUP, gs * S5_STATE).transpose(1, 0, 2, 3)

    bt = jnp.stack([embed_bt(b_re), embed_bt(b_im)], axis=2)
    cm = jnp.stack([embed_c(c_re), embed_c(c_im)], axis=2)
    ct = cm.transpose(0, 1, 2, 4, 3)

    slab5 = lambda shape: pl.BlockSpec((1,) + shape, lambda k: (k, 0, 0, 0, 0))
    par_spec = pl.BlockSpec((1, 2, SUBLANES, S5_SLAB_STATE), lambda k: (k, 0, 0, 0))
    bt_spec = slab5((2, 2, LANES, S5_SLAB_STATE))
    wst, a16 = pl.pallas_call(
        _s5_state_prep_kernel,
        grid=(S5_SLABS,),
        in_specs=[par_spec, bt_spec],
        out_specs=[pl.BlockSpec((1, S5_XK, S5_HK), lambda k: (k, 0, 0)),
                   pl.BlockSpec((1, SUBLANES, S5_HK), lambda k: (k, 0, 0))],
        out_shape=[jax.ShapeDtypeStruct((S5_SLABS, S5_XK, S5_HK), BF16),
                   jax.ShapeDtypeStruct((S5_SLABS, SUBLANES, S5_HK), F32)],
        compiler_params=_params(("arbitrary",), 2 * S5_XK * S5_HK * 2 + 8 * 1024 * 1024),
        name="s5_state_operator",
    )(par, bt)
    wbig = pl.pallas_call(
        _s5_output_prep_kernel,
        grid=(S5_SLABS,),
        in_specs=[par_spec, bt_spec, bt_spec, slab5((2, 2, S5_SLAB_STATE, LANES))],
        out_specs=pl.BlockSpec((1, S5_XK + S5_HK, S5_XK), lambda k: (k, 0, 0)),
        out_shape=jax.ShapeDtypeStruct((S5_SLABS, S5_XK + S5_HK, S5_XK), BF16),
        compiler_params=_params(("arbitrary",), 2 * (S5_XK + S5_HK) * S5_XK * 2 + 8 * 1024 * 1024),
        name="s5_output_operator",
    )(par, bt, cm, ct)
    return wst, a16, wbig


def _s5_states_kernel(*refs, n_ctx, n_lat):
    xc_refs = refs[:S5_CHUNK]
    xl_refs = refs[S5_CHUNK:2 * S5_CHUNK]
    wst_ref, a16_ref, h_ref, s_s, h_s = refs[2 * S5_CHUNK:]
    x = jnp.concatenate([jnp.concatenate([r[0] for r in xc_refs], axis=1),
                         jnp.concatenate([r[0] for r in xl_refs], axis=1)], axis=0)
    s_s[...] = _dot(x, wst_ref[0])
    p = S5_SLAB_STATE
    af_re = a16_ref[0, 0:1, 0:p]
    af_im = a16_ref[0, 0:1, p:2 * p]
    ar_re = a16_ref[0, 0:1, 2 * p:3 * p]
    ar_im = a16_ref[0, 0:1, 3 * p:4 * p]
    total = n_ctx + n_lat

    def step(i, carry):
        f_re, f_im, r_re, r_im = carry
        h_s[pl.ds(i, 1), 0:p] = f_re
        h_s[pl.ds(i, 1), p:2 * p] = f_im
        s_re = s_s[pl.ds(i, 1), 0:p]
        s_im = s_s[pl.ds(i, 1), p:2 * p]
        f_re, f_im = af_re * f_re - af_im * f_im + s_re, af_re * f_im + af_im * f_re + s_im
        j = jnp.where(i < n_ctx, n_ctx - 1 - i, total + n_ctx - 1 - i)
        h_s[pl.ds(j, 1), 2 * p:3 * p] = r_re
        h_s[pl.ds(j, 1), 3 * p:4 * p] = r_im
        s_re = s_s[pl.ds(j, 1), 2 * p:3 * p]
        s_im = s_s[pl.ds(j, 1), 3 * p:4 * p]
        r_re, r_im = ar_re * r_re - ar_im * r_im + s_re, ar_re * r_im + ar_im * r_re + s_im
        return f_re, f_im, r_re, r_im

    zero = jnp.zeros((1, p), F32)
    lax.fori_loop(0, total, step, (zero, zero, zero, zero), unroll=4)
    h_ref[0, 0] = h_s[pl.ds(n_ctx, n_lat), :].astype(BF16)


def _s5_states(uc_flat, ul_flat, wst, a16):
    bsz, n_ctx, _ = uc_flat.shape
    n_lat = ul_flat.shape[1]
    piece = lambda rows, s: pl.BlockSpec((1, rows, LANES), lambda b, k, s=s: (b, 0, s * S5_SLABS + k))
    in_specs = ([piece(n_ctx, s) for s in range(S5_CHUNK)] + [piece(n_lat, s) for s in range(S5_CHUNK)]
                + [pl.BlockSpec((1, S5_XK, S5_HK), lambda b, k: (k, 0, 0)),
                   pl.BlockSpec((1, SUBLANES, S5_HK), lambda b, k: (k, 0, 0))])
    total = n_ctx + n_lat
    nbytes = 2 * S5_XK * S5_HK * 2 + 2 * total * S5_XK * 2 + 3 * total * S5_HK * 4 + 2 * n_lat * S5_HK * 2
    return pl.pallas_call(
        functools.partial(_s5_states_kernel, n_ctx=n_ctx, n_lat=n_lat),
        grid=(bsz, S5_SLABS),
        in_specs=in_specs,
        out_specs=pl.BlockSpec((1, 1, n_lat, S5_HK), lambda b, k: (b, k, 0, 0)),
        out_shape=jax.ShapeDtypeStruct((bsz, S5_SLABS, n_lat, S5_HK), BF16),
        scratch_shapes=[pltpu.VMEM((total, S5_HK), F32), pltpu.VMEM((total, S5_HK), F32)],
        compiler_params=_params(("arbitrary", "arbitrary"), nbytes),
        name="s5_states",
    )(*([uc_flat] * S5_CHUNK + [ul_flat] * S5_CHUNK + [wst, a16]))


def _s5_readout_kernel(*refs):
    x_refs = refs[:S5_CHUNK]
    h_ref, w_ref, y_ref = refs[S5_CHUNK:]
    lhs = jnp.concatenate([r[0] for r in x_refs] + [h_ref[0, 0]], axis=1)
    y = _dot(lhs, w_ref[0])
    for t in range(S5_CHUNK):
        y_ref[t, 0] = y[:, t * LANES:(t + 1) * LANES].astype(BF16)


def _s5_readout(ul_flat, h_in, wbig):
    bsz, n_lat, _ = ul_flat.shape
    piece = lambda s: pl.BlockSpec((1, n_lat, LANES), lambda b, k, s=s: (b, 0, s * S5_SLABS + k))
    nbytes = 2 * (S5_XK + S5_HK) * S5_XK * 2 + 2 * n_lat * (S5_XK + S5_HK) * 2 * 2 + n_lat * S5_XK * 4 * 2
    return pl.pallas_call(
        _s5_readout_kernel,
        grid=(bsz, S5_SLABS),
        in_specs=[piece(s) for s in range(S5_CHUNK)]
                 + [pl.BlockSpec((1, 1, n_lat, S5_HK), lambda b, k: (b, k, 0, 0)),
                    pl.BlockSpec((1, S5_XK + S5_HK, S5_XK), lambda b, k: (k, 0, 0))],
        out_specs=pl.BlockSpec((S5_CHUNK, 1, n_lat, LANES), lambda b, k: (0, b, 0, k)),
        out_shape=jax.ShapeDtypeStruct((S5_CHUNK, bsz, n_lat, S5_WIDTH), BF16),
        compiler_params=_params(("arbitrary", "arbitrary"), nbytes),
        name="s5_readout",
    )(*([ul_flat] * S5_CHUNK + [h_in, wbig]))


SC_HALO = 16


def _outproj1_kernel(y_ref, u_ref, gb_ref, p_ref, pprev_ref, pnext_ref, x_ref, gate_ref,
                     dskip_ref, wglu_ref, bglu_ref, cw_ref, cb_ref, w_ref, lg_ref, lb_ref, o_ref):
    i = pl.program_id(1)
    tm = x_ref.shape[1]
    yc = y_ref[0].astype(F32) + dskip_ref[...] * u_ref[0].astype(F32)
    z = _gelu_tanh(yc)
    y_c = z * _sigmoid(_dot(z.astype(BF16), wglu_ref[...]) + bglu_ref[...])
    p = p_ref[0].astype(F32)
    row = lax.broadcasted_iota(jnp.int32, (tm, 1), 0)
    prev_row = jnp.where(i > 0, pprev_ref[0, SC_HALO - 1:SC_HALO, :].astype(F32), 0.0)
    next_row = jnp.where(i < pl.num_programs(1) - 1, pnext_ref[0, 0:1, :].astype(F32), 0.0)
    p_dn = jnp.where(row == 0, prev_row, pltpu.roll(p, 1, 0))
    p_up = jnp.where(row == tm - 1, next_row, pltpu.roll(p, tm - 1, 0))
    conv = cb_ref[...] + cw_ref[0:1, :] * p_dn + cw_ref[1:2, :] * p + cw_ref[2:3, :] * p_up
    y_d = gb_ref[0].astype(F32) * conv
    y = jnp.concatenate([y_c.astype(BF16), y_d.astype(BF16)], axis=1)
    out = _dot(y, w_ref[...])
    o_ref[0] = _layer_norm(ALPHA * x_ref[0] + gate_ref[0] * out, lg_ref[...], lb_ref[...])


def _outproj1(y, u, gb, p, x, gate, d_skip, w_glu, b_glu, conv_w, conv_b, w_out, ln_g, ln_b, tm):
    bsz, n, d = x.shape
    tok = lambda b, i: (b, i, 0)
    vec = lambda b, i: (b, 0, 0)
    const = lambda b, i: (0, 0)
    per = tm // SC_HALO
    last = n // SC_HALO - 1
    nbytes = (2 * 2 * tm * d * 4 + w_out.size * 2 + 2 * 2 * tm * (2 * S5_WIDTH + 2 * SC_WIDTH) * 2
              + 6 * tm * SC_WIDTH * 4 + 2 * tm * d * 4)
    return pl.pallas_call(
        _outproj1_kernel,
        grid=(bsz, n // tm),
        in_specs=[pl.BlockSpec((1, tm, S5_WIDTH), tok), pl.BlockSpec((1, tm, S5_WIDTH), tok),
                  pl.BlockSpec((1, tm, SC_WIDTH), tok), pl.BlockSpec((1, tm, SC_WIDTH), tok),
                  pl.BlockSpec((1, SC_HALO, SC_WIDTH), lambda b, i: (b, jnp.maximum(i * per - 1, 0), 0)),
                  pl.BlockSpec((1, SC_HALO, SC_WIDTH), lambda b, i: (b, jnp.minimum((i + 1) * per, last), 0)),
                  pl.BlockSpec((1, tm, d), tok), pl.BlockSpec((1, 1, d), vec),
                  pl.BlockSpec((1, S5_WIDTH), const), _resident(w_glu.shape, const),
                  pl.BlockSpec((1, S5_WIDTH), const),
                  pl.BlockSpec((SC_CONV, SC_WIDTH), const), pl.BlockSpec((1, SC_WIDTH), const),
                  _resident(w_out.shape, const),
                  pl.BlockSpec((1, d), const), pl.BlockSpec((1, d), const)],
        out_specs=pl.BlockSpec((1, tm, d), tok),
        out_shape=jax.ShapeDtypeStruct(x.shape, F32),
        compiler_params=_params(("arbitrary", "arbitrary"), nbytes),
        name="outproj1_ln",
    )(y, u, gb, p, p, p, x, gate, d_skip, w_glu, b_glu, conv_w, conv_b, w_out, ln_g, ln_b)


def _rope_tables(n):
    t = jnp.arange(n)
    row = (t // GRID_W).astype(F32)
    col = (t % GRID_W).astype(F32)
    inv = ROPE_BASE ** (-jnp.arange(ROPE_FREQS, dtype=F32) / ROPE_FREQS)
    ang_r = row[:, None] * inv[None, :]
    ang_c = col[:, None] * inv[None, :]
    cos = jnp.concatenate([jnp.cos(ang_r), jnp.cos(ang_r), jnp.cos(ang_c), jnp.cos(ang_c)], axis=1)
    sin = jnp.concatenate([-jnp.sin(ang_r), jnp.sin(ang_r), -jnp.sin(ang_c), jnp.sin(ang_c)], axis=1)
    return cos, sin


TM_LATENT = 512
TM_CONTEXT = 256
FFN_TH = 1024


def kernel(x, c, ctx, c_ctx, mod_w, mod_b, ln1_g, ln1_b, ln2_g, ln2_b, ffn_w1, ffn_b1, ffn_w2, ffn_b2,
           ab_w_in, ab_w_out, lru_conv_w, lru_conv_b, lru_w_a, lru_b_a, lru_w_x, lru_b_x, lru_lam, att_sink,
           cd_w_in, cd_w_out, s5_log_dt, s5_a_re, s5_a_im, s5_b_re, s5_b_im, s5_c_re, s5_c_im, s5_d,
           s5_w_glu, s5_b_glu, sc_conv_w, sc_conv_b):
    bsz, n, d = x.shape
    lc = ctx.shape[1]
    assert n % TM_LATENT == 0 and lc % TM_CONTEXT == 0 and n % (GRID_W) == 0

    pad_rows = SUBLANES - (bsz + 1) % SUBLANES if (bsz + 1) % SUBLANES else 0
    cc = jnp.concatenate([c, c_ctx[None, :], jnp.zeros((pad_rows, d), F32)], axis=0)
    mod = _modulation(cc, mod_w, mod_b)

    def mod_vectors(layer):
        m = mod[layer]
        lat = [m[:bsz, k * d:(k + 1) * d][:, None, :] for k in range(6)]
        con = [jnp.broadcast_to(m[bsz:bsz + 1, k * d:(k + 1) * d][None], (bsz, 1, d)) for k in range(6)]
        return lat, con

    row2 = lambda v: v.reshape(1, -1)
    rope_tabs = _rope_tables(n)

    for i in range(DEPTH):
        last = i == DEPTH - 1
        j = i // 2
        (sh1, sc1, g1, sh2, sc2, g2), (csh1, csc1, cg1, csh2, csc2, cg2) = mod_vectors(i)
        w1 = ffn_w1[i].astype(BF16)
        w2 = ffn_w2[i].astype(BF16)
        if i % 2 == 0:
            w_in = ab_w_in[j].astype(BF16)
            w_out = ab_w_out[j].astype(BF16)
            u, gate, q, k, v = _inproj0(x, sc1, sh1, w_in, rope_tabs, TM_LATENT)
            uc, gatec, qc, kc, vc = _inproj0(ctx, csc1, csh1, w_in, None, TM_CONTEXT)
            w_gate = jnp.concatenate([lru_w_a[j, 0], lru_w_x[j, 0], lru_w_a[j, 1], lru_w_x[j, 1]],
                                     axis=-1).astype(BF16)
            hb = lambda b: b.reshape(LRU_HEADS, 1, LRU_BLOCK)
            b_gate = jnp.concatenate([hb(lru_b_a[j, 0]), hb(lru_b_x[j, 0]), hb(lru_b_a[j, 1]), hb(lru_b_x[j, 1])],
                                     axis=-1)
            lam = jnp.concatenate([hb(lru_lam[j, 0]), hb(lru_lam[j, 1])], axis=-1)
            ya, yac = _rglru(u, gate, uc, gatec, lru_conv_w[j], row2(lru_conv_b[j]), w_gate, b_gate, lam)
            yb = _window_attention(q, k, v, kc, vc, att_sink[j])
            x = _outproj0(ya, yb, x, g1, w_out, row2(ln1_g[i]), row2(ln1_b[i]), TM_LATENT)
            if not last:
                ybc = _context_attention(qc, kc, vc, att_sink[j])
                ctx = _outproj0(yac, ybc, ctx, cg1, w_out, row2(ln1_g[i]), row2(ln1_b[i]), TM_CONTEXT)
        else:
            assert last
            w_in = cd_w_in[j].astype(BF16)
            w_out = cd_w_out[j].astype(BF16)
            u, gb, p = _inproj1(x, sc1, sh1, w_in, TM_LATENT, True)
            (uc,) = _inproj1(ctx, csc1, csh1, w_in, TM_CONTEXT, False)
            wst, a16, wbig = _s5_operators(s5_log_dt[j], s5_a_re[j], s5_a_im[j], s5_b_re[j], s5_b_im[j],
                                           s5_c_re[j], s5_c_im[j])
            flat = lambda a: a.reshape(bsz, a.shape[1] // S5_CHUNK, S5_CHUNK * S5_WIDTH)
            h_in = _s5_states(flat(uc), flat(u), wst, a16)
            y_chunks = _s5_readout(flat(u), h_in, wbig)
            y = y_chunks.transpose(1, 2, 0, 3).reshape(bsz, n, S5_WIDTH)
            x = _outproj1(y, u, gb, p, x, g1, row2(s5_d[j]), s5_w_glu[j].astype(BF16), row2(s5_b_glu[j]),
                          sc_conv_w[j], row2(sc_conv_b[j]), w_out, row2(ln1_g[i]), row2(ln1_b[i]), TM_LATENT)
        x = _ffn(x, sc2, sh2, g2, w1, row2(ffn_b1[i]), w2, row2(ffn_b2[i]), row2(ln2_g[i]), row2(ln2_b[i]),
                 TM_LATENT, FFN_TH)
        if not last:
            ctx = _ffn(ctx, csc2, csh2, cg2, w1, row2(ffn_b1[i]), w2, row2(ffn_b2[i]), row2(ln2_g[i]),
                       row2(ln2_b[i]), TM_CONTEXT, FFN_TH)
    return x
```

```python
import functools
import math

import jax
import jax.numpy as jnp
from jax import lax
from jax.experimental import pallas as pl
from jax.experimental.pallas import tpu as pltpu

F32 = jnp.float32
BF16 = jnp.bfloat16

D_MODEL = 2048
DEPTH = 2
GRID_W = 64
LRU_WIDTH = D_MODEL // 2
LRU_HEADS = 8
LRU_BLOCK = LRU_WIDTH // LRU_HEADS
LRU_CONV = 4
LRU_CONV_LEFT = 2
LRU_C = 8.0
ATT_HEAD_DIM = 128
ATT_Q_HEADS = (D_MODEL // 2) // ATT_HEAD_DIM
ATT_KV_HEADS = 2
ATT_GROUP = ATT_Q_HEADS // ATT_KV_HEADS
ATT_Q_WIDTH = ATT_Q_HEADS * ATT_HEAD_DIM
ATT_KV_WIDTH = ATT_KV_HEADS * ATT_HEAD_DIM
WINDOW = 128
ATT_BLOCK = 128
ROPE_BASE = 10000.0
ROPE_FREQS = ATT_HEAD_DIM // 4
S5_WIDTH = D_MODEL // 4
S5_GROUP = 16
S5_GROUPS = S5_WIDTH // S5_GROUP
S5_STATE = 64
SC_WIDTH = D_MODEL - S5_WIDTH
SC_CONV = 3
FFN_HIDDEN = 4 * D_MODEL
ALPHA = (2.0 * DEPTH) ** 0.25
LN_EPS = 1e-5
NEG_INF = -1e30

LANES = 128
SUBLANES = 8
V7X_VMEM_BYTES = 64 * 1024 * 1024
V7X_VMEM_BUDGET = 56 * 1024 * 1024

S5_CHUNK = 16
S5_SLAB_GROUPS = LANES // S5_GROUP
S5_SLABS = S5_WIDTH // LANES
S5_SLAB_STATE = S5_SLAB_GROUPS * S5_STATE
S5_XK = S5_CHUNK * LANES
S5_HK = 4 * S5_SLAB_STATE
S5_POW_ROWS = 64


def _vmem_limit(nbytes):
    return int(min(V7X_VMEM_BUDGET, max(nbytes * 3 // 2, 16 * 1024 * 1024)))


def _resident(shape, index_map):
    return pl.BlockSpec(shape, index_map, pipeline_mode=pl.Buffered(1))


def _params(sem, nbytes):
    return pltpu.CompilerParams(dimension_semantics=sem, vmem_limit_bytes=_vmem_limit(nbytes))


def _dot(a, b):
    return jnp.dot(a, b, preferred_element_type=F32)


def _dot_nt(a, b):
    return lax.dot_general(a, b, (((1,), (1,)), ((), ())), preferred_element_type=F32)


def _layer_norm(v, g, b):
    mu = jnp.mean(v, axis=-1, keepdims=True)
    c = v - mu
    var = jnp.mean(c * c, axis=-1, keepdims=True)
    return c * lax.rsqrt(var + LN_EPS) * g + b


def _gelu_tanh(x):
    return 0.5 * x * (1.0 + jnp.tanh(math.sqrt(2.0 / math.pi) * (x + 0.044715 * (x * x * x))))


def _sigmoid(x):
    return 0.5 * (1.0 + jnp.tanh(0.5 * x))


MOD_SH1, MOD_SC1, MOD_G1, MOD_SH2, MOD_SC2, MOD_G2 = range(6)


def _mod_spec(mod, layer, chunk):
    _, rows, width = mod.shape
    return pl.BlockSpec((1, rows, width // 6), lambda *_: (layer, 0, chunk))


def _mod_row(ref, ctx_row):
    row = pl.program_id(0) if ctx_row is None else ctx_row
    return ref[0, pl.ds(row, 1), :]


def _modulate(x, sc_ref, sh_ref, ctx_row):
    return (x * (1.0 + _mod_row(sc_ref, ctx_row)) + _mod_row(sh_ref, ctx_row)).astype(BF16)


CAST_BLOCK_BYTES = 8 * 1024 * 1024


def _cast_kernel(w_ref, o_ref):
    o_ref[...] = w_ref[0].astype(BF16)


def _to_bf16(w, layer):
    _, rows, cols = w.shape
    tr = rows
    while tr * cols * 4 > CAST_BLOCK_BYTES and tr % 2 == 0 and tr // 2 >= 2 * SUBLANES:
        tr //= 2
    return pl.pallas_call(
        _cast_kernel,
        grid=(rows // tr,),
        in_specs=[pl.BlockSpec((1, tr, cols), lambda i: (layer, i, 0))],
        out_specs=pl.BlockSpec((tr, cols), lambda i: (i, 0)),
        out_shape=jax.ShapeDtypeStruct((rows, cols), BF16),
        compiler_params=_params(("arbitrary",), 2 * tr * cols * 6),
        name="weight_cast",
    )(w)


def _mod_kernel(cc_ref, w_ref, b_ref, o_ref):
    a = cc_ref[...]
    a = (a * _sigmoid(a)).astype(BF16)
    o_ref[0] = _dot(a, w_ref[0].astype(BF16)) + b_ref[0]


def _modulation(cc, mod_w, mod_b):
    depth, d, n = mod_w.shape
    tn = 1024
    rows = cc.shape[0]
    return pl.pallas_call(
        _mod_kernel,
        grid=(depth, n // tn),
        in_specs=[pl.BlockSpec((rows, d), lambda l, j: (0, 0)),
                  pl.BlockSpec((1, d, tn), lambda l, j: (l, 0, j)),
                  pl.BlockSpec((1, 1, tn), lambda l, j: (l, 0, j))],
        out_specs=pl.BlockSpec((1, rows, tn), lambda l, j: (l, 0, j)),
        out_shape=jax.ShapeDtypeStruct((depth, rows, n), F32),
        compiler_params=_params(("arbitrary", "arbitrary"), 2 * d * tn * 4 + d * tn * 2),
        name="modulation",
    )(cc, mod_w, mod_b.reshape(depth, 1, n))


def _rope(x, cos, sin_signed, heads):
    lane = lax.broadcasted_iota(jnp.int32, (x.shape[0], ATT_HEAD_DIM), 1)
    first = (lane % (2 * ROPE_FREQS)) < ROPE_FREQS
    out = []
    for h in range(heads):
        xs = x[:, h * ATT_HEAD_DIM:(h + 1) * ATT_HEAD_DIM]
        swapped = jnp.where(first, pltpu.roll(xs, ATT_HEAD_DIM - ROPE_FREQS, 1), pltpu.roll(xs, ROPE_FREQS, 1))
        out.append(xs * cos + swapped * sin_signed)
    return out


def _inproj0_kernel(*refs, rope, ctx_row):
    if rope:
        (x_ref, sc_ref, sh_ref, cos_ref, sin_ref, wu_ref, wg_ref, wq_ref, wk_ref, wv_ref,
         u_ref, g_ref, q_ref, k_ref, v_ref) = refs
    else:
        (x_ref, sc_ref, sh_ref, wu_ref, wg_ref, wq_ref, wk_ref, wv_ref,
         u_ref, g_ref, q_ref, k_ref, v_ref) = refs
    h = _modulate(x_ref[0], sc_ref, sh_ref, ctx_row)
    u = _dot(h, wu_ref[...])
    g = _dot(h, wg_ref[...])
    for hd in range(LRU_HEADS):
        u_ref[0, hd] = u[:, hd * LRU_BLOCK:(hd + 1) * LRU_BLOCK].astype(BF16)
        g_ref[0, hd] = g[:, hd * LRU_BLOCK:(hd + 1) * LRU_BLOCK].astype(BF16)
    q = _dot(h, wq_ref[...]) * (ATT_HEAD_DIM ** -0.5)
    k = _dot(h, wk_ref[...])
    if rope:
        cos = cos_ref[...]
        sin = sin_ref[...]
        for hd, piece in enumerate(_rope(q, cos, sin, ATT_Q_HEADS)):
            q_ref[0, :, hd * ATT_HEAD_DIM:(hd + 1) * ATT_HEAD_DIM] = piece.astype(BF16)
        for hd, piece in enumerate(_rope(k, cos, sin, ATT_KV_HEADS)):
            k_ref[0, :, hd * ATT_HEAD_DIM:(hd + 1) * ATT_HEAD_DIM] = piece.astype(BF16)
    else:
        q_ref[0] = q.astype(BF16)
        k_ref[0] = k.astype(BF16)
    v_ref[0] = _dot(h, wv_ref[...]).astype(BF16)


def _inproj0(x, mod, layer, ctx_row, w_in, rope_tabs, tm):
    bsz, n, d = x.shape
    rope = rope_tabs is not None
    tok = lambda b, i: (b, i, 0)
    in_specs = [pl.BlockSpec((1, tm, d), tok), _mod_spec(mod, layer, MOD_SC1), _mod_spec(mod, layer, MOD_SH1)]
    args = [x, mod, mod]
    if rope:
        in_specs += [pl.BlockSpec((tm, ATT_HEAD_DIM), lambda b, i: (i, 0))] * 2
        args += list(rope_tabs)
    kv_at = (2 * LRU_WIDTH + ATT_Q_WIDTH) // ATT_KV_WIDTH
    in_specs += [_resident((d, LRU_WIDTH), lambda b, i: (0, 0)), _resident((d, LRU_WIDTH), lambda b, i: (0, 1)),
                 _resident((d, ATT_Q_WIDTH), lambda b, i: (0, 2)),
                 _resident((d, ATT_KV_WIDTH), lambda b, i: (0, kv_at)),
                 _resident((d, ATT_KV_WIDTH), lambda b, i: (0, kv_at + 1))]
    args += [w_in] * 5
    head_major = jax.ShapeDtypeStruct((bsz, LRU_HEADS, n, LRU_BLOCK), BF16)
    head_spec = pl.BlockSpec((1, LRU_HEADS, tm, LRU_BLOCK), lambda b, i: (b, 0, i, 0))
    nbytes = 2 * tm * d * 4 + 2 * w_in.size + 2 * 2 * tm * w_in.shape[1] + tm * w_in.shape[1] * 4
    return pl.pallas_call(
        functools.partial(_inproj0_kernel, rope=rope, ctx_row=ctx_row),
        grid=(bsz, n // tm),
        in_specs=in_specs,
        out_specs=[head_spec, head_spec,
                   pl.BlockSpec((1, tm, ATT_Q_WIDTH), tok),
                   pl.BlockSpec((1, tm, ATT_KV_WIDTH), tok),
                   pl.BlockSpec((1, tm, ATT_KV_WIDTH), tok)],
        out_shape=[head_major, head_major,
                   jax.ShapeDtypeStruct((bsz, n, ATT_Q_WIDTH), BF16),
                   jax.ShapeDtypeStruct((bsz, n, ATT_KV_WIDTH), BF16),
                   jax.ShapeDtypeStruct((bsz, n, ATT_KV_WIDTH), BF16)],
        compiler_params=_params(("arbitrary", "arbitrary"), nbytes),
        name="inproj0_rope" if rope else "inproj0_ctx",
    )(*args)


LRU_PAD = SUBLANES
LRU_TILE = 256


def _scan8(a, b, row, reverse):
    for k in (1, 2, 4):
        if reverse:
            keep = row < SUBLANES - k
            shift = SUBLANES - k
        else:
            keep = row >= k
            shift = k
        a_sh = jnp.where(keep, pltpu.roll(a, shift, 0), 1.0)
        b_sh = jnp.where(keep, pltpu.roll(b, shift, 0), 0.0)
        b = a * b_sh + b
        a = a * a_sh
    return a, b


def _scan8_rows(a, b, row, reverse):
    out_a, out_b = [], []
    for g in range(a.shape[0] // SUBLANES):
        sl = slice(g * SUBLANES, (g + 1) * SUBLANES)
        ag, bg = _scan8(a[sl], b[sl], row, reverse)
        out_a.append(ag)
        out_b.append(bg)
    return jnp.concatenate(out_a, axis=0), jnp.concatenate(out_b, axis=0)


def _lru_kernel(ul_ref, gl_ref, uc_ref, gc_ref, cw_ref, cb_ref, wg_ref, bg_ref, lam_ref,
                yl_ref, yc_ref, upad, af_s, bf_s, ar_s, br_s, cin_s, *, n_lat, n_ctx):
    cw = cw_ref[...]
    cb = cb_ref[...]
    wg = wg_ref[0]
    bg = bg_ref[0]
    lam = lam_ref[0]
    neg = -lam
    softplus = jnp.maximum(neg, 0.0) + jnp.log1p(jnp.exp(-jnp.abs(neg)))
    half_rate = (-0.5 * LRU_C) * softplus
    row = lax.broadcasted_iota(jnp.int32, (SUBLANES, LRU_BLOCK), 0)
    chains = ((af_s, bf_s, False), (ar_s, br_s, True))

    def coefficients(src_ref, n_rows, row_off):
        upad[pl.ds(0, LRU_PAD), :] = jnp.zeros((LRU_PAD, LRU_BLOCK), F32)
        upad[pl.ds(LRU_PAD, n_rows), :] = src_ref[0, 0].astype(F32)
        upad[pl.ds(LRU_PAD + n_rows, LRU_PAD), :] = jnp.zeros((LRU_PAD, LRU_BLOCK), F32)

        def tile(i, carry):
            t0 = pl.multiple_of(i * LRU_TILE, LRU_TILE)
            xp = upad[pl.ds(t0, LRU_TILE + 2 * LRU_PAD), :]
            conv = cb
            for k in range(LRU_CONV):
                o = LRU_PAD - LRU_CONV_LEFT + k
                conv = conv + cw[k:k + 1, :] * xp[o:o + LRU_TILE, :]
            z = _dot(conv.astype(BF16), wg) + bg
            half_conv = 0.5 * conv
            for d, (a_ref, b_ref, reverse) in enumerate(chains):
                t_a = jnp.tanh(0.5 * z[:, (2 * d) * LRU_BLOCK:(2 * d + 1) * LRU_BLOCK])
                t_x = jnp.tanh(0.5 * z[:, (2 * d + 1) * LRU_BLOCK:(2 * d + 2) * LRU_BLOCK])
                rate = half_rate[:, d * LRU_BLOCK:(d + 1) * LRU_BLOCK]
                a = jnp.exp(rate + rate * t_a)
                gated = half_conv + half_conv * t_x
                b = jnp.sqrt((1.0 - a) * (1.0 + a)) * gated
                a, b = _scan8_rows(a, b, row, reverse)
                a_ref[pl.ds(row_off + t0, LRU_TILE), :] = a
                b_ref[pl.ds(row_off + t0, LRU_TILE), :] = b
            return carry

        lax.fori_loop(0, n_rows // LRU_TILE, tile, 0)

    coefficients(uc_ref, n_ctx, 0)
    coefficients(ul_ref, n_lat, n_ctx)

    groups_ctx = n_ctx // SUBLANES
    groups = (n_ctx + n_lat) // SUBLANES
    fa = af_s[pl.ds(SUBLANES - 1, groups, stride=SUBLANES), :]
    fb = bf_s[pl.ds(SUBLANES - 1, groups, stride=SUBLANES), :]
    ra = ar_s[pl.ds(0, groups, stride=SUBLANES), :]
    rb = br_s[pl.ds(0, groups, stride=SUBLANES), :]
    zero = jnp.zeros((SUBLANES, LRU_BLOCK), F32)
    carry = zero
    for v in range(groups // SUBLANES):
        sl = slice(v * SUBLANES, (v + 1) * SUBLANES)
        a, b = _scan8(fa[sl], fb[sl], row, False)
        incl = b + a * carry
        cin_s[0, sl, :] = jnp.where(row == 0, carry, pltpu.roll(incl, 1, 0))
        carry = jnp.broadcast_to(incl[SUBLANES - 1:SUBLANES, :], incl.shape)
    carry = zero
    order = list(range(groups_ctx // SUBLANES - 1, -1, -1)) + list(range(groups // SUBLANES - 1,
                                                                        groups_ctx // SUBLANES - 1, -1))
    for v in order:
        sl = slice(v * SUBLANES, (v + 1) * SUBLANES)
        a, b = _scan8(ra[sl], rb[sl], row, True)
        incl = b + a * carry
        cin_s[1, sl, :] = jnp.where(row == SUBLANES - 1, carry, pltpu.roll(incl, SUBLANES - 1, 0))
        carry = jnp.broadcast_to(incl[0:1, :], incl.shape)

    def emit(g_ref, y_ref, n_rows, row_off):
        tile_groups = LRU_TILE // SUBLANES

        def tile(i, carry):
            t0 = pl.multiple_of(i * LRU_TILE, LRU_TILE)
            g0 = pl.multiple_of(row_off // SUBLANES + i * tile_groups, tile_groups)
            rows = pl.ds(row_off + t0, LRU_TILE)
            a_f, b_f, a_r, b_r = af_s[rows, :], bf_s[rows, :], ar_s[rows, :], br_s[rows, :]
            pieces = []
            for g in range(tile_groups):
                sl = slice(g * SUBLANES, (g + 1) * SUBLANES)
                c_f = jnp.broadcast_to(cin_s[0, pl.ds(g0 + g, 1), :], (SUBLANES, LRU_BLOCK))
                c_r = jnp.broadcast_to(cin_s[1, pl.ds(g0 + g, 1), :], (SUBLANES, LRU_BLOCK))
                pieces.append((b_f[sl] + a_f[sl] * c_f) + (b_r[sl] + a_r[sl] * c_r))
            h = jnp.concatenate(pieces, axis=0)
            gate = g_ref[0, 0, pl.ds(t0, LRU_TILE), :].astype(F32)
            y_ref[0, 0, pl.ds(t0, LRU_TILE), :] = (h * _gelu_tanh(gate)).astype(BF16)
            return carry

        lax.fori_loop(0, n_rows // LRU_TILE, tile, 0)

    emit(gc_ref, yc_ref, n_ctx, 0)
    emit(gl_ref, yl_ref, n_lat, n_ctx)


def _rglru(u_lat, g_lat, u_ctx, g_ctx, conv_w, conv_b, w_gate, b_gate, lam):
    bsz, heads, n_lat, blk = u_lat.shape
    n_ctx = u_ctx.shape[2]
    total = n_lat + n_ctx
    assert n_ctx % (SUBLANES * SUBLANES) == 0 and n_lat % LRU_TILE == 0 and n_ctx % LRU_TILE == 0
    seq = lambda n: pl.BlockSpec((1, 1, n, blk), lambda b, h: (b, h, 0, 0))
    per_head = lambda shape: pl.BlockSpec((1,) + shape, lambda b, h: (h, 0, 0))
    nbytes =4 * total * blk * 4 + (n_lat + 2 * LRU_PAD) * blk * 4 + 2 * 3 * 2 * total * blk * 2
    return pl.pallas_call(
        functools.partial(_lru_kernel, n_lat=n_lat, n_ctx=n_ctx),
        grid=(bsz, heads),
        in_specs=[seq(n_lat), seq(n_lat), seq(n_ctx), seq(n_ctx),
                  pl.BlockSpec((LRU_CONV, blk), lambda b, h: (0, h)),
                  pl.BlockSpec((1, blk), lambda b, h: (0, h)),
                  per_head((blk, 4 * blk)), per_head((1, 4 * blk)), per_head((1, 2 * blk))],
        out_specs=[seq(n_lat), seq(n_ctx)],
        out_shape=[jax.ShapeDtypeStruct(u_lat.shape, BF16), jax.ShapeDtypeStruct(u_ctx.shape, BF16)],
        scratch_shapes=[pltpu.VMEM((n_lat + 2 * LRU_PAD, blk), F32)]
                       + [pltpu.VMEM((total, blk), F32)] * 4
                       + [pltpu.VMEM((2, total // SUBLANES, blk), F32)],
        compiler_params=_params(("arbitrary", "arbitrary"), nbytes),
        name="rglru",
    )(u_lat, g_lat, u_ctx, g_ctx, conv_w, conv_b, w_gate, b_gate, lam)


ATT_TQ = 256
ATT_BAND = 3 * ATT_BLOCK


def _dot_tn(a, b):
    return lax.dot_general(a, b, (((0,), (0,)), ((), ())), preferred_element_type=F32)


def _softmax_pv(parts, sink_row):
    m = sink_row
    for s, _ in parts:
        m = jnp.maximum(m, jnp.max(s, axis=0, keepdims=True))
    den = jnp.exp(sink_row - m)
    probs = []
    for s, _ in parts:
        p = jnp.exp(s - m)
        den = den + jnp.sum(p, axis=0, keepdims=True)
        probs.append(p)
    inv = 1.0 / den
    acc = None
    for p, (_, v) in zip(probs, parts):
        pv = _dot_tn((p * inv).astype(BF16), v)
        acc = pv if acc is None else acc + pv
    return acc


def _sink_rows(sink, reps):
    return jnp.repeat(sink.reshape(ATT_KV_HEADS, 1, ATT_GROUP), reps, axis=2)


def _attn_kernel(q_ref, k_ref, v_ref, kc_ref, vc_ref, sink_ref, o_ref, *, seq):
    tile = pl.program_id(1)
    blocks = ATT_TQ // ATT_BLOCK
    qcol = lax.broadcasted_iota(jnp.int32, (1, ATT_GROUP * ATT_BLOCK), 1) % ATT_BLOCK
    krow = lax.broadcasted_iota(jnp.int32, (ATT_BAND, 1), 0)
    for i in range(blocks):
        q0 = (tile * blocks + i) * ATT_BLOCK
        start = pl.multiple_of(jnp.clip(q0 - ATT_BLOCK, 0, seq - ATT_BAND), ATT_BLOCK)
        valid = jnp.abs((q0 + qcol) - (start + krow)) <= WINDOW
        for g in range(ATT_KV_HEADS):
            heads = [q_ref[0, i * ATT_BLOCK:(i + 1) * ATT_BLOCK,
                           (g * ATT_GROUP + r) * ATT_HEAD_DIM:(g * ATT_GROUP + r + 1) * ATT_HEAD_DIM]
                     for r in range(ATT_GROUP)]
            qs = jnp.concatenate(heads, axis=0)
            kv = slice(g * ATT_HEAD_DIM, (g + 1) * ATT_HEAD_DIM)
            s_loc = jnp.where(valid, _dot_nt(k_ref[0, pl.ds(start, ATT_BAND), kv], qs), NEG_INF)
            s_ctx = _dot_nt(kc_ref[0, :, kv], qs)
            o = _softmax_pv([(s_loc, v_ref[0, pl.ds(start, ATT_BAND), kv]), (s_ctx, vc_ref[0, :, kv])],
                            sink_ref[g])
            for r in range(ATT_GROUP):
                col = (g * ATT_GROUP + r) * ATT_HEAD_DIM
                o_ref[0, i * ATT_BLOCK:(i + 1) * ATT_BLOCK, col:col + ATT_HEAD_DIM] = (
                    o[r * ATT_BLOCK:(r + 1) * ATT_BLOCK].astype(BF16))


def _window_attention(q, k, v, kc, vc, sink):
    bsz, n, _ = q.shape
    sink_rows = _sink_rows(sink, ATT_BLOCK)
    lc = kc.shape[1]
    whole = lambda rows: pl.BlockSpec((1, rows, ATT_KV_WIDTH), lambda b, i: (b, 0, 0))
    nbytes = 2 * 2 * (2 * ATT_TQ * ATT_Q_WIDTH + 2 * n * ATT_KV_WIDTH + 2 * lc * ATT_KV_WIDTH) + 8 * 1024 * 1024
    return pl.pallas_call(
        functools.partial(_attn_kernel, seq=n),
        grid=(bsz, n // ATT_TQ),
        in_specs=[pl.BlockSpec((1, ATT_TQ, ATT_Q_WIDTH), lambda b, i: (b, i, 0)),
                  whole(n), whole(n), whole(lc), whole(lc),
                  pl.BlockSpec(sink_rows.shape, lambda b, i: (0, 0, 0))],
        out_specs=pl.BlockSpec((1, ATT_TQ, ATT_Q_WIDTH), lambda b, i: (b, i, 0)),
        out_shape=jax.ShapeDtypeStruct(q.shape, BF16),
        compiler_params=_params(("arbitrary", "arbitrary"), nbytes),
        name="window_attention",
    )(q, k, v, kc, vc, sink_rows)


def _ctx_attn_kernel(q_ref, kc_ref, vc_ref, sink_ref, o_ref):
    lc = q_ref.shape[1]
    for g in range(ATT_KV_HEADS):
        heads = [q_ref[0, :, (g * ATT_GROUP + r) * ATT_HEAD_DIM:(g * ATT_GROUP + r + 1) * ATT_HEAD_DIM]
                 for r in range(ATT_GROUP)]
        qs = jnp.concatenate(heads, axis=0)
        kv = slice(g * ATT_HEAD_DIM, (g + 1) * ATT_HEAD_DIM)
        o = _softmax_pv([(_dot_nt(kc_ref[0, :, kv], qs), vc_ref[0, :, kv])], sink_ref[g])
        for r in range(ATT_GROUP):
            col = (g * ATT_GROUP + r) * ATT_HEAD_DIM
            o_ref[0, :, col:col + ATT_HEAD_DIM] = o[r * lc:(r + 1) * lc].astype(BF16)


def _context_attention(qc, kc, vc, sink):
    bsz, lc, _ = qc.shape
    sink_rows = _sink_rows(sink, lc)
    kv_spec = pl.BlockSpec((1, lc, ATT_KV_WIDTH), lambda b: (b, 0, 0))
    q_spec = pl.BlockSpec((1, lc, ATT_Q_WIDTH), lambda b: (b, 0, 0))
    return pl.pallas_call(
        _ctx_attn_kernel,
        grid=(bsz,),
        in_specs=[q_spec, kv_spec, kv_spec, pl.BlockSpec(sink_rows.shape, lambda b: (0, 0, 0))],
        out_specs=q_spec,
        out_shape=jax.ShapeDtypeStruct(qc.shape, BF16),
        compiler_params=_params(("arbitrary",), 8 * 1024 * 1024),
        name="context_attention",
    )(qc, kc, vc, sink_rows)


def _outproj0_kernel(ya_ref, yb_ref, x_ref, gate_ref, w_ref, lg_ref, lb_ref, o_ref, *, ctx_row):
    y = jnp.concatenate([ya_ref[0, h] for h in range(LRU_HEADS)] + [yb_ref[0]], axis=1)
    out = _dot(y, w_ref[...])
    o_ref[0] = _layer_norm(ALPHA * x_ref[0] + _mod_row(gate_ref, ctx_row) * out, lg_ref[...], lb_ref[...])


def _outproj0(ya, yb, x, mod, layer, ctx_row, w_out, ln_g, ln_b, tm):
    bsz, n, d = x.shape
    tok = lambda b, i: (b, i, 0)
    const = lambda b, i: (0, 0)
    nbytes = 2 * 2 * tm * d * 4 + w_out.size * 2 + 2 * 2 * tm * d * 2 + 2 * tm * d * 4
    return pl.pallas_call(
        functools.partial(_outproj0_kernel, ctx_row=ctx_row),
        grid=(bsz, n // tm),
        in_specs=[pl.BlockSpec((1, LRU_HEADS, tm, LRU_BLOCK), lambda b, i: (b, 0, i, 0)),
                  pl.BlockSpec((1, tm, ATT_Q_WIDTH), tok),
                  pl.BlockSpec((1, tm, d), tok),
                  _mod_spec(mod, layer, MOD_G1),
                  _resident(w_out.shape, const),
                  pl.BlockSpec((1, d), const), pl.BlockSpec((1, d), const)],
        out_specs=pl.BlockSpec((1, tm, d), tok),
        out_shape=jax.ShapeDtypeStruct(x.shape, F32),
        compiler_params=_params(("arbitrary", "arbitrary"), nbytes),
        name="outproj0_ln",
    )(ya, yb, x, mod, w_out, ln_g, ln_b)


def _ffn_kernel(x_ref, sc_ref, sh_ref, gate_ref, w1_ref, b1_ref, w2_ref, b2_ref, lg_ref, lb_ref,
                o_ref, h_s, acc_s, *, ctx_row):
    j = pl.program_id(2)

    @pl.when(j == 0)
    def _():
        h_s[...] = _modulate(x_ref[0], sc_ref, sh_ref, ctx_row)
        acc_s[...] = jnp.zeros(acc_s.shape, F32)

    a = jnp.maximum(_dot(h_s[...], w1_ref[...]) + b1_ref[...], 0.0)
    acc_s[...] += _dot((a * a).astype(BF16), w2_ref[...])

    @pl.when(j == pl.num_programs(2) - 1)
    def _():
        f = acc_s[...] + b2_ref[...]
        o_ref[0] = _layer_norm(ALPHA * x_ref[0] + _mod_row(gate_ref, ctx_row) * f, lg_ref[...], lb_ref[...])


def _ffn(x, mod, layer, ctx_row, w1, b1, w2, b2, ln_g, ln_b, tm, th):
    bsz, n, d = x.shape
    hidden = w1.shape[1]
    tok = lambda b, i, j: (b, i, 0)
    const = lambda b, i, j: (0, 0)
    nbytes = 2 * 2 * tm * d * 4 + 2 * 2 * 2 * d * th * 2 + tm * d * (2 + 4) + 2 * tm * th * 4 + tm * d * 4
    return pl.pallas_call(
        functools.partial(_ffn_kernel, ctx_row=ctx_row),
        grid=(bsz, n // tm, hidden // th),
        in_specs=[pl.BlockSpec((1, tm, d), tok),
                  _mod_spec(mod, layer, MOD_SC2), _mod_spec(mod, layer, MOD_SH2), _mod_spec(mod, layer, MOD_G2),
                  pl.BlockSpec((d, th), lambda b, i, j: (0, j)),
                  pl.BlockSpec((1, th), lambda b, i, j: (0, j)),
                  pl.BlockSpec((th, d), lambda b, i, j: (j, 0)),
                  pl.BlockSpec((1, d), const), pl.BlockSpec((1, d), const), pl.BlockSpec((1, d), const)],
        out_specs=pl.BlockSpec((1, tm, d), tok),
        out_shape=jax.ShapeDtypeStruct(x.shape, F32),
        scratch_shapes=[pltpu.VMEM((tm, d), BF16), pltpu.VMEM((tm, d), F32)],
        compiler_params=_params(("arbitrary", "arbitrary", "arbitrary"), nbytes),
        name="ffn_ln",
    )(x, mod, mod, mod, w1, b1, w2, b2, ln_g, ln_b)


SC_PARTS = SC_WIDTH // S5_WIDTH


def _inproj1_kernel(*refs, full, ctx_row):
    x_ref, sc_ref, sh_ref, wu_ref = refs[:4]
    h = _modulate(x_ref[0], sc_ref, sh_ref, ctx_row)
    if not full:
        u_ref, = refs[4:]
        u_ref[0] = _dot(h, wu_ref[...]).astype(BF16)
        return
    wb_refs = refs[4:4 + SC_PARTS]
    wc_refs = refs[4 + SC_PARTS:4 + 2 * SC_PARTS]
    wx_refs = refs[4 + 2 * SC_PARTS:4 + 3 * SC_PARTS]
    u_ref, gb_ref, p_ref = refs[4 + 3 * SC_PARTS:]
    u_ref[0] = _dot(h, wu_ref[...]).astype(BF16)
    for c in range(SC_PARTS):
        cols = slice(c * S5_WIDTH, (c + 1) * S5_WIDTH)
        gb_ref[0, :, cols] = _dot(h, wb_refs[c][...]).astype(BF16)
        p_ref[0, :, cols] = (_dot(h, wc_refs[c][...]) * _dot(h, wx_refs[c][...])).astype(BF16)


def _inproj1(x, mod, layer, ctx_row, w_in, tm, full):
    bsz, n, d = x.shape
    tok = lambda b, i: (b, i, 0)
    n_blocks = 1 + 3 * SC_PARTS if full else 1
    widths = [S5_WIDTH] + ([SC_WIDTH, SC_WIDTH] if full else [])
    ncols = n_blocks * S5_WIDTH
    nbytes = 2 * tm * d * 4 + 2 * d * ncols + 2 * 2 * tm * ncols + tm * ncols * 4
    outs = pl.pallas_call(
        functools.partial(_inproj1_kernel, full=full, ctx_row=ctx_row),
        grid=(bsz, n // tm),
        in_specs=[pl.BlockSpec((1, tm, d), tok), _mod_spec(mod, layer, MOD_SC1), _mod_spec(mod, layer, MOD_SH1)]
                 + [_resident((d, S5_WIDTH), lambda b, i, c=c: (0, c)) for c in range(n_blocks)],
        out_specs=[pl.BlockSpec((1, tm, w), tok) for w in widths],
        out_shape=[jax.ShapeDtypeStruct((bsz, n, w), BF16) for w in widths],
        compiler_params=_params(("arbitrary", "arbitrary"), nbytes),
        name="inproj1" if full else "inproj1_ctx",
    )(x, mod, mod, *([w_in] * n_blocks))
    return outs


def _s5_power_table(par_ref):
    rows = 2 * S5_POW_ROWS
    r = lax.broadcasted_iota(jnp.int32, (rows, S5_SLAB_STATE), 0)
    first = r < S5_POW_ROWS
    lag = (r % S5_POW_ROWS).astype(F32)
    pick = lambda i: jnp.where(first, par_ref[0, 0, i:i + 1, :], par_ref[0, 1, i:i + 1, :])
    dt = jnp.exp(pick(0))
    mag = jnp.exp(lag * dt * pick(1))
    ang = lag * dt * pick(2)
    return mag * jnp.cos(ang), mag * jnp.sin(ang)


def _s5_input_matrix(par_ref, bt_ref, d, pw_re, pw_im):
    a_re = par_ref[0, d, 1:2, :]
    a_im = par_ref[0, d, 2:3, :]
    ab_re = pw_re[d * S5_POW_ROWS + 1:d * S5_POW_ROWS + 2, :]
    ab_im = pw_im[d * S5_POW_ROWS + 1:d * S5_POW_ROWS + 2, :]
    den = a_re * a_re + a_im * a_im
    k_re = ((ab_re - 1.0) * a_re + ab_im * a_im) / den
    k_im = (ab_im * a_re - (ab_re - 1.0) * a_im) / den
    b_re = bt_ref[0, d, 0]
    b_im = bt_ref[0, d, 1]
    return k_re * b_re - k_im * b_im, k_re * b_im + k_im * b_re


def _cmul_row(x_re, x_im, p_re, p_im):
    return x_re * p_re - x_im * p_im, x_re * p_im + x_im * p_re


def _s5_state_prep_kernel(par_ref, bt_ref, wst_ref, a16_ref):
    pw_re, pw_im = _s5_power_table(par_ref)
    for d in range(2):
        bb_re, bb_im = _s5_input_matrix(par_ref, bt_ref, d, pw_re, pw_im)
        base = d * S5_POW_ROWS
        for lag in range(S5_CHUNK):
            e_re, e_im = _cmul_row(bb_re, bb_im, pw_re[base + lag:base + lag + 1, :],
                                   pw_im[base + lag:base + lag + 1, :])
            s = S5_CHUNK - 1 - lag if d == 0 else lag
            col = 2 * d * S5_SLAB_STATE
            wst_ref[0, s * LANES:(s + 1) * LANES, col:col + S5_SLAB_STATE] = e_re.astype(BF16)
            wst_ref[0, s * LANES:(s + 1) * LANES, col + S5_SLAB_STATE:col + 2 * S5_SLAB_STATE] = e_im.astype(BF16)
        row16 = base + S5_CHUNK
        a16_ref[0, :, 2 * d * S5_SLAB_STATE:(2 * d + 1) * S5_SLAB_STATE] = jnp.broadcast_to(
            pw_re[row16:row16 + 1, :], (SUBLANES, S5_SLAB_STATE))
        a16_ref[0, :, (2 * d + 1) * S5_SLAB_STATE:(2 * d + 2) * S5_SLAB_STATE] = jnp.broadcast_to(
            pw_im[row16:row16 + 1, :], (SUBLANES, S5_SLAB_STATE))


def _s5_output_prep_kernel(par_ref, bt_ref, c_ref, ct_ref, wbig_ref):
    pw_re, pw_im = _s5_power_table(par_ref)
    kern = []
    for d in range(2):
        bb_re, bb_im = _s5_input_matrix(par_ref, bt_ref, d, pw_re, pw_im)
        c_re = c_ref[0, d, 0].astype(BF16)
        c_im = c_ref[0, d, 1].astype(BF16)
        base = d * S5_POW_ROWS
        per_lag = []
        for lag in range(S5_CHUNK):
            e_re, e_im = _cmul_row(bb_re, bb_im, pw_re[base + lag:base + lag + 1, :],
                                   pw_im[base + lag:base + lag + 1, :])
            per_lag.append(_dot_nt(e_re.astype(BF16), c_re) - _dot_nt(e_im.astype(BF16), c_im))
        kern.append(per_lag)
    for s in range(S5_CHUNK):
        for t in range(S5_CHUNK):
            if s < t:
                blk = kern[0][t - s]
            elif s > t:
                blk = kern[1][s - t]
            else:
                blk = kern[0][0] + kern[1][0]
            wbig_ref[0, s * LANES:(s + 1) * LANES, t * LANES:(t + 1) * LANES] = blk.astype(BF16)
    pt_re = pw_re.T
    pt_im = pw_im.T
    for d in range(2):
        ct_re = ct_ref[0, d, 0]
        ct_im = ct_ref[0, d, 1]
        for t in range(S5_CHUNK):
            lag = t + 1 if d == 0 else S5_CHUNK - t
            col = d * S5_POW_ROWS + lag
            p_re = pt_re[:, col:col + 1]
            p_im = pt_im[:, col:col + 1]
            g_re = ct_re * p_re - ct_im * p_im
            g_im = ct_re * p_im + ct_im * p_re
            r0 = S5_XK + 2 * d * S5_SLAB_STATE
            wbig_ref[0, r0:r0 + S5_SLAB_STATE, t * LANES:(t + 1) * LANES] = g_re.astype(BF16)
            wbig_ref[0, r0 + S5_SLAB_STATE:r0 + 2 * S5_SLAB_STATE, t * LANES:(t + 1) * LANES] = (-g_im).astype(BF16)


def _s5_operators(log_dt, a_re, a_im, b_re, b_im, c_re, c_im):
    gs = S5_SLAB_GROUPS
    eye = jnp.eye(gs, dtype=F32)

    def lanes(v):
        return v.reshape(2, S5_SLABS, gs * S5_STATE)

    par = jnp.stack([lanes(jnp.broadcast_to(log_dt[:, :, None], a_re.shape)), lanes(a_re), lanes(a_im)], axis=2)
    par = jnp.pad(par, ((0, 0), (0, 0), (0, SUBLANES - 3), (0, 0))).transpose(1, 0, 2, 3)

    def embed_bt(b):
        b = b.reshape(2, S5_SLABS, gs, S5_STATE, S5_GROUP)
        e = b.transpose(0, 1, 2, 4, 3)[:, :, :, :, None, :] * eye[None, None, :, None, :, None]
        return e.reshape(2, S5_SLABS, gs * S5_GROUP, gs * S5_STATE).transpose(1, 0, 2, 3)

    def embed_c(c):
        c = c.reshape(2, S5_SLABS, gs, S5_GROUP, S5_STATE)
        e = c[:, :, :, :, None, :] * eye[None, None, :, None, :, None]
        return e.reshape(2, S5_SLABS, gs * S5_GROUP, gs * S5_STATE).transpose(1, 0, 2, 3)

    bt = jnp.stack([embed_bt(b_re), embed_bt(b_im)], axis=2)
    cm = jnp.stack([embed_c(c_re), embed_c(c_im)], axis=2)
    ct = cm.transpose(0, 1, 2, 4, 3)

    slab5 = lambda shape: pl.BlockSpec((1,) + shape, lambda k: (k, 0, 0, 0, 0))
    par_spec = pl.BlockSpec((1, 2, SUBLANES, S5_SLAB_STATE), lambda k: (k, 0, 0, 0))
    bt_spec = slab5((2, 2, LANES, S5_SLAB_STATE))
    wst, a16 = pl.pallas_call(
        _s5_state_prep_kernel,
        grid=(S5_SLABS,),
        in_specs=[par_spec, bt_spec],
        out_specs=[pl.BlockSpec((1, S5_XK, S5_HK), lambda k: (k, 0, 0)),
                   pl.BlockSpec((1, SUBLANES, S5_HK), lambda k: (k, 0, 0))],
        out_shape=[jax.ShapeDtypeStruct((S5_SLABS, S5_XK, S5_HK), BF16),
                   jax.ShapeDtypeStruct((S5_SLABS, SUBLANES, S5_HK), F32)],
        compiler_params=_params(("arbitrary",), 2 * S5_XK * S5_HK * 2 + 8 * 1024 * 1024),
        name="s5_state_operator",
    )(par, bt)
    wbig = pl.pallas_call(
        _s5_output_prep_kernel,
        grid=(S5_SLABS,),
        in_specs=[par_spec, bt_spec, bt_spec, slab5((2, 2, S5_SLAB_STATE, LANES))],
        out_specs=pl.BlockSpec((1, S5_XK + S5_HK, S5_XK), lambda k: (k, 0, 0)),
        out_shape=jax.ShapeDtypeStruct((S5_SLABS, S5_XK + S5_HK, S5_XK), BF16),
        compiler_params=_params(("arbitrary",), 2 * (S5_XK + S5_HK) * S5_XK * 2 + 8 * 1024 * 1024),
        name="s5_output_operator",
    )(par, bt, cm, ct)
    return wst, a16, wbig


def _s5_states_kernel(*refs, n_ctx, n_lat):
    xc_refs = refs[:S5_CHUNK]
    xl_refs = refs[S5_CHUNK:2 * S5_CHUNK]
    wst_ref, a16_ref, h_ref, s_s, h_s = refs[2 * S5_CHUNK:]
    x = jnp.concatenate([jnp.concatenate([r[0] for r in xc_refs], axis=1),
                         jnp.concatenate([r[0] for r in xl_refs], axis=1)], axis=0)
    s_s[...] = _dot(x, wst_ref[0])
    p = S5_SLAB_STATE
    af_re = a16_ref[0, 0:1, 0:p]
    af_im = a16_ref[0, 0:1, p:2 * p]
    ar_re = a16_ref[0, 0:1, 2 * p:3 * p]
    ar_im = a16_ref[0, 0:1, 3 * p:4 * p]
    total = n_ctx + n_lat

    def step(i, carry):
        f_re, f_im, r_re, r_im = carry
        h_s[pl.ds(i, 1), 0:p] = f_re
        h_s[pl.ds(i, 1), p:2 * p] = f_im
        s_re = s_s[pl.ds(i, 1), 0:p]
        s_im = s_s[pl.ds(i, 1), p:2 * p]
        f_re, f_im = af_re * f_re - af_im * f_im + s_re, af_re * f_im + af_im * f_re + s_im
        j = jnp.where(i < n_ctx, n_ctx - 1 - i, total + n_ctx - 1 - i)
        h_s[pl.ds(j, 1), 2 * p:3 * p] = r_re
        h_s[pl.ds(j, 1), 3 * p:4 * p] = r_im
        s_re = s_s[pl.ds(j, 1), 2 * p:3 * p]
        s_im = s_s[pl.ds(j, 1), 3 * p:4 * p]
        r_re, r_im = ar_re * r_re - ar_im * r_im + s_re, ar_re * r_im + ar_im * r_re + s_im
        return f_re, f_im, r_re, r_im

    zero = jnp.zeros((1, p), F32)
    lax.fori_loop(0, total, step, (zero, zero, zero, zero), unroll=4)
    h_ref[0, 0] = h_s[pl.ds(n_ctx, n_lat), :].astype(BF16)


def _s5_states(uc_flat, ul_flat, wst, a16):
    bsz, n_ctx, _ = uc_flat.shape
    n_lat = ul_flat.shape[1]
    piece = lambda rows, s: pl.BlockSpec((1, rows, LANES), lambda k, b, s=s: (b, 0, s * S5_SLABS + k))
    in_specs = ([piece(n_ctx, s) for s in range(S5_CHUNK)] + [piece(n_lat, s) for s in range(S5_CHUNK)]
                + [pl.BlockSpec((1, S5_XK, S5_HK), lambda k, b: (k, 0, 0)),
                   pl.BlockSpec((1, SUBLANES, S5_HK), lambda k, b: (k, 0, 0))])
    total = n_ctx + n_lat
    nbytes = 2 * S5_XK * S5_HK * 2 + 2 * total * S5_XK * 2 + 3 * total * S5_HK * 4 + 2 * n_lat * S5_HK * 2
    return pl.pallas_call(
        functools.partial(_s5_states_kernel, n_ctx=n_ctx, n_lat=n_lat),
        grid=(S5_SLABS, bsz),
        in_specs=in_specs,
        out_specs=pl.BlockSpec((1, 1, n_lat, S5_HK), lambda k, b: (b, k, 0, 0)),
        out_shape=jax.ShapeDtypeStruct((bsz, S5_SLABS, n_lat, S5_HK), BF16),
        scratch_shapes=[pltpu.VMEM((total, S5_HK), F32), pltpu.VMEM((total, S5_HK), F32)],
        compiler_params=_params(("arbitrary", "arbitrary"), nbytes),
        name="s5_states",
    )(*([uc_flat] * S5_CHUNK + [ul_flat] * S5_CHUNK + [wst, a16]))


def _s5_readout_kernel(*refs):
    x_refs = refs[:S5_CHUNK]
    h_ref, w_ref, y_ref = refs[S5_CHUNK:]
    lhs = jnp.concatenate([r[0] for r in x_refs] + [h_ref[0, 0]], axis=1)
    y = _dot(lhs, w_ref[0])
    for t in range(S5_CHUNK):
        y_ref[t, 0] = y[:, t * LANES:(t + 1) * LANES].astype(BF16)


def _s5_readout(ul_flat, h_in, wbig):
    bsz, n_lat, _ = ul_flat.shape
    piece = lambda s: pl.BlockSpec((1, n_lat, LANES), lambda k, b, s=s: (b, 0, s * S5_SLABS + k))
    nbytes = 2 * (S5_XK + S5_HK) * S5_XK * 2 + 2 * n_lat * (S5_XK + S5_HK) * 2 * 2 + n_lat * S5_XK * 4 * 2
    return pl.pallas_call(
        _s5_readout_kernel,
        grid=(S5_SLABS, bsz),
        in_specs=[piece(s) for s in range(S5_CHUNK)]
                 + [pl.BlockSpec((1, 1, n_lat, S5_HK), lambda k, b: (b, k, 0, 0)),
                    pl.BlockSpec((1, S5_XK + S5_HK, S5_XK), lambda k, b: (k, 0, 0))],
        out_specs=pl.BlockSpec((S5_CHUNK, 1, n_lat, LANES), lambda k, b: (0, b, 0, k)),
        out_shape=jax.ShapeDtypeStruct((S5_CHUNK, bsz, n_lat, S5_WIDTH), BF16),
        compiler_params=_params(("arbitrary", "arbitrary"), nbytes),
        name="s5_readout",
    )(*([ul_flat] * S5_CHUNK + [h_in, wbig]))


SC_HALO = 16


def _outproj1_kernel(y_ref, u_ref, gb_ref, p_ref, pprev_ref, pnext_ref, x_ref, gate_ref,
                     dskip_ref, wglu_ref, bglu_ref, cw_ref, cb_ref, w_ref, lg_ref, lb_ref, o_ref, *, ctx_row):
    i = pl.program_id(1)
    tm = x_ref.shape[1]
    yc = y_ref[0].astype(F32) + dskip_ref[...] * u_ref[0].astype(F32)
    z = _gelu_tanh(yc)
    y_c = z * _sigmoid(_dot(z.astype(BF16), wglu_ref[...]) + bglu_ref[...])
    p = p_ref[0].astype(F32)
    row = lax.broadcasted_iota(jnp.int32, (tm, 1), 0)
    prev_row = jnp.where(i > 0, pprev_ref[0, SC_HALO - 1:SC_HALO, :].astype(F32), 0.0)
    next_row = jnp.where(i < pl.num_programs(1) - 1, pnext_ref[0, 0:1, :].astype(F32), 0.0)
    p_dn = jnp.where(row == 0, prev_row, pltpu.roll(p, 1, 0))
    p_up = jnp.where(row == tm - 1, next_row, pltpu.roll(p, tm - 1, 0))
    conv = cb_ref[...] + cw_ref[0:1, :] * p_dn + cw_ref[1:2, :] * p + cw_ref[2:3, :] * p_up
    y_d = gb_ref[0].astype(F32) * conv
    y = jnp.concatenate([y_c.astype(BF16), y_d.astype(BF16)], axis=1)
    out = _dot(y, w_ref[...])
    o_ref[0] = _layer_norm(ALPHA * x_ref[0] + _mod_row(gate_ref, ctx_row) * out, lg_ref[...], lb_ref[...])


def _outproj1(y, u, gb, p, x, mod, layer, ctx_row, d_skip, w_glu, b_glu, conv_w, conv_b, w_out, ln_g, ln_b, tm):
    bsz, n, d = x.shape
    tok = lambda b, i: (b, i, 0)
    const = lambda b, i: (0, 0)
    per = tm // SC_HALO
    last = n // SC_HALO - 1
    nbytes = (2 * 2 * tm * d * 4 + w_out.size * 2 + 2 * 2 * tm * (2 * S5_WIDTH + 2 * SC_WIDTH) * 2
              + 6 * tm * SC_WIDTH * 4 + 2 * tm * d * 4)
    return pl.pallas_call(
        functools.partial(_outproj1_kernel, ctx_row=ctx_row),
        grid=(bsz, n // tm),
        in_specs=[pl.BlockSpec((1, tm, S5_WIDTH), tok), pl.BlockSpec((1, tm, S5_WIDTH), tok),
                  pl.BlockSpec((1, tm, SC_WIDTH), tok), pl.BlockSpec((1, tm, SC_WIDTH), tok),
                  pl.BlockSpec((1, SC_HALO, SC_WIDTH), lambda b, i: (b, jnp.maximum(i * per - 1, 0), 0)),
                  pl.BlockSpec((1, SC_HALO, SC_WIDTH), lambda b, i: (b, jnp.minimum((i + 1) * per, last), 0)),
                  pl.BlockSpec((1, tm, d), tok), _mod_spec(mod, layer, MOD_G1),
                  pl.BlockSpec((1, S5_WIDTH), const), _resident(w_glu.shape, const),
                  pl.BlockSpec((1, S5_WIDTH), const),
                  pl.BlockSpec((SC_CONV, SC_WIDTH), const), pl.BlockSpec((1, SC_WIDTH), const),
                  _resident(w_out.shape, const),
                  pl.BlockSpec((1, d), const), pl.BlockSpec((1, d), const)],
        out_specs=pl.BlockSpec((1, tm, d), tok),
        out_shape=jax.ShapeDtypeStruct(x.shape, F32),
        compiler_params=_params(("arbitrary", "arbitrary"), nbytes),
        name="outproj1_ln",
    )(y, u, gb, p, p, p, x, mod, d_skip, w_glu, b_glu, conv_w, conv_b, w_out, ln_g, ln_b)


def _rope_tables(n):
    t = jnp.arange(n)
    row = (t // GRID_W).astype(F32)
    col = (t % GRID_W).astype(F32)
    inv = ROPE_BASE ** (-jnp.arange(ROPE_FREQS, dtype=F32) / ROPE_FREQS)
    ang_r = row[:, None] * inv[None, :]
    ang_c = col[:, None] * inv[None, :]
    cos = jnp.concatenate([jnp.cos(ang_r), jnp.cos(ang_r), jnp.cos(ang_c), jnp.cos(ang_c)], axis=1)
    sin = jnp.concatenate([-jnp.sin(ang_r), jnp.sin(ang_r), -jnp.sin(ang_c), jnp.sin(ang_c)], axis=1)
    return cos, sin


TM_LATENT = 512
TM_CONTEXT = 256
FFN_TH = 1024


def kernel(x, c, ctx, c_ctx, mod_w, mod_b, ln1_g, ln1_b, ln2_g, ln2_b, ffn_w1, ffn_b1, ffn_w2, ffn_b2,
           ab_w_in, ab_w_out, lru_conv_w, lru_conv_b, lru_w_a, lru_b_a, lru_w_x, lru_b_x, lru_lam, att_sink,
           cd_w_in, cd_w_out, s5_log_dt, s5_a_re, s5_a_im, s5_b_re, s5_b_im, s5_c_re, s5_c_im, s5_d,
           s5_w_glu, s5_b_glu, sc_conv_w, sc_conv_b):
    bsz, n, d = x.shape
    lc = ctx.shape[1]
    assert n % TM_LATENT == 0 and lc % TM_CONTEXT == 0 and n % (GRID_W) == 0

    pad_rows = SUBLANES - (bsz + 1) % SUBLANES if (bsz + 1) % SUBLANES else 0
    cc = jnp.concatenate([c, c_ctx[None, :], jnp.zeros((pad_rows, d), F32)], axis=0)
    mod = _modulation(cc, mod_w, mod_b)
    ctx_row = bsz

    row2 = lambda v: v.reshape(1, -1)
    rope_tabs = _rope_tables(n)

    for i in range(DEPTH):
        last = i == DEPTH - 1
        j = i // 2
        w1 = _to_bf16(ffn_w1, i)
        w2 = _to_bf16(ffn_w2, i)
        if i % 2 == 0:
            w_in = _to_bf16(ab_w_in, j)
            w_out = _to_bf16(ab_w_out, j)
            u, gate, q, k, v = _inproj0(x, mod, i, None, w_in, rope_tabs, TM_LATENT)
            uc, gatec, qc, kc, vc = _inproj0(ctx, mod, i, ctx_row, w_in, None, TM_CONTEXT)
            w_gate = jnp.concatenate([lru_w_a[j, 0], lru_w_x[j, 0], lru_w_a[j, 1], lru_w_x[j, 1]],
                                     axis=-1).astype(BF16)
            hb = lambda b: b.reshape(LRU_HEADS, 1, LRU_BLOCK)
            b_gate = jnp.concatenate([hb(lru_b_a[j, 0]), hb(lru_b_x[j, 0]), hb(lru_b_a[j, 1]), hb(lru_b_x[j, 1])],
                                     axis=-1)
            lam = jnp.concatenate([hb(lru_lam[j, 0]), hb(lru_lam[j, 1])], axis=-1)
            ya, yac = _rglru(u, gate, uc, gatec, lru_conv_w[j], row2(lru_conv_b[j]), w_gate, b_gate, lam)
            yb = _window_attention(q, k, v, kc, vc, att_sink[j])
            x = _outproj0(ya, yb, x, mod, i, None, w_out, row2(ln1_g[i]), row2(ln1_b[i]), TM_LATENT)
            if not last:
                ybc = _context_attention(qc, kc, vc, att_sink[j])
                ctx = _outproj0(yac, ybc, ctx, mod, i, ctx_row, w_out, row2(ln1_g[i]), row2(ln1_b[i]), TM_CONTEXT)
        else:
            assert last
            w_in = _to_bf16(cd_w_in, j)
            w_out = _to_bf16(cd_w_out, j)
            u, gb, p = _inproj1(x, mod, i, None, w_in, TM_LATENT, True)
            (uc,) = _inproj1(ctx, mod, i, ctx_row, w_in, TM_CONTEXT, False)
            wst, a16, wbig = _s5_operators(s5_log_dt[j], s5_a_re[j], s5_a_im[j], s5_b_re[j], s5_b_im[j],
                                           s5_c_re[j], s5_c_im[j])
            flat = lambda a: a.reshape(bsz, a.shape[1] // S5_CHUNK, S5_CHUNK * S5_WIDTH)
            h_in = _s5_states(flat(uc), flat(u), wst, a16)
            y_chunks = _s5_readout(flat(u), h_in, wbig)
            y = y_chunks.transpose(1, 2, 0, 3).reshape(bsz, n, S5_WIDTH)
            x = _outproj1(y, u, gb, p, x, mod, i, None, row2(s5_d[j]), s5_w_glu[j].astype(BF16), row2(s5_b_glu[j]),
                          sc_conv_w[j], row2(sc_conv_b[j]), w_out, row2(ln1_g[i]), row2(ln1_b[i]), TM_LATENT)
        x = _ffn(x, mod, i, None, w1, row2(ffn_b1[i]), w2, row2(ffn_b2[i]), row2(ln2_g[i]), row2(ln2_b[i]),
                 TM_LATENT, FFN_TH)
        if not last:
            ctx = _ffn(ctx, mod, i, ctx_row, w1, row2(ffn_b1[i]), w2, row2(ffn_b2[i]), row2(ln2_g[i]),
                       row2(ln2_b[i]), TM_CONTEXT, FFN_TH)
    return x
```

```python
import functools
import math

import jax
import jax.numpy as jnp
from jax import lax
from jax.experimental import pallas as pl
from jax.experimental.pallas import tpu as pltpu

F32 = jnp.float32
BF16 = jnp.bfloat16

D_MODEL = 2048
DEPTH = 2
GRID_W = 64
LRU_WIDTH = D_MODEL // 2
LRU_HEADS = 8
LRU_BLOCK = LRU_WIDTH // LRU_HEADS
LRU_CONV = 4
LRU_CONV_LEFT = 2
LRU_C = 8.0
ATT_HEAD_DIM = 128
ATT_Q_HEADS = (D_MODEL // 2) // ATT_HEAD_DIM
ATT_KV_HEADS = 2
ATT_GROUP = ATT_Q_HEADS // ATT_KV_HEADS
ATT_Q_WIDTH = ATT_Q_HEADS * ATT_HEAD_DIM
ATT_KV_WIDTH = ATT_KV_HEADS * ATT_HEAD_DIM
WINDOW = 128
ATT_BLOCK = 128
ROPE_BASE = 10000.0
ROPE_FREQS = ATT_HEAD_DIM // 4
S5_WIDTH = D_MODEL // 4
S5_GROUP = 16
S5_GROUPS = S5_WIDTH // S5_GROUP
S5_STATE = 64
SC_WIDTH = D_MODEL - S5_WIDTH
SC_CONV = 3
FFN_HIDDEN = 4 * D_MODEL
ALPHA = (2.0 * DEPTH) ** 0.25
LN_EPS = 1e-5
NEG_INF = -1e30
LOG2E = math.log2(math.e)

LANES = 128
SUBLANES = 8
V7X_VMEM_BYTES = 64 * 1024 * 1024
V7X_VMEM_BUDGET = 56 * 1024 * 1024

S5_CHUNK = 16
S5_SLAB_GROUPS = LANES // S5_GROUP
S5_SLABS = S5_WIDTH // LANES
S5_SLAB_STATE = S5_SLAB_GROUPS * S5_STATE
S5_XK = S5_CHUNK * LANES
S5_HK = 4 * S5_SLAB_STATE
S5_POW_ROWS = 64


def _vmem_limit(nbytes):
    return int(min(V7X_VMEM_BUDGET, max(nbytes * 3 // 2, 16 * 1024 * 1024)))


def _resident(shape, index_map):
    return pl.BlockSpec(shape, index_map, pipeline_mode=pl.Buffered(1))


def _params(sem, nbytes):
    return pltpu.CompilerParams(dimension_semantics=sem, vmem_limit_bytes=_vmem_limit(nbytes))


def _dot(a, b):
    return jnp.dot(a, b, preferred_element_type=F32)


def _dot_nt(a, b):
    return lax.dot_general(a, b, (((1,), (1,)), ((), ())), preferred_element_type=F32)


def _layer_norm(v, g, b):
    mu = jnp.mean(v, axis=-1, keepdims=True)
    c = v - mu
    var = jnp.mean(c * c, axis=-1, keepdims=True)
    return c * lax.rsqrt(var + LN_EPS) * g + b


def _gelu_tanh(x):
    return 0.5 * x * (1.0 + jnp.tanh(math.sqrt(2.0 / math.pi) * (x + 0.044715 * (x * x * x))))


def _sigmoid(x):
    return 0.5 * (1.0 + jnp.tanh(0.5 * x))


MOD_SH1, MOD_SC1, MOD_G1, MOD_SH2, MOD_SC2, MOD_G2 = range(6)


def _mod_spec(mod, layer, chunk):
    _, rows, width = mod.shape
    return pl.BlockSpec((1, rows, width // 6), lambda *_: (layer, 0, chunk))


def _mod_row(ref, ctx_row):
    row = pl.program_id(0) if ctx_row is None else ctx_row
    return ref[0, pl.ds(row, 1), :]


def _modulate(x, sc_ref, sh_ref, ctx_row):
    return (x * (1.0 + _mod_row(sc_ref, ctx_row)) + _mod_row(sh_ref, ctx_row)).astype(BF16)


CAST_BLOCK_BYTES = 8 * 1024 * 1024


def _cast_kernel(w_ref, o_ref):
    o_ref[...] = w_ref[0].astype(BF16)


def _to_bf16(w, layer):
    _, rows, cols = w.shape
    tr = rows
    while tr * cols * 4 > CAST_BLOCK_BYTES and tr % 2 == 0 and tr // 2 >= 2 * SUBLANES:
        tr //= 2
    return pl.pallas_call(
        _cast_kernel,
        grid=(rows // tr,),
        in_specs=[pl.BlockSpec((1, tr, cols), lambda i: (layer, i, 0))],
        out_specs=pl.BlockSpec((tr, cols), lambda i: (i, 0)),
        out_shape=jax.ShapeDtypeStruct((rows, cols), BF16),
        compiler_params=_params(("arbitrary",), 2 * tr * cols * 6),
        name="weight_cast",
    )(w)


CAST_ROWS = 2 * SUBLANES


def _hosted_call(body, *, grid, in_specs, out_specs, out_shape, args, casts=(), **kwargs):
    steps = math.prod(grid)

    def flat(*ids):
        step = 0
        for extent, i in zip(grid, ids):
            step = step * extent + i
        return step

    cast_in, cast_out, cast_shape = [], [], []
    for w, layer in casts:
        _, rows, cols = w.shape
        rps = max(CAST_ROWS, rows // steps)
        used = rows // rps
        assert rows % rps == 0 and used <= steps
        cast_in.append(pl.BlockSpec(
            (1, rps, cols), lambda *ids, used=used, layer=layer: (layer, jnp.minimum(flat(*ids), used - 1), 0)))
        cast_out.append(pl.BlockSpec((rps, cols), lambda *ids, used=used: (jnp.minimum(flat(*ids), used - 1), 0)))
        cast_shape.append(jax.ShapeDtypeStruct((rows, cols), BF16))
    n_in, n_out, n_cast = len(in_specs), len(out_specs), len(casts)

    def kernel(*refs):
        ins = refs[:n_in]
        w_ins = refs[n_in:n_in + n_cast]
        outs = refs[n_in + n_cast:n_in + n_cast + n_out]
        w_outs = refs[n_in + n_cast + n_out:n_in + 2 * n_cast + n_out]
        for w_in, w_out in zip(w_ins, w_outs):
            w_out[...] = w_in[0].astype(BF16)
        body(*ins, *outs, *refs[n_in + 2 * n_cast + n_out:])

    return pl.pallas_call(
        kernel, grid=grid, in_specs=list(in_specs) + cast_in, out_specs=list(out_specs) + cast_out,
        out_shape=list(out_shape) + cast_shape, **kwargs,
    )(*args, *[w for w, _ in casts])


def _mod_kernel(cc_ref, w_ref, b_ref, o_ref):
    a = cc_ref[...]
    a = (a * _sigmoid(a)).astype(BF16)
    o_ref[0] = _dot(a, w_ref[0].astype(BF16)) + b_ref[0]


def _modulation(cc, mod_w, mod_b):
    depth, d, n = mod_w.shape
    tn = 1024
    rows = cc.shape[0]
    return pl.pallas_call(
        _mod_kernel,
        grid=(depth, n // tn),
        in_specs=[pl.BlockSpec((rows, d), lambda l, j: (0, 0)),
                  pl.BlockSpec((1, d, tn), lambda l, j: (l, 0, j)),
                  pl.BlockSpec((1, 1, tn), lambda l, j: (l, 0, j))],
        out_specs=pl.BlockSpec((1, rows, tn), lambda l, j: (l, 0, j)),
        out_shape=jax.ShapeDtypeStruct((depth, rows, n), F32),
        compiler_params=_params(("arbitrary", "arbitrary"), 2 * d * tn * 4 + d * tn * 2),
        name="modulation",
    )(cc, mod_w, mod_b.reshape(depth, 1, n))


def _rope(x, cos, sin_signed, heads):
    lane = lax.broadcasted_iota(jnp.int32, (x.shape[0], ATT_HEAD_DIM), 1)
    first = (lane % (2 * ROPE_FREQS)) < ROPE_FREQS
    out = []
    for h in range(heads):
        xs = x[:, h * ATT_HEAD_DIM:(h + 1) * ATT_HEAD_DIM]
        swapped = jnp.where(first, pltpu.roll(xs, ATT_HEAD_DIM - ROPE_FREQS, 1), pltpu.roll(xs, ROPE_FREQS, 1))
        out.append(xs * cos + swapped * sin_signed)
    return out


def _inproj0_kernel(*refs, rope, ctx_row):
    if rope:
        (x_ref, sc_ref, sh_ref, cos_ref, sin_ref, wu_ref, wg_ref, wq_ref, wk_ref, wv_ref,
         u_ref, g_ref, q_ref, k_ref, v_ref) = refs
    else:
        (x_ref, sc_ref, sh_ref, wu_ref, wg_ref, wq_ref, wk_ref, wv_ref,
         u_ref, g_ref, q_ref, k_ref, v_ref) = refs
    h = _modulate(x_ref[0], sc_ref, sh_ref, ctx_row)
    u = _dot(h, wu_ref[...])
    g = _dot(h, wg_ref[...])
    for hd in range(LRU_HEADS):
        u_ref[0, hd] = u[:, hd * LRU_BLOCK:(hd + 1) * LRU_BLOCK].astype(BF16)
        g_ref[0, hd] = g[:, hd * LRU_BLOCK:(hd + 1) * LRU_BLOCK].astype(BF16)
    q = _dot(h, wq_ref[...]) * (ATT_HEAD_DIM ** -0.5 * LOG2E)
    k = _dot(h, wk_ref[...])
    if rope:
        cos = cos_ref[...]
        sin = sin_ref[...]
        for hd, piece in enumerate(_rope(q, cos, sin, ATT_Q_HEADS)):
            q_ref[0, :, hd * ATT_HEAD_DIM:(hd + 1) * ATT_HEAD_DIM] = piece.astype(BF16)
        for hd, piece in enumerate(_rope(k, cos, sin, ATT_KV_HEADS)):
            k_ref[0, :, hd * ATT_HEAD_DIM:(hd + 1) * ATT_HEAD_DIM] = piece.astype(BF16)
    else:
        q_ref[0] = q.astype(BF16)
        k_ref[0] = k.astype(BF16)
    v_ref[0] = _dot(h, wv_ref[...]).astype(BF16)


def _inproj0(x, mod, layer, ctx_row, w_in, rope_tabs, tm):
    bsz, n, d = x.shape
    rope = rope_tabs is not None
    tok = lambda b, i: (b, i, 0)
    in_specs = [pl.BlockSpec((1, tm, d), tok), _mod_spec(mod, layer, MOD_SC1), _mod_spec(mod, layer, MOD_SH1)]
    args = [x, mod, mod]
    if rope:
        in_specs += [pl.BlockSpec((tm, ATT_HEAD_DIM), lambda b, i: (i, 0))] * 2
        args += list(rope_tabs)
    kv_at = (2 * LRU_WIDTH + ATT_Q_WIDTH) // ATT_KV_WIDTH
    in_specs += [_resident((d, LRU_WIDTH), lambda b, i: (0, 0)), _resident((d, LRU_WIDTH), lambda b, i: (0, 1)),
                 _resident((d, ATT_Q_WIDTH), lambda b, i: (0, 2)),
                 _resident((d, ATT_KV_WIDTH), lambda b, i: (0, kv_at)),
                 _resident((d, ATT_KV_WIDTH), lambda b, i: (0, kv_at + 1))]
    args += [w_in] * 5
    head_major = jax.ShapeDtypeStruct((bsz, LRU_HEADS, n, LRU_BLOCK), BF16)
    head_spec = pl.BlockSpec((1, LRU_HEADS, tm, LRU_BLOCK), lambda b, i: (b, 0, i, 0))
    nbytes = 2 * tm * d * 4 + 2 * w_in.size + 2 * 2 * tm * w_in.shape[1] + tm * w_in.shape[1] * 4
    return pl.pallas_call(
        functools.partial(_inproj0_kernel, rope=rope, ctx_row=ctx_row),
        grid=(bsz, n // tm),
        in_specs=in_specs,
        out_specs=[head_spec, head_spec,
                   pl.BlockSpec((1, tm, ATT_Q_WIDTH), tok),
                   pl.BlockSpec((1, tm, ATT_KV_WIDTH), tok),
                   pl.BlockSpec((1, tm, ATT_KV_WIDTH), tok)],
        out_shape=[head_major, head_major,
                   jax.ShapeDtypeStruct((bsz, n, ATT_Q_WIDTH), BF16),
                   jax.ShapeDtypeStruct((bsz, n, ATT_KV_WIDTH), BF16),
                   jax.ShapeDtypeStruct((bsz, n, ATT_KV_WIDTH), BF16)],
        compiler_params=_params(("arbitrary", "arbitrary"), nbytes),
        name="inproj0_rope" if rope else "inproj0_ctx",
    )(*args)


LRU_PAD = SUBLANES
LRU_TILE = 256


def _scan8(a, b, row, reverse):
    for k in (1, 2, 4):
        if reverse:
            keep = row < SUBLANES - k
            shift = SUBLANES - k
        else:
            keep = row >= k
            shift = k
        a_sh = jnp.where(keep, pltpu.roll(a, shift, 0), 1.0)
        b_sh = jnp.where(keep, pltpu.roll(b, shift, 0), 0.0)
        b = a * b_sh + b
        a = a * a_sh
    return a, b


def _scan8_rows(a, b, row, reverse):
    out_a, out_b = [], []
    for g in range(a.shape[0] // SUBLANES):
        sl = slice(g * SUBLANES, (g + 1) * SUBLANES)
        ag, bg = _scan8(a[sl], b[sl], row, reverse)
        out_a.append(ag)
        out_b.append(bg)
    return jnp.concatenate(out_a, axis=0), jnp.concatenate(out_b, axis=0)


def _lru_kernel(ul_ref, gl_ref, uc_ref, gc_ref, cw_ref, cb_ref, wg_ref, bg_ref, lam_ref,
                yl_ref, yc_ref, upad, af_s, bf_s, ar_s, br_s, cin_s, *, n_lat, n_ctx):
    cw = cw_ref[...]
    cb = cb_ref[...]
    wg = wg_ref[0]
    bg = bg_ref[0]
    lam = lam_ref[0]
    neg = -lam
    softplus = jnp.maximum(neg, 0.0) + jnp.log1p(jnp.exp(-jnp.abs(neg)))
    half_rate = (-0.5 * LRU_C) * softplus
    row = lax.broadcasted_iota(jnp.int32, (SUBLANES, LRU_BLOCK), 0)
    chains = ((af_s, bf_s, False), (ar_s, br_s, True))

    def coefficients(src_ref, n_rows, row_off):
        upad[pl.ds(0, LRU_PAD), :] = jnp.zeros((LRU_PAD, LRU_BLOCK), F32)
        upad[pl.ds(LRU_PAD, n_rows), :] = src_ref[0, 0].astype(F32)
        upad[pl.ds(LRU_PAD + n_rows, LRU_PAD), :] = jnp.zeros((LRU_PAD, LRU_BLOCK), F32)

        def tile(i, carry):
            t0 = pl.multiple_of(i * LRU_TILE, LRU_TILE)
            xp = upad[pl.ds(t0, LRU_TILE + 2 * LRU_PAD), :]
            conv = cb
            for k in range(LRU_CONV):
                o = LRU_PAD - LRU_CONV_LEFT + k
                conv = conv + cw[k:k + 1, :] * xp[o:o + LRU_TILE, :]
            z = _dot(conv.astype(BF16), wg) + bg
            half_conv = 0.5 * conv
            for d, (a_ref, b_ref, reverse) in enumerate(chains):
                t_a = jnp.tanh(z[:, (2 * d) * LRU_BLOCK:(2 * d + 1) * LRU_BLOCK])
                t_x = jnp.tanh(z[:, (2 * d + 1) * LRU_BLOCK:(2 * d + 2) * LRU_BLOCK])
                rate = half_rate[:, d * LRU_BLOCK:(d + 1) * LRU_BLOCK]
                a = jnp.exp(rate + rate * t_a)
                gated = half_conv + half_conv * t_x
                b = jnp.sqrt(1.0 - a * a) * gated
                a, b = _scan8_rows(a, b, row, reverse)
                a_ref[pl.ds(row_off + t0, LRU_TILE), :] = a
                b_ref[pl.ds(row_off + t0, LRU_TILE), :] = b
            return carry

        lax.fori_loop(0, n_rows // LRU_TILE, tile, 0)

    coefficients(uc_ref, n_ctx, 0)
    coefficients(ul_ref, n_lat, n_ctx)

    groups_ctx = n_ctx // SUBLANES
    groups = (n_ctx + n_lat) // SUBLANES
    fa = af_s[pl.ds(SUBLANES - 1, groups, stride=SUBLANES), :]
    fb = bf_s[pl.ds(SUBLANES - 1, groups, stride=SUBLANES), :]
    ra = ar_s[pl.ds(0, groups, stride=SUBLANES), :]
    rb = br_s[pl.ds(0, groups, stride=SUBLANES), :]
    zero = jnp.zeros((SUBLANES, LRU_BLOCK), F32)
    carry = zero
    for v in range(groups // SUBLANES):
        sl = slice(v * SUBLANES, (v + 1) * SUBLANES)
        a, b = _scan8(fa[sl], fb[sl], row, False)
        incl = b + a * carry
        cin_s[0, sl, :] = jnp.where(row == 0, carry, pltpu.roll(incl, 1, 0))
        carry = jnp.broadcast_to(incl[SUBLANES - 1:SUBLANES, :], incl.shape)
    carry = zero
    order = list(range(groups_ctx // SUBLANES - 1, -1, -1)) + list(range(groups // SUBLANES - 1,
                                                                        groups_ctx // SUBLANES - 1, -1))
    for v in order:
        sl = slice(v * SUBLANES, (v + 1) * SUBLANES)
        a, b = _scan8(ra[sl], rb[sl], row, True)
        incl = b + a * carry
        cin_s[1, sl, :] = jnp.where(row == SUBLANES - 1, carry, pltpu.roll(incl, SUBLANES - 1, 0))
        carry = jnp.broadcast_to(incl[0:1, :], incl.shape)

    def emit(g_ref, y_ref, n_rows, row_off):
        tile_groups = LRU_TILE // SUBLANES

        def tile(i, carry):
            t0 = pl.multiple_of(i * LRU_TILE, LRU_TILE)
            g0 = pl.multiple_of(row_off // SUBLANES + i * tile_groups, tile_groups)
            rows = pl.ds(row_off + t0, LRU_TILE)
            a_f, b_f, a_r, b_r = af_s[rows, :], bf_s[rows, :], ar_s[rows, :], br_s[rows, :]
            pieces = []
            for g in range(tile_groups):
                sl = slice(g * SUBLANES, (g + 1) * SUBLANES)
                c_f = jnp.broadcast_to(cin_s[0, pl.ds(g0 + g, 1), :], (SUBLANES, LRU_BLOCK))
                c_r = jnp.broadcast_to(cin_s[1, pl.ds(g0 + g, 1), :], (SUBLANES, LRU_BLOCK))
                pieces.append((b_f[sl] + a_f[sl] * c_f) + (b_r[sl] + a_r[sl] * c_r))
            h = jnp.concatenate(pieces, axis=0)
            gate = g_ref[0, 0, pl.ds(t0, LRU_TILE), :].astype(F32)
            y_ref[0, 0, pl.ds(t0, LRU_TILE), :] = (h * _gelu_tanh(gate)).astype(BF16)
            return carry

        lax.fori_loop(0, n_rows // LRU_TILE, tile, 0)

    emit(gc_ref, yc_ref, n_ctx, 0)
    emit(gl_ref, yl_ref, n_lat, n_ctx)


def _rglru(u_lat, g_lat, u_ctx, g_ctx, conv_w, conv_b, w_gate, b_gate, lam, casts=()):
    bsz, heads, n_lat, blk = u_lat.shape
    n_ctx = u_ctx.shape[2]
    total = n_lat + n_ctx
    assert n_ctx % (SUBLANES * SUBLANES) == 0 and n_lat % LRU_TILE == 0 and n_ctx % LRU_TILE == 0
    seq = lambda n: pl.BlockSpec((1, 1, n, blk), lambda b, h: (b, h, 0, 0))
    per_head = lambda shape: pl.BlockSpec((1,) + shape, lambda b, h: (h, 0, 0))
    nbytes =4 * total * blk * 4 + (n_lat + 2 * LRU_PAD) * blk * 4 + 2 * 3 * 2 * total * blk * 2
    return _hosted_call(
        functools.partial(_lru_kernel, n_lat=n_lat, n_ctx=n_ctx),
        casts=casts,
        args=(u_lat, g_lat, u_ctx, g_ctx, conv_w, conv_b, w_gate, b_gate, lam),
        grid=(bsz, heads),
        in_specs=[seq(n_lat), seq(n_lat), seq(n_ctx), seq(n_ctx),
                  pl.BlockSpec((LRU_CONV, blk), lambda b, h: (0, h)),
                  pl.BlockSpec((1, blk), lambda b, h: (0, h)),
                  per_head((blk, 4 * blk)), per_head((1, 4 * blk)), per_head((1, 2 * blk))],
        out_specs=[seq(n_lat), seq(n_ctx)],
        out_shape=[jax.ShapeDtypeStruct(u_lat.shape, BF16), jax.ShapeDtypeStruct(u_ctx.shape, BF16)],
        scratch_shapes=[pltpu.VMEM((n_lat + 2 * LRU_PAD, blk), F32)]
                       + [pltpu.VMEM((total, blk), F32)] * 4
                       + [pltpu.VMEM((2, total // SUBLANES, blk), F32)],
        compiler_params=_params(("arbitrary", "arbitrary"), nbytes),
        name="rglru",
    )


ATT_TQ = 256
ATT_BAND = 3 * ATT_BLOCK


def _dot_tn(a, b):
    return lax.dot_general(a, b, (((0,), (0,)), ((), ())), preferred_element_type=F32)


def _softmax_pv(parts, sink_row):
    dh = ATT_HEAD_DIM
    m = sink_row
    for s, _ in parts:
        m = jnp.maximum(m, jnp.max(s, axis=0, keepdims=True))
    acc = None
    for idx, (s, v) in enumerate(parts):
        p = jnp.exp2(s - m).astype(BF16)
        v_ext = jnp.concatenate([v, jnp.ones(v.shape, BF16)], axis=1)
        if idx == len(parts) - 1:
            pad = 2 * SUBLANES
            first = lax.broadcasted_iota(jnp.int32, (pad, m.shape[1]), 0) == 0
            p_sink = jnp.where(first, jnp.exp2(sink_row - m), 0.0).astype(BF16)
            p = jnp.concatenate([p, p_sink], axis=0)
            v_sink = jnp.concatenate([jnp.zeros((pad, dh), BF16), jnp.ones((pad, dh), BF16)], axis=1)
            v_ext = jnp.concatenate([v_ext, v_sink], axis=0)
        pv = _dot_tn(p, v_ext)
        acc = pv if acc is None else acc + pv
    return acc[:, :dh] / acc[:, dh:]


def _sink_rows(sink, reps):
    return jnp.repeat(sink.reshape(ATT_KV_HEADS, 1, ATT_GROUP), reps, axis=2)


def _attn_kernel(q_ref, k_ref, v_ref, kc_ref, vc_ref, sink_ref, o_ref, *, seq):
    tile = pl.program_id(1)
    blocks = ATT_TQ // ATT_BLOCK
    qcol = lax.broadcasted_iota(jnp.int32, (1, ATT_GROUP * ATT_BLOCK), 1) % ATT_BLOCK
    krow = lax.broadcasted_iota(jnp.int32, (ATT_BAND, 1), 0)
    for i in range(blocks):
        q0 = (tile * blocks + i) * ATT_BLOCK
        start = pl.multiple_of(jnp.clip(q0 - ATT_BLOCK, 0, seq - ATT_BAND), ATT_BLOCK)
        valid = jnp.abs((q0 + qcol) - (start + krow)) <= WINDOW
        for g in range(ATT_KV_HEADS):
            heads = [q_ref[0, i * ATT_BLOCK:(i + 1) * ATT_BLOCK,
                           (g * ATT_GROUP + r) * ATT_HEAD_DIM:(g * ATT_GROUP + r + 1) * ATT_HEAD_DIM]
                     for r in range(ATT_GROUP)]
            qs = jnp.concatenate(heads, axis=0)
            kv = slice(g * ATT_HEAD_DIM, (g + 1) * ATT_HEAD_DIM)
            s_loc = jnp.where(valid, _dot_nt(k_ref[0, pl.ds(start, ATT_BAND), kv], qs), NEG_INF)
            s_ctx = _dot_nt(kc_ref[0, :, kv], qs)
            o = _softmax_pv([(s_loc, v_ref[0, pl.ds(start, ATT_BAND), kv]), (s_ctx, vc_ref[0, :, kv])],
                            sink_ref[g] * LOG2E)
            for r in range(ATT_GROUP):
                col = (g * ATT_GROUP + r) * ATT_HEAD_DIM
                o_ref[0, i * ATT_BLOCK:(i + 1) * ATT_BLOCK, col:col + ATT_HEAD_DIM] = (
                    o[r * ATT_BLOCK:(r + 1) * ATT_BLOCK].astype(BF16))


def _window_attention(q, k, v, kc, vc, sink, casts=()):
    bsz, n, _ = q.shape
    sink_rows = _sink_rows(sink, ATT_BLOCK)
    lc = kc.shape[1]
    whole = lambda rows: pl.BlockSpec((1, rows, ATT_KV_WIDTH), lambda b, i: (b, 0, 0))
    nbytes = 2 * 2 * (2 * ATT_TQ * ATT_Q_WIDTH + 2 * n * ATT_KV_WIDTH + 2 * lc * ATT_KV_WIDTH) + 8 * 1024 * 1024
    return _hosted_call(
        functools.partial(_attn_kernel, seq=n),
        casts=casts,
        args=(q, k, v, kc, vc, sink_rows),
        grid=(bsz, n // ATT_TQ),
        in_specs=[pl.BlockSpec((1, ATT_TQ, ATT_Q_WIDTH), lambda b, i: (b, i, 0)),
                  whole(n), whole(n), whole(lc), whole(lc),
                  pl.BlockSpec(sink_rows.shape, lambda b, i: (0, 0, 0))],
        out_specs=[pl.BlockSpec((1, ATT_TQ, ATT_Q_WIDTH), lambda b, i: (b, i, 0))],
        out_shape=[jax.ShapeDtypeStruct(q.shape, BF16)],
        compiler_params=_params(("arbitrary", "arbitrary"), nbytes),
        name="window_attention",
    )


def _ctx_attn_kernel(q_ref, kc_ref, vc_ref, sink_ref, o_ref):
    lc = q_ref.shape[1]
    for g in range(ATT_KV_HEADS):
        heads = [q_ref[0, :, (g * ATT_GROUP + r) * ATT_HEAD_DIM:(g * ATT_GROUP + r + 1) * ATT_HEAD_DIM]
                 for r in range(ATT_GROUP)]
        qs = jnp.concatenate(heads, axis=0)
        kv = slice(g * ATT_HEAD_DIM, (g + 1) * ATT_HEAD_DIM)
        o = _softmax_pv([(_dot_nt(kc_ref[0, :, kv], qs), vc_ref[0, :, kv])], sink_ref[g] * LOG2E)
        for r in range(ATT_GROUP):
            col = (g * ATT_GROUP + r) * ATT_HEAD_DIM
            o_ref[0, :, col:col + ATT_HEAD_DIM] = o[r * lc:(r + 1) * lc].astype(BF16)


def _context_attention(qc, kc, vc, sink):
    bsz, lc, _ = qc.shape
    sink_rows = _sink_rows(sink, lc)
    kv_spec = pl.BlockSpec((1, lc, ATT_KV_WIDTH), lambda b: (b, 0, 0))
    q_spec = pl.BlockSpec((1, lc, ATT_Q_WIDTH), lambda b: (b, 0, 0))
    return pl.pallas_call(
        _ctx_attn_kernel,
        grid=(bsz,),
        in_specs=[q_spec, kv_spec, kv_spec, pl.BlockSpec(sink_rows.shape, lambda b: (0, 0, 0))],
        out_specs=q_spec,
        out_shape=jax.ShapeDtypeStruct(qc.shape, BF16),
        compiler_params=_params(("arbitrary",), 8 * 1024 * 1024),
        name="context_attention",
    )(qc, kc, vc, sink_rows)


def _outproj0_kernel(ya_ref, yb_ref, x_ref, gate_ref, w_ref, lg_ref, lb_ref, o_ref, *, ctx_row):
    y = jnp.concatenate([ya_ref[0, h] for h in range(LRU_HEADS)] + [yb_ref[0]], axis=1)
    out = _dot(y, w_ref[...])
    o_ref[0] = _layer_norm(ALPHA * x_ref[0] + _mod_row(gate_ref, ctx_row) * out, lg_ref[...], lb_ref[...])


def _outproj0(ya, yb, x, mod, layer, ctx_row, w_out, ln_g, ln_b, tm):
    bsz, n, d = x.shape
    tok = lambda b, i: (b, i, 0)
    const = lambda b, i: (0, 0)
    nbytes = 2 * 2 * tm * d * 4 + w_out.size * 2 + 2 * 2 * tm * d * 2 + 2 * tm * d * 4
    return pl.pallas_call(
        functools.partial(_outproj0_kernel, ctx_row=ctx_row),
        grid=(bsz, n // tm),
        in_specs=[pl.BlockSpec((1, LRU_HEADS, tm, LRU_BLOCK), lambda b, i: (b, 0, i, 0)),
                  pl.BlockSpec((1, tm, ATT_Q_WIDTH), tok),
                  pl.BlockSpec((1, tm, d), tok),
                  _mod_spec(mod, layer, MOD_G1),
                  _resident(w_out.shape, const),
                  pl.BlockSpec((1, d), const), pl.BlockSpec((1, d), const)],
        out_specs=pl.BlockSpec((1, tm, d), tok),
        out_shape=jax.ShapeDtypeStruct(x.shape, F32),
        compiler_params=_params(("arbitrary", "arbitrary"), nbytes),
        name="outproj0_ln",
    )(ya, yb, x, mod, w_out, ln_g, ln_b)


def _ffn_kernel(x_ref, sc_ref, sh_ref, gate_ref, w1_ref, b1_ref, w2_ref, b2_ref, lg_ref, lb_ref,
                o_ref, h_s, acc_s, *, ctx_row):
    j = pl.program_id(2)

    @pl.when(j == 0)
    def _():
        h_s[...] = _modulate(x_ref[0], sc_ref, sh_ref, ctx_row)
        acc_s[...] = jnp.zeros(acc_s.shape, F32)

    a = jnp.maximum(_dot(h_s[...], w1_ref[...]) + b1_ref[...], 0.0)
    acc_s[...] += _dot((a * a).astype(BF16), w2_ref[...])

    @pl.when(j == pl.num_programs(2) - 1)
    def _():
        f = acc_s[...] + b2_ref[...]
        o_ref[0] = _layer_norm(ALPHA * x_ref[0] + _mod_row(gate_ref, ctx_row) * f, lg_ref[...], lb_ref[...])


def _ffn(x, mod, layer, ctx_row, w1, b1, w2, b2, ln_g, ln_b, tm, th, casts=()):
    bsz, n, d = x.shape
    hidden = w1.shape[1]
    tok = lambda b, i, j: (b, i, 0)
    const = lambda b, i, j: (0, 0)
    nbytes = 2 * 2 * tm * d * 4 + 2 * 2 * 2 * d * th * 2 + tm * d * (2 + 4) + 2 * tm * th * 4 + tm * d * 4
    return _hosted_call(
        functools.partial(_ffn_kernel, ctx_row=ctx_row),
        casts=casts,
        args=(x, mod, mod, mod, w1, b1, w2, b2, ln_g, ln_b),
        grid=(bsz, n // tm, hidden // th),
        in_specs=[pl.BlockSpec((1, tm, d), tok),
                  _mod_spec(mod, layer, MOD_SC2), _mod_spec(mod, layer, MOD_SH2), _mod_spec(mod, layer, MOD_G2),
                  pl.BlockSpec((d, th), lambda b, i, j: (0, j)),
                  pl.BlockSpec((1, th), lambda b, i, j: (0, j)),
                  pl.BlockSpec((th, d), lambda b, i, j: (j, 0)),
                  pl.BlockSpec((1, d), const), pl.BlockSpec((1, d), const), pl.BlockSpec((1, d), const)],
        out_specs=[pl.BlockSpec((1, tm, d), tok)],
        out_shape=[jax.ShapeDtypeStruct(x.shape, F32)],
        scratch_shapes=[pltpu.VMEM((tm, d), BF16), pltpu.VMEM((tm, d), F32)],
        compiler_params=_params(("arbitrary", "arbitrary", "arbitrary"), nbytes),
        name="ffn_ln",
    )


SC_PARTS = SC_WIDTH // S5_WIDTH


def _inproj1_kernel(*refs, full, ctx_row):
    x_ref, sc_ref, sh_ref, wu_ref = refs[:4]
    h = _modulate(x_ref[0], sc_ref, sh_ref, ctx_row)
    if not full:
        u_ref, = refs[4:]
        u_ref[0] = _dot(h, wu_ref[...]).astype(BF16)
        return
    wb_refs = refs[4:4 + SC_PARTS]
    wc_refs = refs[4 + SC_PARTS:4 + 2 * SC_PARTS]
    wx_refs = refs[4 + 2 * SC_PARTS:4 + 3 * SC_PARTS]
    u_ref, gb_ref, p_ref = refs[4 + 3 * SC_PARTS:]
    u_ref[0] = _dot(h, wu_ref[...]).astype(BF16)
    for c in range(SC_PARTS):
        cols = slice(c * S5_WIDTH, (c + 1) * S5_WIDTH)
        gb_ref[0, :, cols] = _dot(h, wb_refs[c][...]).astype(BF16)
        p_ref[0, :, cols] = (_dot(h, wc_refs[c][...]) * _dot(h, wx_refs[c][...])).astype(BF16)


def _inproj1(x, mod, layer, ctx_row, w_in, tm, full):
    bsz, n, d = x.shape
    tok = lambda b, i: (b, i, 0)
    n_blocks = 1 + 3 * SC_PARTS if full else 1
    widths = [S5_WIDTH] + ([SC_WIDTH, SC_WIDTH] if full else [])
    ncols = n_blocks * S5_WIDTH
    nbytes = 2 * tm * d * 4 + 2 * d * ncols + 2 * 2 * tm * ncols + tm * ncols * 4
    outs = pl.pallas_call(
        functools.partial(_inproj1_kernel, full=full, ctx_row=ctx_row),
        grid=(bsz, n // tm),
        in_specs=[pl.BlockSpec((1, tm, d), tok), _mod_spec(mod, layer, MOD_SC1), _mod_spec(mod, layer, MOD_SH1)]
                 + [_resident((d, S5_WIDTH), lambda b, i, c=c: (0, c)) for c in range(n_blocks)],
        out_specs=[pl.BlockSpec((1, tm, w), tok) for w in widths],
        out_shape=[jax.ShapeDtypeStruct((bsz, n, w), BF16) for w in widths],
        compiler_params=_params(("arbitrary", "arbitrary"), nbytes),
        name="inproj1" if full else "inproj1_ctx",
    )(x, mod, mod, *([w_in] * n_blocks))
    return outs


def _s5_power_table(par_ref):
    rows = 2 * S5_POW_ROWS
    r = lax.broadcasted_iota(jnp.int32, (rows, S5_SLAB_STATE), 0)
    first = r < S5_POW_ROWS
    lag = (r % S5_POW_ROWS).astype(F32)
    pick = lambda i: jnp.where(first, par_ref[0, 0, i:i + 1, :], par_ref[0, 1, i:i + 1, :])
    dt = jnp.exp(pick(0))
    mag = jnp.exp(lag * dt * pick(1))
    ang = lag * dt * pick(2)
    return mag * jnp.cos(ang), mag * jnp.sin(ang)


def _s5_input_matrix(par_ref, bt_ref, d, pw_re, pw_im):
    a_re = par_ref[0, d, 1:2, :]
    a_im = par_ref[0, d, 2:3, :]
    ab_re = pw_re[d * S5_POW_ROWS + 1:d * S5_POW_ROWS + 2, :]
    ab_im = pw_im[d * S5_POW_ROWS + 1:d * S5_POW_ROWS + 2, :]
    den = a_re * a_re + a_im * a_im
    k_re = ((ab_re - 1.0) * a_re + ab_im * a_im) / den
    k_im = (ab_im * a_re - (ab_re - 1.0) * a_im) / den
    b_re = bt_ref[0, d, 0]
    b_im = bt_ref[0, d, 1]
    return k_re * b_re - k_im * b_im, k_re * b_im + k_im * b_re


def _cmul_row(x_re, x_im, p_re, p_im):
    return x_re * p_re - x_im * p_im, x_re * p_im + x_im * p_re


def _s5_state_prep_kernel(par_ref, bt_ref, wst_ref, a16_ref):
    pw_re, pw_im = _s5_power_table(par_ref)
    for d in range(2):
        bb_re, bb_im = _s5_input_matrix(par_ref, bt_ref, d, pw_re, pw_im)
        base = d * S5_POW_ROWS
        for lag in range(S5_CHUNK):
            e_re, e_im = _cmul_row(bb_re, bb_im, pw_re[base + lag:base + lag + 1, :],
                                   pw_im[base + lag:base + lag + 1, :])
            s = S5_CHUNK - 1 - lag if d == 0 else lag
            col = 2 * d * S5_SLAB_STATE
            wst_ref[0, s * LANES:(s + 1) * LANES, col:col + S5_SLAB_STATE] = e_re.astype(BF16)
            wst_ref[0, s * LANES:(s + 1) * LANES, col + S5_SLAB_STATE:col + 2 * S5_SLAB_STATE] = e_im.astype(BF16)
        row16 = base + S5_CHUNK
        a16_ref[0, :, 2 * d * S5_SLAB_STATE:(2 * d + 1) * S5_SLAB_STATE] = jnp.broadcast_to(
            pw_re[row16:row16 + 1, :], (SUBLANES, S5_SLAB_STATE))
        a16_ref[0, :, (2 * d + 1) * S5_SLAB_STATE:(2 * d + 2) * S5_SLAB_STATE] = jnp.broadcast_to(
            pw_im[row16:row16 + 1, :], (SUBLANES, S5_SLAB_STATE))


def _s5_output_prep_kernel(par_ref, bt_ref, c_ref, ct_ref, wbig_ref):
    pw_re, pw_im = _s5_power_table(par_ref)
    kern = []
    for d in range(2):
        bb_re, bb_im = _s5_input_matrix(par_ref, bt_ref, d, pw_re, pw_im)
        c_re = c_ref[0, d, 0].astype(BF16)
        c_im = c_ref[0, d, 1].astype(BF16)
        base = d * S5_POW_ROWS
        per_lag = []
        for lag in range(S5_CHUNK):
            e_re, e_im = _cmul_row(bb_re, bb_im, pw_re[base + lag:base + lag + 1, :],
                                   pw_im[base + lag:base + lag + 1, :])
            per_lag.append(_dot_nt(e_re.astype(BF16), c_re) - _dot_nt(e_im.astype(BF16), c_im))
        kern.append(per_lag)
    for s in range(S5_CHUNK):
        for t in range(S5_CHUNK):
            if s < t:
                blk = kern[0][t - s]
            elif s > t:
                blk = kern[1][s - t]
            else:
                blk = kern[0][0] + kern[1][0]
            wbig_ref[0, s * LANES:(s + 1) * LANES, t * LANES:(t + 1) * LANES] = blk.astype(BF16)
    pt_re = pw_re.T
    pt_im = pw_im.T
    for d in range(2):
        ct_re = ct_ref[0, d, 0]
        ct_im = ct_ref[0, d, 1]
        for t in range(S5_CHUNK):
            lag = t + 1 if d == 0 else S5_CHUNK - t
            col = d * S5_POW_ROWS + lag
            p_re = pt_re[:, col:col + 1]
            p_im = pt_im[:, col:col + 1]
            g_re = ct_re * p_re - ct_im * p_im
            g_im = ct_re * p_im + ct_im * p_re
            r0 = S5_XK + 2 * d * S5_SLAB_STATE
            wbig_ref[0, r0:r0 + S5_SLAB_STATE, t * LANES:(t + 1) * LANES] = g_re.astype(BF16)
            wbig_ref[0, r0 + S5_SLAB_STATE:r0 + 2 * S5_SLAB_STATE, t * LANES:(t + 1) * LANES] = (-g_im).astype(BF16)


def _s5_operators(log_dt, a_re, a_im, b_re, b_im, c_re, c_im):
    gs = S5_SLAB_GROUPS
    eye = jnp.eye(gs, dtype=F32)

    def lanes(v):
        return v.reshape(2, S5_SLABS, gs * S5_STATE)

    par = jnp.stack([lanes(jnp.broadcast_to(log_dt[:, :, None], a_re.shape)), lanes(a_re), lanes(a_im)], axis=2)
    par = jnp.pad(par, ((0, 0), (0, 0), (0, SUBLANES - 3), (0, 0))).transpose(1, 0, 2, 3)

    def embed_bt(b):
        b = b.reshape(2, S5_SLABS, gs, S5_STATE, S5_GROUP)
        e = b.transpose(0, 1, 2, 4, 3)[:, :, :, :, None, :] * eye[None, None, :, None, :, None]
        return e.reshape(2, S5_SLABS, gs * S5_GROUP, gs * S5_STATE).transpose(1, 0, 2, 3)

    def embed_c(c):
        c = c.reshape(2, S5_SLABS, gs, S5_GROUP, S5_STATE)
        e = c[:, :, :, :, None, :] * eye[None, None, :, None, :, None]
        return e.reshape(2, S5_SLABS, gs * S5_GROUP, gs * S5_STATE).transpose(1, 0, 2, 3)

    bt = jnp.stack([embed_bt(b_re), embed_bt(b_im)], axis=2)
    cm = jnp.stack([embed_c(c_re), embed_c(c_im)], axis=2)
    ct = cm.transpose(0, 1, 2, 4, 3)

    slab5 = lambda shape: pl.BlockSpec((1,) + shape, lambda k: (k, 0, 0, 0, 0))
    par_spec = pl.BlockSpec((1, 2, SUBLANES, S5_SLAB_STATE), lambda k: (k, 0, 0, 0))
    bt_spec = slab5((2, 2, LANES, S5_SLAB_STATE))
    wst, a16 = pl.pallas_call(
        _s5_state_prep_kernel,
        grid=(S5_SLABS,),
        in_specs=[par_spec, bt_spec],
        out_specs=[pl.BlockSpec((1, S5_XK, S5_HK), lambda k: (k, 0, 0)),
                   pl.BlockSpec((1, SUBLANES, S5_HK), lambda k: (k, 0, 0))],
        out_shape=[jax.ShapeDtypeStruct((S5_SLABS, S5_XK, S5_HK), BF16),
                   jax.ShapeDtypeStruct((S5_SLABS, SUBLANES, S5_HK), F32)],
        compiler_params=_params(("arbitrary",), 2 * S5_XK * S5_HK * 2 + 8 * 1024 * 1024),
        name="s5_state_operator",
    )(par, bt)
    wbig = pl.pallas_call(
        _s5_output_prep_kernel,
        grid=(S5_SLABS,),
        in_specs=[par_spec, bt_spec, bt_spec, slab5((2, 2, S5_SLAB_STATE, LANES))],
        out_specs=pl.BlockSpec((1, S5_XK + S5_HK, S5_XK), lambda k: (k, 0, 0)),
        out_shape=jax.ShapeDtypeStruct((S5_SLABS, S5_XK + S5_HK, S5_XK), BF16),
        compiler_params=_params(("arbitrary",), 2 * (S5_XK + S5_HK) * S5_XK * 2 + 8 * 1024 * 1024),
        name="s5_output_operator",
    )(par, bt, cm, ct)
    return wst, a16, wbig


def _s5_states_kernel(*refs, n_ctx, n_lat):
    xc_refs = refs[:S5_CHUNK]
    xl_refs = refs[S5_CHUNK:2 * S5_CHUNK]
    wst_ref, a16_ref, h_ref, s_s, h_s = refs[2 * S5_CHUNK:]
    x = jnp.concatenate([jnp.concatenate([r[0] for r in xc_refs], axis=1),
                         jnp.concatenate([r[0] for r in xl_refs], axis=1)], axis=0)
    s_s[...] = _dot(x, wst_ref[0])
    p = S5_SLAB_STATE
    af_re = a16_ref[0, 0:1, 0:p]
    af_im = a16_ref[0, 0:1, p:2 * p]
    ar_re = a16_ref[0, 0:1, 2 * p:3 * p]
    ar_im = a16_ref[0, 0:1, 3 * p:4 * p]
    total = n_ctx + n_lat

    def step(i, carry):
        f_re, f_im, r_re, r_im = carry
        h_s[pl.ds(i, 1), 0:p] = f_re
        h_s[pl.ds(i, 1), p:2 * p] = f_im
        s_re = s_s[pl.ds(i, 1), 0:p]
        s_im = s_s[pl.ds(i, 1), p:2 * p]
        f_re, f_im = af_re * f_re - af_im * f_im + s_re, af_re * f_im + af_im * f_re + s_im
        j = jnp.where(i < n_ctx, n_ctx - 1 - i, total + n_ctx - 1 - i)
        h_s[pl.ds(j, 1), 2 * p:3 * p] = r_re
        h_s[pl.ds(j, 1), 3 * p:4 * p] = r_im
        s_re = s_s[pl.ds(j, 1), 2 * p:3 * p]
        s_im = s_s[pl.ds(j, 1), 3 * p:4 * p]
        r_re, r_im = ar_re * r_re - ar_im * r_im + s_re, ar_re * r_im + ar_im * r_re + s_im
        return f_re, f_im, r_re, r_im

    zero = jnp.zeros((1, p), F32)
    lax.fori_loop(0, total, step, (zero, zero, zero, zero), unroll=4)
    h_ref[0, 0] = h_s[pl.ds(n_ctx, n_lat), :].astype(BF16)


def _s5_states(uc_flat, ul_flat, wst, a16):
    bsz, n_ctx, _ = uc_flat.shape
    n_lat = ul_flat.shape[1]
    piece = lambda rows, s: pl.BlockSpec((1, rows, LANES), lambda k, b, s=s: (b, 0, s * S5_SLABS + k))
    in_specs = ([piece(n_ctx, s) for s in range(S5_CHUNK)] + [piece(n_lat, s) for s in range(S5_CHUNK)]
                + [pl.BlockSpec((1, S5_XK, S5_HK), lambda k, b: (k, 0, 0)),
                   pl.BlockSpec((1, SUBLANES, S5_HK), lambda k, b: (k, 0, 0))])
    total = n_ctx + n_lat
    nbytes = 2 * S5_XK * S5_HK * 2 + 2 * total * S5_XK * 2 + 3 * total * S5_HK * 4 + 2 * n_lat * S5_HK * 2
    return pl.pallas_call(
        functools.partial(_s5_states_kernel, n_ctx=n_ctx, n_lat=n_lat),
        grid=(S5_SLABS, bsz),
        in_specs=in_specs,
        out_specs=pl.BlockSpec((1, 1, n_lat, S5_HK), lambda k, b: (b, k, 0, 0)),
        out_shape=jax.ShapeDtypeStruct((bsz, S5_SLABS, n_lat, S5_HK), BF16),
        scratch_shapes=[pltpu.VMEM((total, S5_HK), F32), pltpu.VMEM((total, S5_HK), F32)],
        compiler_params=_params(("arbitrary", "arbitrary"), nbytes),
        name="s5_states",
    )(*([uc_flat] * S5_CHUNK + [ul_flat] * S5_CHUNK + [wst, a16]))


def _s5_readout_kernel(*refs):
    x_refs = refs[:S5_CHUNK]
    h_ref, w_ref, y_ref = refs[S5_CHUNK:]
    lhs = jnp.concatenate([r[0] for r in x_refs] + [h_ref[0, 0]], axis=1)
    y = _dot(lhs, w_ref[0])
    for t in range(S5_CHUNK):
        y_ref[t, 0] = y[:, t * LANES:(t + 1) * LANES].astype(BF16)


def _s5_readout(ul_flat, h_in, wbig):
    bsz, n_lat, _ = ul_flat.shape
    piece = lambda s: pl.BlockSpec((1, n_lat, LANES), lambda k, b, s=s: (b, 0, s * S5_SLABS + k))
    nbytes = 2 * (S5_XK + S5_HK) * S5_XK * 2 + 2 * n_lat * (S5_XK + S5_HK) * 2 * 2 + n_lat * S5_XK * 4 * 2
    return pl.pallas_call(
        _s5_readout_kernel,
        grid=(S5_SLABS, bsz),
        in_specs=[piece(s) for s in range(S5_CHUNK)]
                 + [pl.BlockSpec((1, 1, n_lat, S5_HK), lambda k, b: (b, k, 0, 0)),
                    pl.BlockSpec((1, S5_XK + S5_HK, S5_XK), lambda k, b: (k, 0, 0))],
        out_specs=pl.BlockSpec((S5_CHUNK, 1, n_lat, LANES), lambda k, b: (0, b, 0, k)),
        out_shape=jax.ShapeDtypeStruct((S5_CHUNK, bsz, n_lat, S5_WIDTH), BF16),
        compiler_params=_params(("arbitrary", "arbitrary"), nbytes),
        name="s5_readout",
    )(*([ul_flat] * S5_CHUNK + [h_in, wbig]))


SC_HALO = 16


def _outproj1_kernel(y_ref, u_ref, gb_ref, p_ref, pprev_ref, pnext_ref, x_ref, gate_ref,
                     dskip_ref, wglu_ref, bglu_ref, cw_ref, cb_ref, w_ref, lg_ref, lb_ref, o_ref, *, ctx_row):
    i = pl.program_id(1)
    tm = x_ref.shape[1]
    yc = y_ref[0].astype(F32) + dskip_ref[...] * u_ref[0].astype(F32)
    z = _gelu_tanh(yc)
    y_c = z * _sigmoid(_dot(z.astype(BF16), wglu_ref[...]) + bglu_ref[...])
    p = p_ref[0].astype(F32)
    row = lax.broadcasted_iota(jnp.int32, (tm, 1), 0)
    prev_row = jnp.where(i > 0, pprev_ref[0, SC_HALO - 1:SC_HALO, :].astype(F32), 0.0)
    next_row = jnp.where(i < pl.num_programs(1) - 1, pnext_ref[0, 0:1, :].astype(F32), 0.0)
    p_dn = jnp.where(row == 0, prev_row, pltpu.roll(p, 1, 0))
    p_up = jnp.where(row == tm - 1, next_row, pltpu.roll(p, tm - 1, 0))
    conv = cb_ref[...] + cw_ref[0:1, :] * p_dn + cw_ref[1:2, :] * p + cw_ref[2:3, :] * p_up
    y_d = gb_ref[0].astype(F32) * conv
    y = jnp.concatenate([y_c.astype(BF16), y_d.astype(BF16)], axis=1)
    out = _dot(y, w_ref[...])
    o_ref[0] = _layer_norm(ALPHA * x_ref[0] + _mod_row(gate_ref, ctx_row) * out, lg_ref[...], lb_ref[...])


def _outproj1(y, u, gb, p, x, mod, layer, ctx_row, d_skip, w_glu, b_glu, conv_w, conv_b, w_out, ln_g, ln_b, tm):
    bsz, n, d = x.shape
    tok = lambda b, i: (b, i, 0)
    const = lambda b, i: (0, 0)
    per = tm // SC_HALO
    last = n // SC_HALO - 1
    nbytes = (2 * 2 * tm * d * 4 + w_out.size * 2 + 2 * 2 * tm * (2 * S5_WIDTH + 2 * SC_WIDTH) * 2
              + 6 * tm * SC_WIDTH * 4 + 2 * tm * d * 4)
    return pl.pallas_call(
        functools.partial(_outproj1_kernel, ctx_row=ctx_row),
        grid=(bsz, n // tm),
        in_specs=[pl.BlockSpec((1, tm, S5_WIDTH), tok), pl.BlockSpec((1, tm, S5_WIDTH), tok),
                  pl.BlockSpec((1, tm, SC_WIDTH), tok), pl.BlockSpec((1, tm, SC_WIDTH), tok),
                  pl.BlockSpec((1, SC_HALO, SC_WIDTH), lambda b, i: (b, jnp.maximum(i * per - 1, 0), 0)),
                  pl.BlockSpec((1, SC_HALO, SC_WIDTH), lambda b, i: (b, jnp.minimum((i + 1) * per, last), 0)),
                  pl.BlockSpec((1, tm, d), tok), _mod_spec(mod, layer, MOD_G1),
                  pl.BlockSpec((1, S5_WIDTH), const), _resident(w_glu.shape, const),
                  pl.BlockSpec((1, S5_WIDTH), const),
                  pl.BlockSpec((SC_CONV, SC_WIDTH), const), pl.BlockSpec((1, SC_WIDTH), const),
                  _resident(w_out.shape, const),
                  pl.BlockSpec((1, d), const), pl.BlockSpec((1, d), const)],
        out_specs=pl.BlockSpec((1, tm, d), tok),
        out_shape=jax.ShapeDtypeStruct(x.shape, F32),
        compiler_params=_params(("arbitrary", "arbitrary"), nbytes),
        name="outproj1_ln",
    )(y, u, gb, p, p, p, x, mod, d_skip, w_glu, b_glu, conv_w, conv_b, w_out, ln_g, ln_b)


def _rope_tables(n):
    t = jnp.arange(n)
    row = (t // GRID_W).astype(F32)
    col = (t % GRID_W).astype(F32)
    inv = ROPE_BASE ** (-jnp.arange(ROPE_FREQS, dtype=F32) / ROPE_FREQS)
    ang_r = row[:, None] * inv[None, :]
    ang_c = col[:, None] * inv[None, :]
    cos = jnp.concatenate([jnp.cos(ang_r), jnp.cos(ang_r), jnp.cos(ang_c), jnp.cos(ang_c)], axis=1)
    sin = jnp.concatenate([-jnp.sin(ang_r), jnp.sin(ang_r), -jnp.sin(ang_c), jnp.sin(ang_c)], axis=1)
    return cos, sin


TM_LATENT = 512
TM_CONTEXT = 256
FFN_TH = 1024


def kernel(x, c, ctx, c_ctx, mod_w, mod_b, ln1_g, ln1_b, ln2_g, ln2_b, ffn_w1, ffn_b1, ffn_w2, ffn_b2,
           ab_w_in, ab_w_out, lru_conv_w, lru_conv_b, lru_w_a, lru_b_a, lru_w_x, lru_b_x, lru_lam, att_sink,
           cd_w_in, cd_w_out, s5_log_dt, s5_a_re, s5_a_im, s5_b_re, s5_b_im, s5_c_re, s5_c_im, s5_d,
           s5_w_glu, s5_b_glu, sc_conv_w, sc_conv_b):
    bsz, n, d = x.shape
    lc = ctx.shape[1]
    assert n % TM_LATENT == 0 and lc % TM_CONTEXT == 0 and n % (GRID_W) == 0

    pad_rows = SUBLANES - (bsz + 1) % SUBLANES if (bsz + 1) % SUBLANES else 0
    cc = jnp.concatenate([c, c_ctx[None, :], jnp.zeros((pad_rows, d), F32)], axis=0)
    mod = _modulation(cc, mod_w, mod_b)
    ctx_row = bsz

    row2 = lambda v: v.reshape(1, -1)
    rope_tabs = _rope_tables(n)

    side = {}

    def weight(stack, name, idx):
        return side.pop((name, idx)) if (name, idx) in side else _to_bf16(stack, idx)

    for i in range(DEPTH):
        last = i == DEPTH - 1
        j = i // 2
        if i % 2 == 0:
            w_in = weight(ab_w_in, "ab_in", j)
            w_out = weight(ab_w_out, "ab_out", j)
            u, gate, q, k, v = _inproj0(x, mod, i, None, w_in, rope_tabs, TM_LATENT)
            uc, gatec, qc, kc, vc = _inproj0(ctx, mod, i, ctx_row, w_in, None, TM_CONTEXT)
            w_gate = (0.5 * jnp.concatenate([lru_w_a[j, 0], lru_w_x[j, 0], lru_w_a[j, 1], lru_w_x[j, 1]],
                                            axis=-1)).astype(BF16)
            hb = lambda b: b.reshape(LRU_HEADS, 1, LRU_BLOCK)
            b_gate = 0.5 * jnp.concatenate([hb(lru_b_a[j, 0]), hb(lru_b_x[j, 0]), hb(lru_b_a[j, 1]),
                                            hb(lru_b_x[j, 1])], axis=-1)
            lam = jnp.concatenate([hb(lru_lam[j, 0]), hb(lru_lam[j, 1])], axis=-1)
            ya, yac, *done = _rglru(u, gate, uc, gatec, lru_conv_w[j], row2(lru_conv_b[j]), w_gate, b_gate, lam,
                                    casts=[] if ("w1", i) in side else [(ffn_w1, i)])
            side.update({("w1", i): w for w in done})
            yb, *done = _window_attention(q, k, v, kc, vc, att_sink[j],
                                          casts=[] if ("w2", i) in side else [(ffn_w2, i)])
            side.update({("w2", i): w for w in done})
            x = _outproj0(ya, yb, x, mod, i, None, w_out, row2(ln1_g[i]), row2(ln1_b[i]), TM_LATENT)
            if not last:
                ybc = _context_attention(qc, kc, vc, att_sink[j])
                ctx = _outproj0(yac, ybc, ctx, mod, i, ctx_row, w_out, row2(ln1_g[i]), row2(ln1_b[i]), TM_CONTEXT)
        else:
            assert last
            w_in = weight(cd_w_in, "cd_in", j)
            w_out = weight(cd_w_out, "cd_out", j)
            u, gb, p = _inproj1(x, mod, i, None, w_in, TM_LATENT, True)
            (uc,) = _inproj1(ctx, mod, i, ctx_row, w_in, TM_CONTEXT, False)
            wst, a16, wbig = _s5_operators(s5_log_dt[j], s5_a_re[j], s5_a_im[j], s5_b_re[j], s5_b_im[j],
                                           s5_c_re[j], s5_c_im[j])
            flat = lambda a: a.reshape(bsz, a.shape[1] // S5_CHUNK, S5_CHUNK * S5_WIDTH)
            h_in = _s5_states(flat(uc), flat(u), wst, a16)
            y_chunks = _s5_readout(flat(u), h_in, wbig)
            y = y_chunks.transpose(1, 2, 0, 3).reshape(bsz, n, S5_WIDTH)
            x = _outproj1(y, u, gb, p, x, mod, i, None, row2(s5_d[j]), s5_w_glu[j].astype(BF16), row2(s5_b_glu[j]),
                          sc_conv_w[j], row2(sc_conv_b[j]), w_out, row2(ln1_g[i]), row2(ln1_b[i]), TM_LATENT)
        w1 = weight(ffn_w1, "w1", i)
        w2 = weight(ffn_w2, "w2", i)
        ahead = []
        if not last:
            nj = (i + 1) // 2
            ahead = [("w1", i + 1, ffn_w1), ("w2", i + 1, ffn_w2)]
            ahead += ([("ab_in", nj, ab_w_in), ("ab_out", nj, ab_w_out)] if (i + 1) % 2 == 0
                      else [("cd_in", nj, cd_w_in), ("cd_out", nj, cd_w_out)])
        x, *done = _ffn(x, mod, i, None, w1, row2(ffn_b1[i]), w2, row2(ffn_b2[i]), row2(ln2_g[i]), row2(ln2_b[i]),
                        TM_LATENT, FFN_TH, casts=[(stack, idx) for _, idx, stack in ahead])
        side.update({(name, idx): w for (name, idx, _), w in zip(ahead, done)})
        if not last:
            ctx, = _ffn(ctx, mod, i, ctx_row, w1, row2(ffn_b1[i]), w2, row2(ffn_b2[i]), row2(ln2_g[i]),
                        row2(ln2_b[i]), TM_CONTEXT, FFN_TH)
    return x
```

```python
import functools
import math

import jax
import jax.numpy as jnp
import numpy as np
from jax import lax
from jax.experimental import pallas as pl
from jax.experimental.pallas import tpu as pltpu

F32 = jnp.float32
BF16 = jnp.bfloat16

D_MODEL = 2048
DEPTH = 2
GRID_W = 64
LRU_WIDTH = D_MODEL // 2
LRU_HEADS = 8
LRU_BLOCK = LRU_WIDTH // LRU_HEADS
LRU_CONV = 4
LRU_CONV_LEFT = 2
LRU_C = 8.0
ATT_HEAD_DIM = 128
ATT_Q_HEADS = (D_MODEL // 2) // ATT_HEAD_DIM
ATT_KV_HEADS = 2
ATT_GROUP = ATT_Q_HEADS // ATT_KV_HEADS
ATT_Q_WIDTH = ATT_Q_HEADS * ATT_HEAD_DIM
ATT_KV_WIDTH = ATT_KV_HEADS * ATT_HEAD_DIM
WINDOW = 128
ATT_BLOCK = 128
ROPE_BASE = 10000.0
ROPE_FREQS = ATT_HEAD_DIM // 4
S5_WIDTH = D_MODEL // 4
S5_GROUP = 16
S5_GROUPS = S5_WIDTH // S5_GROUP
S5_STATE = 64
SC_WIDTH = D_MODEL - S5_WIDTH
SC_CONV = 3
FFN_HIDDEN = 4 * D_MODEL
ALPHA = (2.0 * DEPTH) ** 0.25
LN_EPS = 1e-5
NEG_INF = -1e30
LOG2E = math.log2(math.e)

LANES = 128
SUBLANES = 8
V7X_VMEM_BYTES = 64 * 1024 * 1024
V7X_VMEM_BUDGET = 56 * 1024 * 1024

S5_CHUNK = 16
S5_SLAB_GROUPS = LANES // S5_GROUP
S5_SLABS = S5_WIDTH // LANES
S5_SLAB_STATE = S5_SLAB_GROUPS * S5_STATE
S5_XK = S5_CHUNK * LANES
S5_HK = 4 * S5_SLAB_STATE
S5_POW_ROWS = 64


def _vmem_limit(nbytes):
    return int(min(V7X_VMEM_BUDGET, max(nbytes * 3 // 2, 16 * 1024 * 1024)))


def _resident(shape, index_map):
    return pl.BlockSpec(shape, index_map, pipeline_mode=pl.Buffered(1))


def _params(sem, nbytes):
    return pltpu.CompilerParams(dimension_semantics=sem, vmem_limit_bytes=_vmem_limit(nbytes))


def _dot(a, b):
    return jnp.dot(a, b, preferred_element_type=F32)


def _dot_nt(a, b):
    return lax.dot_general(a, b, (((1,), (1,)), ((), ())), preferred_element_type=F32)


def _layer_norm(v, g, b):
    mu = jnp.mean(v, axis=-1, keepdims=True)
    c = v - mu
    var = jnp.mean(c * c, axis=-1, keepdims=True)
    return c * lax.rsqrt(var + LN_EPS) * g + b


def _gelu_tanh(x):
    return 0.5 * x * (1.0 + jnp.tanh(math.sqrt(2.0 / math.pi) * (x + 0.044715 * (x * x * x))))


def _sigmoid(x):
    return 0.5 * (1.0 + jnp.tanh(0.5 * x))


MOD_SH1, MOD_SC1, MOD_G1, MOD_SH2, MOD_SC2, MOD_G2 = range(6)


def _mod_spec(mod, layer, chunk):
    _, rows, width = mod.shape
    return pl.BlockSpec((1, rows, width // 6), lambda *_: (layer, 0, chunk))


def _mod_row(ref, ctx_row):
    row = pl.program_id(0) if ctx_row is None else ctx_row
    return ref[0, pl.ds(row, 1), :]


def _modulate(x, sc_ref, sh_ref, ctx_row):
    return (x * (1.0 + _mod_row(sc_ref, ctx_row)) + _mod_row(sh_ref, ctx_row)).astype(BF16)


CAST_BLOCK_BYTES = 8 * 1024 * 1024


def _cast_kernel(w_ref, o_ref):
    o_ref[...] = w_ref[0].astype(BF16)


def _to_bf16(w, layer):
    _, rows, cols = w.shape
    tr = rows
    while tr * cols * 4 > CAST_BLOCK_BYTES and tr % 2 == 0 and tr // 2 >= 2 * SUBLANES:
        tr //= 2
    return pl.pallas_call(
        _cast_kernel,
        grid=(rows // tr,),
        in_specs=[pl.BlockSpec((1, tr, cols), lambda i: (layer, i, 0))],
        out_specs=pl.BlockSpec((tr, cols), lambda i: (i, 0)),
        out_shape=jax.ShapeDtypeStruct((rows, cols), BF16),
        compiler_params=_params(("arbitrary",), 2 * tr * cols * 6),
        name="weight_cast",
    )(w)


CAST_ROWS = 2 * SUBLANES


def _hosted_call(body, *, grid, in_specs, out_specs, out_shape, args, casts=(), **kwargs):
    steps = math.prod(grid)

    def flat(*ids):
        step = 0
        for extent, i in zip(grid, ids):
            step = step * extent + i
        return step

    cast_in, cast_out, cast_shape = [], [], []
    for w, layer in casts:
        _, rows, cols = w.shape
        rps = max(CAST_ROWS, rows // steps)
        used = rows // rps
        assert rows % rps == 0 and used <= steps
        cast_in.append(pl.BlockSpec(
            (1, rps, cols), lambda *ids, used=used, layer=layer: (layer, jnp.minimum(flat(*ids), used - 1), 0)))
        cast_out.append(pl.BlockSpec((rps, cols), lambda *ids, used=used: (jnp.minimum(flat(*ids), used - 1), 0)))
        cast_shape.append(jax.ShapeDtypeStruct((rows, cols), BF16))
    n_in, n_out, n_cast = len(in_specs), len(out_specs), len(casts)

    def kernel(*refs):
        ins = refs[:n_in]
        w_ins = refs[n_in:n_in + n_cast]
        outs = refs[n_in + n_cast:n_in + n_cast + n_out]
        w_outs = refs[n_in + n_cast + n_out:n_in + 2 * n_cast + n_out]
        for w_in, w_out in zip(w_ins, w_outs):
            w_out[...] = w_in[0].astype(BF16)
        body(*ins, *outs, *refs[n_in + 2 * n_cast + n_out:])

    return pl.pallas_call(
        kernel, grid=grid, in_specs=list(in_specs) + cast_in, out_specs=list(out_specs) + cast_out,
        out_shape=list(out_shape) + cast_shape, **kwargs,
    )(*args, *[w for w, _ in casts])


def _mod_kernel(cc_ref, w_ref, b_ref, o_ref):
    a = cc_ref[...]
    a = (a * _sigmoid(a)).astype(BF16)
    o_ref[0] = _dot(a, w_ref[0].astype(BF16)) + b_ref[0]


def _modulation(cc, mod_w, mod_b):
    depth, d, n = mod_w.shape
    tn = 1024
    rows = cc.shape[0]
    return pl.pallas_call(
        _mod_kernel,
        grid=(depth, n // tn),
        in_specs=[pl.BlockSpec((rows, d), lambda l, j: (0, 0)),
                  pl.BlockSpec((1, d, tn), lambda l, j: (l, 0, j)),
                  pl.BlockSpec((1, 1, tn), lambda l, j: (l, 0, j))],
        out_specs=pl.BlockSpec((1, rows, tn), lambda l, j: (l, 0, j)),
        out_shape=jax.ShapeDtypeStruct((depth, rows, n), F32),
        compiler_params=_params(("arbitrary", "arbitrary"), 2 * d * tn * 4 + d * tn * 2),
        name="modulation",
    )(cc, mod_w, mod_b.reshape(depth, 1, n))


def _rope(x, cos, sin_signed, heads):
    lane = lax.broadcasted_iota(jnp.int32, (x.shape[0], ATT_HEAD_DIM), 1)
    first = (lane % (2 * ROPE_FREQS)) < ROPE_FREQS
    out = []
    for h in range(heads):
        xs = x[:, h * ATT_HEAD_DIM:(h + 1) * ATT_HEAD_DIM]
        swapped = jnp.where(first, pltpu.roll(xs, ATT_HEAD_DIM - ROPE_FREQS, 1), pltpu.roll(xs, ROPE_FREQS, 1))
        out.append(xs * cos + swapped * sin_signed)
    return out


def _inproj0_kernel(*refs, rope, ctx_row):
    if rope:
        (x_ref, sc_ref, sh_ref, cos_ref, sin_ref, wu_ref, wg_ref, wq_ref, wk_ref, wv_ref,
         u_ref, g_ref, q_ref, k_ref, v_ref) = refs
    else:
        (x_ref, sc_ref, sh_ref, wu_ref, wg_ref, wq_ref, wk_ref, wv_ref,
         u_ref, g_ref, q_ref, k_ref, v_ref) = refs
    h = _modulate(x_ref[0], sc_ref, sh_ref, ctx_row)
    u = _dot(h, wu_ref[...])
    g = _dot(h, wg_ref[...])
    for hd in range(LRU_HEADS):
        u_ref[0, hd] = u[:, hd * LRU_BLOCK:(hd + 1) * LRU_BLOCK].astype(BF16)
        g_ref[0, hd] = g[:, hd * LRU_BLOCK:(hd + 1) * LRU_BLOCK].astype(BF16)
    q = _dot(h, wq_ref[...]) * (ATT_HEAD_DIM ** -0.5 * LOG2E)
    k = _dot(h, wk_ref[...])
    if rope:
        cos = cos_ref[...]
        sin = sin_ref[...]
        for hd, piece in enumerate(_rope(q, cos, sin, ATT_Q_HEADS)):
            q_ref[0, :, hd * ATT_HEAD_DIM:(hd + 1) * ATT_HEAD_DIM] = piece.astype(BF16)
        for hd, piece in enumerate(_rope(k, cos, sin, ATT_KV_HEADS)):
            k_ref[0, :, hd * ATT_HEAD_DIM:(hd + 1) * ATT_HEAD_DIM] = piece.astype(BF16)
    else:
        q_ref[0] = q.astype(BF16)
        k_ref[0] = k.astype(BF16)
    v_ref[0] = _dot(h, wv_ref[...]).astype(BF16)


def _inproj0(x, mod, layer, ctx_row, w_in, rope_tabs, tm):
    bsz, n, d = x.shape
    rope = rope_tabs is not None
    tok = lambda b, i: (b, i, 0)
    in_specs = [pl.BlockSpec((1, tm, d), tok), _mod_spec(mod, layer, MOD_SC1), _mod_spec(mod, layer, MOD_SH1)]
    args = [x, mod, mod]
    if rope:
        in_specs += [pl.BlockSpec((tm, ATT_HEAD_DIM), lambda b, i: (i, 0))] * 2
        args += list(rope_tabs)
    kv_at = (2 * LRU_WIDTH + ATT_Q_WIDTH) // ATT_KV_WIDTH
    in_specs += [_resident((d, LRU_WIDTH), lambda b, i: (0, 0)), _resident((d, LRU_WIDTH), lambda b, i: (0, 1)),
                 _resident((d, ATT_Q_WIDTH), lambda b, i: (0, 2)),
                 _resident((d, ATT_KV_WIDTH), lambda b, i: (0, kv_at)),
                 _resident((d, ATT_KV_WIDTH), lambda b, i: (0, kv_at + 1))]
    args += [w_in] * 5
    head_major = jax.ShapeDtypeStruct((bsz, LRU_HEADS, n, LRU_BLOCK), BF16)
    head_spec = pl.BlockSpec((1, LRU_HEADS, tm, LRU_BLOCK), lambda b, i: (b, 0, i, 0))
    nbytes = 2 * tm * d * 4 + 2 * w_in.size + 2 * 2 * tm * w_in.shape[1] + tm * w_in.shape[1] * 4
    return pl.pallas_call(
        functools.partial(_inproj0_kernel, rope=rope, ctx_row=ctx_row),
        grid=(bsz, n // tm),
        in_specs=in_specs,
        out_specs=[head_spec, head_spec,
                   pl.BlockSpec((1, tm, ATT_Q_WIDTH), tok),
                   pl.BlockSpec((1, tm, ATT_KV_WIDTH), tok),
                   pl.BlockSpec((1, tm, ATT_KV_WIDTH), tok)],
        out_shape=[head_major, head_major,
                   jax.ShapeDtypeStruct((bsz, n, ATT_Q_WIDTH), BF16),
                   jax.ShapeDtypeStruct((bsz, n, ATT_KV_WIDTH), BF16),
                   jax.ShapeDtypeStruct((bsz, n, ATT_KV_WIDTH), BF16)],
        compiler_params=_params(("arbitrary", "arbitrary"), nbytes),
        name="inproj0_rope" if rope else "inproj0_ctx",
    )(*args)


LRU_PAD = SUBLANES
LRU_TILE = 256


def _scan8(a, b, row, reverse):
    for k in (1, 2, 4):
        if reverse:
            keep = row < SUBLANES - k
            shift = SUBLANES - k
        else:
            keep = row >= k
            shift = k
        a_sh = jnp.where(keep, pltpu.roll(a, shift, 0), 1.0)
        b_sh = jnp.where(keep, pltpu.roll(b, shift, 0), 0.0)
        b = a * b_sh + b
        a = a * a_sh
    return a, b


def _scan8_rows(a, b, row, reverse):
    out_a, out_b = [], []
    for g in range(a.shape[0] // SUBLANES):
        sl = slice(g * SUBLANES, (g + 1) * SUBLANES)
        ag, bg = _scan8(a[sl], b[sl], row, reverse)
        out_a.append(ag)
        out_b.append(bg)
    return jnp.concatenate(out_a, axis=0), jnp.concatenate(out_b, axis=0)


def _lru_kernel(ul_ref, gl_ref, uc_ref, gc_ref, cw_ref, cb_ref, wg_ref, bg_ref, lam_ref,
                yl_ref, yc_ref, upad, af_s, bf_s, ar_s, br_s, cin_s, *, n_lat, n_ctx):
    cw = cw_ref[...]
    cb = cb_ref[...]
    wg = wg_ref[0]
    bg = bg_ref[0]
    lam = lam_ref[0]
    neg = -lam
    softplus = jnp.maximum(neg, 0.0) + jnp.log1p(jnp.exp(-jnp.abs(neg)))
    half_rate = (-0.5 * LRU_C) * softplus
    row = lax.broadcasted_iota(jnp.int32, (SUBLANES, LRU_BLOCK), 0)
    chains = ((af_s, bf_s, False), (ar_s, br_s, True))

    def coefficients(src_ref, n_rows, row_off):
        upad[pl.ds(0, LRU_PAD), :] = jnp.zeros((LRU_PAD, LRU_BLOCK), F32)
        upad[pl.ds(LRU_PAD, n_rows), :] = src_ref[0, 0].astype(F32)
        upad[pl.ds(LRU_PAD + n_rows, LRU_PAD), :] = jnp.zeros((LRU_PAD, LRU_BLOCK), F32)

        def tile(i, carry):
            t0 = pl.multiple_of(i * LRU_TILE, LRU_TILE)
            xp = upad[pl.ds(t0, LRU_TILE + 2 * LRU_PAD), :]
            conv = cb
            for k in range(LRU_CONV):
                o = LRU_PAD - LRU_CONV_LEFT + k
                conv = conv + cw[k:k + 1, :] * xp[o:o + LRU_TILE, :]
            z = _dot(conv.astype(BF16), wg) + bg
            half_conv = 0.5 * conv
            for d, (a_ref, b_ref, reverse) in enumerate(chains):
                t_a = jnp.tanh(z[:, (2 * d) * LRU_BLOCK:(2 * d + 1) * LRU_BLOCK])
                t_x = jnp.tanh(z[:, (2 * d + 1) * LRU_BLOCK:(2 * d + 2) * LRU_BLOCK])
                rate = half_rate[:, d * LRU_BLOCK:(d + 1) * LRU_BLOCK]
                a = jnp.exp(rate + rate * t_a)
                gated = half_conv + half_conv * t_x
                b = jnp.sqrt(1.0 - a * a) * gated
                a, b = _scan8_rows(a, b, row, reverse)
                a_ref[pl.ds(row_off + t0, LRU_TILE), :] = a
                b_ref[pl.ds(row_off + t0, LRU_TILE), :] = b
            return carry

        lax.fori_loop(0, n_rows // LRU_TILE, tile, 0)

    coefficients(uc_ref, n_ctx, 0)
    coefficients(ul_ref, n_lat, n_ctx)

    groups_ctx = n_ctx // SUBLANES
    groups = (n_ctx + n_lat) // SUBLANES
    fa = af_s[pl.ds(SUBLANES - 1, groups, stride=SUBLANES), :]
    fb = bf_s[pl.ds(SUBLANES - 1, groups, stride=SUBLANES), :]
    ra = ar_s[pl.ds(0, groups, stride=SUBLANES), :]
    rb = br_s[pl.ds(0, groups, stride=SUBLANES), :]
    zero = jnp.zeros((SUBLANES, LRU_BLOCK), F32)
    carry = zero
    for v in range(groups // SUBLANES):
        sl = slice(v * SUBLANES, (v + 1) * SUBLANES)
        a, b = _scan8(fa[sl], fb[sl], row, False)
        incl = b + a * carry
        cin_s[0, sl, :] = jnp.where(row == 0, carry, pltpu.roll(incl, 1, 0))
        carry = jnp.broadcast_to(incl[SUBLANES - 1:SUBLANES, :], incl.shape)
    carry = zero
    order = list(range(groups_ctx // SUBLANES - 1, -1, -1)) + list(range(groups // SUBLANES - 1,
                                                                        groups_ctx // SUBLANES - 1, -1))
    for v in order:
        sl = slice(v * SUBLANES, (v + 1) * SUBLANES)
        a, b = _scan8(ra[sl], rb[sl], row, True)
        incl = b + a * carry
        cin_s[1, sl, :] = jnp.where(row == SUBLANES - 1, carry, pltpu.roll(incl, SUBLANES - 1, 0))
        carry = jnp.broadcast_to(incl[0:1, :], incl.shape)

    def emit(g_ref, y_ref, n_rows, row_off):
        tile_groups = LRU_TILE // SUBLANES

        def tile(i, carry):
            t0 = pl.multiple_of(i * LRU_TILE, LRU_TILE)
            g0 = pl.multiple_of(row_off // SUBLANES + i * tile_groups, tile_groups)
            rows = pl.ds(row_off + t0, LRU_TILE)
            a_f, b_f, a_r, b_r = af_s[rows, :], bf_s[rows, :], ar_s[rows, :], br_s[rows, :]
            pieces = []
            for g in range(tile_groups):
                sl = slice(g * SUBLANES, (g + 1) * SUBLANES)
                c_f = jnp.broadcast_to(cin_s[0, pl.ds(g0 + g, 1), :], (SUBLANES, LRU_BLOCK))
                c_r = jnp.broadcast_to(cin_s[1, pl.ds(g0 + g, 1), :], (SUBLANES, LRU_BLOCK))
                pieces.append((b_f[sl] + a_f[sl] * c_f) + (b_r[sl] + a_r[sl] * c_r))
            h = jnp.concatenate(pieces, axis=0)
            gate = g_ref[0, 0, pl.ds(t0, LRU_TILE), :].astype(F32)
            y_ref[0, 0, pl.ds(t0, LRU_TILE), :] = (h * _gelu_tanh(gate)).astype(BF16)
            return carry

        lax.fori_loop(0, n_rows // LRU_TILE, tile, 0)

    emit(gc_ref, yc_ref, n_ctx, 0)
    emit(gl_ref, yl_ref, n_lat, n_ctx)


def _rglru(u_lat, g_lat, u_ctx, g_ctx, conv_w, conv_b, w_gate, b_gate, lam, casts=()):
    bsz, heads, n_lat, blk = u_lat.shape
    n_ctx = u_ctx.shape[2]
    total = n_lat + n_ctx
    assert n_ctx % (SUBLANES * SUBLANES) == 0 and n_lat % LRU_TILE == 0 and n_ctx % LRU_TILE == 0
    seq = lambda n: pl.BlockSpec((1, 1, n, blk), lambda b, h: (b, h, 0, 0))
    per_head = lambda shape: pl.BlockSpec((1,) + shape, lambda b, h: (h, 0, 0))
    nbytes =4 * total * blk * 4 + (n_lat + 2 * LRU_PAD) * blk * 4 + 2 * 3 * 2 * total * blk * 2
    return _hosted_call(
        functools.partial(_lru_kernel, n_lat=n_lat, n_ctx=n_ctx),
        casts=casts,
        args=(u_lat, g_lat, u_ctx, g_ctx, conv_w, conv_b, w_gate, b_gate, lam),
        grid=(bsz, heads),
        in_specs=[seq(n_lat), seq(n_lat), seq(n_ctx), seq(n_ctx),
                  pl.BlockSpec((LRU_CONV, blk), lambda b, h: (0, h)),
                  pl.BlockSpec((1, blk), lambda b, h: (0, h)),
                  per_head((blk, 4 * blk)), per_head((1, 4 * blk)), per_head((1, 2 * blk))],
        out_specs=[seq(n_lat), seq(n_ctx)],
        out_shape=[jax.ShapeDtypeStruct(u_lat.shape, BF16), jax.ShapeDtypeStruct(u_ctx.shape, BF16)],
        scratch_shapes=[pltpu.VMEM((n_lat + 2 * LRU_PAD, blk), F32)]
                       + [pltpu.VMEM((total, blk), F32)] * 4
                       + [pltpu.VMEM((2, total // SUBLANES, blk), F32)],
        compiler_params=_params(("arbitrary", "arbitrary"), nbytes),
        name="rglru",
    )


ATT_TQ = 256
ATT_BAND = 3 * ATT_BLOCK


def _dot_tn(a, b):
    return lax.dot_general(a, b, (((0,), (0,)), ((), ())), preferred_element_type=F32)


def _softmax_pv(parts, sink_row):
    dh = ATT_HEAD_DIM
    m = sink_row
    for s, _ in parts:
        m = jnp.maximum(m, jnp.max(s, axis=0, keepdims=True))
    acc = None
    for idx, (s, v) in enumerate(parts):
        p = jnp.exp2(s - m).astype(BF16)
        v_ext = jnp.concatenate([v, jnp.ones(v.shape, BF16)], axis=1)
        if idx == len(parts) - 1:
            pad = 2 * SUBLANES
            first = lax.broadcasted_iota(jnp.int32, (pad, m.shape[1]), 0) == 0
            p_sink = jnp.where(first, jnp.exp2(sink_row - m), 0.0).astype(BF16)
            p = jnp.concatenate([p, p_sink], axis=0)
            v_sink = jnp.concatenate([jnp.zeros((pad, dh), BF16), jnp.ones((pad, dh), BF16)], axis=1)
            v_ext = jnp.concatenate([v_ext, v_sink], axis=0)
        pv = _dot_tn(p, v_ext)
        acc = pv if acc is None else acc + pv
    return acc[:, :dh] / acc[:, dh:]


def _sink_rows(sink, reps):
    return jnp.repeat(sink.reshape(ATT_KV_HEADS, 1, ATT_GROUP), reps, axis=2)


def _attn_kernel(q_ref, k_ref, v_ref, kc_ref, vc_ref, sink_ref, o_ref, *, seq):
    tile = pl.program_id(1)
    blocks = ATT_TQ // ATT_BLOCK
    qcol = lax.broadcasted_iota(jnp.int32, (1, ATT_GROUP * ATT_BLOCK), 1) % ATT_BLOCK
    krow = lax.broadcasted_iota(jnp.int32, (ATT_BAND, 1), 0)
    for i in range(blocks):
        q0 = (tile * blocks + i) * ATT_BLOCK
        start = pl.multiple_of(jnp.clip(q0 - ATT_BLOCK, 0, seq - ATT_BAND), ATT_BLOCK)
        valid = jnp.abs((q0 + qcol) - (start + krow)) <= WINDOW
        for g in range(ATT_KV_HEADS):
            heads = [q_ref[0, i * ATT_BLOCK:(i + 1) * ATT_BLOCK,
                           (g * ATT_GROUP + r) * ATT_HEAD_DIM:(g * ATT_GROUP + r + 1) * ATT_HEAD_DIM]
                     for r in range(ATT_GROUP)]
            qs = jnp.concatenate(heads, axis=0)
            kv = slice(g * ATT_HEAD_DIM, (g + 1) * ATT_HEAD_DIM)
            s_loc = jnp.where(valid, _dot_nt(k_ref[0, pl.ds(start, ATT_BAND), kv], qs), NEG_INF)
            s_ctx = _dot_nt(kc_ref[0, :, kv], qs)
            o = _softmax_pv([(s_loc, v_ref[0, pl.ds(start, ATT_BAND), kv]), (s_ctx, vc_ref[0, :, kv])],
                            sink_ref[g] * LOG2E)
            for r in range(ATT_GROUP):
                col = (g * ATT_GROUP + r) * ATT_HEAD_DIM
                o_ref[0, i * ATT_BLOCK:(i + 1) * ATT_BLOCK, col:col + ATT_HEAD_DIM] = (
                    o[r * ATT_BLOCK:(r + 1) * ATT_BLOCK].astype(BF16))


def _window_attention(q, k, v, kc, vc, sink, casts=()):
    bsz, n, _ = q.shape
    sink_rows = _sink_rows(sink, ATT_BLOCK)
    lc = kc.shape[1]
    whole = lambda rows: pl.BlockSpec((1, rows, ATT_KV_WIDTH), lambda b, i: (b, 0, 0))
    nbytes = 2 * 2 * (2 * ATT_TQ * ATT_Q_WIDTH + 2 * n * ATT_KV_WIDTH + 2 * lc * ATT_KV_WIDTH) + 8 * 1024 * 1024
    return _hosted_call(
        functools.partial(_attn_kernel, seq=n),
        casts=casts,
        args=(q, k, v, kc, vc, sink_rows),
        grid=(bsz, n // ATT_TQ),
        in_specs=[pl.BlockSpec((1, ATT_TQ, ATT_Q_WIDTH), lambda b, i: (b, i, 0)),
                  whole(n), whole(n), whole(lc), whole(lc),
                  pl.BlockSpec(sink_rows.shape, lambda b, i: (0, 0, 0))],
        out_specs=[pl.BlockSpec((1, ATT_TQ, ATT_Q_WIDTH), lambda b, i: (b, i, 0))],
        out_shape=[jax.ShapeDtypeStruct(q.shape, BF16)],
        compiler_params=_params(("arbitrary", "arbitrary"), nbytes),
        name="window_attention",
    )


def _ctx_attn_kernel(q_ref, kc_ref, vc_ref, sink_ref, o_ref):
    lc = q_ref.shape[1]
    for g in range(ATT_KV_HEADS):
        heads = [q_ref[0, :, (g * ATT_GROUP + r) * ATT_HEAD_DIM:(g * ATT_GROUP + r + 1) * ATT_HEAD_DIM]
                 for r in range(ATT_GROUP)]
        qs = jnp.concatenate(heads, axis=0)
        kv = slice(g * ATT_HEAD_DIM, (g + 1) * ATT_HEAD_DIM)
        o = _softmax_pv([(_dot_nt(kc_ref[0, :, kv], qs), vc_ref[0, :, kv])], sink_ref[g] * LOG2E)
        for r in range(ATT_GROUP):
            col = (g * ATT_GROUP + r) * ATT_HEAD_DIM
            o_ref[0, :, col:col + ATT_HEAD_DIM] = o[r * lc:(r + 1) * lc].astype(BF16)


def _context_attention(qc, kc, vc, sink):
    bsz, lc, _ = qc.shape
    sink_rows = _sink_rows(sink, lc)
    kv_spec = pl.BlockSpec((1, lc, ATT_KV_WIDTH), lambda b: (b, 0, 0))
    q_spec = pl.BlockSpec((1, lc, ATT_Q_WIDTH), lambda b: (b, 0, 0))
    return pl.pallas_call(
        _ctx_attn_kernel,
        grid=(bsz,),
        in_specs=[q_spec, kv_spec, kv_spec, pl.BlockSpec(sink_rows.shape, lambda b: (0, 0, 0))],
        out_specs=q_spec,
        out_shape=jax.ShapeDtypeStruct(qc.shape, BF16),
        compiler_params=_params(("arbitrary",), 8 * 1024 * 1024),
        name="context_attention",
    )(qc, kc, vc, sink_rows)


def _outproj0_kernel(ya_ref, yb_ref, x_ref, gate_ref, w_ref, lg_ref, lb_ref, o_ref, *, ctx_row):
    y = jnp.concatenate([ya_ref[0, h] for h in range(LRU_HEADS)] + [yb_ref[0]], axis=1)
    out = _dot(y, w_ref[...])
    o_ref[0] = _layer_norm(ALPHA * x_ref[0] + _mod_row(gate_ref, ctx_row) * out, lg_ref[...], lb_ref[...])


def _outproj0(ya, yb, x, mod, layer, ctx_row, w_out, ln_g, ln_b, tm):
    bsz, n, d = x.shape
    tok = lambda b, i: (b, i, 0)
    const = lambda b, i: (0, 0)
    nbytes = 2 * 2 * tm * d * 4 + w_out.size * 2 + 2 * 2 * tm * d * 2 + 2 * tm * d * 4
    return pl.pallas_call(
        functools.partial(_outproj0_kernel, ctx_row=ctx_row),
        grid=(bsz, n // tm),
        in_specs=[pl.BlockSpec((1, LRU_HEADS, tm, LRU_BLOCK), lambda b, i: (b, 0, i, 0)),
                  pl.BlockSpec((1, tm, ATT_Q_WIDTH), tok),
                  pl.BlockSpec((1, tm, d), tok),
                  _mod_spec(mod, layer, MOD_G1),
                  _resident(w_out.shape, const),
                  pl.BlockSpec((1, d), const), pl.BlockSpec((1, d), const)],
        out_specs=pl.BlockSpec((1, tm, d), tok),
        out_shape=jax.ShapeDtypeStruct(x.shape, F32),
        compiler_params=_params(("arbitrary", "arbitrary"), nbytes),
        name="outproj0_ln",
    )(ya, yb, x, mod, w_out, ln_g, ln_b)


def _ffn_kernel(x_ref, sc_ref, sh_ref, gate_ref, w1_ref, b1_ref, w2_ref, b2_ref, lg_ref, lb_ref,
                o_ref, h_s, acc_s, *, ctx_row):
    j = pl.program_id(2)

    @pl.when(j == 0)
    def _():
        h_s[...] = _modulate(x_ref[0], sc_ref, sh_ref, ctx_row)
        acc_s[...] = jnp.zeros(acc_s.shape, F32)

    a = jnp.maximum(_dot(h_s[...], w1_ref[...]) + b1_ref[...], 0.0)
    acc_s[...] += _dot((a * a).astype(BF16), w2_ref[...])

    @pl.when(j == pl.num_programs(2) - 1)
    def _():
        f = acc_s[...] + b2_ref[...]
        o_ref[0] = _layer_norm(ALPHA * x_ref[0] + _mod_row(gate_ref, ctx_row) * f, lg_ref[...], lb_ref[...])


def _ffn(x, mod, layer, ctx_row, w1, b1, w2, b2, ln_g, ln_b, tm, th, casts=()):
    bsz, n, d = x.shape
    hidden = w1.shape[1]
    tok = lambda b, i, j: (b, i, 0)
    const = lambda b, i, j: (0, 0)
    nbytes = 2 * 2 * tm * d * 4 + 2 * 2 * 2 * d * th * 2 + tm * d * (2 + 4) + 2 * tm * th * 4 + tm * d * 4
    return _hosted_call(
        functools.partial(_ffn_kernel, ctx_row=ctx_row),
        casts=casts,
        args=(x, mod, mod, mod, w1, b1, w2, b2, ln_g, ln_b),
        grid=(bsz, n // tm, hidden // th),
        in_specs=[pl.BlockSpec((1, tm, d), tok),
                  _mod_spec(mod, layer, MOD_SC2), _mod_spec(mod, layer, MOD_SH2), _mod_spec(mod, layer, MOD_G2),
                  pl.BlockSpec((d, th), lambda b, i, j: (0, j)),
                  pl.BlockSpec((1, th), lambda b, i, j: (0, j)),
                  pl.BlockSpec((th, d), lambda b, i, j: (j, 0)),
                  pl.BlockSpec((1, d), const), pl.BlockSpec((1, d), const), pl.BlockSpec((1, d), const)],
        out_specs=[pl.BlockSpec((1, tm, d), tok)],
        out_shape=[jax.ShapeDtypeStruct(x.shape, F32)],
        scratch_shapes=[pltpu.VMEM((tm, d), BF16), pltpu.VMEM((tm, d), F32)],
        compiler_params=_params(("arbitrary", "arbitrary", "arbitrary"), nbytes),
        name="ffn_ln",
    )


SC_PARTS = SC_WIDTH // S5_WIDTH


def _inproj1_kernel(*refs, full, ctx_row):
    x_ref, sc_ref, sh_ref, wu_ref = refs[:4]
    u_s = refs[-1]
    h = _modulate(x_ref[0], sc_ref, sh_ref, ctx_row)
    u = _dot(h, wu_ref[...])
    if full:
        wb_refs = refs[4:4 + SC_PARTS]
        wc_refs = refs[4 + SC_PARTS:4 + 2 * SC_PARTS]
        wx_refs = refs[4 + 2 * SC_PARTS:4 + 3 * SC_PARTS]
        u_ref, uflat_ref, gb_ref, p_ref = refs[4 + 3 * SC_PARTS:-1]
        u_ref[0] = u.astype(BF16)
        for c in range(SC_PARTS):
            cols = slice(c * S5_WIDTH, (c + 1) * S5_WIDTH)
            gb_ref[0, :, cols] = _dot(h, wb_refs[c][...]).astype(BF16)
            p_ref[0, :, cols] = (_dot(h, wc_refs[c][...]) * _dot(h, wx_refs[c][...])).astype(BF16)
    else:
        uflat_ref, = refs[4:-1]
    for k in range(S5_SLABS):
        u_s[k] = u[:, k * LANES:(k + 1) * LANES]
    chunks = u.shape[0] // S5_CHUNK
    for t in range(S5_CHUNK):
        for k in range(S5_SLABS):
            col = t * S5_WIDTH + k * LANES
            uflat_ref[0, :, col:col + LANES] = u_s[k, pl.ds(t, chunks, stride=S5_CHUNK), :].astype(BF16)


def _inproj1(x, mod, layer, ctx_row, w_in, tm, full):
    bsz, n, d = x.shape
    tok = lambda b, i: (b, i, 0)
    n_blocks = 1 + 3 * SC_PARTS if full else 1
    flat_spec = pl.BlockSpec((1, tm // S5_CHUNK, S5_CHUNK * S5_WIDTH), tok)
    flat_shape = jax.ShapeDtypeStruct((bsz, n // S5_CHUNK, S5_CHUNK * S5_WIDTH), BF16)
    token = lambda w: (pl.BlockSpec((1, tm, w), tok), jax.ShapeDtypeStruct((bsz, n, w), BF16))
    outs = ([token(S5_WIDTH), (flat_spec, flat_shape), token(SC_WIDTH), token(SC_WIDTH)] if full
            else [(flat_spec, flat_shape)])
    ncols = n_blocks * S5_WIDTH
    nbytes = 2 * tm * d * 4 + 2 * d * ncols + 2 * 2 * tm * (ncols + S5_WIDTH) + tm * ncols * 4 + tm * S5_WIDTH * 4
    return pl.pallas_call(
        functools.partial(_inproj1_kernel, full=full, ctx_row=ctx_row),
        grid=(bsz, n // tm),
        in_specs=[pl.BlockSpec((1, tm, d), tok), _mod_spec(mod, layer, MOD_SC1), _mod_spec(mod, layer, MOD_SH1)]
                 + [_resident((d, S5_WIDTH), lambda b, i, c=c: (0, c)) for c in range(n_blocks)],
        out_specs=[spec for spec, _ in outs],
        out_shape=[shape for _, shape in outs],
        scratch_shapes=[pltpu.VMEM((S5_SLABS, tm, LANES), F32)],
        compiler_params=_params(("arbitrary", "arbitrary"), nbytes),
        name="inproj1" if full else "inproj1_ctx",
    )(x, mod, mod, *([w_in] * n_blocks))


def _s5_power_table(par_ref):
    rows = 2 * S5_POW_ROWS
    r = lax.broadcasted_iota(jnp.int32, (rows, S5_SLAB_STATE), 0)
    first = r < S5_POW_ROWS
    lag = (r % S5_POW_ROWS).astype(F32)
    pick = lambda i: jnp.where(first, par_ref[0, 0, i:i + 1, :], par_ref[0, 1, i:i + 1, :])
    dt = jnp.exp(pick(0))
    mag = jnp.exp(lag * dt * pick(1))
    ang = lag * dt * pick(2)
    return mag * jnp.cos(ang), mag * jnp.sin(ang)


def _s5_input_matrix(par_ref, bt_ref, d, pw_re, pw_im):
    a_re = par_ref[0, d, 1:2, :]
    a_im = par_ref[0, d, 2:3, :]
    ab_re = pw_re[d * S5_POW_ROWS + 1:d * S5_POW_ROWS + 2, :]
    ab_im = pw_im[d * S5_POW_ROWS + 1:d * S5_POW_ROWS + 2, :]
    den = a_re * a_re + a_im * a_im
    k_re = ((ab_re - 1.0) * a_re + ab_im * a_im) / den
    k_im = (ab_im * a_re - (ab_re - 1.0) * a_im) / den
    b_re = bt_ref[0, d, 0]
    b_im = bt_ref[0, d, 1]
    return k_re * b_re - k_im * b_im, k_re * b_im + k_im * b_re


def _cmul_row(x_re, x_im, p_re, p_im):
    return x_re * p_re - x_im * p_im, x_re * p_im + x_im * p_re


def _s5_state_prep_kernel(par_ref, bt_ref, wst_ref, a16_ref):
    pw_re, pw_im = _s5_power_table(par_ref)
    for d in range(2):
        bb_re, bb_im = _s5_input_matrix(par_ref, bt_ref, d, pw_re, pw_im)
        base = d * S5_POW_ROWS
        for lag in range(S5_CHUNK):
            e_re, e_im = _cmul_row(bb_re, bb_im, pw_re[base + lag:base + lag + 1, :],
                                   pw_im[base + lag:base + lag + 1, :])
            s = S5_CHUNK - 1 - lag if d == 0 else lag
            col = 2 * d * S5_SLAB_STATE
            wst_ref[0, s * LANES:(s + 1) * LANES, col:col + S5_SLAB_STATE] = e_re.astype(BF16)
            wst_ref[0, s * LANES:(s + 1) * LANES, col + S5_SLAB_STATE:col + 2 * S5_SLAB_STATE] = e_im.astype(BF16)
        row16 = base + S5_CHUNK
        a16_ref[0, :, 2 * d * S5_SLAB_STATE:(2 * d + 1) * S5_SLAB_STATE] = jnp.broadcast_to(
            pw_re[row16:row16 + 1, :], (SUBLANES, S5_SLAB_STATE))
        a16_ref[0, :, (2 * d + 1) * S5_SLAB_STATE:(2 * d + 2) * S5_SLAB_STATE] = jnp.broadcast_to(
            pw_im[row16:row16 + 1, :], (SUBLANES, S5_SLAB_STATE))


def _s5_output_prep_kernel(par_ref, bt_ref, c_ref, ct_ref, wbig_ref):
    pw_re, pw_im = _s5_power_table(par_ref)
    kern = []
    for d in range(2):
        bb_re, bb_im = _s5_input_matrix(par_ref, bt_ref, d, pw_re, pw_im)
        c_re = c_ref[0, d, 0].astype(BF16)
        c_im = c_ref[0, d, 1].astype(BF16)
        base = d * S5_POW_ROWS
        per_lag = []
        for lag in range(S5_CHUNK):
            e_re, e_im = _cmul_row(bb_re, bb_im, pw_re[base + lag:base + lag + 1, :],
                                   pw_im[base + lag:base + lag + 1, :])
            per_lag.append(_dot_nt(e_re.astype(BF16), c_re) - _dot_nt(e_im.astype(BF16), c_im))
        kern.append(per_lag)
    for s in range(S5_CHUNK):
        for t in range(S5_CHUNK):
            if s < t:
                blk = kern[0][t - s]
            elif s > t:
                blk = kern[1][s - t]
            else:
                blk = kern[0][0] + kern[1][0]
            wbig_ref[0, s * LANES:(s + 1) * LANES, t * LANES:(t + 1) * LANES] = blk.astype(BF16)
    pt_re = pw_re.T
    pt_im = pw_im.T
    for d in range(2):
        ct_re = ct_ref[0, d, 0]
        ct_im = ct_ref[0, d, 1]
        for t in range(S5_CHUNK):
            lag = t + 1 if d == 0 else S5_CHUNK - t
            col = d * S5_POW_ROWS + lag
            p_re = pt_re[:, col:col + 1]
            p_im = pt_im[:, col:col + 1]
            g_re = ct_re * p_re - ct_im * p_im
            g_im = ct_re * p_im + ct_im * p_re
            r0 = S5_XK + 2 * d * S5_SLAB_STATE
            wbig_ref[0, r0:r0 + S5_SLAB_STATE, t * LANES:(t + 1) * LANES] = g_re.astype(BF16)
            wbig_ref[0, r0 + S5_SLAB_STATE:r0 + 2 * S5_SLAB_STATE, t * LANES:(t + 1) * LANES] = (-g_im).astype(BF16)


def _s5_operators(log_dt, a_re, a_im, b_re, b_im, c_re, c_im):
    gs = S5_SLAB_GROUPS
    eye = jnp.eye(gs, dtype=F32)

    def lanes(v):
        return v.reshape(2, S5_SLABS, gs * S5_STATE)

    par = jnp.stack([lanes(jnp.broadcast_to(log_dt[:, :, None], a_re.shape)), lanes(a_re), lanes(a_im)], axis=2)
    par = jnp.pad(par, ((0, 0), (0, 0), (0, SUBLANES - 3), (0, 0))).transpose(1, 0, 2, 3)

    def embed_bt(b):
        b = b.reshape(2, S5_SLABS, gs, S5_STATE, S5_GROUP)
        e = b.transpose(0, 1, 2, 4, 3)[:, :, :, :, None, :] * eye[None, None, :, None, :, None]
        return e.reshape(2, S5_SLABS, gs * S5_GROUP, gs * S5_STATE).transpose(1, 0, 2, 3)

    def embed_c(c):
        c = c.reshape(2, S5_SLABS, gs, S5_GROUP, S5_STATE)
        e = c[:, :, :, :, None, :] * eye[None, None, :, None, :, None]
        return e.reshape(2, S5_SLABS, gs * S5_GROUP, gs * S5_STATE).transpose(1, 0, 2, 3)

    bt = jnp.stack([embed_bt(b_re), embed_bt(b_im)], axis=2)
    cm = jnp.stack([embed_c(c_re), embed_c(c_im)], axis=2)
    ct = cm.transpose(0, 1, 2, 4, 3)

    slab5 = lambda shape: pl.BlockSpec((1,) + shape, lambda k: (k, 0, 0, 0, 0))
    par_spec = pl.BlockSpec((1, 2, SUBLANES, S5_SLAB_STATE), lambda k: (k, 0, 0, 0))
    bt_spec = slab5((2, 2, LANES, S5_SLAB_STATE))
    wst, a16 = pl.pallas_call(
        _s5_state_prep_kernel,
        grid=(S5_SLABS,),
        in_specs=[par_spec, bt_spec],
        out_specs=[pl.BlockSpec((1, S5_XK, S5_HK), lambda k: (k, 0, 0)),
                   pl.BlockSpec((1, SUBLANES, S5_HK), lambda k: (k, 0, 0))],
        out_shape=[jax.ShapeDtypeStruct((S5_SLABS, S5_XK, S5_HK), BF16),
                   jax.ShapeDtypeStruct((S5_SLABS, SUBLANES, S5_HK), F32)],
        compiler_params=_params(("arbitrary",), 2 * S5_XK * S5_HK * 2 + 8 * 1024 * 1024),
        name="s5_state_operator",
    )(par, bt)
    wbig = pl.pallas_call(
        _s5_output_prep_kernel,
        grid=(S5_SLABS,),
        in_specs=[par_spec, bt_spec, bt_spec, slab5((2, 2, S5_SLAB_STATE, LANES))],
        out_specs=pl.BlockSpec((1, S5_XK + S5_HK, S5_XK), lambda k: (k, 0, 0)),
        out_shape=jax.ShapeDtypeStruct((S5_SLABS, S5_XK + S5_HK, S5_XK), BF16),
        compiler_params=_params(("arbitrary",), 2 * (S5_XK + S5_HK) * S5_XK * 2 + 8 * 1024 * 1024),
        name="s5_output_operator",
    )(par, bt, cm, ct)
    return wst, a16, wbig


def _s5_states_kernel(*refs, n_ctx, n_lat):
    xc_refs = refs[:S5_CHUNK]
    xl_refs = refs[S5_CHUNK:2 * S5_CHUNK]
    wst_ref, a16_ref, h_ref, s_s, h_s = refs[2 * S5_CHUNK:]
    x = jnp.concatenate([jnp.concatenate([r[0] for r in xc_refs], axis=1),
                         jnp.concatenate([r[0] for r in xl_refs], axis=1)], axis=0)
    s_s[...] = _dot(x, wst_ref[0])
    p = S5_SLAB_STATE
    af_re = a16_ref[0, 0:1, 0:p]
    af_im = a16_ref[0, 0:1, p:2 * p]
    ar_re = a16_ref[0, 0:1, 2 * p:3 * p]
    ar_im = a16_ref[0, 0:1, 3 * p:4 * p]
    total = n_ctx + n_lat

    def step(i, carry):
        f_re, f_im, r_re, r_im = carry
        h_s[pl.ds(i, 1), 0:p] = f_re
        h_s[pl.ds(i, 1), p:2 * p] = f_im
        s_re = s_s[pl.ds(i, 1), 0:p]
        s_im = s_s[pl.ds(i, 1), p:2 * p]
        f_re, f_im = af_re * f_re - af_im * f_im + s_re, af_re * f_im + af_im * f_re + s_im
        j = jnp.where(i < n_ctx, n_ctx - 1 - i, total + n_ctx - 1 - i)
        h_s[pl.ds(j, 1), 2 * p:3 * p] = r_re
        h_s[pl.ds(j, 1), 3 * p:4 * p] = r_im
        s_re = s_s[pl.ds(j, 1), 2 * p:3 * p]
        s_im = s_s[pl.ds(j, 1), 3 * p:4 * p]
        r_re, r_im = ar_re * r_re - ar_im * r_im + s_re, ar_re * r_im + ar_im * r_re + s_im
        return f_re, f_im, r_re, r_im

    zero = jnp.zeros((1, p), F32)
    lax.fori_loop(0, total, step, (zero, zero, zero, zero), unroll=4)
    h_ref[0, 0] = h_s[pl.ds(n_ctx, n_lat), :].astype(BF16)


def _s5_states(uc_flat, ul_flat, wst, a16):
    bsz, n_ctx, _ = uc_flat.shape
    n_lat = ul_flat.shape[1]
    piece = lambda rows, s: pl.BlockSpec((1, rows, LANES), lambda k, b, s=s: (b, 0, s * S5_SLABS + k))
    in_specs = ([piece(n_ctx, s) for s in range(S5_CHUNK)] + [piece(n_lat, s) for s in range(S5_CHUNK)]
                + [pl.BlockSpec((1, S5_XK, S5_HK), lambda k, b: (k, 0, 0)),
                   pl.BlockSpec((1, SUBLANES, S5_HK), lambda k, b: (k, 0, 0))])
    total = n_ctx + n_lat
    nbytes = 2 * S5_XK * S5_HK * 2 + 2 * total * S5_XK * 2 + 3 * total * S5_HK * 4 + 2 * n_lat * S5_HK * 2
    return pl.pallas_call(
        functools.partial(_s5_states_kernel, n_ctx=n_ctx, n_lat=n_lat),
        grid=(S5_SLABS, bsz),
        in_specs=in_specs,
        out_specs=pl.BlockSpec((1, 1, n_lat, S5_HK), lambda k, b: (b, k, 0, 0)),
        out_shape=jax.ShapeDtypeStruct((bsz, S5_SLABS, n_lat, S5_HK), BF16),
        scratch_shapes=[pltpu.VMEM((total, S5_HK), F32), pltpu.VMEM((total, S5_HK), F32)],
        compiler_params=_params(("arbitrary", "arbitrary"), nbytes),
        name="s5_states",
    )(*([uc_flat] * S5_CHUNK + [ul_flat] * S5_CHUNK + [wst, a16]))


def _s5_readout_kernel(*refs):
    x_refs = refs[:S5_CHUNK]
    h_ref, w_ref, y_ref = refs[S5_CHUNK:]
    lhs = jnp.concatenate([r[0] for r in x_refs] + [h_ref[0, 0]], axis=1)
    y = _dot(lhs, w_ref[0])
    chunks = y.shape[0]
    for t in range(S5_CHUNK):
        y_ref[0, pl.ds(t, chunks, stride=S5_CHUNK), :] = y[:, t * LANES:(t + 1) * LANES]


def _s5_readout(ul_flat, h_in, wbig):
    bsz, n_lat, _ = ul_flat.shape
    piece = lambda s: pl.BlockSpec((1, n_lat, LANES), lambda k, b, s=s: (b, 0, s * S5_SLABS + k))
    nbytes = 2 * (S5_XK + S5_HK) * S5_XK * 2 + 2 * n_lat * (S5_XK + S5_HK) * 2 * 2 + n_lat * S5_XK * 4 * 3
    return pl.pallas_call(
        _s5_readout_kernel,
        grid=(S5_SLABS, bsz),
        in_specs=[piece(s) for s in range(S5_CHUNK)]
                 + [pl.BlockSpec((1, 1, n_lat, S5_HK), lambda k, b: (b, k, 0, 0)),
                    pl.BlockSpec((1, S5_XK + S5_HK, S5_XK), lambda k, b: (k, 0, 0))],
        out_specs=pl.BlockSpec((1, n_lat * S5_CHUNK, LANES), lambda k, b: (b, 0, k)),
        out_shape=jax.ShapeDtypeStruct((bsz, n_lat * S5_CHUNK, S5_WIDTH), F32),
        compiler_params=_params(("arbitrary", "arbitrary"), nbytes),
        name="s5_readout",
    )(*([ul_flat] * S5_CHUNK + [h_in, wbig]))


SC_HALO = 16


def _outproj1_kernel(y_ref, u_ref, gb_ref, p_ref, pprev_ref, pnext_ref, x_ref, gate_ref,
                     dskip_ref, wglu_ref, bglu_ref, cw_ref, cb_ref, w_ref, lg_ref, lb_ref, o_ref, *, ctx_row):
    i = pl.program_id(1)
    tm = x_ref.shape[1]
    yc = y_ref[0] + dskip_ref[...] * u_ref[0].astype(F32)
    z = _gelu_tanh(yc)
    y_c = z * _sigmoid(_dot(z.astype(BF16), wglu_ref[...]) + bglu_ref[...])
    p = p_ref[0].astype(F32)
    row = lax.broadcasted_iota(jnp.int32, (tm, 1), 0)
    prev_row = jnp.where(i > 0, pprev_ref[0, SC_HALO - 1:SC_HALO, :].astype(F32), 0.0)
    next_row = jnp.where(i < pl.num_programs(1) - 1, pnext_ref[0, 0:1, :].astype(F32), 0.0)
    p_dn = jnp.where(row == 0, prev_row, pltpu.roll(p, 1, 0))
    p_up = jnp.where(row == tm - 1, next_row, pltpu.roll(p, tm - 1, 0))
    conv = cb_ref[...] + cw_ref[0:1, :] * p_dn + cw_ref[1:2, :] * p + cw_ref[2:3, :] * p_up
    y_d = gb_ref[0].astype(F32) * conv
    y = jnp.concatenate([y_c.astype(BF16), y_d.astype(BF16)], axis=1)
    out = _dot(y, w_ref[...])
    o_ref[0] = _layer_norm(ALPHA * x_ref[0] + _mod_row(gate_ref, ctx_row) * out, lg_ref[...], lb_ref[...])


def _outproj1(y, u, gb, p, x, mod, layer, ctx_row, d_skip, w_glu, b_glu, conv_w, conv_b, w_out, ln_g, ln_b, tm):
    bsz, n, d = x.shape
    tok = lambda b, i: (b, i, 0)
    const = lambda b, i: (0, 0)
    per = tm // SC_HALO
    last = n // SC_HALO - 1
    nbytes = (2 * 2 * tm * d * 4 + w_out.size * 2 + 2 * 2 * tm * (2 * S5_WIDTH + 2 * SC_WIDTH) * 2
              + 6 * tm * SC_WIDTH * 4 + 2 * tm * d * 4)
    return pl.pallas_call(
        functools.partial(_outproj1_kernel, ctx_row=ctx_row),
        grid=(bsz, n // tm),
        in_specs=[pl.BlockSpec((1, tm, S5_WIDTH), tok), pl.BlockSpec((1, tm, S5_WIDTH), tok),
                  pl.BlockSpec((1, tm, SC_WIDTH), tok), pl.BlockSpec((1, tm, SC_WIDTH), tok),
                  pl.BlockSpec((1, SC_HALO, SC_WIDTH), lambda b, i: (b, jnp.maximum(i * per - 1, 0), 0)),
                  pl.BlockSpec((1, SC_HALO, SC_WIDTH), lambda b, i: (b, jnp.minimum((i + 1) * per, last), 0)),
                  pl.BlockSpec((1, tm, d), tok), _mod_spec(mod, layer, MOD_G1),
                  pl.BlockSpec((1, S5_WIDTH), const), _resident(w_glu.shape, const),
                  pl.BlockSpec((1, S5_WIDTH), const),
                  pl.BlockSpec((SC_CONV, SC_WIDTH), const), pl.BlockSpec((1, SC_WIDTH), const),
                  _resident(w_out.shape, const),
                  pl.BlockSpec((1, d), const), pl.BlockSpec((1, d), const)],
        out_specs=pl.BlockSpec((1, tm, d), tok),
        out_shape=jax.ShapeDtypeStruct(x.shape, F32),
        compiler_params=_params(("arbitrary", "arbitrary"), nbytes),
        name="outproj1_ln",
    )(y, u, gb, p, p, p, x, mod, d_skip, w_glu, b_glu, conv_w, conv_b, w_out, ln_g, ln_b)


def _rope_tables(n):
    t = np.arange(n)
    inv = ROPE_BASE ** (-np.arange(ROPE_FREQS, dtype=np.float64) / ROPE_FREQS)
    ang_r = (t // GRID_W)[:, None] * inv[None, :]
    ang_c = (t % GRID_W)[:, None] * inv[None, :]
    cos = np.concatenate([np.cos(ang_r), np.cos(ang_r), np.cos(ang_c), np.cos(ang_c)], axis=1)
    sin = np.concatenate([-np.sin(ang_r), np.sin(ang_r), -np.sin(ang_c), np.sin(ang_c)], axis=1)
    return jnp.asarray(cos, F32), jnp.asarray(sin, F32)


TM_LATENT = 512
TM_CONTEXT = 256
FFN_TH = 1024


def kernel(x, c, ctx, c_ctx, mod_w, mod_b, ln1_g, ln1_b, ln2_g, ln2_b, ffn_w1, ffn_b1, ffn_w2, ffn_b2,
           ab_w_in, ab_w_out, lru_conv_w, lru_conv_b, lru_w_a, lru_b_a, lru_w_x, lru_b_x, lru_lam, att_sink,
           cd_w_in, cd_w_out, s5_log_dt, s5_a_re, s5_a_im, s5_b_re, s5_b_im, s5_c_re, s5_c_im, s5_d,
           s5_w_glu, s5_b_glu, sc_conv_w, sc_conv_b):
    bsz, n, d = x.shape
    lc = ctx.shape[1]
    assert n % TM_LATENT == 0 and lc % TM_CONTEXT == 0 and n % (GRID_W) == 0

    pad_rows = SUBLANES - (bsz + 1) % SUBLANES if (bsz + 1) % SUBLANES else 0
    cc = jnp.concatenate([c, c_ctx[None, :], jnp.zeros((pad_rows, d), F32)], axis=0)
    mod = _modulation(cc, mod_w, mod_b)
    ctx_row = bsz

    row2 = lambda v: v.reshape(1, -1)
    rope_tabs = _rope_tables(n)

    side = {}

    def weight(stack, name, idx):
        return side.pop((name, idx)) if (name, idx) in side else _to_bf16(stack, idx)

    for i in range(DEPTH):
        last = i == DEPTH - 1
        j = i // 2
        if i % 2 == 0:
            w_in = weight(ab_w_in, "ab_in", j)
            w_out = weight(ab_w_out, "ab_out", j)
            u, gate, q, k, v = _inproj0(x, mod, i, None, w_in, rope_tabs, TM_LATENT)
            uc, gatec, qc, kc, vc = _inproj0(ctx, mod, i, ctx_row, w_in, None, TM_CONTEXT)
            w_gate = (0.5 * jnp.concatenate([lru_w_a[j, 0], lru_w_x[j, 0], lru_w_a[j, 1], lru_w_x[j, 1]],
                                            axis=-1)).astype(BF16)
            hb = lambda b: b.reshape(LRU_HEADS, 1, LRU_BLOCK)
            b_gate = 0.5 * jnp.concatenate([hb(lru_b_a[j, 0]), hb(lru_b_x[j, 0]), hb(lru_b_a[j, 1]),
                                            hb(lru_b_x[j, 1])], axis=-1)
            lam = jnp.concatenate([hb(lru_lam[j, 0]), hb(lru_lam[j, 1])], axis=-1)
            ya, yac, *done = _rglru(u, gate, uc, gatec, lru_conv_w[j], row2(lru_conv_b[j]), w_gate, b_gate, lam,
                                    casts=[] if ("w1", i) in side else [(ffn_w1, i)])
            side.update({("w1", i): w for w in done})
            yb, *done = _window_attention(q, k, v, kc, vc, att_sink[j],
                                          casts=[] if ("w2", i) in side else [(ffn_w2, i)])
            side.update({("w2", i): w for w in done})
            x = _outproj0(ya, yb, x, mod, i, None, w_out, row2(ln1_g[i]), row2(ln1_b[i]), TM_LATENT)
            if not last:
                ybc = _context_attention(qc, kc, vc, att_sink[j])
                ctx = _outproj0(yac, ybc, ctx, mod, i, ctx_row, w_out, row2(ln1_g[i]), row2(ln1_b[i]), TM_CONTEXT)
        else:
            assert last
            w_in = weight(cd_w_in, "cd_in", j)
            w_out = weight(cd_w_out, "cd_out", j)
            u, u_flat, gb, p = _inproj1(x, mod, i, None, w_in, TM_LATENT, True)
            (uc_flat,) = _inproj1(ctx, mod, i, ctx_row, w_in, TM_CONTEXT, False)
            wst, a16, wbig = _s5_operators(s5_log_dt[j], s5_a_re[j], s5_a_im[j], s5_b_re[j], s5_b_im[j],
                                           s5_c_re[j], s5_c_im[j])
            h_in = _s5_states(uc_flat, u_flat, wst, a16)
            y = _s5_readout(u_flat, h_in, wbig)
            x = _outproj1(y, u, gb, p, x, mod, i, None, row2(s5_d[j]), s5_w_glu[j].astype(BF16), row2(s5_b_glu[j]),
                          sc_conv_w[j], row2(sc_conv_b[j]), w_out, row2(ln1_g[i]), row2(ln1_b[i]), TM_LATENT)
        w1 = weight(ffn_w1, "w1", i)
        w2 = weight(ffn_w2, "w2", i)
        ahead = []
        if not last:
            nj = (i + 1) // 2
            ahead = [("w1", i + 1, ffn_w1), ("w2", i + 1, ffn_w2)]
            ahead += ([("ab_in", nj, ab_w_in), ("ab_out", nj, ab_w_out)] if (i + 1) % 2 == 0
                      else [("cd_in", nj, cd_w_in), ("cd_out", nj, cd_w_out)])
        x, *done = _ffn(x, mod, i, None, w1, row2(ffn_b1[i]), w2, row2(ffn_b2[i]), row2(ln2_g[i]), row2(ln2_b[i]),
                        TM_LATENT, FFN_TH, casts=[(stack, idx) for _, idx, stack in ahead])
        side.update({(name, idx): w for (name, idx, _), w in zip(ahead, done)})
        if not last:
            ctx, = _ffn(ctx, mod, i, ctx_row, w1, row2(ffn_b1[i]), w2, row2(ffn_b2[i]), row2(ln2_g[i]),
                        row2(ln2_b[i]), TM_CONTEXT, FFN_TH)
    return x
```

```python
import functools
import math

import jax
import jax.numpy as jnp
import numpy as np
from jax import lax
from jax.experimental import pallas as pl
from jax.experimental.pallas import tpu as pltpu

F32 = jnp.float32
BF16 = jnp.bfloat16

D_MODEL = 2048
DEPTH = 2
GRID_W = 64
LRU_WIDTH = D_MODEL // 2
LRU_HEADS = 8
LRU_BLOCK = LRU_WIDTH // LRU_HEADS
LRU_CONV = 4
LRU_CONV_LEFT = 2
LRU_C = 8.0
ATT_HEAD_DIM = 128
ATT_Q_HEADS = (D_MODEL // 2) // ATT_HEAD_DIM
ATT_KV_HEADS = 2
ATT_GROUP = ATT_Q_HEADS // ATT_KV_HEADS
ATT_Q_WIDTH = ATT_Q_HEADS * ATT_HEAD_DIM
ATT_KV_WIDTH = ATT_KV_HEADS * ATT_HEAD_DIM
WINDOW = 128
ATT_BLOCK = 128
ROPE_BASE = 10000.0
ROPE_FREQS = ATT_HEAD_DIM // 4
S5_WIDTH = D_MODEL // 4
S5_GROUP = 16
S5_GROUPS = S5_WIDTH // S5_GROUP
S5_STATE = 64
SC_WIDTH = D_MODEL - S5_WIDTH
SC_CONV = 3
FFN_HIDDEN = 4 * D_MODEL
ALPHA = (2.0 * DEPTH) ** 0.25
LN_EPS = 1e-5
NEG_INF = -1e30
LOG2E = math.log2(math.e)

LANES = 128
SUBLANES = 8
V7X_VMEM_BYTES = 64 * 1024 * 1024
V7X_VMEM_BUDGET = 56 * 1024 * 1024

S5_CHUNK = 16
S5_SLAB_GROUPS = LANES // S5_GROUP
S5_SLABS = S5_WIDTH // LANES
S5_SLAB_STATE = S5_SLAB_GROUPS * S5_STATE
S5_XK = S5_CHUNK * LANES
S5_HK = 4 * S5_SLAB_STATE
S5_POW_ROWS = 64


def _vmem_limit(nbytes):
    return int(min(V7X_VMEM_BUDGET, max(nbytes * 3 // 2, 16 * 1024 * 1024)))


def _resident(shape, index_map):
    return pl.BlockSpec(shape, index_map, pipeline_mode=pl.Buffered(1))


def _params(sem, nbytes):
    return pltpu.CompilerParams(dimension_semantics=sem, vmem_limit_bytes=_vmem_limit(nbytes))


def _dot(a, b):
    return jnp.dot(a, b, preferred_element_type=F32)


def _dot_nt(a, b):
    return lax.dot_general(a, b, (((1,), (1,)), ((), ())), preferred_element_type=F32)


def _layer_norm(v, g, b):
    mu = jnp.mean(v, axis=-1, keepdims=True)
    c = v - mu
    var = jnp.mean(c * c, axis=-1, keepdims=True)
    return c * lax.rsqrt(var + LN_EPS) * g + b


def _gelu_tanh(x):
    return 0.5 * x * (1.0 + jnp.tanh(math.sqrt(2.0 / math.pi) * (x + 0.044715 * (x * x * x))))


def _sigmoid(x):
    return 0.5 * (1.0 + jnp.tanh(0.5 * x))


MOD_SH1, MOD_SC1, MOD_G1, MOD_SH2, MOD_SC2, MOD_G2 = range(6)


def _mod_spec(mod, layer, chunk):
    _, rows, width = mod.shape
    return pl.BlockSpec((1, rows, width // 6), lambda *_: (layer, 0, chunk))


def _mod_row(ref, ctx_row):
    row = pl.program_id(0) if ctx_row is None else ctx_row
    return ref[0, pl.ds(row, 1), :]


def _modulate(x, sc_ref, sh_ref, ctx_row):
    return (x * (1.0 + _mod_row(sc_ref, ctx_row)) + _mod_row(sh_ref, ctx_row)).astype(BF16)


CAST_BLOCK_BYTES = 8 * 1024 * 1024


def _cast_kernel(w_ref, o_ref):
    o_ref[...] = w_ref[0].astype(BF16)


def _to_bf16(w, layer):
    _, rows, cols = w.shape
    tr = rows
    while tr * cols * 4 > CAST_BLOCK_BYTES and tr % 2 == 0 and tr // 2 >= 2 * SUBLANES:
        tr //= 2
    return pl.pallas_call(
        _cast_kernel,
        grid=(rows // tr,),
        in_specs=[pl.BlockSpec((1, tr, cols), lambda i: (layer, i, 0))],
        out_specs=pl.BlockSpec((tr, cols), lambda i: (i, 0)),
        out_shape=jax.ShapeDtypeStruct((rows, cols), BF16),
        compiler_params=_params(("arbitrary",), 2 * tr * cols * 6),
        name="weight_cast",
    )(w)


CAST_ROWS = 2 * SUBLANES


def _hosted_call(body, *, grid, in_specs, out_specs, out_shape, args, casts=(), **kwargs):
    steps = math.prod(grid)

    def flat(*ids):
        step = 0
        for extent, i in zip(grid, ids):
            step = step * extent + i
        return step

    cast_in, cast_out, cast_shape = [], [], []
    for w, layer in casts:
        _, rows, cols = w.shape
        rps = max(CAST_ROWS, rows // steps)
        used = rows // rps
        assert rows % rps == 0 and used <= steps
        cast_in.append(pl.BlockSpec(
            (1, rps, cols), lambda *ids, used=used, layer=layer: (layer, jnp.minimum(flat(*ids), used - 1), 0)))
        cast_out.append(pl.BlockSpec((rps, cols), lambda *ids, used=used: (jnp.minimum(flat(*ids), used - 1), 0)))
        cast_shape.append(jax.ShapeDtypeStruct((rows, cols), BF16))
    n_in, n_out, n_cast = len(in_specs), len(out_specs), len(casts)

    def kernel(*refs):
        ins = refs[:n_in]
        w_ins = refs[n_in:n_in + n_cast]
        outs = refs[n_in + n_cast:n_in + n_cast + n_out]
        w_outs = refs[n_in + n_cast + n_out:n_in + 2 * n_cast + n_out]
        for w_in, w_out in zip(w_ins, w_outs):
            w_out[...] = w_in[0].astype(BF16)
        body(*ins, *outs, *refs[n_in + 2 * n_cast + n_out:])

    return pl.pallas_call(
        kernel, grid=grid, in_specs=list(in_specs) + cast_in, out_specs=list(out_specs) + cast_out,
        out_shape=list(out_shape) + cast_shape, **kwargs,
    )(*args, *[w for w, _ in casts])


def _mod_kernel(cc_ref, w_ref, b_ref, o_ref):
    a = cc_ref[...]
    a = (a * _sigmoid(a)).astype(BF16)
    o_ref[0] = _dot(a, w_ref[0].astype(BF16)) + b_ref[0]


def _modulation(cc, mod_w, mod_b):
    depth, d, n = mod_w.shape
    tn = 1024
    rows = cc.shape[0]
    return pl.pallas_call(
        _mod_kernel,
        grid=(depth, n // tn),
        in_specs=[pl.BlockSpec((rows, d), lambda l, j: (0, 0)),
                  pl.BlockSpec((1, d, tn), lambda l, j: (l, 0, j)),
                  pl.BlockSpec((1, 1, tn), lambda l, j: (l, 0, j))],
        out_specs=pl.BlockSpec((1, rows, tn), lambda l, j: (l, 0, j)),
        out_shape=jax.ShapeDtypeStruct((depth, rows, n), F32),
        compiler_params=_params(("arbitrary", "arbitrary"), 2 * d * tn * 4 + d * tn * 2),
        name="modulation",
    )(cc, mod_w, mod_b.reshape(depth, 1, n))


def _rope(x, cos, sin_signed, heads):
    lane = lax.broadcasted_iota(jnp.int32, (x.shape[0], ATT_HEAD_DIM), 1)
    first = (lane % (2 * ROPE_FREQS)) < ROPE_FREQS
    out = []
    for h in range(heads):
        xs = x[:, h * ATT_HEAD_DIM:(h + 1) * ATT_HEAD_DIM]
        swapped = jnp.where(first, pltpu.roll(xs, ATT_HEAD_DIM - ROPE_FREQS, 1), pltpu.roll(xs, ROPE_FREQS, 1))
        out.append(xs * cos + swapped * sin_signed)
    return out


def _inproj0_kernel(*refs, rope, ctx_row):
    if rope:
        (x_ref, sc_ref, sh_ref, cos_ref, sin_ref, wu_ref, wg_ref, wq_ref, wk_ref, wv_ref,
         u_ref, g_ref, q_ref, k_ref, v_ref) = refs
    else:
        (x_ref, sc_ref, sh_ref, wu_ref, wg_ref, wq_ref, wk_ref, wv_ref,
         u_ref, g_ref, q_ref, k_ref, v_ref) = refs
    h = _modulate(x_ref[0], sc_ref, sh_ref, ctx_row)
    u = _dot(h, wu_ref[...])
    g = _dot(h, wg_ref[...])
    for hd in range(LRU_HEADS):
        u_ref[0, hd] = u[:, hd * LRU_BLOCK:(hd + 1) * LRU_BLOCK].astype(BF16)
        g_ref[0, hd] = g[:, hd * LRU_BLOCK:(hd + 1) * LRU_BLOCK].astype(BF16)
    q = _dot(h, wq_ref[...]) * (ATT_HEAD_DIM ** -0.5 * LOG2E)
    k = _dot(h, wk_ref[...])
    if rope:
        cos = cos_ref[...]
        sin = sin_ref[...]
        for hd, piece in enumerate(_rope(q, cos, sin, ATT_Q_HEADS)):
            q_ref[0, :, hd * ATT_HEAD_DIM:(hd + 1) * ATT_HEAD_DIM] = piece.astype(BF16)
        for hd, piece in enumerate(_rope(k, cos, sin, ATT_KV_HEADS)):
            k_ref[0, :, hd * ATT_HEAD_DIM:(hd + 1) * ATT_HEAD_DIM] = piece.astype(BF16)
    else:
        q_ref[0] = q.astype(BF16)
        k_ref[0] = k.astype(BF16)
    v_ref[0] = _dot(h, wv_ref[...]).astype(BF16)


def _inproj0(x, mod, layer, ctx_row, w_in, rope_tabs, tm):
    bsz, n, d = x.shape
    rope = rope_tabs is not None
    tok = lambda b, i: (b, i, 0)
    in_specs = [pl.BlockSpec((1, tm, d), tok), _mod_spec(mod, layer, MOD_SC1), _mod_spec(mod, layer, MOD_SH1)]
    args = [x, mod, mod]
    if rope:
        in_specs += [pl.BlockSpec((tm, ATT_HEAD_DIM), lambda b, i: (i, 0))] * 2
        args += list(rope_tabs)
    kv_at = (2 * LRU_WIDTH + ATT_Q_WIDTH) // ATT_KV_WIDTH
    in_specs += [_resident((d, LRU_WIDTH), lambda b, i: (0, 0)), _resident((d, LRU_WIDTH), lambda b, i: (0, 1)),
                 _resident((d, ATT_Q_WIDTH), lambda b, i: (0, 2)),
                 _resident((d, ATT_KV_WIDTH), lambda b, i: (0, kv_at)),
                 _resident((d, ATT_KV_WIDTH), lambda b, i: (0, kv_at + 1))]
    args += [w_in] * 5
    head_major = jax.ShapeDtypeStruct((bsz, LRU_HEADS, n, LRU_BLOCK), BF16)
    head_spec = pl.BlockSpec((1, LRU_HEADS, tm, LRU_BLOCK), lambda b, i: (b, 0, i, 0))
    nbytes = 2 * tm * d * 4 + 2 * w_in.size + 2 * 2 * tm * w_in.shape[1] + tm * w_in.shape[1] * 4
    return pl.pallas_call(
        functools.partial(_inproj0_kernel, rope=rope, ctx_row=ctx_row),
        grid=(bsz, n // tm),
        in_specs=in_specs,
        out_specs=[head_spec, head_spec,
                   pl.BlockSpec((1, tm, ATT_Q_WIDTH), tok),
                   pl.BlockSpec((1, tm, ATT_KV_WIDTH), tok),
                   pl.BlockSpec((1, tm, ATT_KV_WIDTH), tok)],
        out_shape=[head_major, head_major,
                   jax.ShapeDtypeStruct((bsz, n, ATT_Q_WIDTH), BF16),
                   jax.ShapeDtypeStruct((bsz, n, ATT_KV_WIDTH), BF16),
                   jax.ShapeDtypeStruct((bsz, n, ATT_KV_WIDTH), BF16)],
        compiler_params=_params(("arbitrary", "arbitrary"), nbytes),
        name="inproj0_rope" if rope else "inproj0_ctx",
    )(*args)


LRU_PAD = SUBLANES
LRU_TILE = 256


def _scan8(a, b, row, reverse):
    for k in (1, 2, 4):
        if reverse:
            keep = row < SUBLANES - k
            shift = SUBLANES - k
        else:
            keep = row >= k
            shift = k
        a_sh = jnp.where(keep, pltpu.roll(a, shift, 0), 1.0)
        b_sh = jnp.where(keep, pltpu.roll(b, shift, 0), 0.0)
        b = a * b_sh + b
        a = a * a_sh
    return a, b


def _scan8_rows(a, b, row, reverse):
    out_a, out_b = [], []
    for g in range(a.shape[0] // SUBLANES):
        sl = slice(g * SUBLANES, (g + 1) * SUBLANES)
        ag, bg = _scan8(a[sl], b[sl], row, reverse)
        out_a.append(ag)
        out_b.append(bg)
    return jnp.concatenate(out_a, axis=0), jnp.concatenate(out_b, axis=0)


def _lru_kernel(ul_ref, gl_ref, uc_ref, gc_ref, cw_ref, cb_ref, wg_ref, bg_ref, lam_ref,
                yl_ref, yc_ref, upad, af_s, bf_s, ar_s, br_s, cin_s, *, n_lat, n_ctx):
    cw = cw_ref[...]
    cb = cb_ref[...]
    wg = wg_ref[0]
    bg = bg_ref[0]
    lam = lam_ref[0]
    neg = -lam
    softplus = jnp.maximum(neg, 0.0) + jnp.log1p(jnp.exp(-jnp.abs(neg)))
    half_rate = (-0.5 * LRU_C) * softplus
    row = lax.broadcasted_iota(jnp.int32, (SUBLANES, LRU_BLOCK), 0)
    chains = ((af_s, bf_s, False), (ar_s, br_s, True))

    def coefficients(src_ref, n_rows, row_off):
        upad[pl.ds(0, LRU_PAD), :] = jnp.zeros((LRU_PAD, LRU_BLOCK), F32)
        upad[pl.ds(LRU_PAD, n_rows), :] = src_ref[0, 0].astype(F32)
        upad[pl.ds(LRU_PAD + n_rows, LRU_PAD), :] = jnp.zeros((LRU_PAD, LRU_BLOCK), F32)

        def tile(i, carry):
            t0 = pl.multiple_of(i * LRU_TILE, LRU_TILE)
            xp = upad[pl.ds(t0, LRU_TILE + 2 * LRU_PAD), :]
            conv = cb
            for k in range(LRU_CONV):
                o = LRU_PAD - LRU_CONV_LEFT + k
                conv = conv + cw[k:k + 1, :] * xp[o:o + LRU_TILE, :]
            z = _dot(conv.astype(BF16), wg) + bg
            half_conv = 0.5 * conv
            for d, (a_ref, b_ref, reverse) in enumerate(chains):
                t_a = jnp.tanh(z[:, (2 * d) * LRU_BLOCK:(2 * d + 1) * LRU_BLOCK])
                t_x = jnp.tanh(z[:, (2 * d + 1) * LRU_BLOCK:(2 * d + 2) * LRU_BLOCK])
                rate = half_rate[:, d * LRU_BLOCK:(d + 1) * LRU_BLOCK]
                a = jnp.exp(rate + rate * t_a)
                gated = half_conv + half_conv * t_x
                b = jnp.sqrt(1.0 - a * a) * gated
                a, b = _scan8_rows(a, b, row, reverse)
                a_ref[pl.ds(row_off + t0, LRU_TILE), :] = a
                b_ref[pl.ds(row_off + t0, LRU_TILE), :] = b
            return carry

        lax.fori_loop(0, n_rows // LRU_TILE, tile, 0)

    coefficients(uc_ref, n_ctx, 0)
    coefficients(ul_ref, n_lat, n_ctx)

    groups_ctx = n_ctx // SUBLANES
    groups = (n_ctx + n_lat) // SUBLANES
    fa = af_s[pl.ds(SUBLANES - 1, groups, stride=SUBLANES), :]
    fb = bf_s[pl.ds(SUBLANES - 1, groups, stride=SUBLANES), :]
    ra = ar_s[pl.ds(0, groups, stride=SUBLANES), :]
    rb = br_s[pl.ds(0, groups, stride=SUBLANES), :]
    zero = jnp.zeros((SUBLANES, LRU_BLOCK), F32)
    carry = zero
    for v in range(groups // SUBLANES):
        sl = slice(v * SUBLANES, (v + 1) * SUBLANES)
        a, b = _scan8(fa[sl], fb[sl], row, False)
        incl = b + a * carry
        cin_s[0, sl, :] = jnp.where(row == 0, carry, pltpu.roll(incl, 1, 0))
        carry = jnp.broadcast_to(incl[SUBLANES - 1:SUBLANES, :], incl.shape)
    carry = zero
    order = list(range(groups_ctx // SUBLANES - 1, -1, -1)) + list(range(groups // SUBLANES - 1,
                                                                        groups_ctx // SUBLANES - 1, -1))
    for v in order:
        sl = slice(v * SUBLANES, (v + 1) * SUBLANES)
        a, b = _scan8(ra[sl], rb[sl], row, True)
        incl = b + a * carry
        cin_s[1, sl, :] = jnp.where(row == SUBLANES - 1, carry, pltpu.roll(incl, SUBLANES - 1, 0))
        carry = jnp.broadcast_to(incl[0:1, :], incl.shape)

    def emit(g_ref, y_ref, n_rows, row_off):
        tile_groups = LRU_TILE // SUBLANES

        def tile(i, carry):
            t0 = pl.multiple_of(i * LRU_TILE, LRU_TILE)
            g0 = pl.multiple_of(row_off // SUBLANES + i * tile_groups, tile_groups)
            rows = pl.ds(row_off + t0, LRU_TILE)
            a_f, b_f, a_r, b_r = af_s[rows, :], bf_s[rows, :], ar_s[rows, :], br_s[rows, :]
            pieces = []
            for g in range(tile_groups):
                sl = slice(g * SUBLANES, (g + 1) * SUBLANES)
                c_f = jnp.broadcast_to(cin_s[0, pl.ds(g0 + g, 1), :], (SUBLANES, LRU_BLOCK))
                c_r = jnp.broadcast_to(cin_s[1, pl.ds(g0 + g, 1), :], (SUBLANES, LRU_BLOCK))
                pieces.append((b_f[sl] + a_f[sl] * c_f) + (b_r[sl] + a_r[sl] * c_r))
            h = jnp.concatenate(pieces, axis=0)
            gate = g_ref[0, 0, pl.ds(t0, LRU_TILE), :].astype(F32)
            y_ref[0, 0, pl.ds(t0, LRU_TILE), :] = (h * _gelu_tanh(gate)).astype(BF16)
            return carry

        lax.fori_loop(0, n_rows // LRU_TILE, tile, 0)

    emit(gc_ref, yc_ref, n_ctx, 0)
    emit(gl_ref, yl_ref, n_lat, n_ctx)


def _rglru(u_lat, g_lat, u_ctx, g_ctx, conv_w, conv_b, w_gate, b_gate, lam, casts=()):
    bsz, heads, n_lat, blk = u_lat.shape
    n_ctx = u_ctx.shape[2]
    total = n_lat + n_ctx
    assert n_ctx % (SUBLANES * SUBLANES) == 0 and n_lat % LRU_TILE == 0 and n_ctx % LRU_TILE == 0
    seq = lambda n: pl.BlockSpec((1, 1, n, blk), lambda b, h: (b, h, 0, 0))
    per_head = lambda shape: pl.BlockSpec((1,) + shape, lambda b, h: (h, 0, 0))
    nbytes =4 * total * blk * 4 + (n_lat + 2 * LRU_PAD) * blk * 4 + 2 * 3 * 2 * total * blk * 2
    return _hosted_call(
        functools.partial(_lru_kernel, n_lat=n_lat, n_ctx=n_ctx),
        casts=casts,
        args=(u_lat, g_lat, u_ctx, g_ctx, conv_w, conv_b, w_gate, b_gate, lam),
        grid=(bsz, heads),
        in_specs=[seq(n_lat), seq(n_lat), seq(n_ctx), seq(n_ctx),
                  pl.BlockSpec((LRU_CONV, blk), lambda b, h: (0, h)),
                  pl.BlockSpec((1, blk), lambda b, h: (0, h)),
                  per_head((blk, 4 * blk)), per_head((1, 4 * blk)), per_head((1, 2 * blk))],
        out_specs=[seq(n_lat), seq(n_ctx)],
        out_shape=[jax.ShapeDtypeStruct(u_lat.shape, BF16), jax.ShapeDtypeStruct(u_ctx.shape, BF16)],
        scratch_shapes=[pltpu.VMEM((n_lat + 2 * LRU_PAD, blk), F32)]
                       + [pltpu.VMEM((total, blk), F32)] * 4
                       + [pltpu.VMEM((2, total // SUBLANES, blk), F32)],
        compiler_params=_params(("arbitrary", "arbitrary"), nbytes),
        name="rglru",
    )


ATT_TQ = 256
ATT_BAND = 3 * ATT_BLOCK


def _dot_tn(a, b):
    return lax.dot_general(a, b, (((0,), (0,)), ((), ())), preferred_element_type=F32)


def _softmax_pv(parts, sink_row):
    dh = ATT_HEAD_DIM
    m = sink_row
    for s, _ in parts:
        m = jnp.maximum(m, jnp.max(s, axis=0, keepdims=True))
    acc = None
    for idx, (s, v) in enumerate(parts):
        p = jnp.exp2(s - m).astype(BF16)
        v_ext = jnp.concatenate([v, jnp.ones(v.shape, BF16)], axis=1)
        if idx == len(parts) - 1:
            pad = 2 * SUBLANES
            first = lax.broadcasted_iota(jnp.int32, (pad, m.shape[1]), 0) == 0
            p_sink = jnp.where(first, jnp.exp2(sink_row - m), 0.0).astype(BF16)
            p = jnp.concatenate([p, p_sink], axis=0)
            v_sink = jnp.concatenate([jnp.zeros((pad, dh), BF16), jnp.ones((pad, dh), BF16)], axis=1)
            v_ext = jnp.concatenate([v_ext, v_sink], axis=0)
        pv = _dot_tn(p, v_ext)
        acc = pv if acc is None else acc + pv
    return acc[:, :dh] / acc[:, dh:]


def _sink_rows(sink, reps):
    return jnp.repeat(sink.reshape(ATT_KV_HEADS, 1, ATT_GROUP), reps, axis=2)


def _attn_kernel(q_ref, k_ref, v_ref, kc_ref, vc_ref, sink_ref, o_ref, *, seq):
    tile = pl.program_id(1)
    blocks = ATT_TQ // ATT_BLOCK
    qcol = lax.broadcasted_iota(jnp.int32, (1, ATT_GROUP * ATT_BLOCK), 1) % ATT_BLOCK
    krow = lax.broadcasted_iota(jnp.int32, (ATT_BAND, 1), 0)
    for i in range(blocks):
        q0 = (tile * blocks + i) * ATT_BLOCK
        start = pl.multiple_of(jnp.clip(q0 - ATT_BLOCK, 0, seq - ATT_BAND), ATT_BLOCK)
        valid = jnp.abs((q0 + qcol) - (start + krow)) <= WINDOW
        for g in range(ATT_KV_HEADS):
            heads = [q_ref[0, i * ATT_BLOCK:(i + 1) * ATT_BLOCK,
                           (g * ATT_GROUP + r) * ATT_HEAD_DIM:(g * ATT_GROUP + r + 1) * ATT_HEAD_DIM]
                     for r in range(ATT_GROUP)]
            qs = jnp.concatenate(heads, axis=0)
            kv = slice(g * ATT_HEAD_DIM, (g + 1) * ATT_HEAD_DIM)
            s_loc = jnp.where(valid, _dot_nt(k_ref[0, pl.ds(start, ATT_BAND), kv], qs), NEG_INF)
            s_ctx = _dot_nt(kc_ref[0, :, kv], qs)
            o = _softmax_pv([(s_loc, v_ref[0, pl.ds(start, ATT_BAND), kv]), (s_ctx, vc_ref[0, :, kv])],
                            sink_ref[g] * LOG2E)
            for r in range(ATT_GROUP):
                col = (g * ATT_GROUP + r) * ATT_HEAD_DIM
                o_ref[0, i * ATT_BLOCK:(i + 1) * ATT_BLOCK, col:col + ATT_HEAD_DIM] = (
                    o[r * ATT_BLOCK:(r + 1) * ATT_BLOCK].astype(BF16))


def _window_attention(q, k, v, kc, vc, sink, casts=()):
    bsz, n, _ = q.shape
    sink_rows = _sink_rows(sink, ATT_BLOCK)
    lc = kc.shape[1]
    whole = lambda rows: pl.BlockSpec((1, rows, ATT_KV_WIDTH), lambda b, i: (b, 0, 0))
    nbytes = 2 * 2 * (2 * ATT_TQ * ATT_Q_WIDTH + 2 * n * ATT_KV_WIDTH + 2 * lc * ATT_KV_WIDTH) + 8 * 1024 * 1024
    return _hosted_call(
        functools.partial(_attn_kernel, seq=n),
        casts=casts,
        args=(q, k, v, kc, vc, sink_rows),
        grid=(bsz, n // ATT_TQ),
        in_specs=[pl.BlockSpec((1, ATT_TQ, ATT_Q_WIDTH), lambda b, i: (b, i, 0)),
                  whole(n), whole(n), whole(lc), whole(lc),
                  pl.BlockSpec(sink_rows.shape, lambda b, i: (0, 0, 0))],
        out_specs=[pl.BlockSpec((1, ATT_TQ, ATT_Q_WIDTH), lambda b, i: (b, i, 0))],
        out_shape=[jax.ShapeDtypeStruct(q.shape, BF16)],
        compiler_params=_params(("arbitrary", "arbitrary"), nbytes),
        name="window_attention",
    )


def _ctx_attn_kernel(q_ref, kc_ref, vc_ref, sink_ref, o_ref):
    lc = q_ref.shape[1]
    for g in range(ATT_KV_HEADS):
        heads = [q_ref[0, :, (g * ATT_GROUP + r) * ATT_HEAD_DIM:(g * ATT_GROUP + r + 1) * ATT_HEAD_DIM]
                 for r in range(ATT_GROUP)]
        qs = jnp.concatenate(heads, axis=0)
        kv = slice(g * ATT_HEAD_DIM, (g + 1) * ATT_HEAD_DIM)
        o = _softmax_pv([(_dot_nt(kc_ref[0, :, kv], qs), vc_ref[0, :, kv])], sink_ref[g] * LOG2E)
        for r in range(ATT_GROUP):
            col = (g * ATT_GROUP + r) * ATT_HEAD_DIM
            o_ref[0, :, col:col + ATT_HEAD_DIM] = o[r * lc:(r + 1) * lc].astype(BF16)


def _context_attention(qc, kc, vc, sink):
    bsz, lc, _ = qc.shape
    sink_rows = _sink_rows(sink, lc)
    kv_spec = pl.BlockSpec((1, lc, ATT_KV_WIDTH), lambda b: (b, 0, 0))
    q_spec = pl.BlockSpec((1, lc, ATT_Q_WIDTH), lambda b: (b, 0, 0))
    return pl.pallas_call(
        _ctx_attn_kernel,
        grid=(bsz,),
        in_specs=[q_spec, kv_spec, kv_spec, pl.BlockSpec(sink_rows.shape, lambda b: (0, 0, 0))],
        out_specs=q_spec,
        out_shape=jax.ShapeDtypeStruct(qc.shape, BF16),
        compiler_params=_params(("arbitrary",), 8 * 1024 * 1024),
        name="context_attention",
    )(qc, kc, vc, sink_rows)


def _outproj0_kernel(ya_ref, yb_ref, x_ref, gate_ref, w_ref, lg_ref, lb_ref, o_ref, *, ctx_row):
    y = jnp.concatenate([ya_ref[0, h] for h in range(LRU_HEADS)] + [yb_ref[0]], axis=1)
    out = _dot(y, w_ref[...])
    o_ref[0] = _layer_norm(ALPHA * x_ref[0] + _mod_row(gate_ref, ctx_row) * out, lg_ref[...], lb_ref[...])


def _outproj0(ya, yb, x, mod, layer, ctx_row, w_out, ln_g, ln_b, tm):
    bsz, n, d = x.shape
    tok = lambda b, i: (b, i, 0)
    const = lambda b, i: (0, 0)
    nbytes = 2 * 2 * tm * d * 4 + w_out.size * 2 + 2 * 2 * tm * d * 2 + 2 * tm * d * 4
    return pl.pallas_call(
        functools.partial(_outproj0_kernel, ctx_row=ctx_row),
        grid=(bsz, n // tm),
        in_specs=[pl.BlockSpec((1, LRU_HEADS, tm, LRU_BLOCK), lambda b, i: (b, 0, i, 0)),
                  pl.BlockSpec((1, tm, ATT_Q_WIDTH), tok),
                  pl.BlockSpec((1, tm, d), tok),
                  _mod_spec(mod, layer, MOD_G1),
                  _resident(w_out.shape, const),
                  pl.BlockSpec((1, d), const), pl.BlockSpec((1, d), const)],
        out_specs=pl.BlockSpec((1, tm, d), tok),
        out_shape=jax.ShapeDtypeStruct(x.shape, F32),
        compiler_params=_params(("arbitrary", "arbitrary"), nbytes),
        name="outproj0_ln",
    )(ya, yb, x, mod, w_out, ln_g, ln_b)


def _ffn_kernel(x_ref, sc_ref, sh_ref, gate_ref, w1_ref, b1_ref, w2_ref, b2_ref, lg_ref, lb_ref,
                o_ref, h_s, acc_s, *, ctx_row):
    j = pl.program_id(2)

    @pl.when(j == 0)
    def _():
        h_s[...] = _modulate(x_ref[0], sc_ref, sh_ref, ctx_row)
        acc_s[...] = jnp.zeros(acc_s.shape, F32)

    a = jnp.maximum(_dot(h_s[...], w1_ref[...]) + b1_ref[...], 0.0)
    acc_s[...] += _dot((a * a).astype(BF16), w2_ref[...])

    @pl.when(j == pl.num_programs(2) - 1)
    def _():
        f = acc_s[...] + b2_ref[...]
        o_ref[0] = _layer_norm(ALPHA * x_ref[0] + _mod_row(gate_ref, ctx_row) * f, lg_ref[...], lb_ref[...])


def _ffn(x, mod, layer, ctx_row, w1, b1, w2, b2, ln_g, ln_b, tm, th, casts=()):
    bsz, n, d = x.shape
    hidden = w1.shape[1]
    tok = lambda b, i, j: (b, i, 0)
    const = lambda b, i, j: (0, 0)
    nbytes = 2 * 2 * tm * d * 4 + 2 * 2 * 2 * d * th * 2 + tm * d * (2 + 4) + 2 * tm * th * 4 + tm * d * 4
    return _hosted_call(
        functools.partial(_ffn_kernel, ctx_row=ctx_row),
        casts=casts,
        args=(x, mod, mod, mod, w1, b1, w2, b2, ln_g, ln_b),
        grid=(bsz, n // tm, hidden // th),
        in_specs=[pl.BlockSpec((1, tm, d), tok),
                  _mod_spec(mod, layer, MOD_SC2), _mod_spec(mod, layer, MOD_SH2), _mod_spec(mod, layer, MOD_G2),
                  pl.BlockSpec((d, th), lambda b, i, j: (0, j)),
                  pl.BlockSpec((1, th), lambda b, i, j: (0, j)),
                  pl.BlockSpec((th, d), lambda b, i, j: (j, 0)),
                  pl.BlockSpec((1, d), const), pl.BlockSpec((1, d), const), pl.BlockSpec((1, d), const)],
        out_specs=[pl.BlockSpec((1, tm, d), tok)],
        out_shape=[jax.ShapeDtypeStruct(x.shape, F32)],
        scratch_shapes=[pltpu.VMEM((tm, d), BF16), pltpu.VMEM((tm, d), F32)],
        compiler_params=_params(("arbitrary", "arbitrary", "arbitrary"), nbytes),
        name="ffn_ln",
    )


SC_PARTS = SC_WIDTH // S5_WIDTH


def _inproj1_kernel(*refs, full, ctx_row):
    x_ref, sc_ref, sh_ref, wu_ref = refs[:4]
    u_s = refs[-1]
    h = _modulate(x_ref[0], sc_ref, sh_ref, ctx_row)
    u = _dot(h, wu_ref[...])
    if full:
        wb_refs = refs[4:4 + SC_PARTS]
        wc_refs = refs[4 + SC_PARTS:4 + 2 * SC_PARTS]
        wx_refs = refs[4 + 2 * SC_PARTS:4 + 3 * SC_PARTS]
        u_ref, uflat_ref, gb_ref, p_ref = refs[4 + 3 * SC_PARTS:-1]
        u_ref[0] = u.astype(BF16)
        for c in range(SC_PARTS):
            cols = slice(c * S5_WIDTH, (c + 1) * S5_WIDTH)
            gb_ref[0, :, cols] = _dot(h, wb_refs[c][...]).astype(BF16)
            p_ref[0, :, cols] = (_dot(h, wc_refs[c][...]) * _dot(h, wx_refs[c][...])).astype(BF16)
    else:
        uflat_ref, = refs[4:-1]
    for k in range(S5_SLABS):
        u_s[k] = u[:, k * LANES:(k + 1) * LANES]
    chunks = u.shape[0] // S5_CHUNK
    for t in range(S5_CHUNK):
        for k in range(S5_SLABS):
            col = t * S5_WIDTH + k * LANES
            uflat_ref[0, :, col:col + LANES] = u_s[k, pl.ds(t, chunks, stride=S5_CHUNK), :].astype(BF16)


def _inproj1(x, mod, layer, ctx_row, w_in, tm, full):
    bsz, n, d = x.shape
    tok = lambda b, i: (b, i, 0)
    n_blocks = 1 + 3 * SC_PARTS if full else 1
    flat_spec = pl.BlockSpec((1, tm // S5_CHUNK, S5_CHUNK * S5_WIDTH), tok)
    flat_shape = jax.ShapeDtypeStruct((bsz, n // S5_CHUNK, S5_CHUNK * S5_WIDTH), BF16)
    token = lambda w: (pl.BlockSpec((1, tm, w), tok), jax.ShapeDtypeStruct((bsz, n, w), BF16))
    outs = ([token(S5_WIDTH), (flat_spec, flat_shape), token(SC_WIDTH), token(SC_WIDTH)] if full
            else [(flat_spec, flat_shape)])
    ncols = n_blocks * S5_WIDTH
    nbytes = 2 * tm * d * 4 + 2 * d * ncols + 2 * 2 * tm * (ncols + S5_WIDTH) + tm * ncols * 4 + tm * S5_WIDTH * 4
    return pl.pallas_call(
        functools.partial(_inproj1_kernel, full=full, ctx_row=ctx_row),
        grid=(bsz, n // tm),
        in_specs=[pl.BlockSpec((1, tm, d), tok), _mod_spec(mod, layer, MOD_SC1), _mod_spec(mod, layer, MOD_SH1)]
                 + [_resident((d, S5_WIDTH), lambda b, i, c=c: (0, c)) for c in range(n_blocks)],
        out_specs=[spec for spec, _ in outs],
        out_shape=[shape for _, shape in outs],
        scratch_shapes=[pltpu.VMEM((S5_SLABS, tm, LANES), F32)],
        compiler_params=_params(("arbitrary", "arbitrary"), nbytes),
        name="inproj1" if full else "inproj1_ctx",
    )(x, mod, mod, *([w_in] * n_blocks))


def _s5_power_table(par_ref):
    rows = 2 * S5_POW_ROWS
    r = lax.broadcasted_iota(jnp.int32, (rows, S5_SLAB_STATE), 0)
    first = r < S5_POW_ROWS
    lag = (r % S5_POW_ROWS).astype(F32)
    pick = lambda i: jnp.where(first, par_ref[0, 0, i:i + 1, :], par_ref[0, 1, i:i + 1, :])
    dt = jnp.exp(pick(0))
    mag = jnp.exp(lag * dt * pick(1))
    ang = lag * dt * pick(2)
    return mag * jnp.cos(ang), mag * jnp.sin(ang)


def _s5_input_matrix(par_ref, bt_ref, d, pw_re, pw_im):
    a_re = par_ref[0, d, 1:2, :]
    a_im = par_ref[0, d, 2:3, :]
    ab_re = pw_re[d * S5_POW_ROWS + 1:d * S5_POW_ROWS + 2, :]
    ab_im = pw_im[d * S5_POW_ROWS + 1:d * S5_POW_ROWS + 2, :]
    den = a_re * a_re + a_im * a_im
    k_re = ((ab_re - 1.0) * a_re + ab_im * a_im) / den
    k_im = (ab_im * a_re - (ab_re - 1.0) * a_im) / den
    b_re = bt_ref[0, d, 0]
    b_im = bt_ref[0, d, 1]
    return k_re * b_re - k_im * b_im, k_re * b_im + k_im * b_re


def _cmul_row(x_re, x_im, p_re, p_im):
    return x_re * p_re - x_im * p_im, x_re * p_im + x_im * p_re


def _s5_state_prep_kernel(par_ref, bt_ref, wst_ref, a16_ref):
    pw_re, pw_im = _s5_power_table(par_ref)
    for d in range(2):
        bb_re, bb_im = _s5_input_matrix(par_ref, bt_ref, d, pw_re, pw_im)
        base = d * S5_POW_ROWS
        for lag in range(S5_CHUNK):
            e_re, e_im = _cmul_row(bb_re, bb_im, pw_re[base + lag:base + lag + 1, :],
                                   pw_im[base + lag:base + lag + 1, :])
            s = S5_CHUNK - 1 - lag if d == 0 else lag
            col = 2 * d * S5_SLAB_STATE
            wst_ref[0, s * LANES:(s + 1) * LANES, col:col + S5_SLAB_STATE] = e_re.astype(BF16)
            wst_ref[0, s * LANES:(s + 1) * LANES, col + S5_SLAB_STATE:col + 2 * S5_SLAB_STATE] = e_im.astype(BF16)
        row16 = base + S5_CHUNK
        a16_ref[0, :, 2 * d * S5_SLAB_STATE:(2 * d + 1) * S5_SLAB_STATE] = jnp.broadcast_to(
            pw_re[row16:row16 + 1, :], (SUBLANES, S5_SLAB_STATE))
        a16_ref[0, :, (2 * d + 1) * S5_SLAB_STATE:(2 * d + 2) * S5_SLAB_STATE] = jnp.broadcast_to(
            pw_im[row16:row16 + 1, :], (SUBLANES, S5_SLAB_STATE))


def _s5_output_prep_kernel(par_ref, bt_ref, c_ref, ct_ref, wbig_ref):
    pw_re, pw_im = _s5_power_table(par_ref)
    kern = []
    for d in range(2):
        bb_re, bb_im = _s5_input_matrix(par_ref, bt_ref, d, pw_re, pw_im)
        c_re = c_ref[0, d, 0].astype(BF16)
        c_im = c_ref[0, d, 1].astype(BF16)
        base = d * S5_POW_ROWS
        per_lag = []
        for lag in range(S5_CHUNK):
            e_re, e_im = _cmul_row(bb_re, bb_im, pw_re[base + lag:base + lag + 1, :],
                                   pw_im[base + lag:base + lag + 1, :])
            per_lag.append(_dot_nt(e_re.astype(BF16), c_re) - _dot_nt(e_im.astype(BF16), c_im))
        kern.append(per_lag)
    for s in range(S5_CHUNK):
        for t in range(S5_CHUNK):
            if s < t:
                blk = kern[0][t - s]
            elif s > t:
                blk = kern[1][s - t]
            else:
                blk = kern[0][0] + kern[1][0]
            wbig_ref[0, s * LANES:(s + 1) * LANES, t * LANES:(t + 1) * LANES] = blk.astype(BF16)
    pt_re = pw_re.T
    pt_im = pw_im.T
    for d in range(2):
        ct_re = ct_ref[0, d, 0]
        ct_im = ct_ref[0, d, 1]
        for t in range(S5_CHUNK):
            lag = t + 1 if d == 0 else S5_CHUNK - t
            col = d * S5_POW_ROWS + lag
            p_re = pt_re[:, col:col + 1]
            p_im = pt_im[:, col:col + 1]
            g_re = ct_re * p_re - ct_im * p_im
            g_im = ct_re * p_im + ct_im * p_re
            r0 = S5_XK + 2 * d * S5_SLAB_STATE
            wbig_ref[0, r0:r0 + S5_SLAB_STATE, t * LANES:(t + 1) * LANES] = g_re.astype(BF16)
            wbig_ref[0, r0 + S5_SLAB_STATE:r0 + 2 * S5_SLAB_STATE, t * LANES:(t + 1) * LANES] = (-g_im).astype(BF16)


def _s5_operators(log_dt, a_re, a_im, b_re, b_im, c_re, c_im):
    gs = S5_SLAB_GROUPS
    eye = jnp.eye(gs, dtype=F32)

    def lanes(v):
        return v.reshape(2, S5_SLABS, gs * S5_STATE)

    par = jnp.stack([lanes(jnp.broadcast_to(log_dt[:, :, None], a_re.shape)), lanes(a_re), lanes(a_im)], axis=2)
    par = jnp.pad(par, ((0, 0), (0, 0), (0, SUBLANES - 3), (0, 0))).transpose(1, 0, 2, 3)

    def embed_bt(b):
        b = b.reshape(2, S5_SLABS, gs, S5_STATE, S5_GROUP)
        e = b.transpose(0, 1, 2, 4, 3)[:, :, :, :, None, :] * eye[None, None, :, None, :, None]
        return e.reshape(2, S5_SLABS, gs * S5_GROUP, gs * S5_STATE).transpose(1, 0, 2, 3)

    def embed_c(c):
        c = c.reshape(2, S5_SLABS, gs, S5_GROUP, S5_STATE)
        e = c[:, :, :, :, None, :] * eye[None, None, :, None, :, None]
        return e.reshape(2, S5_SLABS, gs * S5_GROUP, gs * S5_STATE).transpose(1, 0, 2, 3)

    bt = jnp.stack([embed_bt(b_re), embed_bt(b_im)], axis=2)
    cm = jnp.stack([embed_c(c_re), embed_c(c_im)], axis=2)
    ct = cm.transpose(0, 1, 2, 4, 3)

    slab5 = lambda shape: pl.BlockSpec((1,) + shape, lambda k: (k, 0, 0, 0, 0))
    par_spec = pl.BlockSpec((1, 2, SUBLANES, S5_SLAB_STATE), lambda k: (k, 0, 0, 0))
    bt_spec = slab5((2, 2, LANES, S5_SLAB_STATE))
    wst, a16 = pl.pallas_call(
        _s5_state_prep_kernel,
        grid=(S5_SLABS,),
        in_specs=[par_spec, bt_spec],
        out_specs=[pl.BlockSpec((1, S5_XK, S5_HK), lambda k: (k, 0, 0)),
                   pl.BlockSpec((1, SUBLANES, S5_HK), lambda k: (k, 0, 0))],
        out_shape=[jax.ShapeDtypeStruct((S5_SLABS, S5_XK, S5_HK), BF16),
                   jax.ShapeDtypeStruct((S5_SLABS, SUBLANES, S5_HK), F32)],
        compiler_params=_params(("arbitrary",), 2 * S5_XK * S5_HK * 2 + 8 * 1024 * 1024),
        name="s5_state_operator",
    )(par, bt)
    wbig = pl.pallas_call(
        _s5_output_prep_kernel,
        grid=(S5_SLABS,),
        in_specs=[par_spec, bt_spec, bt_spec, slab5((2, 2, S5_SLAB_STATE, LANES))],
        out_specs=pl.BlockSpec((1, S5_XK + S5_HK, S5_XK), lambda k: (k, 0, 0)),
        out_shape=jax.ShapeDtypeStruct((S5_SLABS, S5_XK + S5_HK, S5_XK), BF16),
        compiler_params=_params(("arbitrary",), 2 * (S5_XK + S5_HK) * S5_XK * 2 + 8 * 1024 * 1024),
        name="s5_output_operator",
    )(par, bt, cm, ct)
    return wst, a16, wbig


def _s5_states_kernel(*refs, n_ctx, n_lat):
    xc_refs = refs[:S5_CHUNK]
    xl_refs = refs[S5_CHUNK:2 * S5_CHUNK]
    wst_ref, a16_ref, h_ref, s_s, h_s = refs[2 * S5_CHUNK:]
    x = jnp.concatenate([jnp.concatenate([r[0] for r in xc_refs], axis=1),
                         jnp.concatenate([r[0] for r in xl_refs], axis=1)], axis=0)
    s_s[...] = _dot(x, wst_ref[0])
    p = S5_SLAB_STATE
    af_re = a16_ref[0, 0:1, 0:p]
    af_im = a16_ref[0, 0:1, p:2 * p]
    ar_re = a16_ref[0, 0:1, 2 * p:3 * p]
    ar_im = a16_ref[0, 0:1, 3 * p:4 * p]
    total = n_ctx + n_lat

    def step(i, carry):
        f_re, f_im, r_re, r_im = carry
        h_s[pl.ds(i, 1), 0:p] = f_re
        h_s[pl.ds(i, 1), p:2 * p] = f_im
        s_re = s_s[pl.ds(i, 1), 0:p]
        s_im = s_s[pl.ds(i, 1), p:2 * p]
        f_re, f_im = af_re * f_re - af_im * f_im + s_re, af_re * f_im + af_im * f_re + s_im
        j = jnp.where(i < n_ctx, n_ctx - 1 - i, total + n_ctx - 1 - i)
        h_s[pl.ds(j, 1), 2 * p:3 * p] = r_re
        h_s[pl.ds(j, 1), 3 * p:4 * p] = r_im
        s_re = s_s[pl.ds(j, 1), 2 * p:3 * p]
        s_im = s_s[pl.ds(j, 1), 3 * p:4 * p]
        r_re, r_im = ar_re * r_re - ar_im * r_im + s_re, ar_re * r_im + ar_im * r_re + s_im
        return f_re, f_im, r_re, r_im

    zero = jnp.zeros((1, p), F32)
    lax.fori_loop(0, total, step, (zero, zero, zero, zero), unroll=4)
    h_ref[0, 0] = h_s[pl.ds(n_ctx, n_lat), :].astype(BF16)


def _s5_states(uc_flat, ul_flat, wst, a16):
    bsz, n_ctx, _ = uc_flat.shape
    n_lat = ul_flat.shape[1]
    piece = lambda rows, s: pl.BlockSpec((1, rows, LANES), lambda k, b, s=s: (b, 0, s * S5_SLABS + k))
    in_specs = ([piece(n_ctx, s) for s in range(S5_CHUNK)] + [piece(n_lat, s) for s in range(S5_CHUNK)]
                + [pl.BlockSpec((1, S5_XK, S5_HK), lambda k, b: (k, 0, 0)),
                   pl.BlockSpec((1, SUBLANES, S5_HK), lambda k, b: (k, 0, 0))])
    total = n_ctx + n_lat
    nbytes = 2 * S5_XK * S5_HK * 2 + 2 * total * S5_XK * 2 + 3 * total * S5_HK * 4 + 2 * n_lat * S5_HK * 2
    return pl.pallas_call(
        functools.partial(_s5_states_kernel, n_ctx=n_ctx, n_lat=n_lat),
        grid=(S5_SLABS, bsz),
        in_specs=in_specs,
        out_specs=pl.BlockSpec((1, 1, n_lat, S5_HK), lambda k, b: (b, k, 0, 0)),
        out_shape=jax.ShapeDtypeStruct((bsz, S5_SLABS, n_lat, S5_HK), BF16),
        scratch_shapes=[pltpu.VMEM((total, S5_HK), F32), pltpu.VMEM((total, S5_HK), F32)],
        compiler_params=_params(("arbitrary", "arbitrary"), nbytes),
        name="s5_states",
    )(*([uc_flat] * S5_CHUNK + [ul_flat] * S5_CHUNK + [wst, a16]))


def _s5_readout_kernel(*refs):
    x_refs = refs[:S5_CHUNK]
    h_ref, w_ref, y_ref = refs[S5_CHUNK:]
    lhs = jnp.concatenate([r[0] for r in x_refs] + [h_ref[0, 0]], axis=1)
    y = _dot(lhs, w_ref[0])
    chunks = y.shape[0]
    for t in range(S5_CHUNK):
        y_ref[0, pl.ds(t, chunks, stride=S5_CHUNK), :] = y[:, t * LANES:(t + 1) * LANES]


def _s5_readout(ul_flat, h_in, wbig):
    bsz, n_lat, _ = ul_flat.shape
    piece = lambda s: pl.BlockSpec((1, n_lat, LANES), lambda k, b, s=s: (b, 0, s * S5_SLABS + k))
    nbytes = 2 * (S5_XK + S5_HK) * S5_XK * 2 + 2 * n_lat * (S5_XK + S5_HK) * 2 * 2 + n_lat * S5_XK * 4 * 3
    return pl.pallas_call(
        _s5_readout_kernel,
        grid=(S5_SLABS, bsz),
        in_specs=[piece(s) for s in range(S5_CHUNK)]
                 + [pl.BlockSpec((1, 1, n_lat, S5_HK), lambda k, b: (b, k, 0, 0)),
                    pl.BlockSpec((1, S5_XK + S5_HK, S5_XK), lambda k, b: (k, 0, 0))],
        out_specs=pl.BlockSpec((1, n_lat * S5_CHUNK, LANES), lambda k, b: (b, 0, k)),
        out_shape=jax.ShapeDtypeStruct((bsz, n_lat * S5_CHUNK, S5_WIDTH), F32),
        compiler_params=_params(("arbitrary", "arbitrary"), nbytes),
        name="s5_readout",
    )(*([ul_flat] * S5_CHUNK + [h_in, wbig]))


SC_HALO = 16


def _outproj1_kernel(y_ref, u_ref, gb_ref, p_ref, pprev_ref, pnext_ref, x_ref, gate_ref,
                     dskip_ref, wglu_ref, bglu_ref, cw_ref, cb_ref, w_ref, lg_ref, lb_ref, o_ref, *, ctx_row):
    i = pl.program_id(1)
    tm = x_ref.shape[1]
    yc = y_ref[0] + dskip_ref[...] * u_ref[0].astype(F32)
    z = _gelu_tanh(yc)
    y_c = z * _sigmoid(_dot(z.astype(BF16), wglu_ref[...]) + bglu_ref[...])
    p = p_ref[0].astype(F32)
    row = lax.broadcasted_iota(jnp.int32, (tm, 1), 0)
    prev_row = jnp.where(i > 0, pprev_ref[0, SC_HALO - 1:SC_HALO, :].astype(F32), 0.0)
    next_row = jnp.where(i < pl.num_programs(1) - 1, pnext_ref[0, 0:1, :].astype(F32), 0.0)
    p_dn = jnp.where(row == 0, prev_row, pltpu.roll(p, 1, 0))
    p_up = jnp.where(row == tm - 1, next_row, pltpu.roll(p, tm - 1, 0))
    conv = cb_ref[...] + cw_ref[0:1, :] * p_dn + cw_ref[1:2, :] * p + cw_ref[2:3, :] * p_up
    y_d = gb_ref[0].astype(F32) * conv
    y = jnp.concatenate([y_c.astype(BF16), y_d.astype(BF16)], axis=1)
    out = _dot(y, w_ref[...])
    o_ref[0] = _layer_norm(ALPHA * x_ref[0] + _mod_row(gate_ref, ctx_row) * out, lg_ref[...], lb_ref[...])


def _outproj1(y, u, gb, p, x, mod, layer, ctx_row, d_skip, w_glu, b_glu, conv_w, conv_b, w_out, ln_g, ln_b, tm):
    bsz, n, d = x.shape
    tok = lambda b, i: (b, i, 0)
    const = lambda b, i: (0, 0)
    per = tm // SC_HALO
    last = n // SC_HALO - 1
    nbytes = (2 * 2 * tm * d * 4 + w_out.size * 2 + 2 * 2 * tm * (2 * S5_WIDTH + 2 * SC_WIDTH) * 2
              + 6 * tm * SC_WIDTH * 4 + 2 * tm * d * 4)
    return pl.pallas_call(
        functools.partial(_outproj1_kernel, ctx_row=ctx_row),
        grid=(bsz, n // tm),
        in_specs=[pl.BlockSpec((1, tm, S5_WIDTH), tok), pl.BlockSpec((1, tm, S5_WIDTH), tok),
                  pl.BlockSpec((1, tm, SC_WIDTH), tok), pl.BlockSpec((1, tm, SC_WIDTH), tok),
                  pl.BlockSpec((1, SC_HALO, SC_WIDTH), lambda b, i: (b, jnp.maximum(i * per - 1, 0), 0)),
                  pl.BlockSpec((1, SC_HALO, SC_WIDTH), lambda b, i: (b, jnp.minimum((i + 1) * per, last), 0)),
                  pl.BlockSpec((1, tm, d), tok), _mod_spec(mod, layer, MOD_G1),
                  pl.BlockSpec((1, S5_WIDTH), const), _resident(w_glu.shape, const),
                  pl.BlockSpec((1, S5_WIDTH), const),
                  pl.BlockSpec((SC_CONV, SC_WIDTH), const), pl.BlockSpec((1, SC_WIDTH), const),
                  _resident(w_out.shape, const),
                  pl.BlockSpec((1, d), const), pl.BlockSpec((1, d), const)],
        out_specs=pl.BlockSpec((1, tm, d), tok),
        out_shape=jax.ShapeDtypeStruct(x.shape, F32),
        compiler_params=_params(("arbitrary", "arbitrary"), nbytes),
        name="outproj1_ln",
    )(y, u, gb, p, p, p, x, mod, d_skip, w_glu, b_glu, conv_w, conv_b, w_out, ln_g, ln_b)


def _rope_tables(n):
    t = np.arange(n)
    inv = ROPE_BASE ** (-np.arange(ROPE_FREQS, dtype=np.float64) / ROPE_FREQS)
    ang_r = (t // GRID_W)[:, None] * inv[None, :]
    ang_c = (t % GRID_W)[:, None] * inv[None, :]
    cos = np.concatenate([np.cos(ang_r), np.cos(ang_r), np.cos(ang_c), np.cos(ang_c)], axis=1)
    sin = np.concatenate([-np.sin(ang_r), np.sin(ang_r), -np.sin(ang_c), np.sin(ang_c)], axis=1)
    return jnp.asarray(cos, F32), jnp.asarray(sin, F32)


TM_LATENT = 512
TM_CONTEXT = 256
FFN_TH = 1024


def kernel(x, c, ctx, c_ctx, mod_w, mod_b, ln1_g, ln1_b, ln2_g, ln2_b, ffn_w1, ffn_b1, ffn_w2, ffn_b2,
           ab_w_in, ab_w_out, lru_conv_w, lru_conv_b, lru_w_a, lru_b_a, lru_w_x, lru_b_x, lru_lam, att_sink,
           cd_w_in, cd_w_out, s5_log_dt, s5_a_re, s5_a_im, s5_b_re, s5_b_im, s5_c_re, s5_c_im, s5_d,
           s5_w_glu, s5_b_glu, sc_conv_w, sc_conv_b):
    bsz, n, d = x.shape
    lc = ctx.shape[1]
    assert n % TM_LATENT == 0 and lc % TM_CONTEXT == 0 and (bsz * lc) % TM_LATENT == 0 and n % GRID_W == 0

    pad_rows = SUBLANES - (bsz + 1) % SUBLANES if (bsz + 1) % SUBLANES else 0
    cc = jnp.concatenate([c, c_ctx[None, :], jnp.zeros((pad_rows, d), F32)], axis=0)
    mod = _modulation(cc, mod_w, mod_b)
    ctx_row = bsz

    row2 = lambda v: v.reshape(1, -1)
    rope_tabs = _rope_tables(n)

    side = {}

    def weight(stack, name, idx):
        return side.pop((name, idx)) if (name, idx) in side else _to_bf16(stack, idx)

    def hosted(call, jobs):
        outs = call(casts=[(stack, idx) for _, idx, stack in jobs])
        own = len(outs) - len(jobs)
        side.update({(name, idx): w for (name, idx, _), w in zip(jobs, outs[own:])})
        return outs[:own]

    for i in range(DEPTH):
        last = i == DEPTH - 1
        j = i // 2
        todo = [("w1", i, ffn_w1), ("w2", i, ffn_w2)]
        if not last:
            nj = (i + 1) // 2
            todo += [("w1", i + 1, ffn_w1), ("w2", i + 1, ffn_w2)]
            todo += ([("ab_in", nj, ab_w_in), ("ab_out", nj, ab_w_out)] if (i + 1) % 2 == 0
                     else [("cd_in", nj, cd_w_in), ("cd_out", nj, cd_w_out)])
        todo = [job for job in todo if (job[0], job[1]) not in side]
        if i % 2 == 0:
            w_in = weight(ab_w_in, "ab_in", j)
            w_out = weight(ab_w_out, "ab_out", j)
            u, gate, q, k, v = _inproj0(x, mod, i, None, w_in, rope_tabs, TM_LATENT)
            uc, gatec, qc, kc, vc = _inproj0(ctx, mod, i, ctx_row, w_in, None, TM_CONTEXT)
            w_gate = (0.5 * jnp.concatenate([lru_w_a[j, 0], lru_w_x[j, 0], lru_w_a[j, 1], lru_w_x[j, 1]],
                                            axis=-1)).astype(BF16)
            hb = lambda b: b.reshape(LRU_HEADS, 1, LRU_BLOCK)
            b_gate = 0.5 * jnp.concatenate([hb(lru_b_a[j, 0]), hb(lru_b_x[j, 0]), hb(lru_b_a[j, 1]),
                                            hb(lru_b_x[j, 1])], axis=-1)
            lam = jnp.concatenate([hb(lru_lam[j, 0]), hb(lru_lam[j, 1])], axis=-1)
            ya, yac = hosted(functools.partial(_rglru, u, gate, uc, gatec, lru_conv_w[j], row2(lru_conv_b[j]),
                                               w_gate, b_gate, lam), todo[0::2])
            yb, = hosted(functools.partial(_window_attention, q, k, v, kc, vc, att_sink[j]), todo[1::2])
            todo = []
            x = _outproj0(ya, yb, x, mod, i, None, w_out, row2(ln1_g[i]), row2(ln1_b[i]), TM_LATENT)
            if not last:
                ybc = _context_attention(qc, kc, vc, att_sink[j])
                ctx = _outproj0(yac, ybc, ctx, mod, i, ctx_row, w_out, row2(ln1_g[i]), row2(ln1_b[i]), TM_CONTEXT)
        else:
            assert last
            w_in = weight(cd_w_in, "cd_in", j)
            w_out = weight(cd_w_out, "cd_out", j)
            u, u_flat, gb, p = _inproj1(x, mod, i, None, w_in, TM_LATENT, True)
            (uc_flat,) = _inproj1(ctx, mod, i, ctx_row, w_in, TM_CONTEXT, False)
            wst, a16, wbig = _s5_operators(s5_log_dt[j], s5_a_re[j], s5_a_im[j], s5_b_re[j], s5_b_im[j],
                                           s5_c_re[j], s5_c_im[j])
            h_in = _s5_states(uc_flat, u_flat, wst, a16)
            y = _s5_readout(u_flat, h_in, wbig)
            x = _outproj1(y, u, gb, p, x, mod, i, None, row2(s5_d[j]), s5_w_glu[j].astype(BF16), row2(s5_b_glu[j]),
                          sc_conv_w[j], row2(sc_conv_b[j]), w_out, row2(ln1_g[i]), row2(ln1_b[i]), TM_LATENT)
        w1 = weight(ffn_w1, "w1", i)
        w2 = weight(ffn_w2, "w2", i)
        ahead = [job for job in todo if job[1] != i or job[0] not in ("w1", "w2")]
        x, = hosted(functools.partial(_ffn, x, mod, i, None, w1, row2(ffn_b1[i]), w2, row2(ffn_b2[i]),
                                      row2(ln2_g[i]), row2(ln2_b[i]), TM_LATENT, FFN_TH), ahead)
        if not last:
            rows = ctx.reshape(1, bsz * lc, d)
            rows, = _ffn(rows, mod, i, ctx_row, w1, row2(ffn_b1[i]), w2, row2(ffn_b2[i]), row2(ln2_g[i]),
                         row2(ln2_b[i]), TM_LATENT, FFN_TH)
            ctx = rows.reshape(bsz, lc, d)
    return x
```

```python
import functools
import math

import jax
import jax.numpy as jnp
import numpy as np
from jax import lax
from jax.experimental import pallas as pl
from jax.experimental.pallas import tpu as pltpu

F32 = jnp.float32
BF16 = jnp.bfloat16

D_MODEL = 2048
DEPTH = 2
GRID_W = 64
LRU_WIDTH = D_MODEL // 2
LRU_HEADS = 8
LRU_BLOCK = LRU_WIDTH // LRU_HEADS
LRU_CONV = 4
LRU_CONV_LEFT = 2
LRU_C = 8.0
ATT_HEAD_DIM = 128
ATT_Q_HEADS = (D_MODEL // 2) // ATT_HEAD_DIM
ATT_KV_HEADS = 2
ATT_GROUP = ATT_Q_HEADS // ATT_KV_HEADS
ATT_Q_WIDTH = ATT_Q_HEADS * ATT_HEAD_DIM
ATT_KV_WIDTH = ATT_KV_HEADS * ATT_HEAD_DIM
WINDOW = 128
ATT_BLOCK = 128
ROPE_BASE = 10000.0
ROPE_FREQS = ATT_HEAD_DIM // 4
S5_WIDTH = D_MODEL // 4
S5_GROUP = 16
S5_GROUPS = S5_WIDTH // S5_GROUP
S5_STATE = 64
SC_WIDTH = D_MODEL - S5_WIDTH
SC_CONV = 3
FFN_HIDDEN = 4 * D_MODEL
ALPHA = (2.0 * DEPTH) ** 0.25
LN_EPS = 1e-5
NEG_INF = -1e30
LOG2E = math.log2(math.e)

LANES = 128
SUBLANES = 8
V7X_VMEM_BYTES = 64 * 1024 * 1024
V7X_VMEM_BUDGET = 62 * 1024 * 1024

S5_CHUNK = 16
S5_SLAB_GROUPS = LANES // S5_GROUP
S5_SLABS = S5_WIDTH // LANES
S5_SLAB_STATE = S5_SLAB_GROUPS * S5_STATE
S5_XK = S5_CHUNK * LANES
S5_HK = 4 * S5_SLAB_STATE
S5_POW_ROWS = 64


def _vmem_limit(nbytes):
    return int(min(V7X_VMEM_BUDGET, max(nbytes * 3 // 2, 16 * 1024 * 1024)))


def _resident(shape, index_map):
    return pl.BlockSpec(shape, index_map, pipeline_mode=pl.Buffered(1))


def _params(sem, nbytes):
    return pltpu.CompilerParams(dimension_semantics=sem, vmem_limit_bytes=_vmem_limit(nbytes))


def _dot(a, b):
    return jnp.dot(a, b, preferred_element_type=F32)


def _dot_nt(a, b):
    return lax.dot_general(a, b, (((1,), (1,)), ((), ())), preferred_element_type=F32)


def _layer_norm(v, g, b):
    mu = jnp.mean(v, axis=-1, keepdims=True)
    c = v - mu
    var = jnp.mean(c * c, axis=-1, keepdims=True)
    return c * lax.rsqrt(var + LN_EPS) * g + b


def _gelu_tanh(x):
    return 0.5 * x * (1.0 + jnp.tanh(math.sqrt(2.0 / math.pi) * (x + 0.044715 * (x * x * x))))


def _sigmoid(x):
    return 0.5 * (1.0 + jnp.tanh(0.5 * x))


MOD_SH1, MOD_SC1, MOD_G1, MOD_SH2, MOD_SC2, MOD_G2 = range(6)


def _mod_spec(mod, layer, chunk):
    _, rows, width = mod.shape
    return pl.BlockSpec((1, rows, width // 6), lambda *_: (layer, 0, chunk))


def _mod_row(ref, ctx_row):
    row = pl.program_id(0) if ctx_row is None else ctx_row
    return ref[0, pl.ds(row, 1), :]


def _modulate(x, sc_ref, sh_ref, ctx_row):
    return (x * (1.0 + _mod_row(sc_ref, ctx_row)) + _mod_row(sh_ref, ctx_row)).astype(BF16)


CAST_BLOCK_BYTES = 8 * 1024 * 1024


def _cast_kernel(w_ref, o_ref):
    o_ref[...] = w_ref[0].astype(BF16)


def _to_bf16(w, layer):
    _, rows, cols = w.shape
    tr = rows
    while tr * cols * 4 > CAST_BLOCK_BYTES and tr % 2 == 0 and tr // 2 >= 2 * SUBLANES:
        tr //= 2
    return pl.pallas_call(
        _cast_kernel,
        grid=(rows // tr,),
        in_specs=[pl.BlockSpec((1, tr, cols), lambda i: (layer, i, 0))],
        out_specs=pl.BlockSpec((tr, cols), lambda i: (i, 0)),
        out_shape=jax.ShapeDtypeStruct((rows, cols), BF16),
        compiler_params=_params(("arbitrary",), 2 * tr * cols * 6),
        name="weight_cast",
    )(w)


CAST_ROWS = 2 * SUBLANES


def _hosted_call(body, *, grid, in_specs, out_specs, out_shape, args, casts=(), **kwargs):
    steps = math.prod(grid)

    def flat(*ids):
        step = 0
        for extent, i in zip(grid, ids):
            step = step * extent + i
        return step

    cast_in, cast_out, cast_shape = [], [], []
    for w, layer in casts:
        _, rows, cols = w.shape
        rps = max(CAST_ROWS, rows // steps)
        used = rows // rps
        assert rows % rps == 0 and used <= steps
        cast_in.append(pl.BlockSpec(
            (1, rps, cols), lambda *ids, used=used, layer=layer: (layer, jnp.minimum(flat(*ids), used - 1), 0)))
        cast_out.append(pl.BlockSpec((rps, cols), lambda *ids, used=used: (jnp.minimum(flat(*ids), used - 1), 0)))
        cast_shape.append(jax.ShapeDtypeStruct((rows, cols), BF16))
    n_in, n_out, n_cast = len(in_specs), len(out_specs), len(casts)

    def kernel(*refs):
        ins = refs[:n_in]
        w_ins = refs[n_in:n_in + n_cast]
        outs = refs[n_in + n_cast:n_in + n_cast + n_out]
        w_outs = refs[n_in + n_cast + n_out:n_in + 2 * n_cast + n_out]
        for w_in, w_out in zip(w_ins, w_outs):
            w_out[...] = w_in[0].astype(BF16)
        body(*ins, *outs, *refs[n_in + 2 * n_cast + n_out:])

    return pl.pallas_call(
        kernel, grid=grid, in_specs=list(in_specs) + cast_in, out_specs=list(out_specs) + cast_out,
        out_shape=list(out_shape) + cast_shape, **kwargs,
    )(*args, *[w for w, _ in casts])


def _mod_kernel(cc_ref, w_ref, b_ref, o_ref):
    a = cc_ref[...]
    a = (a * _sigmoid(a)).astype(BF16)
    o_ref[0] = _dot(a, w_ref[0].astype(BF16)) + b_ref[0]


def _modulation(cc, mod_w, mod_b):
    depth, d, n = mod_w.shape
    tn = 1024
    rows = cc.shape[0]
    return pl.pallas_call(
        _mod_kernel,
        grid=(depth, n // tn),
        in_specs=[pl.BlockSpec((rows, d), lambda l, j: (0, 0)),
                  pl.BlockSpec((1, d, tn), lambda l, j: (l, 0, j)),
                  pl.BlockSpec((1, 1, tn), lambda l, j: (l, 0, j))],
        out_specs=pl.BlockSpec((1, rows, tn), lambda l, j: (l, 0, j)),
        out_shape=jax.ShapeDtypeStruct((depth, rows, n), F32),
        compiler_params=_params(("arbitrary", "arbitrary"), 2 * d * tn * 4 + d * tn * 2),
        name="modulation",
    )(cc, mod_w, mod_b.reshape(depth, 1, n))


def _rope(x, cos, sin_signed, heads):
    lane = lax.broadcasted_iota(jnp.int32, (x.shape[0], ATT_HEAD_DIM), 1)
    first = (lane % (2 * ROPE_FREQS)) < ROPE_FREQS
    out = []
    for h in range(heads):
        xs = x[:, h * ATT_HEAD_DIM:(h + 1) * ATT_HEAD_DIM]
        swapped = jnp.where(first, pltpu.roll(xs, ATT_HEAD_DIM - ROPE_FREQS, 1), pltpu.roll(xs, ROPE_FREQS, 1))
        out.append(xs * cos + swapped * sin_signed)
    return out


def _inproj0_kernel(*refs, rope, ctx_row):
    if rope:
        (x_ref, sc_ref, sh_ref, cos_ref, sin_ref, wu_ref, wg_ref, wq_ref, wk_ref, wv_ref,
         u_ref, g_ref, q_ref, k_ref, v_ref) = refs
    else:
        (x_ref, sc_ref, sh_ref, wu_ref, wg_ref, wq_ref, wk_ref, wv_ref,
         u_ref, g_ref, q_ref, k_ref, v_ref) = refs
    h = _modulate(x_ref[0], sc_ref, sh_ref, ctx_row)
    u = _dot(h, wu_ref[...])
    g = _dot(h, wg_ref[...])
    for hd in range(LRU_HEADS):
        u_ref[0, hd] = u[:, hd * LRU_BLOCK:(hd + 1) * LRU_BLOCK].astype(BF16)
        g_ref[0, hd] = g[:, hd * LRU_BLOCK:(hd + 1) * LRU_BLOCK].astype(BF16)
    q = _dot(h, wq_ref[...]) * (ATT_HEAD_DIM ** -0.5 * LOG2E)
    k = _dot(h, wk_ref[...])
    if rope:
        cos = cos_ref[...]
        sin = sin_ref[...]
        for hd, piece in enumerate(_rope(q, cos, sin, ATT_Q_HEADS)):
            q_ref[0, :, hd * ATT_HEAD_DIM:(hd + 1) * ATT_HEAD_DIM] = piece.astype(BF16)
        for hd, piece in enumerate(_rope(k, cos, sin, ATT_KV_HEADS)):
            k_ref[0, :, hd * ATT_HEAD_DIM:(hd + 1) * ATT_HEAD_DIM] = piece.astype(BF16)
    else:
        q_ref[0] = q.astype(BF16)
        k_ref[0] = k.astype(BF16)
    v_ref[0] = _dot(h, wv_ref[...]).astype(BF16)


def _inproj0(x, mod, layer, ctx_row, w_in, rope_tabs, tm):
    bsz, n, d = x.shape
    rope = rope_tabs is not None
    tok = lambda b, i: (b, i, 0)
    in_specs = [pl.BlockSpec((1, tm, d), tok), _mod_spec(mod, layer, MOD_SC1), _mod_spec(mod, layer, MOD_SH1)]
    args = [x, mod, mod]
    if rope:
        in_specs += [pl.BlockSpec((tm, ATT_HEAD_DIM), lambda b, i: (i, 0))] * 2
        args += list(rope_tabs)
    kv_at = (2 * LRU_WIDTH + ATT_Q_WIDTH) // ATT_KV_WIDTH
    in_specs += [_resident((d, LRU_WIDTH), lambda b, i: (0, 0)), _resident((d, LRU_WIDTH), lambda b, i: (0, 1)),
                 _resident((d, ATT_Q_WIDTH), lambda b, i: (0, 2)),
                 _resident((d, ATT_KV_WIDTH), lambda b, i: (0, kv_at)),
                 _resident((d, ATT_KV_WIDTH), lambda b, i: (0, kv_at + 1))]
    args += [w_in] * 5
    head_major = jax.ShapeDtypeStruct((bsz, LRU_HEADS, n, LRU_BLOCK), BF16)
    head_spec = pl.BlockSpec((1, LRU_HEADS, tm, LRU_BLOCK), lambda b, i: (b, 0, i, 0))
    nbytes = 2 * tm * d * 4 + 2 * w_in.size + 2 * 2 * tm * w_in.shape[1] + tm * w_in.shape[1] * 4
    return pl.pallas_call(
        functools.partial(_inproj0_kernel, rope=rope, ctx_row=ctx_row),
        grid=(bsz, n // tm),
        in_specs=in_specs,
        out_specs=[head_spec, head_spec,
                   pl.BlockSpec((1, tm, ATT_Q_WIDTH), tok),
                   pl.BlockSpec((1, tm, ATT_KV_WIDTH), tok),
                   pl.BlockSpec((1, tm, ATT_KV_WIDTH), tok)],
        out_shape=[head_major, head_major,
                   jax.ShapeDtypeStruct((bsz, n, ATT_Q_WIDTH), BF16),
                   jax.ShapeDtypeStruct((bsz, n, ATT_KV_WIDTH), BF16),
                   jax.ShapeDtypeStruct((bsz, n, ATT_KV_WIDTH), BF16)],
        compiler_params=_params(("arbitrary", "arbitrary"), nbytes),
        name="inproj0_rope" if rope else "inproj0_ctx",
    )(*args)


LRU_PAD = SUBLANES
LRU_TILE = 256


def _scan8(a, b, row, reverse):
    for k in (1, 2, 4):
        if reverse:
            keep = row < SUBLANES - k
            shift = SUBLANES - k
        else:
            keep = row >= k
            shift = k
        a_sh = jnp.where(keep, pltpu.roll(a, shift, 0), 1.0)
        b_sh = jnp.where(keep, pltpu.roll(b, shift, 0), 0.0)
        b = a * b_sh + b
        a = a * a_sh
    return a, b


def _scan8_rows(a, b, row, reverse):
    out_a, out_b = [], []
    for g in range(a.shape[0] // SUBLANES):
        sl = slice(g * SUBLANES, (g + 1) * SUBLANES)
        ag, bg = _scan8(a[sl], b[sl], row, reverse)
        out_a.append(ag)
        out_b.append(bg)
    return jnp.concatenate(out_a, axis=0), jnp.concatenate(out_b, axis=0)


def _lru_kernel(ul_ref, gl_ref, uc_ref, gc_ref, cw_ref, cb_ref, wg_ref, bg_ref, lam_ref,
                yl_ref, yc_ref, upad, af_s, bf_s, ar_s, br_s, cin_s, *, n_lat, n_ctx):
    cw = cw_ref[...]
    cb = cb_ref[...]
    wg = wg_ref[0]
    bg = bg_ref[0]
    lam = lam_ref[0]
    neg = -lam
    softplus = jnp.maximum(neg, 0.0) + jnp.log1p(jnp.exp(-jnp.abs(neg)))
    half_rate = (-0.5 * LRU_C) * softplus
    row = lax.broadcasted_iota(jnp.int32, (SUBLANES, LRU_BLOCK), 0)
    chains = ((af_s, bf_s, False), (ar_s, br_s, True))

    def coefficients(src_ref, n_rows, row_off):
        upad[pl.ds(0, LRU_PAD), :] = jnp.zeros((LRU_PAD, LRU_BLOCK), F32)
        upad[pl.ds(LRU_PAD, n_rows), :] = src_ref[0, 0].astype(F32)
        upad[pl.ds(LRU_PAD + n_rows, LRU_PAD), :] = jnp.zeros((LRU_PAD, LRU_BLOCK), F32)

        def tile(i, carry):
            t0 = pl.multiple_of(i * LRU_TILE, LRU_TILE)
            xp = upad[pl.ds(t0, LRU_TILE + 2 * LRU_PAD), :]
            conv = cb
            for k in range(LRU_CONV):
                o = LRU_PAD - LRU_CONV_LEFT + k
                conv = conv + cw[k:k + 1, :] * xp[o:o + LRU_TILE, :]
            z = _dot(conv.astype(BF16), wg) + bg
            half_conv = 0.5 * conv
            for d, (a_ref, b_ref, reverse) in enumerate(chains):
                t_a = jnp.tanh(z[:, (2 * d) * LRU_BLOCK:(2 * d + 1) * LRU_BLOCK])
                t_x = jnp.tanh(z[:, (2 * d + 1) * LRU_BLOCK:(2 * d + 2) * LRU_BLOCK])
                rate = half_rate[:, d * LRU_BLOCK:(d + 1) * LRU_BLOCK]
                a = jnp.exp(rate + rate * t_a)
                gated = half_conv + half_conv * t_x
                b = jnp.sqrt(1.0 - a * a) * gated
                a, b = _scan8_rows(a, b, row, reverse)
                a_ref[pl.ds(row_off + t0, LRU_TILE), :] = a
                b_ref[pl.ds(row_off + t0, LRU_TILE), :] = b
            return carry

        lax.fori_loop(0, n_rows // LRU_TILE, tile, 0)

    coefficients(uc_ref, n_ctx, 0)
    coefficients(ul_ref, n_lat, n_ctx)

    groups_ctx = n_ctx // SUBLANES
    groups = (n_ctx + n_lat) // SUBLANES
    fa = af_s[pl.ds(SUBLANES - 1, groups, stride=SUBLANES), :]
    fb = bf_s[pl.ds(SUBLANES - 1, groups, stride=SUBLANES), :]
    ra = ar_s[pl.ds(0, groups, stride=SUBLANES), :]
    rb = br_s[pl.ds(0, groups, stride=SUBLANES), :]
    zero = jnp.zeros((SUBLANES, LRU_BLOCK), F32)
    carry = zero
    for v in range(groups // SUBLANES):
        sl = slice(v * SUBLANES, (v + 1) * SUBLANES)
        a, b = _scan8(fa[sl], fb[sl], row, False)
        incl = b + a * carry
        cin_s[0, sl, :] = jnp.where(row == 0, carry, pltpu.roll(incl, 1, 0))
        carry = jnp.broadcast_to(incl[SUBLANES - 1:SUBLANES, :], incl.shape)
    carry = zero
    order = list(range(groups_ctx // SUBLANES - 1, -1, -1)) + list(range(groups // SUBLANES - 1,
                                                                        groups_ctx // SUBLANES - 1, -1))
    for v in order:
        sl = slice(v * SUBLANES, (v + 1) * SUBLANES)
        a, b = _scan8(ra[sl], rb[sl], row, True)
        incl = b + a * carry
        cin_s[1, sl, :] = jnp.where(row == SUBLANES - 1, carry, pltpu.roll(incl, SUBLANES - 1, 0))
        carry = jnp.broadcast_to(incl[0:1, :], incl.shape)

    def emit(g_ref, y_ref, n_rows, row_off):
        tile_groups = LRU_TILE // SUBLANES

        def tile(i, carry):
            t0 = pl.multiple_of(i * LRU_TILE, LRU_TILE)
            g0 = pl.multiple_of(row_off // SUBLANES + i * tile_groups, tile_groups)
            rows = pl.ds(row_off + t0, LRU_TILE)
            a_f, b_f, a_r, b_r = af_s[rows, :], bf_s[rows, :], ar_s[rows, :], br_s[rows, :]
            pieces = []
            for g in range(tile_groups):
                sl = slice(g * SUBLANES, (g + 1) * SUBLANES)
                c_f = jnp.broadcast_to(cin_s[0, pl.ds(g0 + g, 1), :], (SUBLANES, LRU_BLOCK))
                c_r = jnp.broadcast_to(cin_s[1, pl.ds(g0 + g, 1), :], (SUBLANES, LRU_BLOCK))
                pieces.append((b_f[sl] + a_f[sl] * c_f) + (b_r[sl] + a_r[sl] * c_r))
            h = jnp.concatenate(pieces, axis=0)
            gate = g_ref[0, 0, pl.ds(t0, LRU_TILE), :].astype(F32)
            y_ref[0, 0, pl.ds(t0, LRU_TILE), :] = (h * _gelu_tanh(gate)).astype(BF16)
            return carry

        lax.fori_loop(0, n_rows // LRU_TILE, tile, 0)

    emit(gc_ref, yc_ref, n_ctx, 0)
    emit(gl_ref, yl_ref, n_lat, n_ctx)


def _rglru(u_lat, g_lat, u_ctx, g_ctx, conv_w, conv_b, w_gate, b_gate, lam, casts=()):
    bsz, heads, n_lat, blk = u_lat.shape
    n_ctx = u_ctx.shape[2]
    total = n_lat + n_ctx
    assert n_ctx % (SUBLANES * SUBLANES) == 0 and n_lat % LRU_TILE == 0 and n_ctx % LRU_TILE == 0
    seq = lambda n: pl.BlockSpec((1, 1, n, blk), lambda b, h: (b, h, 0, 0))
    per_head = lambda shape: pl.BlockSpec((1,) + shape, lambda b, h: (h, 0, 0))
    nbytes =4 * total * blk * 4 + (n_lat + 2 * LRU_PAD) * blk * 4 + 2 * 3 * 2 * total * blk * 2
    return _hosted_call(
        functools.partial(_lru_kernel, n_lat=n_lat, n_ctx=n_ctx),
        casts=casts,
        args=(u_lat, g_lat, u_ctx, g_ctx, conv_w, conv_b, w_gate, b_gate, lam),
        grid=(bsz, heads),
        in_specs=[seq(n_lat), seq(n_lat), seq(n_ctx), seq(n_ctx),
                  pl.BlockSpec((LRU_CONV, blk), lambda b, h: (0, h)),
                  pl.BlockSpec((1, blk), lambda b, h: (0, h)),
                  per_head((blk, 4 * blk)), per_head((1, 4 * blk)), per_head((1, 2 * blk))],
        out_specs=[seq(n_lat), seq(n_ctx)],
        out_shape=[jax.ShapeDtypeStruct(u_lat.shape, BF16), jax.ShapeDtypeStruct(u_ctx.shape, BF16)],
        scratch_shapes=[pltpu.VMEM((n_lat + 2 * LRU_PAD, blk), F32)]
                       + [pltpu.VMEM((total, blk), F32)] * 4
                       + [pltpu.VMEM((2, total // SUBLANES, blk), F32)],
        compiler_params=_params(("arbitrary", "arbitrary"), nbytes),
        name="rglru",
    )


ATT_TQ = 512
ATT_BAND = 3 * ATT_BLOCK


def _dot_tn(a, b):
    return lax.dot_general(a, b, (((0,), (0,)), ((), ())), preferred_element_type=F32)


def _softmax_pv(parts, sink_row):
    dh = ATT_HEAD_DIM
    m = sink_row
    for s, _ in parts:
        m = jnp.maximum(m, jnp.max(s, axis=0, keepdims=True))
    acc = None
    for idx, (s, v) in enumerate(parts):
        p = jnp.exp2(s - m).astype(BF16)
        v_ext = jnp.concatenate([v, jnp.ones(v.shape, BF16)], axis=1)
        if idx == len(parts) - 1:
            pad = 2 * SUBLANES
            first = lax.broadcasted_iota(jnp.int32, (pad, m.shape[1]), 0) == 0
            p_sink = jnp.where(first, jnp.exp2(sink_row - m), 0.0).astype(BF16)
            p = jnp.concatenate([p, p_sink], axis=0)
            v_sink = jnp.concatenate([jnp.zeros((pad, dh), BF16), jnp.ones((pad, dh), BF16)], axis=1)
            v_ext = jnp.concatenate([v_ext, v_sink], axis=0)
        pv = _dot_tn(p, v_ext)
        acc = pv if acc is None else acc + pv
    return acc[:, :dh] / acc[:, dh:]


def _sink_rows(sink, reps):
    return jnp.repeat(sink.reshape(ATT_KV_HEADS, 1, ATT_GROUP), reps, axis=2)


def _attn_kernel(q_ref, k_ref, v_ref, kc_ref, vc_ref, sink_ref, o_ref, *, seq):
    tile = pl.program_id(1)
    blocks = ATT_TQ // ATT_BLOCK
    qcol = lax.broadcasted_iota(jnp.int32, (1, ATT_GROUP * ATT_BLOCK), 1) % ATT_BLOCK
    krow = lax.broadcasted_iota(jnp.int32, (ATT_BAND, 1), 0)
    for i in range(blocks):
        q0 = (tile * blocks + i) * ATT_BLOCK
        start = pl.multiple_of(jnp.clip(q0 - ATT_BLOCK, 0, seq - ATT_BAND), ATT_BLOCK)
        valid = jnp.abs((q0 + qcol) - (start + krow)) <= WINDOW
        for g in range(ATT_KV_HEADS):
            heads = [q_ref[0, i * ATT_BLOCK:(i + 1) * ATT_BLOCK,
                           (g * ATT_GROUP + r) * ATT_HEAD_DIM:(g * ATT_GROUP + r + 1) * ATT_HEAD_DIM]
                     for r in range(ATT_GROUP)]
            qs = jnp.concatenate(heads, axis=0)
            kv = slice(g * ATT_HEAD_DIM, (g + 1) * ATT_HEAD_DIM)
            s_loc = jnp.where(valid, _dot_nt(k_ref[0, pl.ds(start, ATT_BAND), kv], qs), NEG_INF)
            s_ctx = _dot_nt(kc_ref[0, :, kv], qs)
            o = _softmax_pv([(s_loc, v_ref[0, pl.ds(start, ATT_BAND), kv]), (s_ctx, vc_ref[0, :, kv])],
                            sink_ref[g] * LOG2E)
            for r in range(ATT_GROUP):
                col = (g * ATT_GROUP + r) * ATT_HEAD_DIM
                o_ref[0, i * ATT_BLOCK:(i + 1) * ATT_BLOCK, col:col + ATT_HEAD_DIM] = (
                    o[r * ATT_BLOCK:(r + 1) * ATT_BLOCK].astype(BF16))


def _window_attention(q, k, v, kc, vc, sink, casts=()):
    bsz, n, _ = q.shape
    sink_rows = _sink_rows(sink, ATT_BLOCK)
    lc = kc.shape[1]
    whole = lambda rows: pl.BlockSpec((1, rows, ATT_KV_WIDTH), lambda b, i: (b, 0, 0))
    nbytes = 2 * 2 * (2 * ATT_TQ * ATT_Q_WIDTH + 2 * n * ATT_KV_WIDTH + 2 * lc * ATT_KV_WIDTH) + 8 * 1024 * 1024
    return _hosted_call(
        functools.partial(_attn_kernel, seq=n),
        casts=casts,
        args=(q, k, v, kc, vc, sink_rows),
        grid=(bsz, n // ATT_TQ),
        in_specs=[pl.BlockSpec((1, ATT_TQ, ATT_Q_WIDTH), lambda b, i: (b, i, 0)),
                  whole(n), whole(n), whole(lc), whole(lc),
                  pl.BlockSpec(sink_rows.shape, lambda b, i: (0, 0, 0))],
        out_specs=[pl.BlockSpec((1, ATT_TQ, ATT_Q_WIDTH), lambda b, i: (b, i, 0))],
        out_shape=[jax.ShapeDtypeStruct(q.shape, BF16)],
        compiler_params=_params(("arbitrary", "arbitrary"), nbytes),
        name="window_attention",
    )


def _ctx_attn_kernel(q_ref, kc_ref, vc_ref, sink_ref, o_ref):
    lc = q_ref.shape[1]
    for g in range(ATT_KV_HEADS):
        heads = [q_ref[0, :, (g * ATT_GROUP + r) * ATT_HEAD_DIM:(g * ATT_GROUP + r + 1) * ATT_HEAD_DIM]
                 for r in range(ATT_GROUP)]
        qs = jnp.concatenate(heads, axis=0)
        kv = slice(g * ATT_HEAD_DIM, (g + 1) * ATT_HEAD_DIM)
        o = _softmax_pv([(_dot_nt(kc_ref[0, :, kv], qs), vc_ref[0, :, kv])], sink_ref[g] * LOG2E)
        for r in range(ATT_GROUP):
            col = (g * ATT_GROUP + r) * ATT_HEAD_DIM
            o_ref[0, :, col:col + ATT_HEAD_DIM] = o[r * lc:(r + 1) * lc].astype(BF16)


def _context_attention(qc, kc, vc, sink):
    bsz, lc, _ = qc.shape
    sink_rows = _sink_rows(sink, lc)
    kv_spec = pl.BlockSpec((1, lc, ATT_KV_WIDTH), lambda b: (b, 0, 0))
    q_spec = pl.BlockSpec((1, lc, ATT_Q_WIDTH), lambda b: (b, 0, 0))
    return pl.pallas_call(
        _ctx_attn_kernel,
        grid=(bsz,),
        in_specs=[q_spec, kv_spec, kv_spec, pl.BlockSpec(sink_rows.shape, lambda b: (0, 0, 0))],
        out_specs=q_spec,
        out_shape=jax.ShapeDtypeStruct(qc.shape, BF16),
        compiler_params=_params(("arbitrary",), 8 * 1024 * 1024),
        name="context_attention",
    )(qc, kc, vc, sink_rows)


def _outproj0_kernel(ya_ref, yb_ref, x_ref, gate_ref, w_ref, lg_ref, lb_ref, o_ref, *, ctx_row):
    y = jnp.concatenate([ya_ref[0, h] for h in range(LRU_HEADS)] + [yb_ref[0]], axis=1)
    out = _dot(y, w_ref[...])
    o_ref[0] = _layer_norm(ALPHA * x_ref[0] + _mod_row(gate_ref, ctx_row) * out, lg_ref[...], lb_ref[...])


def _outproj0(ya, yb, x, mod, layer, ctx_row, w_out, ln_g, ln_b, tm):
    bsz, n, d = x.shape
    tok = lambda b, i: (b, i, 0)
    const = lambda b, i: (0, 0)
    nbytes = 2 * 2 * tm * d * 4 + w_out.size * 2 + 2 * 2 * tm * d * 2 + 2 * tm * d * 4
    return pl.pallas_call(
        functools.partial(_outproj0_kernel, ctx_row=ctx_row),
        grid=(bsz, n // tm),
        in_specs=[pl.BlockSpec((1, LRU_HEADS, tm, LRU_BLOCK), lambda b, i: (b, 0, i, 0)),
                  pl.BlockSpec((1, tm, ATT_Q_WIDTH), tok),
                  pl.BlockSpec((1, tm, d), tok),
                  _mod_spec(mod, layer, MOD_G1),
                  _resident(w_out.shape, const),
                  pl.BlockSpec((1, d), const), pl.BlockSpec((1, d), const)],
        out_specs=pl.BlockSpec((1, tm, d), tok),
        out_shape=jax.ShapeDtypeStruct(x.shape, F32),
        compiler_params=_params(("arbitrary", "arbitrary"), nbytes),
        name="outproj0_ln",
    )(ya, yb, x, mod, w_out, ln_g, ln_b)


def _ffn_kernel(x_ref, sc_ref, sh_ref, gate_ref, w1_ref, b1_ref, w2_ref, b2_ref, lg_ref, lb_ref,
                o_ref, h_s, *, ctx_row):
    j = pl.program_id(2)

    @pl.when(j == 0)
    def _():
        h_s[...] = _modulate(x_ref[0], sc_ref, sh_ref, ctx_row)
        o_ref[0] = jnp.zeros(o_ref.shape[1:], F32)

    a = jnp.maximum(_dot(h_s[...], w1_ref[...]) + b1_ref[...], 0.0)
    o_ref[0] += _dot((a * a).astype(BF16), w2_ref[...])

    @pl.when(j == pl.num_programs(2) - 1)
    def _():
        f = o_ref[0] + b2_ref[...]
        o_ref[0] = _layer_norm(ALPHA * x_ref[0] + _mod_row(gate_ref, ctx_row) * f, lg_ref[...], lb_ref[...])


def _ffn(x, mod, layer, ctx_row, w1, b1, w2, b2, ln_g, ln_b, tm, th, casts=()):
    bsz, n, d = x.shape
    hidden = w1.shape[1]
    tok = lambda b, i, j: (b, i, 0)
    const = lambda b, i, j: (0, 0)
    nbytes = 2 * 2 * tm * d * 4 + 2 * 2 * 2 * d * th * 2 + tm * d * 2 + 2 * tm * th * 4 + tm * d * 4
    return _hosted_call(
        functools.partial(_ffn_kernel, ctx_row=ctx_row),
        casts=casts,
        args=(x, mod, mod, mod, w1, b1, w2, b2, ln_g, ln_b),
        grid=(bsz, n // tm, hidden // th),
        in_specs=[pl.BlockSpec((1, tm, d), tok),
                  _mod_spec(mod, layer, MOD_SC2), _mod_spec(mod, layer, MOD_SH2), _mod_spec(mod, layer, MOD_G2),
                  pl.BlockSpec((d, th), lambda b, i, j: (0, j)),
                  pl.BlockSpec((1, th), lambda b, i, j: (0, j)),
                  pl.BlockSpec((th, d), lambda b, i, j: (j, 0)),
                  pl.BlockSpec((1, d), const), pl.BlockSpec((1, d), const), pl.BlockSpec((1, d), const)],
        out_specs=[pl.BlockSpec((1, tm, d), tok)],
        out_shape=[jax.ShapeDtypeStruct(x.shape, F32)],
        scratch_shapes=[pltpu.VMEM((tm, d), BF16)],
        compiler_params=_params(("arbitrary", "arbitrary", "arbitrary"), nbytes),
        name="ffn_ln",
    )


SC_PARTS = SC_WIDTH // S5_WIDTH


def _inproj1_kernel(*refs, full, ctx_row):
    x_ref, sc_ref, sh_ref, wu_ref = refs[:4]
    u_s = refs[-1]
    h = _modulate(x_ref[0], sc_ref, sh_ref, ctx_row)
    u = _dot(h, wu_ref[...])
    if full:
        wb_refs = refs[4:4 + SC_PARTS]
        wc_refs = refs[4 + SC_PARTS:4 + 2 * SC_PARTS]
        wx_refs = refs[4 + 2 * SC_PARTS:4 + 3 * SC_PARTS]
        u_ref, uflat_ref, gb_ref, p_ref = refs[4 + 3 * SC_PARTS:-1]
        u_ref[0] = u.astype(BF16)
        for c in range(SC_PARTS):
            cols = slice(c * S5_WIDTH, (c + 1) * S5_WIDTH)
            gb_ref[0, :, cols] = _dot(h, wb_refs[c][...]).astype(BF16)
            p_ref[0, :, cols] = (_dot(h, wc_refs[c][...]) * _dot(h, wx_refs[c][...])).astype(BF16)
    else:
        uflat_ref, = refs[4:-1]
    for k in range(S5_SLABS):
        u_s[k] = u[:, k * LANES:(k + 1) * LANES]
    chunks = u.shape[0] // S5_CHUNK
    for t in range(S5_CHUNK):
        for k in range(S5_SLABS):
            col = t * S5_WIDTH + k * LANES
            uflat_ref[0, :, col:col + LANES] = u_s[k, pl.ds(t, chunks, stride=S5_CHUNK), :].astype(BF16)


def _inproj1(x, mod, layer, ctx_row, w_in, tm, full):
    bsz, n, d = x.shape
    tok = lambda b, i: (b, i, 0)
    n_blocks = 1 + 3 * SC_PARTS if full else 1
    flat_spec = pl.BlockSpec((1, tm // S5_CHUNK, S5_CHUNK * S5_WIDTH), tok)
    flat_shape = jax.ShapeDtypeStruct((bsz, n // S5_CHUNK, S5_CHUNK * S5_WIDTH), BF16)
    token = lambda w: (pl.BlockSpec((1, tm, w), tok), jax.ShapeDtypeStruct((bsz, n, w), BF16))
    outs = ([token(S5_WIDTH), (flat_spec, flat_shape), token(SC_WIDTH), token(SC_WIDTH)] if full
            else [(flat_spec, flat_shape)])
    ncols = n_blocks * S5_WIDTH
    nbytes = 2 * tm * d * 4 + 2 * d * ncols + 2 * 2 * tm * (ncols + S5_WIDTH) + tm * ncols * 4 + tm * S5_WIDTH * 4
    return pl.pallas_call(
        functools.partial(_inproj1_kernel, full=full, ctx_row=ctx_row),
        grid=(bsz, n // tm),
        in_specs=[pl.BlockSpec((1, tm, d), tok), _mod_spec(mod, layer, MOD_SC1), _mod_spec(mod, layer, MOD_SH1)]
                 + [_resident((d, S5_WIDTH), lambda b, i, c=c: (0, c)) for c in range(n_blocks)],
        out_specs=[spec for spec, _ in outs],
        out_shape=[shape for _, shape in outs],
        scratch_shapes=[pltpu.VMEM((S5_SLABS, tm, LANES), F32)],
        compiler_params=_params(("arbitrary", "arbitrary"), nbytes),
        name="inproj1" if full else "inproj1_ctx",
    )(x, mod, mod, *([w_in] * n_blocks))


def _s5_power_table(par_ref):
    rows = 2 * S5_POW_ROWS
    r = lax.broadcasted_iota(jnp.int32, (rows, S5_SLAB_STATE), 0)
    first = r < S5_POW_ROWS
    lag = (r % S5_POW_ROWS).astype(F32)
    pick = lambda i: jnp.where(first, par_ref[0, 0, i:i + 1, :], par_ref[0, 1, i:i + 1, :])
    dt = jnp.exp(pick(0))
    mag = jnp.exp(lag * dt * pick(1))
    ang = lag * dt * pick(2)
    return mag * jnp.cos(ang), mag * jnp.sin(ang)


def _s5_input_matrix(par_ref, bt_ref, d, pw_re, pw_im):
    a_re = par_ref[0, d, 1:2, :]
    a_im = par_ref[0, d, 2:3, :]
    ab_re = pw_re[d * S5_POW_ROWS + 1:d * S5_POW_ROWS + 2, :]
    ab_im = pw_im[d * S5_POW_ROWS + 1:d * S5_POW_ROWS + 2, :]
    den = a_re * a_re + a_im * a_im
    k_re = ((ab_re - 1.0) * a_re + ab_im * a_im) / den
    k_im = (ab_im * a_re - (ab_re - 1.0) * a_im) / den
    b_re = bt_ref[0, d, 0]
    b_im = bt_ref[0, d, 1]
    return k_re * b_re - k_im * b_im, k_re * b_im + k_im * b_re


def _cmul_row(x_re, x_im, p_re, p_im):
    return x_re * p_re - x_im * p_im, x_re * p_im + x_im * p_re


def _s5_state_prep_kernel(par_ref, bt_ref, wst_ref, a16_ref):
    pw_re, pw_im = _s5_power_table(par_ref)
    for d in range(2):
        bb_re, bb_im = _s5_input_matrix(par_ref, bt_ref, d, pw_re, pw_im)
        base = d * S5_POW_ROWS
        for lag in range(S5_CHUNK):
            e_re, e_im = _cmul_row(bb_re, bb_im, pw_re[base + lag:base + lag + 1, :],
                                   pw_im[base + lag:base + lag + 1, :])
            s = S5_CHUNK - 1 - lag if d == 0 else lag
            col = 2 * d * S5_SLAB_STATE
            wst_ref[0, s * LANES:(s + 1) * LANES, col:col + S5_SLAB_STATE] = e_re.astype(BF16)
            wst_ref[0, s * LANES:(s + 1) * LANES, col + S5_SLAB_STATE:col + 2 * S5_SLAB_STATE] = e_im.astype(BF16)
        row16 = base + S5_CHUNK
        a16_ref[0, :, 2 * d * S5_SLAB_STATE:(2 * d + 1) * S5_SLAB_STATE] = jnp.broadcast_to(
            pw_re[row16:row16 + 1, :], (SUBLANES, S5_SLAB_STATE))
        a16_ref[0, :, (2 * d + 1) * S5_SLAB_STATE:(2 * d + 2) * S5_SLAB_STATE] = jnp.broadcast_to(
            pw_im[row16:row16 + 1, :], (SUBLANES, S5_SLAB_STATE))


def _s5_output_prep_kernel(par_ref, bt_ref, c_ref, ct_ref, wbig_ref):
    pw_re, pw_im = _s5_power_table(par_ref)
    kern = []
    for d in range(2):
        bb_re, bb_im = _s5_input_matrix(par_ref, bt_ref, d, pw_re, pw_im)
        c_re = c_ref[0, d, 0].astype(BF16)
        c_im = c_ref[0, d, 1].astype(BF16)
        base = d * S5_POW_ROWS
        per_lag = []
        for lag in range(S5_CHUNK):
            e_re, e_im = _cmul_row(bb_re, bb_im, pw_re[base + lag:base + lag + 1, :],
                                   pw_im[base + lag:base + lag + 1, :])
            per_lag.append(_dot_nt(e_re.astype(BF16), c_re) - _dot_nt(e_im.astype(BF16), c_im))
        kern.append(per_lag)
    for s in range(S5_CHUNK):
        for t in range(S5_CHUNK):
            if s < t:
                blk = kern[0][t - s]
            elif s > t:
                blk = kern[1][s - t]
            else:
                blk = kern[0][0] + kern[1][0]
            wbig_ref[0, s * LANES:(s + 1) * LANES, t * LANES:(t + 1) * LANES] = blk.astype(BF16)
    pt_re = pw_re.T
    pt_im = pw_im.T
    for d in range(2):
        ct_re = ct_ref[0, d, 0]
        ct_im = ct_ref[0, d, 1]
        for t in range(S5_CHUNK):
            lag = t + 1 if d == 0 else S5_CHUNK - t
            col = d * S5_POW_ROWS + lag
            p_re = pt_re[:, col:col + 1]
            p_im = pt_im[:, col:col + 1]
            g_re = ct_re * p_re - ct_im * p_im
            g_im = ct_re * p_im + ct_im * p_re
            r0 = S5_XK + 2 * d * S5_SLAB_STATE
            wbig_ref[0, r0:r0 + S5_SLAB_STATE, t * LANES:(t + 1) * LANES] = g_re.astype(BF16)
            wbig_ref[0, r0 + S5_SLAB_STATE:r0 + 2 * S5_SLAB_STATE, t * LANES:(t + 1) * LANES] = (-g_im).astype(BF16)


def _s5_operators(log_dt, a_re, a_im, b_re, b_im, c_re, c_im):
    gs = S5_SLAB_GROUPS
    eye = jnp.eye(gs, dtype=F32)

    def lanes(v):
        return v.reshape(2, S5_SLABS, gs * S5_STATE)

    par = jnp.stack([lanes(jnp.broadcast_to(log_dt[:, :, None], a_re.shape)), lanes(a_re), lanes(a_im)], axis=2)
    par = jnp.pad(par, ((0, 0), (0, 0), (0, SUBLANES - 3), (0, 0))).transpose(1, 0, 2, 3)

    def embed_bt(b):
        b = b.reshape(2, S5_SLABS, gs, S5_STATE, S5_GROUP)
        e = b.transpose(0, 1, 2, 4, 3)[:, :, :, :, None, :] * eye[None, None, :, None, :, None]
        return e.reshape(2, S5_SLABS, gs * S5_GROUP, gs * S5_STATE).transpose(1, 0, 2, 3)

    def embed_c(c):
        c = c.reshape(2, S5_SLABS, gs, S5_GROUP, S5_STATE)
        e = c[:, :, :, :, None, :] * eye[None, None, :, None, :, None]
        return e.reshape(2, S5_SLABS, gs * S5_GROUP, gs * S5_STATE).transpose(1, 0, 2, 3)

    bt = jnp.stack([embed_bt(b_re), embed_bt(b_im)], axis=2)
    cm = jnp.stack([embed_c(c_re), embed_c(c_im)], axis=2)
    ct = cm.transpose(0, 1, 2, 4, 3)

    slab5 = lambda shape: pl.BlockSpec((1,) + shape, lambda k: (k, 0, 0, 0, 0))
    par_spec = pl.BlockSpec((1, 2, SUBLANES, S5_SLAB_STATE), lambda k: (k, 0, 0, 0))
    bt_spec = slab5((2, 2, LANES, S5_SLAB_STATE))
    wst, a16 = pl.pallas_call(
        _s5_state_prep_kernel,
        grid=(S5_SLABS,),
        in_specs=[par_spec, bt_spec],
        out_specs=[pl.BlockSpec((1, S5_XK, S5_HK), lambda k: (k, 0, 0)),
                   pl.BlockSpec((1, SUBLANES, S5_HK), lambda k: (k, 0, 0))],
        out_shape=[jax.ShapeDtypeStruct((S5_SLABS, S5_XK, S5_HK), BF16),
                   jax.ShapeDtypeStruct((S5_SLABS, SUBLANES, S5_HK), F32)],
        compiler_params=_params(("arbitrary",), 2 * S5_XK * S5_HK * 2 + 8 * 1024 * 1024),
        name="s5_state_operator",
    )(par, bt)
    wbig = pl.pallas_call(
        _s5_output_prep_kernel,
        grid=(S5_SLABS,),
        in_specs=[par_spec, bt_spec, bt_spec, slab5((2, 2, S5_SLAB_STATE, LANES))],
        out_specs=pl.BlockSpec((1, S5_XK + S5_HK, S5_XK), lambda k: (k, 0, 0)),
        out_shape=jax.ShapeDtypeStruct((S5_SLABS, S5_XK + S5_HK, S5_XK), BF16),
        compiler_params=_params(("arbitrary",), 2 * (S5_XK + S5_HK) * S5_XK * 2 + 8 * 1024 * 1024),
        name="s5_output_operator",
    )(par, bt, cm, ct)
    return wst, a16, wbig


def _s5_states_kernel(*refs, n_ctx, n_lat):
    xc_refs = refs[:S5_CHUNK]
    xl_refs = refs[S5_CHUNK:2 * S5_CHUNK]
    wst_ref, a16_ref, h_ref, s_s, h_s = refs[2 * S5_CHUNK:]
    x = jnp.concatenate([jnp.concatenate([r[0] for r in xc_refs], axis=1),
                         jnp.concatenate([r[0] for r in xl_refs], axis=1)], axis=0)
    s_s[...] = _dot(x, wst_ref[0])
    p = S5_SLAB_STATE
    af_re = a16_ref[0, 0:1, 0:p]
    af_im = a16_ref[0, 0:1, p:2 * p]
    ar_re = a16_ref[0, 0:1, 2 * p:3 * p]
    ar_im = a16_ref[0, 0:1, 3 * p:4 * p]
    total = n_ctx + n_lat

    def step(i, carry):
        f_re, f_im, r_re, r_im = carry
        h_s[pl.ds(i, 1), 0:p] = f_re
        h_s[pl.ds(i, 1), p:2 * p] = f_im
        s_re = s_s[pl.ds(i, 1), 0:p]
        s_im = s_s[pl.ds(i, 1), p:2 * p]
        f_re, f_im = af_re * f_re - af_im * f_im + s_re, af_re * f_im + af_im * f_re + s_im
        j = jnp.where(i < n_ctx, n_ctx - 1 - i, total + n_ctx - 1 - i)
        h_s[pl.ds(j, 1), 2 * p:3 * p] = r_re
        h_s[pl.ds(j, 1), 3 * p:4 * p] = r_im
        s_re = s_s[pl.ds(j, 1), 2 * p:3 * p]
        s_im = s_s[pl.ds(j, 1), 3 * p:4 * p]
        r_re, r_im = ar_re * r_re - ar_im * r_im + s_re, ar_re * r_im + ar_im * r_re + s_im
        return f_re, f_im, r_re, r_im

    zero = jnp.zeros((1, p), F32)
    lax.fori_loop(0, total, step, (zero, zero, zero, zero), unroll=4)
    h_ref[0, 0] = h_s[pl.ds(n_ctx, n_lat), :].astype(BF16)


def _s5_states(uc_flat, ul_flat, wst, a16):
    bsz, n_ctx, _ = uc_flat.shape
    n_lat = ul_flat.shape[1]
    piece = lambda rows, s: pl.BlockSpec((1, rows, LANES), lambda k, b, s=s: (b, 0, s * S5_SLABS + k))
    in_specs = ([piece(n_ctx, s) for s in range(S5_CHUNK)] + [piece(n_lat, s) for s in range(S5_CHUNK)]
                + [pl.BlockSpec((1, S5_XK, S5_HK), lambda k, b: (k, 0, 0)),
                   pl.BlockSpec((1, SUBLANES, S5_HK), lambda k, b: (k, 0, 0))])
    total = n_ctx + n_lat
    nbytes = 2 * S5_XK * S5_HK * 2 + 2 * total * S5_XK * 2 + 3 * total * S5_HK * 4 + 2 * n_lat * S5_HK * 2
    return pl.pallas_call(
        functools.partial(_s5_states_kernel, n_ctx=n_ctx, n_lat=n_lat),
        grid=(S5_SLABS, bsz),
        in_specs=in_specs,
        out_specs=pl.BlockSpec((1, 1, n_lat, S5_HK), lambda k, b: (b, k, 0, 0)),
        out_shape=jax.ShapeDtypeStruct((bsz, S5_SLABS, n_lat, S5_HK), BF16),
        scratch_shapes=[pltpu.VMEM((total, S5_HK), F32), pltpu.VMEM((total, S5_HK), F32)],
        compiler_params=_params(("arbitrary", "arbitrary"), nbytes),
        name="s5_states",
    )(*([uc_flat] * S5_CHUNK + [ul_flat] * S5_CHUNK + [wst, a16]))


def _s5_readout_kernel(*refs):
    x_refs = refs[:S5_CHUNK]
    h_ref, w_ref, y_ref = refs[S5_CHUNK:]
    lhs = jnp.concatenate([r[0] for r in x_refs] + [h_ref[0, 0]], axis=1)
    y = _dot(lhs, w_ref[0])
    chunks = y.shape[0]
    for t in range(S5_CHUNK):
        y_ref[0, pl.ds(t, chunks, stride=S5_CHUNK), :] = y[:, t * LANES:(t + 1) * LANES]


def _s5_readout(ul_flat, h_in, wbig):
    bsz, n_lat, _ = ul_flat.shape
    piece = lambda s: pl.BlockSpec((1, n_lat, LANES), lambda k, b, s=s: (b, 0, s * S5_SLABS + k))
    nbytes = 2 * (S5_XK + S5_HK) * S5_XK * 2 + 2 * n_lat * (S5_XK + S5_HK) * 2 * 2 + n_lat * S5_XK * 4 * 3
    return pl.pallas_call(
        _s5_readout_kernel,
        grid=(S5_SLABS, bsz),
        in_specs=[piece(s) for s in range(S5_CHUNK)]
                 + [pl.BlockSpec((1, 1, n_lat, S5_HK), lambda k, b: (b, k, 0, 0)),
                    pl.BlockSpec((1, S5_XK + S5_HK, S5_XK), lambda k, b: (k, 0, 0))],
        out_specs=pl.BlockSpec((1, n_lat * S5_CHUNK, LANES), lambda k, b: (b, 0, k)),
        out_shape=jax.ShapeDtypeStruct((bsz, n_lat * S5_CHUNK, S5_WIDTH), F32),
        compiler_params=_params(("arbitrary", "arbitrary"), nbytes),
        name="s5_readout",
    )(*([ul_flat] * S5_CHUNK + [h_in, wbig]))


SC_HALO = 16


def _outproj1_kernel(y_ref, u_ref, gb_ref, p_ref, pprev_ref, pnext_ref, x_ref, gate_ref,
                     dskip_ref, wglu_ref, bglu_ref, cw_ref, cb_ref, w_ref, lg_ref, lb_ref, o_ref, *, ctx_row):
    i = pl.program_id(1)
    tm = x_ref.shape[1]
    yc = y_ref[0] + dskip_ref[...] * u_ref[0].astype(F32)
    z = _gelu_tanh(yc)
    y_c = z * _sigmoid(_dot(z.astype(BF16), wglu_ref[...]) + bglu_ref[...])
    p = p_ref[0].astype(F32)
    row = lax.broadcasted_iota(jnp.int32, (tm, 1), 0)
    prev_row = jnp.where(i > 0, pprev_ref[0, SC_HALO - 1:SC_HALO, :].astype(F32), 0.0)
    next_row = jnp.where(i < pl.num_programs(1) - 1, pnext_ref[0, 0:1, :].astype(F32), 0.0)
    p_dn = jnp.where(row == 0, prev_row, pltpu.roll(p, 1, 0))
    p_up = jnp.where(row == tm - 1, next_row, pltpu.roll(p, tm - 1, 0))
    conv = cb_ref[...] + cw_ref[0:1, :] * p_dn + cw_ref[1:2, :] * p + cw_ref[2:3, :] * p_up
    y_d = gb_ref[0].astype(F32) * conv
    y = jnp.concatenate([y_c.astype(BF16), y_d.astype(BF16)], axis=1)
    out = _dot(y, w_ref[...])
    o_ref[0] = _layer_norm(ALPHA * x_ref[0] + _mod_row(gate_ref, ctx_row) * out, lg_ref[...], lb_ref[...])


def _outproj1(y, u, gb, p, x, mod, layer, ctx_row, d_skip, w_glu, b_glu, conv_w, conv_b, w_out, ln_g, ln_b, tm):
    bsz, n, d = x.shape
    tok = lambda b, i: (b, i, 0)
    const = lambda b, i: (0, 0)
    per = tm // SC_HALO
    last = n // SC_HALO - 1
    nbytes = (2 * 2 * tm * d * 4 + w_out.size * 2 + 2 * 2 * tm * (2 * S5_WIDTH + 2 * SC_WIDTH) * 2
              + 6 * tm * SC_WIDTH * 4 + 2 * tm * d * 4)
    return pl.pallas_call(
        functools.partial(_outproj1_kernel, ctx_row=ctx_row),
        grid=(bsz, n // tm),
        in_specs=[pl.BlockSpec((1, tm, S5_WIDTH), tok), pl.BlockSpec((1, tm, S5_WIDTH), tok),
                  pl.BlockSpec((1, tm, SC_WIDTH), tok), pl.BlockSpec((1, tm, SC_WIDTH), tok),
                  pl.BlockSpec((1, SC_HALO, SC_WIDTH), lambda b, i: (b, jnp.maximum(i * per - 1, 0), 0)),
                  pl.BlockSpec((1, SC_HALO, SC_WIDTH), lambda b, i: (b, jnp.minimum((i + 1) * per, last), 0)),
                  pl.BlockSpec((1, tm, d), tok), _mod_spec(mod, layer, MOD_G1),
                  pl.BlockSpec((1, S5_WIDTH), const), _resident(w_glu.shape, const),
                  pl.BlockSpec((1, S5_WIDTH), const),
                  pl.BlockSpec((SC_CONV, SC_WIDTH), const), pl.BlockSpec((1, SC_WIDTH), const),
                  _resident(w_out.shape, const),
                  pl.BlockSpec((1, d), const), pl.BlockSpec((1, d), const)],
        out_specs=pl.BlockSpec((1, tm, d), tok),
        out_shape=jax.ShapeDtypeStruct(x.shape, F32),
        compiler_params=_params(("arbitrary", "arbitrary"), nbytes),
        name="outproj1_ln",
    )(y, u, gb, p, p, p, x, mod, d_skip, w_glu, b_glu, conv_w, conv_b, w_out, ln_g, ln_b)


def _rope_tables(n):
    t = np.arange(n)
    inv = ROPE_BASE ** (-np.arange(ROPE_FREQS, dtype=np.float64) / ROPE_FREQS)
    ang_r = (t // GRID_W)[:, None] * inv[None, :]
    ang_c = (t % GRID_W)[:, None] * inv[None, :]
    cos = np.concatenate([np.cos(ang_r), np.cos(ang_r), np.cos(ang_c), np.cos(ang_c)], axis=1)
    sin = np.concatenate([-np.sin(ang_r), np.sin(ang_r), -np.sin(ang_c), np.sin(ang_c)], axis=1)
    return jnp.asarray(cos, F32), jnp.asarray(sin, F32)


TM_LATENT = 512
TM_CONTEXT = 256
FFN_TH = 2048


def kernel(x, c, ctx, c_ctx, mod_w, mod_b, ln1_g, ln1_b, ln2_g, ln2_b, ffn_w1, ffn_b1, ffn_w2, ffn_b2,
           ab_w_in, ab_w_out, lru_conv_w, lru_conv_b, lru_w_a, lru_b_a, lru_w_x, lru_b_x, lru_lam, att_sink,
           cd_w_in, cd_w_out, s5_log_dt, s5_a_re, s5_a_im, s5_b_re, s5_b_im, s5_c_re, s5_c_im, s5_d,
           s5_w_glu, s5_b_glu, sc_conv_w, sc_conv_b):
    bsz, n, d = x.shape
    lc = ctx.shape[1]
    assert n % TM_LATENT == 0 and lc % TM_CONTEXT == 0 and (bsz * lc) % TM_LATENT == 0 and n % GRID_W == 0

    pad_rows = SUBLANES - (bsz + 1) % SUBLANES if (bsz + 1) % SUBLANES else 0
    cc = jnp.concatenate([c, c_ctx[None, :], jnp.zeros((pad_rows, d), F32)], axis=0)
    mod = _modulation(cc, mod_w, mod_b)
    ctx_row = bsz

    row2 = lambda v: v.reshape(1, -1)
    rope_tabs = _rope_tables(n)

    side = {}

    def weight(stack, name, idx):
        return side.pop((name, idx)) if (name, idx) in side else _to_bf16(stack, idx)

    def hosted(call, jobs):
        outs = call(casts=[(stack, idx) for _, idx, stack in jobs])
        own = len(outs) - len(jobs)
        side.update({(name, idx): w for (name, idx, _), w in zip(jobs, outs[own:])})
        return outs[:own]

    for i in range(DEPTH):
        last = i == DEPTH - 1
        j = i // 2
        todo = [("w1", i, ffn_w1), ("w2", i, ffn_w2)]
        if not last:
            nj = (i + 1) // 2
            todo += [("w1", i + 1, ffn_w1), ("w2", i + 1, ffn_w2)]
            todo += ([("ab_in", nj, ab_w_in), ("ab_out", nj, ab_w_out)] if (i + 1) % 2 == 0
                     else [("cd_in", nj, cd_w_in), ("cd_out", nj, cd_w_out)])
        todo = [job for job in todo if (job[0], job[1]) not in side]
        if i % 2 == 0:
            w_in = weight(ab_w_in, "ab_in", j)
            w_out = weight(ab_w_out, "ab_out", j)
            u, gate, q, k, v = _inproj0(x, mod, i, None, w_in, rope_tabs, TM_LATENT)
            uc, gatec, qc, kc, vc = _inproj0(ctx, mod, i, ctx_row, w_in, None, TM_CONTEXT)
            w_gate = (0.5 * jnp.concatenate([lru_w_a[j, 0], lru_w_x[j, 0], lru_w_a[j, 1], lru_w_x[j, 1]],
                                            axis=-1)).astype(BF16)
            hb = lambda b: b.reshape(LRU_HEADS, 1, LRU_BLOCK)
            b_gate = 0.5 * jnp.concatenate([hb(lru_b_a[j, 0]), hb(lru_b_x[j, 0]), hb(lru_b_a[j, 1]),
                                            hb(lru_b_x[j, 1])], axis=-1)
            lam = jnp.concatenate([hb(lru_lam[j, 0]), hb(lru_lam[j, 1])], axis=-1)
            ya, yac = hosted(functools.partial(_rglru, u, gate, uc, gatec, lru_conv_w[j], row2(lru_conv_b[j]),
                                               w_gate, b_gate, lam), todo[0::2])
            yb, = hosted(functools.partial(_window_attention, q, k, v, kc, vc, att_sink[j]), todo[1::2])
            todo = []
            x = _outproj0(ya, yb, x, mod, i, None, w_out, row2(ln1_g[i]), row2(ln1_b[i]), TM_LATENT)
            if not last:
                ybc = _context_attention(qc, kc, vc, att_sink[j])
                ctx = _outproj0(yac, ybc, ctx, mod, i, ctx_row, w_out, row2(ln1_g[i]), row2(ln1_b[i]), TM_CONTEXT)
        else:
            assert last
            w_in = weight(cd_w_in, "cd_in", j)
            w_out = weight(cd_w_out, "cd_out", j)
            u, u_flat, gb, p = _inproj1(x, mod, i, None, w_in, TM_LATENT, True)
            (uc_flat,) = _inproj1(ctx, mod, i, ctx_row, w_in, TM_CONTEXT, False)
            wst, a16, wbig = _s5_operators(s5_log_dt[j], s5_a_re[j], s5_a_im[j], s5_b_re[j], s5_b_im[j],
                                           s5_c_re[j], s5_c_im[j])
            h_in = _s5_states(uc_flat, u_flat, wst, a16)
            y = _s5_readout(u_flat, h_in, wbig)
            x = _outproj1(y, u, gb, p, x, mod, i, None, row2(s5_d[j]), s5_w_glu[j].astype(BF16), row2(s5_b_glu[j]),
                          sc_conv_w[j], row2(sc_conv_b[j]), w_out, row2(ln1_g[i]), row2(ln1_b[i]), TM_LATENT)
        w1 = weight(ffn_w1, "w1", i)
        w2 = weight(ffn_w2, "w2", i)
        ahead = [job for job in todo if job[1] != i or job[0] not in ("w1", "w2")]
        x, = hosted(functools.partial(_ffn, x, mod, i, None, w1, row2(ffn_b1[i]), w2, row2(ffn_b2[i]),
                                      row2(ln2_g[i]), row2(ln2_b[i]), TM_LATENT, FFN_TH), ahead)
        if not last:
            rows = ctx.reshape(1, bsz * lc, d)
            rows, = _ffn(rows, mod, i, ctx_row, w1, row2(ffn_b1[i]), w2, row2(ffn_b2[i]), row2(ln2_g[i]),
                         row2(ln2_b[i]), TM_LATENT, FFN_TH)
            ctx = rows.reshape(bsz, lc, d)
    return x
```

```python
import functools
import math

import jax
import jax.numpy as jnp
import numpy as np
from jax import lax
from jax.experimental import pallas as pl
from jax.experimental.pallas import tpu as pltpu

F32 = jnp.float32
BF16 = jnp.bfloat16

D_MODEL = 2048
DEPTH = 2
GRID_W = 64
LRU_WIDTH = D_MODEL // 2
LRU_HEADS = 8
LRU_BLOCK = LRU_WIDTH // LRU_HEADS
LRU_CONV = 4
LRU_CONV_LEFT = 2
LRU_C = 8.0
ATT_HEAD_DIM = 128
ATT_Q_HEADS = (D_MODEL // 2) // ATT_HEAD_DIM
ATT_KV_HEADS = 2
ATT_GROUP = ATT_Q_HEADS // ATT_KV_HEADS
ATT_Q_WIDTH = ATT_Q_HEADS * ATT_HEAD_DIM
ATT_KV_WIDTH = ATT_KV_HEADS * ATT_HEAD_DIM
WINDOW = 128
ATT_BLOCK = 128
ROPE_BASE = 10000.0
ROPE_FREQS = ATT_HEAD_DIM // 4
S5_WIDTH = D_MODEL // 4
S5_GROUP = 16
S5_GROUPS = S5_WIDTH // S5_GROUP
S5_STATE = 64
SC_WIDTH = D_MODEL - S5_WIDTH
SC_CONV = 3
FFN_HIDDEN = 4 * D_MODEL
ALPHA = (2.0 * DEPTH) ** 0.25
LN_EPS = 1e-5
NEG_INF = -1e30
LOG2E = math.log2(math.e)

LANES = 128
SUBLANES = 8
V7X_VMEM_BYTES = 64 * 1024 * 1024
V7X_VMEM_BUDGET = 62 * 1024 * 1024

S5_CHUNK = 16
S5_SLAB_GROUPS = LANES // S5_GROUP
S5_SLABS = S5_WIDTH // LANES
S5_SLAB_STATE = S5_SLAB_GROUPS * S5_STATE
S5_XK = S5_CHUNK * LANES
S5_HK = 4 * S5_SLAB_STATE
S5_POW_ROWS = 64


def _vmem_limit(nbytes):
    return int(min(V7X_VMEM_BUDGET, max(nbytes * 3 // 2, 16 * 1024 * 1024)))


def _resident(shape, index_map):
    return pl.BlockSpec(shape, index_map, pipeline_mode=pl.Buffered(1))


def _params(sem, nbytes):
    return pltpu.CompilerParams(dimension_semantics=sem, vmem_limit_bytes=_vmem_limit(nbytes))


def _dot(a, b):
    return jnp.dot(a, b, preferred_element_type=F32)


def _dot_nt(a, b):
    return lax.dot_general(a, b, (((1,), (1,)), ((), ())), preferred_element_type=F32)


def _layer_norm(v, g, b):
    mu = jnp.mean(v, axis=-1, keepdims=True)
    c = v - mu
    var = jnp.mean(c * c, axis=-1, keepdims=True)
    return c * lax.rsqrt(var + LN_EPS) * g + b


def _gelu_tanh(x):
    return 0.5 * x * (1.0 + jnp.tanh(math.sqrt(2.0 / math.pi) * (x + 0.044715 * (x * x * x))))


def _sigmoid(x):
    return 0.5 * (1.0 + jnp.tanh(0.5 * x))


MOD_SH1, MOD_SC1, MOD_G1, MOD_SH2, MOD_SC2, MOD_G2 = range(6)


def _mod_spec(mod, layer, chunk):
    _, rows, width = mod.shape
    return pl.BlockSpec((1, rows, width // 6), lambda *_: (layer, 0, chunk))


def _mod_row(ref, ctx_row):
    row = pl.program_id(0) if ctx_row is None else ctx_row
    return ref[0, pl.ds(row, 1), :]


def _modulate(x, sc_ref, sh_ref, ctx_row):
    return (x * (1.0 + _mod_row(sc_ref, ctx_row)) + _mod_row(sh_ref, ctx_row)).astype(BF16)


CAST_BLOCK_BYTES = 8 * 1024 * 1024


def _cast_kernel(w_ref, o_ref):
    o_ref[...] = w_ref[0].astype(BF16)


def _to_bf16(w, layer):
    _, rows, cols = w.shape
    tr = rows
    while tr * cols * 4 > CAST_BLOCK_BYTES and tr % 2 == 0 and tr // 2 >= 2 * SUBLANES:
        tr //= 2
    return pl.pallas_call(
        _cast_kernel,
        grid=(rows // tr,),
        in_specs=[pl.BlockSpec((1, tr, cols), lambda i: (layer, i, 0))],
        out_specs=pl.BlockSpec((tr, cols), lambda i: (i, 0)),
        out_shape=jax.ShapeDtypeStruct((rows, cols), BF16),
        compiler_params=_params(("arbitrary",), 2 * tr * cols * 6),
        name="weight_cast",
    )(w)


CAST_ROWS = 2 * SUBLANES


def _silu_bf16(v):
    return (v * _sigmoid(v)).astype(BF16)


def _hosted_call(body, *, grid, in_specs, out_specs, out_shape, args, sem, nbytes, casts=(), mods=(), **kwargs):
    steps = math.prod(grid)

    def flat(*ids):
        step = 0
        for extent, i in zip(grid, ids):
            step = step * extent + i
        return step

    def upto(used):
        return lambda *ids: jnp.minimum(flat(*ids), used - 1)

    side_in, side_out, side_shape, side_args, jobs = [], [], [], [], []
    for w, layer in casts:
        _, rows, cols = w.shape
        rps = max(CAST_ROWS, rows // steps)
        assert rows % rps == 0 and rows // rps <= steps
        at = upto(rows // rps)
        side_in.append(pl.BlockSpec((1, rps, cols), lambda *ids, at=at, layer=layer: (layer, at(*ids), 0)))
        side_out.append(pl.BlockSpec((rps, cols), lambda *ids, at=at: (at(*ids), 0)))
        side_shape.append(jax.ShapeDtypeStruct((rows, cols), BF16))
        side_args.append(w)
        nbytes += 2 * rps * cols * (4 + 2)
        jobs.append((1, lambda ins, out: out.__setitem__(Ellipsis, ins[0][0].astype(BF16))))
    for cc, mod_w, mod_b, layer in mods:
        depth, d, n = mod_w.shape
        width = LANES * max(1, n // LANES // steps)
        assert n % width == 0 and n // width <= steps
        at = upto(n // width)
        side_in += [pl.BlockSpec(cc.shape, lambda *ids: (0, 0)),
                    pl.BlockSpec((1, d, width), lambda *ids, at=at, layer=layer: (layer, 0, at(*ids))),
                    pl.BlockSpec((1, 1, width), lambda *ids, at=at, layer=layer: (layer, 0, at(*ids)))]
        side_out.append(pl.BlockSpec((1, cc.shape[0], width), lambda *ids, at=at: (0, 0, at(*ids))))
        side_shape.append(jax.ShapeDtypeStruct((1, cc.shape[0], n), F32))
        side_args += [cc, mod_w, mod_b.reshape(depth, 1, n)]
        nbytes += 2 * d * width * 4 + d * width * 2
        jobs.append((3, lambda ins, out: out.__setitem__(
            0, _dot(_silu_bf16(ins[0][...]), ins[1][0].astype(BF16)) + ins[2][0])))
    n_in, n_out, n_side_in, n_side_out = len(in_specs), len(out_specs), len(side_in), len(side_out)

    def kernel(*refs):
        ins = refs[:n_in]
        s_ins = refs[n_in:n_in + n_side_in]
        outs = refs[n_in + n_side_in:n_in + n_side_in + n_out]
        s_outs = refs[n_in + n_side_in + n_out:n_in + n_side_in + n_out + n_side_out]
        pos = 0
        for (arity, run), out in zip(jobs, s_outs):
            run(s_ins[pos:pos + arity], out)
            pos += arity
        body(*ins, *outs, *refs[n_in + n_side_in + n_out + n_side_out:])

    return pl.pallas_call(
        kernel, grid=grid, in_specs=list(in_specs) + side_in, out_specs=list(out_specs) + side_out,
        out_shape=list(out_shape) + side_shape, compiler_params=_params(sem, nbytes), **kwargs,
    )(*args, *side_args)


def _mod_kernel(cc_ref, w_ref, b_ref, o_ref):
    o_ref[0] = _dot(_silu_bf16(cc_ref[...]), w_ref[0].astype(BF16)) + b_ref[0]


def _modulation(cc, mod_w, mod_b, layer):
    depth, d, n = mod_w.shape
    tn = 1024
    rows = cc.shape[0]
    return pl.pallas_call(
        _mod_kernel,
        grid=(n // tn,),
        in_specs=[pl.BlockSpec((rows, d), lambda j: (0, 0)),
                  pl.BlockSpec((1, d, tn), lambda j: (layer, 0, j)),
                  pl.BlockSpec((1, 1, tn), lambda j: (layer, 0, j))],
        out_specs=pl.BlockSpec((1, rows, tn), lambda j: (0, 0, j)),
        out_shape=jax.ShapeDtypeStruct((1, rows, n), F32),
        compiler_params=_params(("arbitrary",), 2 * d * tn * 4 + d * tn * 2),
        name="modulation",
    )(cc, mod_w, mod_b.reshape(depth, 1, n))


def _rope(x, cos, sin_signed, heads):
    lane = lax.broadcasted_iota(jnp.int32, (x.shape[0], ATT_HEAD_DIM), 1)
    first = (lane % (2 * ROPE_FREQS)) < ROPE_FREQS
    out = []
    for h in range(heads):
        xs = x[:, h * ATT_HEAD_DIM:(h + 1) * ATT_HEAD_DIM]
        swapped = jnp.where(first, pltpu.roll(xs, ATT_HEAD_DIM - ROPE_FREQS, 1), pltpu.roll(xs, ROPE_FREQS, 1))
        out.append(xs * cos + swapped * sin_signed)
    return out


def _inproj0_kernel(*refs, rope, ctx_row):
    if rope:
        (x_ref, sc_ref, sh_ref, cos_ref, sin_ref, wu_ref, wg_ref, wq_ref, wk_ref, wv_ref,
         u_ref, g_ref, q_ref, k_ref, v_ref) = refs
    else:
        (x_ref, sc_ref, sh_ref, wu_ref, wg_ref, wq_ref, wk_ref, wv_ref,
         u_ref, g_ref, q_ref, k_ref, v_ref) = refs
    h = _modulate(x_ref[0], sc_ref, sh_ref, ctx_row)
    u = _dot(h, wu_ref[...])
    g = _dot(h, wg_ref[...])
    for hd in range(LRU_HEADS):
        u_ref[0, hd] = u[:, hd * LRU_BLOCK:(hd + 1) * LRU_BLOCK].astype(BF16)
        g_ref[0, hd] = g[:, hd * LRU_BLOCK:(hd + 1) * LRU_BLOCK].astype(BF16)
    q = _dot(h, wq_ref[...]) * (ATT_HEAD_DIM ** -0.5 * LOG2E)
    k = _dot(h, wk_ref[...])
    if rope:
        cos = cos_ref[...]
        sin = sin_ref[...]
        for hd, piece in enumerate(_rope(q, cos, sin, ATT_Q_HEADS)):
            q_ref[0, :, hd * ATT_HEAD_DIM:(hd + 1) * ATT_HEAD_DIM] = piece.astype(BF16)
        for hd, piece in enumerate(_rope(k, cos, sin, ATT_KV_HEADS)):
            k_ref[0, :, hd * ATT_HEAD_DIM:(hd + 1) * ATT_HEAD_DIM] = piece.astype(BF16)
    else:
        q_ref[0] = q.astype(BF16)
        k_ref[0] = k.astype(BF16)
    v_ref[0] = _dot(h, wv_ref[...]).astype(BF16)


def _inproj0(x, mod, layer, ctx_row, w_in, rope_tabs, tm):
    bsz, n, d = x.shape
    rope = rope_tabs is not None
    tok = lambda b, i: (b, i, 0)
    in_specs = [pl.BlockSpec((1, tm, d), tok), _mod_spec(mod, layer, MOD_SC1), _mod_spec(mod, layer, MOD_SH1)]
    args = [x, mod, mod]
    if rope:
        in_specs += [pl.BlockSpec((tm, ATT_HEAD_DIM), lambda b, i: (i, 0))] * 2
        args += list(rope_tabs)
    kv_at = (2 * LRU_WIDTH + ATT_Q_WIDTH) // ATT_KV_WIDTH
    in_specs += [_resident((d, LRU_WIDTH), lambda b, i: (0, 0)), _resident((d, LRU_WIDTH), lambda b, i: (0, 1)),
                 _resident((d, ATT_Q_WIDTH), lambda b, i: (0, 2)),
                 _resident((d, ATT_KV_WIDTH), lambda b, i: (0, kv_at)),
                 _resident((d, ATT_KV_WIDTH), lambda b, i: (0, kv_at + 1))]
    args += [w_in] * 5
    head_major = jax.ShapeDtypeStruct((bsz, LRU_HEADS, n, LRU_BLOCK), BF16)
    head_spec = pl.BlockSpec((1, LRU_HEADS, tm, LRU_BLOCK), lambda b, i: (b, 0, i, 0))
    nbytes = 2 * tm * d * 4 + 2 * w_in.size + 2 * 2 * tm * w_in.shape[1] + tm * w_in.shape[1] * 4
    return pl.pallas_call(
        functools.partial(_inproj0_kernel, rope=rope, ctx_row=ctx_row),
        grid=(bsz, n // tm),
        in_specs=in_specs,
        out_specs=[head_spec, head_spec,
                   pl.BlockSpec((1, tm, ATT_Q_WIDTH), tok),
                   pl.BlockSpec((1, tm, ATT_KV_WIDTH), tok),
                   pl.BlockSpec((1, tm, ATT_KV_WIDTH), tok)],
        out_shape=[head_major, head_major,
                   jax.ShapeDtypeStruct((bsz, n, ATT_Q_WIDTH), BF16),
                   jax.ShapeDtypeStruct((bsz, n, ATT_KV_WIDTH), BF16),
                   jax.ShapeDtypeStruct((bsz, n, ATT_KV_WIDTH), BF16)],
        compiler_params=_params(("arbitrary", "arbitrary"), nbytes),
        name="inproj0_rope" if rope else "inproj0_ctx",
    )(*args)


LRU_PAD = SUBLANES
LRU_TILE = 256


def _scan8(a, b, row, reverse):
    for k in (1, 2, 4):
        if reverse:
            keep = row < SUBLANES - k
            shift = SUBLANES - k
        else:
            keep = row >= k
            shift = k
        a_sh = jnp.where(keep, pltpu.roll(a, shift, 0), 1.0)
        b_sh = jnp.where(keep, pltpu.roll(b, shift, 0), 0.0)
        b = a * b_sh + b
        a = a * a_sh
    return a, b


def _scan8_rows(a, b, row, reverse):
    out_a, out_b = [], []
    for g in range(a.shape[0] // SUBLANES):
        sl = slice(g * SUBLANES, (g + 1) * SUBLANES)
        ag, bg = _scan8(a[sl], b[sl], row, reverse)
        out_a.append(ag)
        out_b.append(bg)
    return jnp.concatenate(out_a, axis=0), jnp.concatenate(out_b, axis=0)


def _lru_kernel(ul_ref, gl_ref, uc_ref, gc_ref, cw_ref, cb_ref, wg_ref, bg_ref, lam_ref,
                yl_ref, yc_ref, upad, af_s, bf_s, ar_s, br_s, cin_s, *, n_lat, n_ctx):
    cw = cw_ref[...]
    cb = cb_ref[...]
    wg = wg_ref[0]
    bg = bg_ref[0]
    lam = lam_ref[0]
    neg = -lam
    softplus = jnp.maximum(neg, 0.0) + jnp.log1p(jnp.exp(-jnp.abs(neg)))
    half_rate = (-0.5 * LRU_C) * softplus
    row = lax.broadcasted_iota(jnp.int32, (SUBLANES, LRU_BLOCK), 0)
    chains = ((af_s, bf_s, False), (ar_s, br_s, True))

    def coefficients(src_ref, n_rows, row_off):
        upad[pl.ds(0, LRU_PAD), :] = jnp.zeros((LRU_PAD, LRU_BLOCK), F32)
        upad[pl.ds(LRU_PAD, n_rows), :] = src_ref[0, 0].astype(F32)
        upad[pl.ds(LRU_PAD + n_rows, LRU_PAD), :] = jnp.zeros((LRU_PAD, LRU_BLOCK), F32)

        def tile(i, carry):
            t0 = pl.multiple_of(i * LRU_TILE, LRU_TILE)
            xp = upad[pl.ds(t0, LRU_TILE + 2 * LRU_PAD), :]
            conv = cb
            for k in range(LRU_CONV):
                o = LRU_PAD - LRU_CONV_LEFT + k
                conv = conv + cw[k:k + 1, :] * xp[o:o + LRU_TILE, :]
            z = _dot(conv.astype(BF16), wg) + bg
            half_conv = 0.5 * conv
            for d, (a_ref, b_ref, reverse) in enumerate(chains):
                t_a = jnp.tanh(z[:, (2 * d) * LRU_BLOCK:(2 * d + 1) * LRU_BLOCK])
                t_x = jnp.tanh(z[:, (2 * d + 1) * LRU_BLOCK:(2 * d + 2) * LRU_BLOCK])
                rate = half_rate[:, d * LRU_BLOCK:(d + 1) * LRU_BLOCK]
                a = jnp.exp(rate + rate * t_a)
                gated = half_conv + half_conv * t_x
                b = jnp.sqrt(1.0 - a * a) * gated
                a, b = _scan8_rows(a, b, row, reverse)
                a_ref[pl.ds(row_off + t0, LRU_TILE), :] = a
                b_ref[pl.ds(row_off + t0, LRU_TILE), :] = b
            return carry

        lax.fori_loop(0, n_rows // LRU_TILE, tile, 0)

    coefficients(uc_ref, n_ctx, 0)
    coefficients(ul_ref, n_lat, n_ctx)

    groups_ctx = n_ctx // SUBLANES
    groups = (n_ctx + n_lat) // SUBLANES
    fa = af_s[pl.ds(SUBLANES - 1, groups, stride=SUBLANES), :]
    fb = bf_s[pl.ds(SUBLANES - 1, groups, stride=SUBLANES), :]
    ra = ar_s[pl.ds(0, groups, stride=SUBLANES), :]
    rb = br_s[pl.ds(0, groups, stride=SUBLANES), :]
    zero = jnp.zeros((SUBLANES, LRU_BLOCK), F32)
    carry = zero
    for v in range(groups // SUBLANES):
        sl = slice(v * SUBLANES, (v + 1) * SUBLANES)
        a, b = _scan8(fa[sl], fb[sl], row, False)
        incl = b + a * carry
        cin_s[0, sl, :] = jnp.where(row == 0, carry, pltpu.roll(incl, 1, 0))
        carry = jnp.broadcast_to(incl[SUBLANES - 1:SUBLANES, :], incl.shape)
    carry = zero
    order = list(range(groups_ctx // SUBLANES - 1, -1, -1)) + list(range(groups // SUBLANES - 1,
                                                                        groups_ctx // SUBLANES - 1, -1))
    for v in order:
        sl = slice(v * SUBLANES, (v + 1) * SUBLANES)
        a, b = _scan8(ra[sl], rb[sl], row, True)
        incl = b + a * carry
        cin_s[1, sl, :] = jnp.where(row == SUBLANES - 1, carry, pltpu.roll(incl, SUBLANES - 1, 0))
        carry = jnp.broadcast_to(incl[0:1, :], incl.shape)

    def emit(g_ref, y_ref, n_rows, row_off):
        tile_groups = LRU_TILE // SUBLANES

        def tile(i, carry):
            t0 = pl.multiple_of(i * LRU_TILE, LRU_TILE)
            g0 = pl.multiple_of(row_off // SUBLANES + i * tile_groups, tile_groups)
            rows = pl.ds(row_off + t0, LRU_TILE)
            a_f, b_f, a_r, b_r = af_s[rows, :], bf_s[rows, :], ar_s[rows, :], br_s[rows, :]
            pieces = []
            for g in range(tile_groups):
                sl = slice(g * SUBLANES, (g + 1) * SUBLANES)
                c_f = jnp.broadcast_to(cin_s[0, pl.ds(g0 + g, 1), :], (SUBLANES, LRU_BLOCK))
                c_r = jnp.broadcast_to(cin_s[1, pl.ds(g0 + g, 1), :], (SUBLANES, LRU_BLOCK))
                pieces.append((b_f[sl] + a_f[sl] * c_f) + (b_r[sl] + a_r[sl] * c_r))
            h = jnp.concatenate(pieces, axis=0)
            gate = g_ref[0, 0, pl.ds(t0, LRU_TILE), :].astype(F32)
            y_ref[0, 0, pl.ds(t0, LRU_TILE), :] = (h * _gelu_tanh(gate)).astype(BF16)
            return carry

        lax.fori_loop(0, n_rows // LRU_TILE, tile, 0)

    emit(gc_ref, yc_ref, n_ctx, 0)
    emit(gl_ref, yl_ref, n_lat, n_ctx)


def _rglru(u_lat, g_lat, u_ctx, g_ctx, conv_w, conv_b, w_gate, b_gate, lam, casts=(), mods=()):
    bsz, heads, n_lat, blk = u_lat.shape
    n_ctx = u_ctx.shape[2]
    total = n_lat + n_ctx
    assert n_ctx % (SUBLANES * SUBLANES) == 0 and n_lat % LRU_TILE == 0 and n_ctx % LRU_TILE == 0
    seq = lambda n: pl.BlockSpec((1, 1, n, blk), lambda b, h: (b, h, 0, 0))
    per_head = lambda shape: pl.BlockSpec((1,) + shape, lambda b, h: (h, 0, 0))
    nbytes =4 * total * blk * 4 + (n_lat + 2 * LRU_PAD) * blk * 4 + 2 * 3 * 2 * total * blk * 2
    return _hosted_call(
        functools.partial(_lru_kernel, n_lat=n_lat, n_ctx=n_ctx),
        casts=casts, mods=mods,
        args=(u_lat, g_lat, u_ctx, g_ctx, conv_w, conv_b, w_gate, b_gate, lam),
        grid=(bsz, heads),
        in_specs=[seq(n_lat), seq(n_lat), seq(n_ctx), seq(n_ctx),
                  pl.BlockSpec((LRU_CONV, blk), lambda b, h: (0, h)),
                  pl.BlockSpec((1, blk), lambda b, h: (0, h)),
                  per_head((blk, 4 * blk)), per_head((1, 4 * blk)), per_head((1, 2 * blk))],
        out_specs=[seq(n_lat), seq(n_ctx)],
        out_shape=[jax.ShapeDtypeStruct(u_lat.shape, BF16), jax.ShapeDtypeStruct(u_ctx.shape, BF16)],
        scratch_shapes=[pltpu.VMEM((n_lat + 2 * LRU_PAD, blk), F32)]
                       + [pltpu.VMEM((total, blk), F32)] * 4
                       + [pltpu.VMEM((2, total // SUBLANES, blk), F32)],
        sem=("arbitrary", "arbitrary"), nbytes=nbytes,
        name="rglru",
    )


ATT_TQ = 512
ATT_BAND = 3 * ATT_BLOCK


def _dot_tn(a, b):
    return lax.dot_general(a, b, (((0,), (0,)), ((), ())), preferred_element_type=F32)


def _softmax_pv(parts, sink_row):
    dh = ATT_HEAD_DIM
    m = sink_row
    for s, _ in parts:
        m = jnp.maximum(m, jnp.max(s, axis=0, keepdims=True))
    acc = None
    for idx, (s, v) in enumerate(parts):
        p = jnp.exp2(s - m).astype(BF16)
        v_ext = jnp.concatenate([v, jnp.ones(v.shape, BF16)], axis=1)
        if idx == len(parts) - 1:
            pad = 2 * SUBLANES
            first = lax.broadcasted_iota(jnp.int32, (pad, m.shape[1]), 0) == 0
            p_sink = jnp.where(first, jnp.exp2(sink_row - m), 0.0).astype(BF16)
            p = jnp.concatenate([p, p_sink], axis=0)
            v_sink = jnp.concatenate([jnp.zeros((pad, dh), BF16), jnp.ones((pad, dh), BF16)], axis=1)
            v_ext = jnp.concatenate([v_ext, v_sink], axis=0)
        pv = _dot_tn(p, v_ext)
        acc = pv if acc is None else acc + pv
    return acc[:, :dh] / acc[:, dh:]


def _sink_rows(sink, reps):
    return jnp.repeat(sink.reshape(ATT_KV_HEADS, 1, ATT_GROUP), reps, axis=2)


def _attn_kernel(q_ref, k_ref, v_ref, kc_ref, vc_ref, sink_ref, o_ref, *, seq):
    tile = pl.program_id(1)
    blocks = ATT_TQ // ATT_BLOCK
    qcol = lax.broadcasted_iota(jnp.int32, (1, ATT_GROUP * ATT_BLOCK), 1) % ATT_BLOCK
    krow = lax.broadcasted_iota(jnp.int32, (ATT_BAND, 1), 0)
    for i in range(blocks):
        q0 = (tile * blocks + i) * ATT_BLOCK
        start = pl.multiple_of(jnp.clip(q0 - ATT_BLOCK, 0, seq - ATT_BAND), ATT_BLOCK)
        valid = jnp.abs((q0 + qcol) - (start + krow)) <= WINDOW
        for g in range(ATT_KV_HEADS):
            heads = [q_ref[0, i * ATT_BLOCK:(i + 1) * ATT_BLOCK,
                           (g * ATT_GROUP + r) * ATT_HEAD_DIM:(g * ATT_GROUP + r + 1) * ATT_HEAD_DIM]
                     for r in range(ATT_GROUP)]
            qs = jnp.concatenate(heads, axis=0)
            kv = slice(g * ATT_HEAD_DIM, (g + 1) * ATT_HEAD_DIM)
            s_loc = jnp.where(valid, _dot_nt(k_ref[0, pl.ds(start, ATT_BAND), kv], qs), NEG_INF)
            s_ctx = _dot_nt(kc_ref[0, :, kv], qs)
            o = _softmax_pv([(s_loc, v_ref[0, pl.ds(start, ATT_BAND), kv]), (s_ctx, vc_ref[0, :, kv])],
                            sink_ref[g] * LOG2E)
            for r in range(ATT_GROUP):
                col = (g * ATT_GROUP + r) * ATT_HEAD_DIM
                o_ref[0, i * ATT_BLOCK:(i + 1) * ATT_BLOCK, col:col + ATT_HEAD_DIM] = (
                    o[r * ATT_BLOCK:(r + 1) * ATT_BLOCK].astype(BF16))


def _window_attention(q, k, v, kc, vc, sink, casts=()):
    bsz, n, _ = q.shape
    sink_rows = _sink_rows(sink, ATT_BLOCK)
    lc = kc.shape[1]
    whole = lambda rows: pl.BlockSpec((1, rows, ATT_KV_WIDTH), lambda b, i: (b, 0, 0))
    nbytes = 2 * 2 * (2 * ATT_TQ * ATT_Q_WIDTH + 2 * n * ATT_KV_WIDTH + 2 * lc * ATT_KV_WIDTH) + 8 * 1024 * 1024
    return _hosted_call(
        functools.partial(_attn_kernel, seq=n),
        casts=casts,
        args=(q, k, v, kc, vc, sink_rows),
        grid=(bsz, n // ATT_TQ),
        in_specs=[pl.BlockSpec((1, ATT_TQ, ATT_Q_WIDTH), lambda b, i: (b, i, 0)),
                  whole(n), whole(n), whole(lc), whole(lc),
                  pl.BlockSpec(sink_rows.shape, lambda b, i: (0, 0, 0))],
        out_specs=[pl.BlockSpec((1, ATT_TQ, ATT_Q_WIDTH), lambda b, i: (b, i, 0))],
        out_shape=[jax.ShapeDtypeStruct(q.shape, BF16)],
        sem=("arbitrary", "arbitrary"), nbytes=nbytes,
        name="window_attention",
    )


def _ctx_attn_kernel(q_ref, kc_ref, vc_ref, sink_ref, o_ref):
    lc = q_ref.shape[1]
    for g in range(ATT_KV_HEADS):
        heads = [q_ref[0, :, (g * ATT_GROUP + r) * ATT_HEAD_DIM:(g * ATT_GROUP + r + 1) * ATT_HEAD_DIM]
                 for r in range(ATT_GROUP)]
        qs = jnp.concatenate(heads, axis=0)
        kv = slice(g * ATT_HEAD_DIM, (g + 1) * ATT_HEAD_DIM)
        o = _softmax_pv([(_dot_nt(kc_ref[0, :, kv], qs), vc_ref[0, :, kv])], sink_ref[g] * LOG2E)
        for r in range(ATT_GROUP):
            col = (g * ATT_GROUP + r) * ATT_HEAD_DIM
            o_ref[0, :, col:col + ATT_HEAD_DIM] = o[r * lc:(r + 1) * lc].astype(BF16)


def _context_attention(qc, kc, vc, sink):
    bsz, lc, _ = qc.shape
    sink_rows = _sink_rows(sink, lc)
    kv_spec = pl.BlockSpec((1, lc, ATT_KV_WIDTH), lambda b: (b, 0, 0))
    q_spec = pl.BlockSpec((1, lc, ATT_Q_WIDTH), lambda b: (b, 0, 0))
    return pl.pallas_call(
        _ctx_attn_kernel,
        grid=(bsz,),
        in_specs=[q_spec, kv_spec, kv_spec, pl.BlockSpec(sink_rows.shape, lambda b: (0, 0, 0))],
        out_specs=q_spec,
        out_shape=jax.ShapeDtypeStruct(qc.shape, BF16),
        compiler_params=_params(("arbitrary",), 8 * 1024 * 1024),
        name="context_attention",
    )(qc, kc, vc, sink_rows)


def _outproj0_kernel(ya_ref, yb_ref, x_ref, gate_ref, w_ref, lg_ref, lb_ref, o_ref, *, ctx_row):
    y = jnp.concatenate([ya_ref[0, h] for h in range(LRU_HEADS)] + [yb_ref[0]], axis=1)
    out = _dot(y, w_ref[...])
    o_ref[0] = _layer_norm(ALPHA * x_ref[0] + _mod_row(gate_ref, ctx_row) * out, lg_ref[...], lb_ref[...])


def _outproj0(ya, yb, x, mod, layer, ctx_row, w_out, ln_g, ln_b, tm):
    bsz, n, d = x.shape
    tok = lambda b, i: (b, i, 0)
    const = lambda b, i: (0, 0)
    nbytes = 2 * 2 * tm * d * 4 + w_out.size * 2 + 2 * 2 * tm * d * 2 + 2 * tm * d * 4
    return pl.pallas_call(
        functools.partial(_outproj0_kernel, ctx_row=ctx_row),
        grid=(bsz, n // tm),
        in_specs=[pl.BlockSpec((1, LRU_HEADS, tm, LRU_BLOCK), lambda b, i: (b, 0, i, 0)),
                  pl.BlockSpec((1, tm, ATT_Q_WIDTH), tok),
                  pl.BlockSpec((1, tm, d), tok),
                  _mod_spec(mod, layer, MOD_G1),
                  _resident(w_out.shape, const),
                  pl.BlockSpec((1, d), const), pl.BlockSpec((1, d), const)],
        out_specs=pl.BlockSpec((1, tm, d), tok),
        out_shape=jax.ShapeDtypeStruct(x.shape, F32),
        compiler_params=_params(("arbitrary", "arbitrary"), nbytes),
        name="outproj0_ln",
    )(ya, yb, x, mod, w_out, ln_g, ln_b)


def _ffn_kernel(x_ref, sc_ref, sh_ref, gate_ref, w1_ref, b1_ref, w2_ref, b2_ref, lg_ref, lb_ref,
                o_ref, h_s, *, ctx_row):
    j = pl.program_id(2)

    @pl.when(j == 0)
    def _():
        h_s[...] = _modulate(x_ref[0], sc_ref, sh_ref, ctx_row)
        o_ref[0] = jnp.zeros(o_ref.shape[1:], F32)

    a = jnp.maximum(_dot(h_s[...], w1_ref[...]) + b1_ref[...], 0.0)
    o_ref[0] += _dot((a * a).astype(BF16), w2_ref[...])

    @pl.when(j == pl.num_programs(2) - 1)
    def _():
        f = o_ref[0] + b2_ref[...]
        o_ref[0] = _layer_norm(ALPHA * x_ref[0] + _mod_row(gate_ref, ctx_row) * f, lg_ref[...], lb_ref[...])


def _ffn(x, mod, layer, ctx_row, w1, b1, w2, b2, ln_g, ln_b, tm, th, casts=()):
    bsz, n, d = x.shape
    hidden = w1.shape[1]
    tok = lambda b, i, j: (b, i, 0)
    const = lambda b, i, j: (0, 0)
    nbytes = 2 * 2 * tm * d * 4 + 2 * 2 * 2 * d * th * 2 + tm * d * 2 + 2 * tm * th * 4 + tm * d * 4
    return _hosted_call(
        functools.partial(_ffn_kernel, ctx_row=ctx_row),
        casts=casts,
        args=(x, mod, mod, mod, w1, b1, w2, b2, ln_g, ln_b),
        grid=(bsz, n // tm, hidden // th),
        in_specs=[pl.BlockSpec((1, tm, d), tok),
                  _mod_spec(mod, layer, MOD_SC2), _mod_spec(mod, layer, MOD_SH2), _mod_spec(mod, layer, MOD_G2),
                  pl.BlockSpec((d, th), lambda b, i, j: (0, j)),
                  pl.BlockSpec((1, th), lambda b, i, j: (0, j)),
                  pl.BlockSpec((th, d), lambda b, i, j: (j, 0)),
                  pl.BlockSpec((1, d), const), pl.BlockSpec((1, d), const), pl.BlockSpec((1, d), const)],
        out_specs=[pl.BlockSpec((1, tm, d), tok)],
        out_shape=[jax.ShapeDtypeStruct(x.shape, F32)],
        scratch_shapes=[pltpu.VMEM((tm, d), BF16)],
        sem=("arbitrary", "arbitrary", "arbitrary"), nbytes=nbytes,
        name="ffn_ln",
    )


SC_PARTS = SC_WIDTH // S5_WIDTH


def _inproj1_kernel(*refs, full, ctx_row):
    x_ref, sc_ref, sh_ref, wu_ref = refs[:4]
    u_s = refs[-1]
    h = _modulate(x_ref[0], sc_ref, sh_ref, ctx_row)
    u = _dot(h, wu_ref[...])
    if full:
        wb_refs = refs[4:4 + SC_PARTS]
        wc_refs = refs[4 + SC_PARTS:4 + 2 * SC_PARTS]
        wx_refs = refs[4 + 2 * SC_PARTS:4 + 3 * SC_PARTS]
        u_ref, uflat_ref, gb_ref, p_ref = refs[4 + 3 * SC_PARTS:-1]
        u_ref[0] = u.astype(BF16)
        for c in range(SC_PARTS):
            cols = slice(c * S5_WIDTH, (c + 1) * S5_WIDTH)
            gb_ref[0, :, cols] = _dot(h, wb_refs[c][...]).astype(BF16)
            p_ref[0, :, cols] = (_dot(h, wc_refs[c][...]) * _dot(h, wx_refs[c][...])).astype(BF16)
    else:
        uflat_ref, = refs[4:-1]
    for k in range(S5_SLABS):
        u_s[k] = u[:, k * LANES:(k + 1) * LANES]
    chunks = u.shape[0] // S5_CHUNK
    for t in range(S5_CHUNK):
        for k in range(S5_SLABS):
            col = t * S5_WIDTH + k * LANES
            uflat_ref[0, :, col:col + LANES] = u_s[k, pl.ds(t, chunks, stride=S5_CHUNK), :].astype(BF16)


def _inproj1(x, mod, layer, ctx_row, w_in, tm, full):
    bsz, n, d = x.shape
    tok = lambda b, i: (b, i, 0)
    n_blocks = 1 + 3 * SC_PARTS if full else 1
    flat_spec = pl.BlockSpec((1, tm // S5_CHUNK, S5_CHUNK * S5_WIDTH), tok)
    flat_shape = jax.ShapeDtypeStruct((bsz, n // S5_CHUNK, S5_CHUNK * S5_WIDTH), BF16)
    token = lambda w: (pl.BlockSpec((1, tm, w), tok), jax.ShapeDtypeStruct((bsz, n, w), BF16))
    outs = ([token(S5_WIDTH), (flat_spec, flat_shape), token(SC_WIDTH), token(SC_WIDTH)] if full
            else [(flat_spec, flat_shape)])
    ncols = n_blocks * S5_WIDTH
    nbytes = 2 * tm * d * 4 + 2 * d * ncols + 2 * 2 * tm * (ncols + S5_WIDTH) + tm * ncols * 4 + tm * S5_WIDTH * 4
    return pl.pallas_call(
        functools.partial(_inproj1_kernel, full=full, ctx_row=ctx_row),
        grid=(bsz, n // tm),
        in_specs=[pl.BlockSpec((1, tm, d), tok), _mod_spec(mod, layer, MOD_SC1), _mod_spec(mod, layer, MOD_SH1)]
                 + [_resident((d, S5_WIDTH), lambda b, i, c=c: (0, c)) for c in range(n_blocks)],
        out_specs=[spec for spec, _ in outs],
        out_shape=[shape for _, shape in outs],
        scratch_shapes=[pltpu.VMEM((S5_SLABS, tm, LANES), F32)],
        compiler_params=_params(("arbitrary", "arbitrary"), nbytes),
        name="inproj1" if full else "inproj1_ctx",
    )(x, mod, mod, *([w_in] * n_blocks))


def _s5_power_table(par_ref):
    rows = 2 * S5_POW_ROWS
    r = lax.broadcasted_iota(jnp.int32, (rows, S5_SLAB_STATE), 0)
    first = r < S5_POW_ROWS
    lag = (r % S5_POW_ROWS).astype(F32)
    pick = lambda i: jnp.where(first, par_ref[0, 0, i:i + 1, :], par_ref[0, 1, i:i + 1, :])
    dt = jnp.exp(pick(0))
    mag = jnp.exp(lag * dt * pick(1))
    ang = lag * dt * pick(2)
    return mag * jnp.cos(ang), mag * jnp.sin(ang)


def _s5_input_matrix(par_ref, bt_ref, d, pw_re, pw_im):
    a_re = par_ref[0, d, 1:2, :]
    a_im = par_ref[0, d, 2:3, :]
    ab_re = pw_re[d * S5_POW_ROWS + 1:d * S5_POW_ROWS + 2, :]
    ab_im = pw_im[d * S5_POW_ROWS + 1:d * S5_POW_ROWS + 2, :]
    den = a_re * a_re + a_im * a_im
    k_re = ((ab_re - 1.0) * a_re + ab_im * a_im) / den
    k_im = (ab_im * a_re - (ab_re - 1.0) * a_im) / den
    b_re = bt_ref[0, d, 0]
    b_im = bt_ref[0, d, 1]
    return k_re * b_re - k_im * b_im, k_re * b_im + k_im * b_re


def _cmul_row(x_re, x_im, p_re, p_im):
    return x_re * p_re - x_im * p_im, x_re * p_im + x_im * p_re


def _s5_state_prep_kernel(par_ref, bt_ref, wst_ref, a16_ref):
    pw_re, pw_im = _s5_power_table(par_ref)
    for d in range(2):
        bb_re, bb_im = _s5_input_matrix(par_ref, bt_ref, d, pw_re, pw_im)
        base = d * S5_POW_ROWS
        for lag in range(S5_CHUNK):
            e_re, e_im = _cmul_row(bb_re, bb_im, pw_re[base + lag:base + lag + 1, :],
                                   pw_im[base + lag:base + lag + 1, :])
            s = S5_CHUNK - 1 - lag if d == 0 else lag
            col = 2 * d * S5_SLAB_STATE
            wst_ref[0, s * LANES:(s + 1) * LANES, col:col + S5_SLAB_STATE] = e_re.astype(BF16)
            wst_ref[0, s * LANES:(s + 1) * LANES, col + S5_SLAB_STATE:col + 2 * S5_SLAB_STATE] = e_im.astype(BF16)
        row16 = base + S5_CHUNK
        a16_ref[0, :, 2 * d * S5_SLAB_STATE:(2 * d + 1) * S5_SLAB_STATE] = jnp.broadcast_to(
            pw_re[row16:row16 + 1, :], (SUBLANES, S5_SLAB_STATE))
        a16_ref[0, :, (2 * d + 1) * S5_SLAB_STATE:(2 * d + 2) * S5_SLAB_STATE] = jnp.broadcast_to(
            pw_im[row16:row16 + 1, :], (SUBLANES, S5_SLAB_STATE))


def _s5_output_prep_kernel(par_ref, bt_ref, c_ref, ct_ref, wbig_ref):
    pw_re, pw_im = _s5_power_table(par_ref)
    kern = []
    for d in range(2):
        bb_re, bb_im = _s5_input_matrix(par_ref, bt_ref, d, pw_re, pw_im)
        c_re = c_ref[0, d, 0].astype(BF16)
        c_im = c_ref[0, d, 1].astype(BF16)
        base = d * S5_POW_ROWS
        per_lag = []
        for lag in range(S5_CHUNK):
            e_re, e_im = _cmul_row(bb_re, bb_im, pw_re[base + lag:base + lag + 1, :],
                                   pw_im[base + lag:base + lag + 1, :])
            per_lag.append(_dot_nt(e_re.astype(BF16), c_re) - _dot_nt(e_im.astype(BF16), c_im))
        kern.append(per_lag)
    for s in range(S5_CHUNK):
        for t in range(S5_CHUNK):
            if s < t:
                blk = kern[0][t - s]
            elif s > t:
                blk = kern[1][s - t]
            else:
                blk = kern[0][0] + kern[1][0]
            wbig_ref[0, s * LANES:(s + 1) * LANES, t * LANES:(t + 1) * LANES] = blk.astype(BF16)
    pt_re = pw_re.T
    pt_im = pw_im.T
    for d in range(2):
        ct_re = ct_ref[0, d, 0]
        ct_im = ct_ref[0, d, 1]
        for t in range(S5_CHUNK):
            lag = t + 1 if d == 0 else S5_CHUNK - t
            col = d * S5_POW_ROWS + lag
            p_re = pt_re[:, col:col + 1]
            p_im = pt_im[:, col:col + 1]
            g_re = ct_re * p_re - ct_im * p_im
            g_im = ct_re * p_im + ct_im * p_re
            r0 = S5_XK + 2 * d * S5_SLAB_STATE
            wbig_ref[0, r0:r0 + S5_SLAB_STATE, t * LANES:(t + 1) * LANES] = g_re.astype(BF16)
            wbig_ref[0, r0 + S5_SLAB_STATE:r0 + 2 * S5_SLAB_STATE, t * LANES:(t + 1) * LANES] = (-g_im).astype(BF16)


def _s5_operators(log_dt, a_re, a_im, b_re, b_im, c_re, c_im):
    gs = S5_SLAB_GROUPS
    eye = jnp.eye(gs, dtype=F32)

    def lanes(v):
        return v.reshape(2, S5_SLABS, gs * S5_STATE)

    par = jnp.stack([lanes(jnp.broadcast_to(log_dt[:, :, None], a_re.shape)), lanes(a_re), lanes(a_im)], axis=2)
    par = jnp.pad(par, ((0, 0), (0, 0), (0, SUBLANES - 3), (0, 0))).transpose(1, 0, 2, 3)

    def embed_bt(b):
        b = b.reshape(2, S5_SLABS, gs, S5_STATE, S5_GROUP)
        e = b.transpose(0, 1, 2, 4, 3)[:, :, :, :, None, :] * eye[None, None, :, None, :, None]
        return e.reshape(2, S5_SLABS, gs * S5_GROUP, gs * S5_STATE).transpose(1, 0, 2, 3)

    def embed_c(c):
        c = c.reshape(2, S5_SLABS, gs, S5_GROUP, S5_STATE)
        e = c[:, :, :, :, None, :] * eye[None, None, :, None, :, None]
        return e.reshape(2, S5_SLABS, gs * S5_GROUP, gs * S5_STATE).transpose(1, 0, 2, 3)

    bt = jnp.stack([embed_bt(b_re), embed_bt(b_im)], axis=2)
    cm = jnp.stack([embed_c(c_re), embed_c(c_im)], axis=2)
    ct = cm.transpose(0, 1, 2, 4, 3)

    slab5 = lambda shape: pl.BlockSpec((1,) + shape, lambda k: (k, 0, 0, 0, 0))
    par_spec = pl.BlockSpec((1, 2, SUBLANES, S5_SLAB_STATE), lambda k: (k, 0, 0, 0))
    bt_spec = slab5((2, 2, LANES, S5_SLAB_STATE))
    wst, a16 = pl.pallas_call(
        _s5_state_prep_kernel,
        grid=(S5_SLABS,),
        in_specs=[par_spec, bt_spec],
        out_specs=[pl.BlockSpec((1, S5_XK, S5_HK), lambda k: (k, 0, 0)),
                   pl.BlockSpec((1, SUBLANES, S5_HK), lambda k: (k, 0, 0))],
        out_shape=[jax.ShapeDtypeStruct((S5_SLABS, S5_XK, S5_HK), BF16),
                   jax.ShapeDtypeStruct((S5_SLABS, SUBLANES, S5_HK), F32)],
        compiler_params=_params(("arbitrary",), 2 * S5_XK * S5_HK * 2 + 8 * 1024 * 1024),
        name="s5_state_operator",
    )(par, bt)
    wbig = pl.pallas_call(
        _s5_output_prep_kernel,
        grid=(S5_SLABS,),
        in_specs=[par_spec, bt_spec, bt_spec, slab5((2, 2, S5_SLAB_STATE, LANES))],
        out_specs=pl.BlockSpec((1, S5_XK + S5_HK, S5_XK), lambda k: (k, 0, 0)),
        out_shape=jax.ShapeDtypeStruct((S5_SLABS, S5_XK + S5_HK, S5_XK), BF16),
        compiler_params=_params(("arbitrary",), 2 * (S5_XK + S5_HK) * S5_XK * 2 + 8 * 1024 * 1024),
        name="s5_output_operator",
    )(par, bt, cm, ct)
    return wst, a16, wbig


def _s5_states_kernel(*refs, n_ctx, n_lat):
    xc_refs = refs[:S5_CHUNK]
    xl_refs = refs[S5_CHUNK:2 * S5_CHUNK]
    wst_ref, a16_ref, h_ref, s_s, h_s = refs[2 * S5_CHUNK:]
    x = jnp.concatenate([jnp.concatenate([r[0] for r in xc_refs], axis=1),
                         jnp.concatenate([r[0] for r in xl_refs], axis=1)], axis=0)
    s_s[...] = _dot(x, wst_ref[0])
    p = S5_SLAB_STATE
    af_re = a16_ref[0, 0:1, 0:p]
    af_im = a16_ref[0, 0:1, p:2 * p]
    ar_re = a16_ref[0, 0:1, 2 * p:3 * p]
    ar_im = a16_ref[0, 0:1, 3 * p:4 * p]
    total = n_ctx + n_lat

    def step(i, carry):
        f_re, f_im, r_re, r_im = carry
        h_s[pl.ds(i, 1), 0:p] = f_re
        h_s[pl.ds(i, 1), p:2 * p] = f_im
        s_re = s_s[pl.ds(i, 1), 0:p]
        s_im = s_s[pl.ds(i, 1), p:2 * p]
        f_re, f_im = af_re * f_re - af_im * f_im + s_re, af_re * f_im + af_im * f_re + s_im
        j = jnp.where(i < n_ctx, n_ctx - 1 - i, total + n_ctx - 1 - i)
        h_s[pl.ds(j, 1), 2 * p:3 * p] = r_re
        h_s[pl.ds(j, 1), 3 * p:4 * p] = r_im
        s_re = s_s[pl.ds(j, 1), 2 * p:3 * p]
        s_im = s_s[pl.ds(j, 1), 3 * p:4 * p]
        r_re, r_im = ar_re * r_re - ar_im * r_im + s_re, ar_re * r_im + ar_im * r_re + s_im
        return f_re, f_im, r_re, r_im

    zero = jnp.zeros((1, p), F32)
    lax.fori_loop(0, total, step, (zero, zero, zero, zero), unroll=4)
    h_ref[0, 0] = h_s[pl.ds(n_ctx, n_lat), :].astype(BF16)


def _s5_states(uc_flat, ul_flat, wst, a16):
    bsz, n_ctx, _ = uc_flat.shape
    n_lat = ul_flat.shape[1]
    piece = lambda rows, s: pl.BlockSpec((1, rows, LANES), lambda k, b, s=s: (b, 0, s * S5_SLABS + k))
    in_specs = ([piece(n_ctx, s) for s in range(S5_CHUNK)] + [piece(n_lat, s) for s in range(S5_CHUNK)]
                + [pl.BlockSpec((1, S5_XK, S5_HK), lambda k, b: (k, 0, 0)),
                   pl.BlockSpec((1, SUBLANES, S5_HK), lambda k, b: (k, 0, 0))])
    total = n_ctx + n_lat
    nbytes = 2 * S5_XK * S5_HK * 2 + 2 * total * S5_XK * 2 + 3 * total * S5_HK * 4 + 2 * n_lat * S5_HK * 2
    return pl.pallas_call(
        functools.partial(_s5_states_kernel, n_ctx=n_ctx, n_lat=n_lat),
        grid=(S5_SLABS, bsz),
        in_specs=in_specs,
        out_specs=pl.BlockSpec((1, 1, n_lat, S5_HK), lambda k, b: (b, k, 0, 0)),
        out_shape=jax.ShapeDtypeStruct((bsz, S5_SLABS, n_lat, S5_HK), BF16),
        scratch_shapes=[pltpu.VMEM((total, S5_HK), F32), pltpu.VMEM((total, S5_HK), F32)],
        compiler_params=_params(("arbitrary", "arbitrary"), nbytes),
        name="s5_states",
    )(*([uc_flat] * S5_CHUNK + [ul_flat] * S5_CHUNK + [wst, a16]))


def _s5_readout_kernel(*refs):
    x_refs = refs[:S5_CHUNK]
    h_ref, w_ref, y_ref = refs[S5_CHUNK:]
    lhs = jnp.concatenate([r[0] for r in x_refs] + [h_ref[0, 0]], axis=1)
    y = _dot(lhs, w_ref[0])
    chunks = y.shape[0]
    for t in range(S5_CHUNK):
        y_ref[0, pl.ds(t, chunks, stride=S5_CHUNK), :] = y[:, t * LANES:(t + 1) * LANES]


def _s5_readout(ul_flat, h_in, wbig):
    bsz, n_lat, _ = ul_flat.shape
    piece = lambda s: pl.BlockSpec((1, n_lat, LANES), lambda k, b, s=s: (b, 0, s * S5_SLABS + k))
    nbytes = 2 * (S5_XK + S5_HK) * S5_XK * 2 + 2 * n_lat * (S5_XK + S5_HK) * 2 * 2 + n_lat * S5_XK * 4 * 3
    return pl.pallas_call(
        _s5_readout_kernel,
        grid=(S5_SLABS, bsz),
        in_specs=[piece(s) for s in range(S5_CHUNK)]
                 + [pl.BlockSpec((1, 1, n_lat, S5_HK), lambda k, b: (b, k, 0, 0)),
                    pl.BlockSpec((1, S5_XK + S5_HK, S5_XK), lambda k, b: (k, 0, 0))],
        out_specs=pl.BlockSpec((1, n_lat * S5_CHUNK, LANES), lambda k, b: (b, 0, k)),
        out_shape=jax.ShapeDtypeStruct((bsz, n_lat * S5_CHUNK, S5_WIDTH), F32),
        compiler_params=_params(("arbitrary", "arbitrary"), nbytes),
        name="s5_readout",
    )(*([ul_flat] * S5_CHUNK + [h_in, wbig]))


SC_HALO = 16


def _outproj1_kernel(y_ref, u_ref, gb_ref, p_ref, pprev_ref, pnext_ref, x_ref, gate_ref,
                     dskip_ref, wglu_ref, bglu_ref, cw_ref, cb_ref, w_ref, lg_ref, lb_ref, o_ref, *, ctx_row):
    i = pl.program_id(1)
    tm = x_ref.shape[1]
    yc = y_ref[0] + dskip_ref[...] * u_ref[0].astype(F32)
    z = _gelu_tanh(yc)
    y_c = z * _sigmoid(_dot(z.astype(BF16), wglu_ref[...]) + bglu_ref[...])
    p = p_ref[0].astype(F32)
    row = lax.broadcasted_iota(jnp.int32, (tm, 1), 0)
    prev_row = jnp.where(i > 0, pprev_ref[0, SC_HALO - 1:SC_HALO, :].astype(F32), 0.0)
    next_row = jnp.where(i < pl.num_programs(1) - 1, pnext_ref[0, 0:1, :].astype(F32), 0.0)
    p_dn = jnp.where(row == 0, prev_row, pltpu.roll(p, 1, 0))
    p_up = jnp.where(row == tm - 1, next_row, pltpu.roll(p, tm - 1, 0))
    conv = cb_ref[...] + cw_ref[0:1, :] * p_dn + cw_ref[1:2, :] * p + cw_ref[2:3, :] * p_up
    y_d = gb_ref[0].astype(F32) * conv
    y = jnp.concatenate([y_c.astype(BF16), y_d.astype(BF16)], axis=1)
    out = _dot(y, w_ref[...])
    o_ref[0] = _layer_norm(ALPHA * x_ref[0] + _mod_row(gate_ref, ctx_row) * out, lg_ref[...], lb_ref[...])


def _outproj1(y, u, gb, p, x, mod, layer, ctx_row, d_skip, w_glu, b_glu, conv_w, conv_b, w_out, ln_g, ln_b, tm):
    bsz, n, d = x.shape
    tok = lambda b, i: (b, i, 0)
    const = lambda b, i: (0, 0)
    per = tm // SC_HALO
    last = n // SC_HALO - 1
    nbytes = (2 * 2 * tm * d * 4 + w_out.size * 2 + 2 * 2 * tm * (2 * S5_WIDTH + 2 * SC_WIDTH) * 2
              + 6 * tm * SC_WIDTH * 4 + 2 * tm * d * 4)
    return pl.pallas_call(
        functools.partial(_outproj1_kernel, ctx_row=ctx_row),
        grid=(bsz, n // tm),
        in_specs=[pl.BlockSpec((1, tm, S5_WIDTH), tok), pl.BlockSpec((1, tm, S5_WIDTH), tok),
                  pl.BlockSpec((1, tm, SC_WIDTH), tok), pl.BlockSpec((1, tm, SC_WIDTH), tok),
                  pl.BlockSpec((1, SC_HALO, SC_WIDTH), lambda b, i: (b, jnp.maximum(i * per - 1, 0), 0)),
                  pl.BlockSpec((1, SC_HALO, SC_WIDTH), lambda b, i: (b, jnp.minimum((i + 1) * per, last), 0)),
                  pl.BlockSpec((1, tm, d), tok), _mod_spec(mod, layer, MOD_G1),
                  pl.BlockSpec((1, S5_WIDTH), const), _resident(w_glu.shape, const),
                  pl.BlockSpec((1, S5_WIDTH), const),
                  pl.BlockSpec((SC_CONV, SC_WIDTH), const), pl.BlockSpec((1, SC_WIDTH), const),
                  _resident(w_out.shape, const),
                  pl.BlockSpec((1, d), const), pl.BlockSpec((1, d), const)],
        out_specs=pl.BlockSpec((1, tm, d), tok),
        out_shape=jax.ShapeDtypeStruct(x.shape, F32),
        compiler_params=_params(("arbitrary", "arbitrary"), nbytes),
        name="outproj1_ln",
    )(y, u, gb, p, p, p, x, mod, d_skip, w_glu, b_glu, conv_w, conv_b, w_out, ln_g, ln_b)


def _rope_tables(n):
    t = np.arange(n)
    inv = ROPE_BASE ** (-np.arange(ROPE_FREQS, dtype=np.float64) / ROPE_FREQS)
    ang_r = (t // GRID_W)[:, None] * inv[None, :]
    ang_c = (t % GRID_W)[:, None] * inv[None, :]
    cos = np.concatenate([np.cos(ang_r), np.cos(ang_r), np.cos(ang_c), np.cos(ang_c)], axis=1)
    sin = np.concatenate([-np.sin(ang_r), np.sin(ang_r), -np.sin(ang_c), np.sin(ang_c)], axis=1)
    return jnp.asarray(cos, F32), jnp.asarray(sin, F32)


TM_LATENT = 512
TM_CONTEXT = 256
FFN_TH = 2048


def kernel(x, c, ctx, c_ctx, mod_w, mod_b, ln1_g, ln1_b, ln2_g, ln2_b, ffn_w1, ffn_b1, ffn_w2, ffn_b2,
           ab_w_in, ab_w_out, lru_conv_w, lru_conv_b, lru_w_a, lru_b_a, lru_w_x, lru_b_x, lru_lam, att_sink,
           cd_w_in, cd_w_out, s5_log_dt, s5_a_re, s5_a_im, s5_b_re, s5_b_im, s5_c_re, s5_c_im, s5_d,
           s5_w_glu, s5_b_glu, sc_conv_w, sc_conv_b):
    bsz, n, d = x.shape
    lc = ctx.shape[1]
    assert n % TM_LATENT == 0 and lc % TM_CONTEXT == 0 and (bsz * lc) % TM_LATENT == 0 and n % GRID_W == 0

    pad_rows = SUBLANES - (bsz + 1) % SUBLANES if (bsz + 1) % SUBLANES else 0
    cc = jnp.concatenate([c, c_ctx[None, :], jnp.zeros((pad_rows, d), F32)], axis=0)
    mods = {0: _modulation(cc, mod_w, mod_b, 0)}
    ctx_row = bsz

    row2 = lambda v: v.reshape(1, -1)
    rope_tabs = _rope_tables(n)

    side = {}

    def weight(stack, name, idx):
        return side.pop((name, idx)) if (name, idx) in side else _to_bf16(stack, idx)

    def hosted(call, own, jobs):
        outs = call(casts=[(stack, idx) for _, idx, stack in jobs])
        side.update({(name, idx): w for (name, idx, _), w in zip(jobs, outs[own:own + len(jobs)])})
        return outs[:own] + outs[own + len(jobs):]

    for i in range(DEPTH):
        last = i == DEPTH - 1
        j = i // 2
        mod = mods.pop(i) if i in mods else _modulation(cc, mod_w, mod_b, i)
        todo = [("w1", i, ffn_w1), ("w2", i, ffn_w2)]
        if not last:
            nj = (i + 1) // 2
            todo += [("w1", i + 1, ffn_w1), ("w2", i + 1, ffn_w2)]
            todo += ([("ab_in", nj, ab_w_in), ("ab_out", nj, ab_w_out)] if (i + 1) % 2 == 0
                     else [("cd_in", nj, cd_w_in), ("cd_out", nj, cd_w_out)])
        todo = [job for job in todo if (job[0], job[1]) not in side]
        if i % 2 == 0:
            w_in = weight(ab_w_in, "ab_in", j)
            w_out = weight(ab_w_out, "ab_out", j)
            u, gate, q, k, v = _inproj0(x, mod, 0, None, w_in, rope_tabs, TM_LATENT)
            uc, gatec, qc, kc, vc = _inproj0(ctx, mod, 0, ctx_row, w_in, None, TM_CONTEXT)
            w_gate = (0.5 * jnp.concatenate([lru_w_a[j, 0], lru_w_x[j, 0], lru_w_a[j, 1], lru_w_x[j, 1]],
                                            axis=-1)).astype(BF16)
            hb = lambda b: b.reshape(LRU_HEADS, 1, LRU_BLOCK)
            b_gate = 0.5 * jnp.concatenate([hb(lru_b_a[j, 0]), hb(lru_b_x[j, 0]), hb(lru_b_a[j, 1]),
                                            hb(lru_b_x[j, 1])], axis=-1)
            lam = jnp.concatenate([hb(lru_lam[j, 0]), hb(lru_lam[j, 1])], axis=-1)
            ahead_mod = [] if last else [(cc, mod_w, mod_b, i + 1)]
            *lru_out, = hosted(functools.partial(_rglru, u, gate, uc, gatec, lru_conv_w[j], row2(lru_conv_b[j]),
                                                 w_gate, b_gate, lam, mods=ahead_mod), 2, todo[0::2])
            ya, yac = lru_out[:2]
            mods.update({i + 1: m for m in lru_out[2:]})
            yb, = hosted(functools.partial(_window_attention, q, k, v, kc, vc, att_sink[j]), 1, todo[1::2])
            todo = []
            x = _outproj0(ya, yb, x, mod, 0, None, w_out, row2(ln1_g[i]), row2(ln1_b[i]), TM_LATENT)
            if not last:
                ybc = _context_attention(qc, kc, vc, att_sink[j])
                ctx = _outproj0(yac, ybc, ctx, mod, 0, ctx_row, w_out, row2(ln1_g[i]), row2(ln1_b[i]), TM_CONTEXT)
        else:
            assert last
            w_in = weight(cd_w_in, "cd_in", j)
            w_out = weight(cd_w_out, "cd_out", j)
            u, u_flat, gb, p = _inproj1(x, mod, 0, None, w_in, TM_LATENT, True)
            (uc_flat,) = _inproj1(ctx, mod, 0, ctx_row, w_in, TM_CONTEXT, False)
            wst, a16, wbig = _s5_operators(s5_log_dt[j], s5_a_re[j], s5_a_im[j], s5_b_re[j], s5_b_im[j],
                                           s5_c_re[j], s5_c_im[j])
            h_in = _s5_states(uc_flat, u_flat, wst, a16)
            y = _s5_readout(u_flat, h_in, wbig)
            x = _outproj1(y, u, gb, p, x, mod, 0, None, row2(s5_d[j]), s5_w_glu[j].astype(BF16), row2(s5_b_glu[j]),
                          sc_conv_w[j], row2(sc_conv_b[j]), w_out, row2(ln1_g[i]), row2(ln1_b[i]), TM_LATENT)
        w1 = weight(ffn_w1, "w1", i)
        w2 = weight(ffn_w2, "w2", i)
        ahead = [job for job in todo if job[1] != i or job[0] not in ("w1", "w2")]
        x, = hosted(functools.partial(_ffn, x, mod, 0, None, w1, row2(ffn_b1[i]), w2, row2(ffn_b2[i]),
                                      row2(ln2_g[i]), row2(ln2_b[i]), TM_LATENT, FFN_TH), 1, ahead)
        if not last:
            rows = ctx.reshape(1, bsz * lc, d)
            rows, = _ffn(rows, mod, 0, ctx_row, w1, row2(ffn_b1[i]), w2, row2(ffn_b2[i]), row2(ln2_g[i]),
                         row2(ln2_b[i]), TM_LATENT, FFN_TH)
            ctx = rows.reshape(bsz, lc, d)
    return x
```

```python
import functools
import math

import jax
import jax.numpy as jnp
import numpy as np
from jax import lax
from jax.experimental import pallas as pl
from jax.experimental.pallas import tpu as pltpu

F32 = jnp.float32
BF16 = jnp.bfloat16

D_MODEL = 2048
DEPTH = 2
GRID_W = 64
LRU_WIDTH = D_MODEL // 2
LRU_HEADS = 8
LRU_BLOCK = LRU_WIDTH // LRU_HEADS
LRU_CONV = 4
LRU_CONV_LEFT = 2
LRU_C = 8.0
ATT_HEAD_DIM = 128
ATT_Q_HEADS = (D_MODEL // 2) // ATT_HEAD_DIM
ATT_KV_HEADS = 2
ATT_GROUP = ATT_Q_HEADS // ATT_KV_HEADS
ATT_Q_WIDTH = ATT_Q_HEADS * ATT_HEAD_DIM
ATT_KV_WIDTH = ATT_KV_HEADS * ATT_HEAD_DIM
WINDOW = 128
ATT_BLOCK = 128
ROPE_BASE = 10000.0
ROPE_FREQS = ATT_HEAD_DIM // 4
S5_WIDTH = D_MODEL // 4
S5_GROUP = 16
S5_GROUPS = S5_WIDTH // S5_GROUP
S5_STATE = 64
SC_WIDTH = D_MODEL - S5_WIDTH
SC_CONV = 3
FFN_HIDDEN = 4 * D_MODEL
ALPHA = (2.0 * DEPTH) ** 0.25
LN_EPS = 1e-5
NEG_INF = -1e30
LOG2E = math.log2(math.e)

LANES = 128
SUBLANES = 8
V7X_VMEM_BYTES = 64 * 1024 * 1024
V7X_VMEM_BUDGET = 62 * 1024 * 1024

S5_CHUNK = 16
S5_SLAB_GROUPS = LANES // S5_GROUP
S5_SLABS = S5_WIDTH // LANES
S5_SLAB_STATE = S5_SLAB_GROUPS * S5_STATE
S5_XK = S5_CHUNK * LANES
S5_HK = 4 * S5_SLAB_STATE
S5_POW_ROWS = 64


def _vmem_limit(nbytes):
    return int(min(V7X_VMEM_BUDGET, max(nbytes * 3 // 2, 16 * 1024 * 1024)))


def _resident(shape, index_map):
    return pl.BlockSpec(shape, index_map, pipeline_mode=pl.Buffered(1))


def _params(sem, nbytes):
    return pltpu.CompilerParams(dimension_semantics=sem, vmem_limit_bytes=_vmem_limit(nbytes))


def _dot(a, b):
    return jnp.dot(a, b, preferred_element_type=F32)


def _dot_nt(a, b):
    return lax.dot_general(a, b, (((1,), (1,)), ((), ())), preferred_element_type=F32)


def _layer_norm(v, g, b):
    mu = jnp.mean(v, axis=-1, keepdims=True)
    c = v - mu
    var = jnp.mean(c * c, axis=-1, keepdims=True)
    return c * lax.rsqrt(var + LN_EPS) * g + b


def _gelu_tanh(x):
    return 0.5 * x * (1.0 + jnp.tanh(math.sqrt(2.0 / math.pi) * (x + 0.044715 * (x * x * x))))


def _sigmoid(x):
    return 0.5 * (1.0 + jnp.tanh(0.5 * x))


MOD_SH1, MOD_SC1, MOD_G1, MOD_SH2, MOD_SC2, MOD_G2 = range(6)


def _mod_spec(mod, chunk):
    _, rows, width = mod.shape
    return pl.BlockSpec((1, rows, width // 6), lambda *_: (0, 0, chunk))


def _mod_row(ref, ctx_row):
    row = pl.program_id(0) if ctx_row is None else ctx_row
    return ref[0, pl.ds(row, 1), :]


def _modulate(x, sc_ref, sh_ref, ctx_row):
    return (x * (1.0 + _mod_row(sc_ref, ctx_row)) + _mod_row(sh_ref, ctx_row)).astype(BF16)


CAST_BLOCK_BYTES = 8 * 1024 * 1024


def _cast_kernel(w_ref, o_ref):
    o_ref[...] = w_ref[0].astype(BF16)


def _to_bf16(w, layer):
    _, rows, cols = w.shape
    tr = rows
    while tr * cols * 4 > CAST_BLOCK_BYTES and tr % 2 == 0 and tr // 2 >= 2 * SUBLANES:
        tr //= 2
    return pl.pallas_call(
        _cast_kernel,
        grid=(rows // tr,),
        in_specs=[pl.BlockSpec((1, tr, cols), lambda i: (layer, i, 0))],
        out_specs=pl.BlockSpec((tr, cols), lambda i: (i, 0)),
        out_shape=jax.ShapeDtypeStruct((rows, cols), BF16),
        compiler_params=_params(("arbitrary",), 2 * tr * cols * 6),
        name="weight_cast",
    )(w)


CAST_ROWS = 2 * SUBLANES


def _silu_bf16(v):
    return (v * _sigmoid(v)).astype(BF16)


def _hosted_call(body, *, grid, in_specs, out_specs, out_shape, args, sem, nbytes, casts=(), mods=(), **kwargs):
    steps = math.prod(grid)

    def flat(*ids):
        step = 0
        for extent, i in zip(grid, ids):
            step = step * extent + i
        return step

    def upto(used):
        return lambda *ids: jnp.minimum(flat(*ids), used - 1)

    side_in, side_out, side_shape, side_args, jobs = [], [], [], [], []
    for w, layer in casts:
        _, rows, cols = w.shape
        rps = max(CAST_ROWS, rows // steps)
        assert rows % rps == 0 and rows // rps <= steps
        at = upto(rows // rps)
        side_in.append(pl.BlockSpec((1, rps, cols), lambda *ids, at=at, layer=layer: (layer, at(*ids), 0)))
        side_out.append(pl.BlockSpec((rps, cols), lambda *ids, at=at: (at(*ids), 0)))
        side_shape.append(jax.ShapeDtypeStruct((rows, cols), BF16))
        side_args.append(w)
        nbytes += 2 * rps * cols * (4 + 2)
        jobs.append((1, lambda ins, out: out.__setitem__(Ellipsis, ins[0][0].astype(BF16))))
    for cc, mod_w, mod_b, layer in mods:
        depth, d, n = mod_w.shape
        width = LANES * max(1, n // LANES // steps)
        assert n % width == 0 and n // width <= steps
        at = upto(n // width)
        side_in += [pl.BlockSpec(cc.shape, lambda *ids: (0, 0)),
                    pl.BlockSpec((1, d, width), lambda *ids, at=at, layer=layer: (layer, 0, at(*ids))),
                    pl.BlockSpec((1, 1, width), lambda *ids, at=at, layer=layer: (layer, 0, at(*ids)))]
        side_out.append(pl.BlockSpec((1, cc.shape[0], width), lambda *ids, at=at: (0, 0, at(*ids))))
        side_shape.append(jax.ShapeDtypeStruct((1, cc.shape[0], n), F32))
        side_args += [cc, mod_w, mod_b.reshape(depth, 1, n)]
        nbytes += 2 * d * width * 4 + d * width * 2
        jobs.append((3, lambda ins, out: out.__setitem__(
            0, _dot(_silu_bf16(ins[0][...]), ins[1][0].astype(BF16)) + ins[2][0])))
    n_in, n_out, n_side_in, n_side_out = len(in_specs), len(out_specs), len(side_in), len(side_out)

    def kernel(*refs):
        ins = refs[:n_in]
        s_ins = refs[n_in:n_in + n_side_in]
        outs = refs[n_in + n_side_in:n_in + n_side_in + n_out]
        s_outs = refs[n_in + n_side_in + n_out:n_in + n_side_in + n_out + n_side_out]
        pos = 0
        for (arity, run), out in zip(jobs, s_outs):
            run(s_ins[pos:pos + arity], out)
            pos += arity
        body(*ins, *outs, *refs[n_in + n_side_in + n_out + n_side_out:])

    return pl.pallas_call(
        kernel, grid=grid, in_specs=list(in_specs) + side_in, out_specs=list(out_specs) + side_out,
        out_shape=list(out_shape) + side_shape, compiler_params=_params(sem, nbytes), **kwargs,
    )(*args, *side_args)


def _mod_kernel(cc_ref, w_ref, b_ref, o_ref):
    o_ref[0] = _dot(_silu_bf16(cc_ref[...]), w_ref[0].astype(BF16)) + b_ref[0]


def _modulation(cc, mod_w, mod_b, layer):
    depth, d, n = mod_w.shape
    tn = 1024
    rows = cc.shape[0]
    return pl.pallas_call(
        _mod_kernel,
        grid=(n // tn,),
        in_specs=[pl.BlockSpec((rows, d), lambda j: (0, 0)),
                  pl.BlockSpec((1, d, tn), lambda j: (layer, 0, j)),
                  pl.BlockSpec((1, 1, tn), lambda j: (layer, 0, j))],
        out_specs=pl.BlockSpec((1, rows, tn), lambda j: (0, 0, j)),
        out_shape=jax.ShapeDtypeStruct((1, rows, n), F32),
        compiler_params=_params(("arbitrary",), 2 * d * tn * 4 + d * tn * 2),
        name="modulation",
    )(cc, mod_w, mod_b.reshape(depth, 1, n))


def _rope(x, cos, sin_signed, heads):
    lane = lax.broadcasted_iota(jnp.int32, (x.shape[0], ATT_HEAD_DIM), 1)
    first = (lane % (2 * ROPE_FREQS)) < ROPE_FREQS
    out = []
    for h in range(heads):
        xs = x[:, h * ATT_HEAD_DIM:(h + 1) * ATT_HEAD_DIM]
        swapped = jnp.where(first, pltpu.roll(xs, ATT_HEAD_DIM - ROPE_FREQS, 1), pltpu.roll(xs, ROPE_FREQS, 1))
        out.append(xs * cos + swapped * sin_signed)
    return out


def _inproj0_kernel(*refs, rope, ctx_row):
    if rope:
        (x_ref, sc_ref, sh_ref, cos_ref, sin_ref, wu_ref, wg_ref, wq_ref, wk_ref, wv_ref,
         u_ref, g_ref, q_ref, k_ref, v_ref) = refs
    else:
        (x_ref, sc_ref, sh_ref, wu_ref, wg_ref, wq_ref, wk_ref, wv_ref,
         u_ref, g_ref, q_ref, k_ref, v_ref) = refs
    h = _modulate(x_ref[0], sc_ref, sh_ref, ctx_row)
    u = _dot(h, wu_ref[...])
    g = _dot(h, wg_ref[...])
    for hd in range(LRU_HEADS):
        u_ref[0, hd] = u[:, hd * LRU_BLOCK:(hd + 1) * LRU_BLOCK].astype(BF16)
        g_ref[0, hd] = g[:, hd * LRU_BLOCK:(hd + 1) * LRU_BLOCK].astype(BF16)
    q = _dot(h, wq_ref[...]) * (ATT_HEAD_DIM ** -0.5 * LOG2E)
    k = _dot(h, wk_ref[...])
    if rope:
        cos = cos_ref[...]
        sin = sin_ref[...]
        for hd, piece in enumerate(_rope(q, cos, sin, ATT_Q_HEADS)):
            q_ref[0, :, hd * ATT_HEAD_DIM:(hd + 1) * ATT_HEAD_DIM] = piece.astype(BF16)
        for hd, piece in enumerate(_rope(k, cos, sin, ATT_KV_HEADS)):
            k_ref[0, :, hd * ATT_HEAD_DIM:(hd + 1) * ATT_HEAD_DIM] = piece.astype(BF16)
    else:
        q_ref[0] = q.astype(BF16)
        k_ref[0] = k.astype(BF16)
    v_ref[0] = _dot(h, wv_ref[...]).astype(BF16)


def _inproj0(x, mod, ctx_row, w_in, rope_tabs, tm):
    bsz, n, d = x.shape
    rope = rope_tabs is not None
    tok = lambda b, i: (b, i, 0)
    in_specs = [pl.BlockSpec((1, tm, d), tok), _mod_spec(mod, MOD_SC1), _mod_spec(mod, MOD_SH1)]
    args = [x, mod, mod]
    if rope:
        in_specs += [pl.BlockSpec((tm, ATT_HEAD_DIM), lambda b, i: (i, 0))] * 2
        args += list(rope_tabs)
    kv_at = (2 * LRU_WIDTH + ATT_Q_WIDTH) // ATT_KV_WIDTH
    in_specs += [_resident((d, LRU_WIDTH), lambda b, i: (0, 0)), _resident((d, LRU_WIDTH), lambda b, i: (0, 1)),
                 _resident((d, ATT_Q_WIDTH), lambda b, i: (0, 2)),
                 _resident((d, ATT_KV_WIDTH), lambda b, i: (0, kv_at)),
                 _resident((d, ATT_KV_WIDTH), lambda b, i: (0, kv_at + 1))]
    args += [w_in] * 5
    head_major = jax.ShapeDtypeStruct((bsz, LRU_HEADS, n, LRU_BLOCK), BF16)
    head_spec = pl.BlockSpec((1, LRU_HEADS, tm, LRU_BLOCK), lambda b, i: (b, 0, i, 0))
    nbytes = 2 * tm * d * 4 + 2 * w_in.size + 2 * 2 * tm * w_in.shape[1] + tm * w_in.shape[1] * 4
    return pl.pallas_call(
        functools.partial(_inproj0_kernel, rope=rope, ctx_row=ctx_row),
        grid=(bsz, n // tm),
        in_specs=in_specs,
        out_specs=[head_spec, head_spec,
                   pl.BlockSpec((1, tm, ATT_Q_WIDTH), tok),
                   pl.BlockSpec((1, tm, ATT_KV_WIDTH), tok),
                   pl.BlockSpec((1, tm, ATT_KV_WIDTH), tok)],
        out_shape=[head_major, head_major,
                   jax.ShapeDtypeStruct((bsz, n, ATT_Q_WIDTH), BF16),
                   jax.ShapeDtypeStruct((bsz, n, ATT_KV_WIDTH), BF16),
                   jax.ShapeDtypeStruct((bsz, n, ATT_KV_WIDTH), BF16)],
        compiler_params=_params(("arbitrary", "arbitrary"), nbytes),
        name="inproj0_rope" if rope else "inproj0_ctx",
    )(*args)


LRU_PAD = SUBLANES
LRU_TILE = 256


def _scan8(a, b, row, reverse):
    for k in (1, 2, 4):
        if reverse:
            keep = row < SUBLANES - k
            shift = SUBLANES - k
        else:
            keep = row >= k
            shift = k
        a_sh = jnp.where(keep, pltpu.roll(a, shift, 0), 1.0)
        b_sh = jnp.where(keep, pltpu.roll(b, shift, 0), 0.0)
        b = a * b_sh + b
        a = a * a_sh
    return a, b


def _scan8_rows(a, b, row, reverse):
    out_a, out_b = [], []
    for g in range(a.shape[0] // SUBLANES):
        sl = slice(g * SUBLANES, (g + 1) * SUBLANES)
        ag, bg = _scan8(a[sl], b[sl], row, reverse)
        out_a.append(ag)
        out_b.append(bg)
    return jnp.concatenate(out_a, axis=0), jnp.concatenate(out_b, axis=0)


def _lru_kernel(ul_ref, gl_ref, uc_ref, gc_ref, cw_ref, cb_ref, wg_ref, bg_ref, lam_ref,
                yl_ref, yc_ref, upad, af_s, bf_s, ar_s, br_s, cin_s, *, n_lat, n_ctx):
    cw = cw_ref[...]
    cb = cb_ref[...]
    wg = wg_ref[0]
    bg = bg_ref[0]
    lam = lam_ref[0]
    neg = -lam
    softplus = jnp.maximum(neg, 0.0) + jnp.log1p(jnp.exp(-jnp.abs(neg)))
    half_rate = (-0.5 * LRU_C) * softplus
    row = lax.broadcasted_iota(jnp.int32, (SUBLANES, LRU_BLOCK), 0)
    chains = ((af_s, bf_s, False), (ar_s, br_s, True))

    def coefficients(src_ref, n_rows, row_off):
        upad[pl.ds(0, LRU_PAD), :] = jnp.zeros((LRU_PAD, LRU_BLOCK), F32)
        upad[pl.ds(LRU_PAD, n_rows), :] = src_ref[0, 0].astype(F32)
        upad[pl.ds(LRU_PAD + n_rows, LRU_PAD), :] = jnp.zeros((LRU_PAD, LRU_BLOCK), F32)

        def tile(i, carry):
            t0 = pl.multiple_of(i * LRU_TILE, LRU_TILE)
            xp = upad[pl.ds(t0, LRU_TILE + 2 * LRU_PAD), :]
            conv = cb
            for k in range(LRU_CONV):
                o = LRU_PAD - LRU_CONV_LEFT + k
                conv = conv + cw[k:k + 1, :] * xp[o:o + LRU_TILE, :]
            z = _dot(conv.astype(BF16), wg) + bg
            half_conv = 0.5 * conv
            for d, (a_ref, b_ref, reverse) in enumerate(chains):
                t_a = jnp.tanh(z[:, (2 * d) * LRU_BLOCK:(2 * d + 1) * LRU_BLOCK])
                t_x = jnp.tanh(z[:, (2 * d + 1) * LRU_BLOCK:(2 * d + 2) * LRU_BLOCK])
                rate = half_rate[:, d * LRU_BLOCK:(d + 1) * LRU_BLOCK]
                a = jnp.exp(rate + rate * t_a)
                gated = half_conv + half_conv * t_x
                b = jnp.sqrt(1.0 - a * a) * gated
                a, b = _scan8_rows(a, b, row, reverse)
                a_ref[pl.ds(row_off + t0, LRU_TILE), :] = a
                b_ref[pl.ds(row_off + t0, LRU_TILE), :] = b
            return carry

        lax.fori_loop(0, n_rows // LRU_TILE, tile, 0)

    coefficients(uc_ref, n_ctx, 0)
    coefficients(ul_ref, n_lat, n_ctx)

    groups_ctx = n_ctx // SUBLANES
    groups = (n_ctx + n_lat) // SUBLANES
    fa = af_s[pl.ds(SUBLANES - 1, groups, stride=SUBLANES), :]
    fb = bf_s[pl.ds(SUBLANES - 1, groups, stride=SUBLANES), :]
    ra = ar_s[pl.ds(0, groups, stride=SUBLANES), :]
    rb = br_s[pl.ds(0, groups, stride=SUBLANES), :]
    zero = jnp.zeros((SUBLANES, LRU_BLOCK), F32)
    carry = zero
    for v in range(groups // SUBLANES):
        sl = slice(v * SUBLANES, (v + 1) * SUBLANES)
        a, b = _scan8(fa[sl], fb[sl], row, False)
        incl = b + a * carry
        cin_s[0, sl, :] = jnp.where(row == 0, carry, pltpu.roll(incl, 1, 0))
        carry = jnp.broadcast_to(incl[SUBLANES - 1:SUBLANES, :], incl.shape)
    carry = zero
    order = list(range(groups_ctx // SUBLANES - 1, -1, -1)) + list(range(groups // SUBLANES - 1,
                                                                        groups_ctx // SUBLANES - 1, -1))
    for v in order:
        sl = slice(v * SUBLANES, (v + 1) * SUBLANES)
        a, b = _scan8(ra[sl], rb[sl], row, True)
        incl = b + a * carry
        cin_s[1, sl, :] = jnp.where(row == SUBLANES - 1, carry, pltpu.roll(incl, SUBLANES - 1, 0))
        carry = jnp.broadcast_to(incl[0:1, :], incl.shape)

    def emit(g_ref, y_ref, n_rows, row_off):
        tile_groups = LRU_TILE // SUBLANES

        def tile(i, carry):
            t0 = pl.multiple_of(i * LRU_TILE, LRU_TILE)
            g0 = pl.multiple_of(row_off // SUBLANES + i * tile_groups, tile_groups)
            rows = pl.ds(row_off + t0, LRU_TILE)
            a_f, b_f, a_r, b_r = af_s[rows, :], bf_s[rows, :], ar_s[rows, :], br_s[rows, :]
            pieces = []
            for g in range(tile_groups):
                sl = slice(g * SUBLANES, (g + 1) * SUBLANES)
                c_f = jnp.broadcast_to(cin_s[0, pl.ds(g0 + g, 1), :], (SUBLANES, LRU_BLOCK))
                c_r = jnp.broadcast_to(cin_s[1, pl.ds(g0 + g, 1), :], (SUBLANES, LRU_BLOCK))
                pieces.append((b_f[sl] + a_f[sl] * c_f) + (b_r[sl] + a_r[sl] * c_r))
            h = jnp.concatenate(pieces, axis=0)
            gate = g_ref[0, 0, pl.ds(t0, LRU_TILE), :].astype(F32)
            y_ref[0, 0, pl.ds(t0, LRU_TILE), :] = (h * _gelu_tanh(gate)).astype(BF16)
            return carry

        lax.fori_loop(0, n_rows // LRU_TILE, tile, 0)

    emit(gc_ref, yc_ref, n_ctx, 0)
    emit(gl_ref, yl_ref, n_lat, n_ctx)


def _rglru(u_lat, g_lat, u_ctx, g_ctx, conv_w, conv_b, w_gate, b_gate, lam, casts=(), mods=()):
    bsz, heads, n_lat, blk = u_lat.shape
    n_ctx = u_ctx.shape[2]
    total = n_lat + n_ctx
    assert n_ctx % (SUBLANES * SUBLANES) == 0 and n_lat % LRU_TILE == 0 and n_ctx % LRU_TILE == 0
    seq = lambda n: pl.BlockSpec((1, 1, n, blk), lambda b, h: (b, h, 0, 0))
    per_head = lambda shape: pl.BlockSpec((1,) + shape, lambda b, h: (h, 0, 0))
    nbytes =4 * total * blk * 4 + (n_lat + 2 * LRU_PAD) * blk * 4 + 2 * 3 * 2 * total * blk * 2
    return _hosted_call(
        functools.partial(_lru_kernel, n_lat=n_lat, n_ctx=n_ctx),
        casts=casts, mods=mods,
        args=(u_lat, g_lat, u_ctx, g_ctx, conv_w, conv_b, w_gate, b_gate, lam),
        grid=(bsz, heads),
        in_specs=[seq(n_lat), seq(n_lat), seq(n_ctx), seq(n_ctx),
                  pl.BlockSpec((LRU_CONV, blk), lambda b, h: (0, h)),
                  pl.BlockSpec((1, blk), lambda b, h: (0, h)),
                  per_head((blk, 4 * blk)), per_head((1, 4 * blk)), per_head((1, 2 * blk))],
        out_specs=[seq(n_lat), seq(n_ctx)],
        out_shape=[jax.ShapeDtypeStruct(u_lat.shape, BF16), jax.ShapeDtypeStruct(u_ctx.shape, BF16)],
        scratch_shapes=[pltpu.VMEM((n_lat + 2 * LRU_PAD, blk), F32)]
                       + [pltpu.VMEM((total, blk), F32)] * 4
                       + [pltpu.VMEM((2, total // SUBLANES, blk), F32)],
        sem=("arbitrary", "arbitrary"), nbytes=nbytes,
        name="rglru",
    )


ATT_TQ = 512
ATT_BAND = 3 * ATT_BLOCK


def _dot_tn(a, b):
    return lax.dot_general(a, b, (((0,), (0,)), ((), ())), preferred_element_type=F32)


def _softmax_pv(parts, sink_row):
    dh = ATT_HEAD_DIM
    m = sink_row
    for s, _ in parts:
        m = jnp.maximum(m, jnp.max(s, axis=0, keepdims=True))
    acc = None
    for idx, (s, v) in enumerate(parts):
        p = jnp.exp2(s - m).astype(BF16)
        v_ext = jnp.concatenate([v, jnp.ones(v.shape, BF16)], axis=1)
        if idx == len(parts) - 1:
            pad = 2 * SUBLANES
            first = lax.broadcasted_iota(jnp.int32, (pad, m.shape[1]), 0) == 0
            p_sink = jnp.where(first, jnp.exp2(sink_row - m), 0.0).astype(BF16)
            p = jnp.concatenate([p, p_sink], axis=0)
            v_sink = jnp.concatenate([jnp.zeros((pad, dh), BF16), jnp.ones((pad, dh), BF16)], axis=1)
            v_ext = jnp.concatenate([v_ext, v_sink], axis=0)
        pv = _dot_tn(p, v_ext)
        acc = pv if acc is None else acc + pv
    return acc[:, :dh] / acc[:, dh:]


def _sink_rows(sink, reps):
    return jnp.repeat(sink.reshape(ATT_KV_HEADS, 1, ATT_GROUP), reps, axis=2)


def _attn_kernel(q_ref, k_ref, v_ref, kc_ref, vc_ref, sink_ref, o_ref, *, seq):
    tile = pl.program_id(1)
    blocks = ATT_TQ // ATT_BLOCK
    qcol = lax.broadcasted_iota(jnp.int32, (1, ATT_GROUP * ATT_BLOCK), 1) % ATT_BLOCK
    krow = lax.broadcasted_iota(jnp.int32, (ATT_BAND, 1), 0)
    for i in range(blocks):
        q0 = (tile * blocks + i) * ATT_BLOCK
        start = pl.multiple_of(jnp.clip(q0 - ATT_BLOCK, 0, seq - ATT_BAND), ATT_BLOCK)
        valid = jnp.abs((q0 + qcol) - (start + krow)) <= WINDOW
        for g in range(ATT_KV_HEADS):
            heads = [q_ref[0, i * ATT_BLOCK:(i + 1) * ATT_BLOCK,
                           (g * ATT_GROUP + r) * ATT_HEAD_DIM:(g * ATT_GROUP + r + 1) * ATT_HEAD_DIM]
                     for r in range(ATT_GROUP)]
            qs = jnp.concatenate(heads, axis=0)
            kv = slice(g * ATT_HEAD_DIM, (g + 1) * ATT_HEAD_DIM)
            s_loc = jnp.where(valid, _dot_nt(k_ref[0, pl.ds(start, ATT_BAND), kv], qs), NEG_INF)
            s_ctx = _dot_nt(kc_ref[0, :, kv], qs)
            o = _softmax_pv([(s_loc, v_ref[0, pl.ds(start, ATT_BAND), kv]), (s_ctx, vc_ref[0, :, kv])],
                            sink_ref[g] * LOG2E)
            for r in range(ATT_GROUP):
                col = (g * ATT_GROUP + r) * ATT_HEAD_DIM
                o_ref[0, i * ATT_BLOCK:(i + 1) * ATT_BLOCK, col:col + ATT_HEAD_DIM] = (
                    o[r * ATT_BLOCK:(r + 1) * ATT_BLOCK].astype(BF16))


def _window_attention(q, k, v, kc, vc, sink, casts=()):
    bsz, n, _ = q.shape
    sink_rows = _sink_rows(sink, ATT_BLOCK)
    lc = kc.shape[1]
    whole = lambda rows: pl.BlockSpec((1, rows, ATT_KV_WIDTH), lambda b, i: (b, 0, 0))
    nbytes = 2 * 2 * (2 * ATT_TQ * ATT_Q_WIDTH + 2 * n * ATT_KV_WIDTH + 2 * lc * ATT_KV_WIDTH) + 8 * 1024 * 1024
    return _hosted_call(
        functools.partial(_attn_kernel, seq=n),
        casts=casts,
        args=(q, k, v, kc, vc, sink_rows),
        grid=(bsz, n // ATT_TQ),
        in_specs=[pl.BlockSpec((1, ATT_TQ, ATT_Q_WIDTH), lambda b, i: (b, i, 0)),
                  whole(n), whole(n), whole(lc), whole(lc),
                  pl.BlockSpec(sink_rows.shape, lambda b, i: (0, 0, 0))],
        out_specs=[pl.BlockSpec((1, ATT_TQ, ATT_Q_WIDTH), lambda b, i: (b, i, 0))],
        out_shape=[jax.ShapeDtypeStruct(q.shape, BF16)],
        sem=("arbitrary", "arbitrary"), nbytes=nbytes,
        name="window_attention",
    )


def _ctx_attn_kernel(q_ref, kc_ref, vc_ref, sink_ref, o_ref):
    lc = q_ref.shape[1]
    for g in range(ATT_KV_HEADS):
        heads = [q_ref[0, :, (g * ATT_GROUP + r) * ATT_HEAD_DIM:(g * ATT_GROUP + r + 1) * ATT_HEAD_DIM]
                 for r in range(ATT_GROUP)]
        qs = jnp.concatenate(heads, axis=0)
        kv = slice(g * ATT_HEAD_DIM, (g + 1) * ATT_HEAD_DIM)
        o = _softmax_pv([(_dot_nt(kc_ref[0, :, kv], qs), vc_ref[0, :, kv])], sink_ref[g] * LOG2E)
        for r in range(ATT_GROUP):
            col = (g * ATT_GROUP + r) * ATT_HEAD_DIM
            o_ref[0, :, col:col + ATT_HEAD_DIM] = o[r * lc:(r + 1) * lc].astype(BF16)


def _context_attention(qc, kc, vc, sink):
    bsz, lc, _ = qc.shape
    sink_rows = _sink_rows(sink, lc)
    kv_spec = pl.BlockSpec((1, lc, ATT_KV_WIDTH), lambda b: (b, 0, 0))
    q_spec = pl.BlockSpec((1, lc, ATT_Q_WIDTH), lambda b: (b, 0, 0))
    return pl.pallas_call(
        _ctx_attn_kernel,
        grid=(bsz,),
        in_specs=[q_spec, kv_spec, kv_spec, pl.BlockSpec(sink_rows.shape, lambda b: (0, 0, 0))],
        out_specs=q_spec,
        out_shape=jax.ShapeDtypeStruct(qc.shape, BF16),
        compiler_params=_params(("arbitrary",), 8 * 1024 * 1024),
        name="context_attention",
    )(qc, kc, vc, sink_rows)


def _outproj0_kernel(ya_ref, yb_ref, x_ref, gate_ref, w_ref, lg_ref, lb_ref, o_ref, *, ctx_row):
    y = jnp.concatenate([ya_ref[0, h] for h in range(LRU_HEADS)] + [yb_ref[0]], axis=1)
    out = _dot(y, w_ref[...])
    o_ref[0] = _layer_norm(ALPHA * x_ref[0] + _mod_row(gate_ref, ctx_row) * out, lg_ref[...], lb_ref[...])


def _outproj0(ya, yb, x, mod, ctx_row, w_out, ln_g, ln_b, tm):
    bsz, n, d = x.shape
    tok = lambda b, i: (b, i, 0)
    const = lambda b, i: (0, 0)
    nbytes = 2 * 2 * tm * d * 4 + w_out.size * 2 + 2 * 2 * tm * d * 2 + 2 * tm * d * 4
    return pl.pallas_call(
        functools.partial(_outproj0_kernel, ctx_row=ctx_row),
        grid=(bsz, n // tm),
        in_specs=[pl.BlockSpec((1, LRU_HEADS, tm, LRU_BLOCK), lambda b, i: (b, 0, i, 0)),
                  pl.BlockSpec((1, tm, ATT_Q_WIDTH), tok),
                  pl.BlockSpec((1, tm, d), tok),
                  _mod_spec(mod, MOD_G1),
                  _resident(w_out.shape, const),
                  pl.BlockSpec((1, d), const), pl.BlockSpec((1, d), const)],
        out_specs=pl.BlockSpec((1, tm, d), tok),
        out_shape=jax.ShapeDtypeStruct(x.shape, F32),
        compiler_params=_params(("arbitrary", "arbitrary"), nbytes),
        name="outproj0_ln",
    )(ya, yb, x, mod, w_out, ln_g, ln_b)


def _ffn_kernel(x_ref, sc_ref, sh_ref, gate_ref, w1_ref, b1_ref, w2_ref, b2_ref, lg_ref, lb_ref,
                o_ref, h_s, *, ctx_row):
    j = pl.program_id(2)

    @pl.when(j == 0)
    def _():
        h_s[...] = _modulate(x_ref[0], sc_ref, sh_ref, ctx_row)
        o_ref[0] = jnp.zeros(o_ref.shape[1:], F32)

    a = jnp.maximum(_dot(h_s[...], w1_ref[...]) + b1_ref[...], 0.0)
    o_ref[0] += _dot((a * a).astype(BF16), w2_ref[...])

    @pl.when(j == pl.num_programs(2) - 1)
    def _():
        f = o_ref[0] + b2_ref[...]
        o_ref[0] = _layer_norm(ALPHA * x_ref[0] + _mod_row(gate_ref, ctx_row) * f, lg_ref[...], lb_ref[...])


def _ffn(x, mod, ctx_row, w1, b1, w2, b2, ln_g, ln_b, tm, th, casts=()):
    bsz, n, d = x.shape
    hidden = w1.shape[1]
    tok = lambda b, i, j: (b, i, 0)
    const = lambda b, i, j: (0, 0)
    nbytes = 2 * 2 * tm * d * 4 + 2 * 2 * 2 * d * th * 2 + tm * d * 2 + 2 * tm * th * 4 + tm * d * 4
    return _hosted_call(
        functools.partial(_ffn_kernel, ctx_row=ctx_row),
        casts=casts,
        args=(x, mod, mod, mod, w1, b1, w2, b2, ln_g, ln_b),
        grid=(bsz, n // tm, hidden // th),
        in_specs=[pl.BlockSpec((1, tm, d), tok),
                  _mod_spec(mod, MOD_SC2), _mod_spec(mod, MOD_SH2), _mod_spec(mod, MOD_G2),
                  pl.BlockSpec((d, th), lambda b, i, j: (0, j)),
                  pl.BlockSpec((1, th), lambda b, i, j: (0, j)),
                  pl.BlockSpec((th, d), lambda b, i, j: (j, 0)),
                  pl.BlockSpec((1, d), const), pl.BlockSpec((1, d), const), pl.BlockSpec((1, d), const)],
        out_specs=[pl.BlockSpec((1, tm, d), tok)],
        out_shape=[jax.ShapeDtypeStruct(x.shape, F32)],
        scratch_shapes=[pltpu.VMEM((tm, d), BF16)],
        sem=("arbitrary", "arbitrary", "arbitrary"), nbytes=nbytes,
        name="ffn_ln",
    )


SC_PARTS = SC_WIDTH // S5_WIDTH


def _inproj1_kernel(*refs, full, ctx_row):
    x_ref, sc_ref, sh_ref, wu_ref = refs[:4]
    u_s = refs[-1]
    h = _modulate(x_ref[0], sc_ref, sh_ref, ctx_row)
    u = _dot(h, wu_ref[...])
    if full:
        wb_refs = refs[4:4 + SC_PARTS]
        wc_refs = refs[4 + SC_PARTS:4 + 2 * SC_PARTS]
        wx_refs = refs[4 + 2 * SC_PARTS:4 + 3 * SC_PARTS]
        u_ref, uflat_ref, gb_ref, p_ref = refs[4 + 3 * SC_PARTS:-1]
        u_ref[0] = u.astype(BF16)
        for c in range(SC_PARTS):
            cols = slice(c * S5_WIDTH, (c + 1) * S5_WIDTH)
            gb_ref[0, :, cols] = _dot(h, wb_refs[c][...]).astype(BF16)
            p_ref[0, :, cols] = (_dot(h, wc_refs[c][...]) * _dot(h, wx_refs[c][...])).astype(BF16)
    else:
        uflat_ref, = refs[4:-1]
    for k in range(S5_SLABS):
        u_s[k] = u[:, k * LANES:(k + 1) * LANES]
    chunks = u.shape[0] // S5_CHUNK
    for t in range(S5_CHUNK):
        for k in range(S5_SLABS):
            col = t * S5_WIDTH + k * LANES
            uflat_ref[0, :, col:col + LANES] = u_s[k, pl.ds(t, chunks, stride=S5_CHUNK), :].astype(BF16)


def _inproj1(x, mod, ctx_row, w_in, tm, full):
    bsz, n, d = x.shape
    tok = lambda b, i: (b, i, 0)
    n_blocks = 1 + 3 * SC_PARTS if full else 1
    flat_spec = pl.BlockSpec((1, tm // S5_CHUNK, S5_CHUNK * S5_WIDTH), tok)
    flat_shape = jax.ShapeDtypeStruct((bsz, n // S5_CHUNK, S5_CHUNK * S5_WIDTH), BF16)
    token = lambda w: (pl.BlockSpec((1, tm, w), tok), jax.ShapeDtypeStruct((bsz, n, w), BF16))
    outs = ([token(S5_WIDTH), (flat_spec, flat_shape), token(SC_WIDTH), token(SC_WIDTH)] if full
            else [(flat_spec, flat_shape)])
    ncols = n_blocks * S5_WIDTH
    nbytes = 2 * tm * d * 4 + 2 * d * ncols + 2 * 2 * tm * (ncols + S5_WIDTH) + tm * ncols * 4 + tm * S5_WIDTH * 4
    return pl.pallas_call(
        functools.partial(_inproj1_kernel, full=full, ctx_row=ctx_row),
        grid=(bsz, n // tm),
        in_specs=[pl.BlockSpec((1, tm, d), tok), _mod_spec(mod, MOD_SC1), _mod_spec(mod, MOD_SH1)]
                 + [_resident((d, S5_WIDTH), lambda b, i, c=c: (0, c)) for c in range(n_blocks)],
        out_specs=[spec for spec, _ in outs],
        out_shape=[shape for _, shape in outs],
        scratch_shapes=[pltpu.VMEM((S5_SLABS, tm, LANES), F32)],
        compiler_params=_params(("arbitrary", "arbitrary"), nbytes),
        name="inproj1" if full else "inproj1_ctx",
    )(x, mod, mod, *([w_in] * n_blocks))


def _s5_power_table(par_ref):
    rows = 2 * S5_POW_ROWS
    r = lax.broadcasted_iota(jnp.int32, (rows, S5_SLAB_STATE), 0)
    first = r < S5_POW_ROWS
    lag = (r % S5_POW_ROWS).astype(F32)
    pick = lambda i: jnp.where(first, par_ref[0, 0, i:i + 1, :], par_ref[0, 1, i:i + 1, :])
    dt = jnp.exp(pick(0))
    mag = jnp.exp(lag * dt * pick(1))
    ang = lag * dt * pick(2)
    return mag * jnp.cos(ang), mag * jnp.sin(ang)


def _s5_input_matrix(par_ref, bt_ref, d, pw_re, pw_im):
    a_re = par_ref[0, d, 1:2, :]
    a_im = par_ref[0, d, 2:3, :]
    ab_re = pw_re[d * S5_POW_ROWS + 1:d * S5_POW_ROWS + 2, :]
    ab_im = pw_im[d * S5_POW_ROWS + 1:d * S5_POW_ROWS + 2, :]
    den = a_re * a_re + a_im * a_im
    k_re = ((ab_re - 1.0) * a_re + ab_im * a_im) / den
    k_im = (ab_im * a_re - (ab_re - 1.0) * a_im) / den
    b_re = bt_ref[0, d, 0]
    b_im = bt_ref[0, d, 1]
    return k_re * b_re - k_im * b_im, k_re * b_im + k_im * b_re


def _cmul_row(x_re, x_im, p_re, p_im):
    return x_re * p_re - x_im * p_im, x_re * p_im + x_im * p_re


def _s5_state_prep_kernel(par_ref, bt_ref, wst_ref, a16_ref):
    pw_re, pw_im = _s5_power_table(par_ref)
    for d in range(2):
        bb_re, bb_im = _s5_input_matrix(par_ref, bt_ref, d, pw_re, pw_im)
        base = d * S5_POW_ROWS
        for lag in range(S5_CHUNK):
            e_re, e_im = _cmul_row(bb_re, bb_im, pw_re[base + lag:base + lag + 1, :],
                                   pw_im[base + lag:base + lag + 1, :])
            s = S5_CHUNK - 1 - lag if d == 0 else lag
            col = 2 * d * S5_SLAB_STATE
            wst_ref[0, s * LANES:(s + 1) * LANES, col:col + S5_SLAB_STATE] = e_re.astype(BF16)
            wst_ref[0, s * LANES:(s + 1) * LANES, col + S5_SLAB_STATE:col + 2 * S5_SLAB_STATE] = e_im.astype(BF16)
        row16 = base + S5_CHUNK
        a16_ref[0, :, 2 * d * S5_SLAB_STATE:(2 * d + 1) * S5_SLAB_STATE] = jnp.broadcast_to(
            pw_re[row16:row16 + 1, :], (SUBLANES, S5_SLAB_STATE))
        a16_ref[0, :, (2 * d + 1) * S5_SLAB_STATE:(2 * d + 2) * S5_SLAB_STATE] = jnp.broadcast_to(
            pw_im[row16:row16 + 1, :], (SUBLANES, S5_SLAB_STATE))


def _s5_output_prep_kernel(par_ref, bt_ref, c_ref, ct_ref, wbig_ref):
    pw_re, pw_im = _s5_power_table(par_ref)
    kern = []
    for d in range(2):
        bb_re, bb_im = _s5_input_matrix(par_ref, bt_ref, d, pw_re, pw_im)
        c_re = c_ref[0, d, 0].astype(BF16)
        c_im = c_ref[0, d, 1].astype(BF16)
        base = d * S5_POW_ROWS
        per_lag = []
        for lag in range(S5_CHUNK):
            e_re, e_im = _cmul_row(bb_re, bb_im, pw_re[base + lag:base + lag + 1, :],
                                   pw_im[base + lag:base + lag + 1, :])
            per_lag.append(_dot_nt(e_re.astype(BF16), c_re) - _dot_nt(e_im.astype(BF16), c_im))
        kern.append(per_lag)
    for s in range(S5_CHUNK):
        for t in range(S5_CHUNK):
            if s < t:
                blk = kern[0][t - s]
            elif s > t:
                blk = kern[1][s - t]
            else:
                blk = kern[0][0] + kern[1][0]
            wbig_ref[0, s * LANES:(s + 1) * LANES, t * LANES:(t + 1) * LANES] = blk.astype(BF16)
    pt_re = pw_re.T
    pt_im = pw_im.T
    for d in range(2):
        ct_re = ct_ref[0, d, 0]
        ct_im = ct_ref[0, d, 1]
        for t in range(S5_CHUNK):
            lag = t + 1 if d == 0 else S5_CHUNK - t
            col = d * S5_POW_ROWS + lag
            p_re = pt_re[:, col:col + 1]
            p_im = pt_im[:, col:col + 1]
            g_re = ct_re * p_re - ct_im * p_im
            g_im = ct_re * p_im + ct_im * p_re
            r0 = S5_XK + 2 * d * S5_SLAB_STATE
            wbig_ref[0, r0:r0 + S5_SLAB_STATE, t * LANES:(t + 1) * LANES] = g_re.astype(BF16)
            wbig_ref[0, r0 + S5_SLAB_STATE:r0 + 2 * S5_SLAB_STATE, t * LANES:(t + 1) * LANES] = (-g_im).astype(BF16)


def _s5_operators(log_dt, a_re, a_im, b_re, b_im, c_re, c_im):
    gs = S5_SLAB_GROUPS
    eye = jnp.eye(gs, dtype=F32)

    def lanes(v):
        return v.reshape(2, S5_SLABS, gs * S5_STATE)

    par = jnp.stack([lanes(jnp.broadcast_to(log_dt[:, :, None], a_re.shape)), lanes(a_re), lanes(a_im)], axis=2)
    par = jnp.pad(par, ((0, 0), (0, 0), (0, SUBLANES - 3), (0, 0))).transpose(1, 0, 2, 3)

    def embed_bt(b):
        b = b.reshape(2, S5_SLABS, gs, S5_STATE, S5_GROUP)
        e = b.transpose(0, 1, 2, 4, 3)[:, :, :, :, None, :] * eye[None, None, :, None, :, None]
        return e.reshape(2, S5_SLABS, gs * S5_GROUP, gs * S5_STATE).transpose(1, 0, 2, 3)

    def embed_c(c):
        c = c.reshape(2, S5_SLABS, gs, S5_GROUP, S5_STATE)
        e = c[:, :, :, :, None, :] * eye[None, None, :, None, :, None]
        return e.reshape(2, S5_SLABS, gs * S5_GROUP, gs * S5_STATE).transpose(1, 0, 2, 3)

    bt = jnp.stack([embed_bt(b_re), embed_bt(b_im)], axis=2)
    cm = jnp.stack([embed_c(c_re), embed_c(c_im)], axis=2)
    ct = cm.transpose(0, 1, 2, 4, 3)

    slab5 = lambda shape: pl.BlockSpec((1,) + shape, lambda k: (k, 0, 0, 0, 0))
    par_spec = pl.BlockSpec((1, 2, SUBLANES, S5_SLAB_STATE), lambda k: (k, 0, 0, 0))
    bt_spec = slab5((2, 2, LANES, S5_SLAB_STATE))
    wst, a16 = pl.pallas_call(
        _s5_state_prep_kernel,
        grid=(S5_SLABS,),
        in_specs=[par_spec, bt_spec],
        out_specs=[pl.BlockSpec((1, S5_XK, S5_HK), lambda k: (k, 0, 0)),
                   pl.BlockSpec((1, SUBLANES, S5_HK), lambda k: (k, 0, 0))],
        out_shape=[jax.ShapeDtypeStruct((S5_SLABS, S5_XK, S5_HK), BF16),
                   jax.ShapeDtypeStruct((S5_SLABS, SUBLANES, S5_HK), F32)],
        compiler_params=_params(("arbitrary",), 2 * S5_XK * S5_HK * 2 + 8 * 1024 * 1024),
        name="s5_state_operator",
    )(par, bt)
    wbig = pl.pallas_call(
        _s5_output_prep_kernel,
        grid=(S5_SLABS,),
        in_specs=[par_spec, bt_spec, bt_spec, slab5((2, 2, S5_SLAB_STATE, LANES))],
        out_specs=pl.BlockSpec((1, S5_XK + S5_HK, S5_XK), lambda k: (k, 0, 0)),
        out_shape=jax.ShapeDtypeStruct((S5_SLABS, S5_XK + S5_HK, S5_XK), BF16),
        compiler_params=_params(("arbitrary",), 2 * (S5_XK + S5_HK) * S5_XK * 2 + 8 * 1024 * 1024),
        name="s5_output_operator",
    )(par, bt, cm, ct)
    return wst, a16, wbig


def _s5_states_kernel(*refs, n_ctx, n_lat):
    xc_refs = refs[:S5_CHUNK]
    xl_refs = refs[S5_CHUNK:2 * S5_CHUNK]
    wst_ref, a16_ref, h_ref, s_s, h_s = refs[2 * S5_CHUNK:]
    x = jnp.concatenate([jnp.concatenate([r[0] for r in xc_refs], axis=1),
                         jnp.concatenate([r[0] for r in xl_refs], axis=1)], axis=0)
    s_s[...] = _dot(x, wst_ref[0])
    p = S5_SLAB_STATE
    af_re = a16_ref[0, 0:1, 0:p]
    af_im = a16_ref[0, 0:1, p:2 * p]
    ar_re = a16_ref[0, 0:1, 2 * p:3 * p]
    ar_im = a16_ref[0, 0:1, 3 * p:4 * p]
    total = n_ctx + n_lat

    def step(i, carry):
        f_re, f_im, r_re, r_im = carry
        h_s[pl.ds(i, 1), 0:p] = f_re
        h_s[pl.ds(i, 1), p:2 * p] = f_im
        s_re = s_s[pl.ds(i, 1), 0:p]
        s_im = s_s[pl.ds(i, 1), p:2 * p]
        f_re, f_im = af_re * f_re - af_im * f_im + s_re, af_re * f_im + af_im * f_re + s_im
        j = jnp.where(i < n_ctx, n_ctx - 1 - i, total + n_ctx - 1 - i)
        h_s[pl.ds(j, 1), 2 * p:3 * p] = r_re
        h_s[pl.ds(j, 1), 3 * p:4 * p] = r_im
        s_re = s_s[pl.ds(j, 1), 2 * p:3 * p]
        s_im = s_s[pl.ds(j, 1), 3 * p:4 * p]
        r_re, r_im = ar_re * r_re - ar_im * r_im + s_re, ar_re * r_im + ar_im * r_re + s_im
        return f_re, f_im, r_re, r_im

    zero = jnp.zeros((1, p), F32)
    lax.fori_loop(0, total, step, (zero, zero, zero, zero), unroll=4)
    h_ref[0, 0] = h_s[pl.ds(n_ctx, n_lat), :].astype(BF16)


def _s5_states(uc_flat, ul_flat, wst, a16):
    bsz, n_ctx, _ = uc_flat.shape
    n_lat = ul_flat.shape[1]
    piece = lambda rows, s: pl.BlockSpec((1, rows, LANES), lambda k, b, s=s: (b, 0, s * S5_SLABS + k))
    in_specs = ([piece(n_ctx, s) for s in range(S5_CHUNK)] + [piece(n_lat, s) for s in range(S5_CHUNK)]
                + [pl.BlockSpec((1, S5_XK, S5_HK), lambda k, b: (k, 0, 0)),
                   pl.BlockSpec((1, SUBLANES, S5_HK), lambda k, b: (k, 0, 0))])
    total = n_ctx + n_lat
    nbytes = 2 * S5_XK * S5_HK * 2 + 2 * total * S5_XK * 2 + 3 * total * S5_HK * 4 + 2 * n_lat * S5_HK * 2
    return pl.pallas_call(
        functools.partial(_s5_states_kernel, n_ctx=n_ctx, n_lat=n_lat),
        grid=(S5_SLABS, bsz),
        in_specs=in_specs,
        out_specs=pl.BlockSpec((1, 1, n_lat, S5_HK), lambda k, b: (b, k, 0, 0)),
        out_shape=jax.ShapeDtypeStruct((bsz, S5_SLABS, n_lat, S5_HK), BF16),
        scratch_shapes=[pltpu.VMEM((total, S5_HK), F32), pltpu.VMEM((total, S5_HK), F32)],
        compiler_params=_params(("arbitrary", "arbitrary"), nbytes),
        name="s5_states",
    )(*([uc_flat] * S5_CHUNK + [ul_flat] * S5_CHUNK + [wst, a16]))


def _s5_readout_kernel(*refs):
    x_refs = refs[:S5_CHUNK]
    h_ref, w_ref, y_ref = refs[S5_CHUNK:]
    lhs = jnp.concatenate([r[0] for r in x_refs] + [h_ref[0, 0]], axis=1)
    y = _dot(lhs, w_ref[0])
    chunks = y.shape[0]
    for t in range(S5_CHUNK):
        y_ref[0, pl.ds(t, chunks, stride=S5_CHUNK), :] = y[:, t * LANES:(t + 1) * LANES]


def _s5_readout(ul_flat, h_in, wbig):
    bsz, n_lat, _ = ul_flat.shape
    piece = lambda s: pl.BlockSpec((1, n_lat, LANES), lambda k, b, s=s: (b, 0, s * S5_SLABS + k))
    nbytes = 2 * (S5_XK + S5_HK) * S5_XK * 2 + 2 * n_lat * (S5_XK + S5_HK) * 2 * 2 + n_lat * S5_XK * 4 * 3
    return pl.pallas_call(
        _s5_readout_kernel,
        grid=(S5_SLABS, bsz),
        in_specs=[piece(s) for s in range(S5_CHUNK)]
                 + [pl.BlockSpec((1, 1, n_lat, S5_HK), lambda k, b: (b, k, 0, 0)),
                    pl.BlockSpec((1, S5_XK + S5_HK, S5_XK), lambda k, b: (k, 0, 0))],
        out_specs=pl.BlockSpec((1, n_lat * S5_CHUNK, LANES), lambda k, b: (b, 0, k)),
        out_shape=jax.ShapeDtypeStruct((bsz, n_lat * S5_CHUNK, S5_WIDTH), F32),
        compiler_params=_params(("arbitrary", "arbitrary"), nbytes),
        name="s5_readout",
    )(*([ul_flat] * S5_CHUNK + [h_in, wbig]))


SC_HALO = 16


def _outproj1_kernel(y_ref, u_ref, gb_ref, p_ref, pprev_ref, pnext_ref, x_ref, gate_ref,
                     dskip_ref, wglu_ref, bglu_ref, cw_ref, cb_ref, w_ref, lg_ref, lb_ref, o_ref, *, ctx_row):
    i = pl.program_id(1)
    tm = x_ref.shape[1]
    yc = y_ref[0] + dskip_ref[...] * u_ref[0].astype(F32)
    z = _gelu_tanh(yc)
    y_c = z * _sigmoid(_dot(z.astype(BF16), wglu_ref[...]) + bglu_ref[...])
    p = p_ref[0].astype(F32)
    row = lax.broadcasted_iota(jnp.int32, (tm, 1), 0)
    prev_row = jnp.where(i > 0, pprev_ref[0, SC_HALO - 1:SC_HALO, :].astype(F32), 0.0)
    next_row = jnp.where(i < pl.num_programs(1) - 1, pnext_ref[0, 0:1, :].astype(F32), 0.0)
    p_dn = jnp.where(row == 0, prev_row, pltpu.roll(p, 1, 0))
    p_up = jnp.where(row == tm - 1, next_row, pltpu.roll(p, tm - 1, 0))
    conv = cb_ref[...] + cw_ref[0:1, :] * p_dn + cw_ref[1:2, :] * p + cw_ref[2:3, :] * p_up
    y_d = gb_ref[0].astype(F32) * conv
    y = jnp.concatenate([y_c.astype(BF16), y_d.astype(BF16)], axis=1)
    out = _dot(y, w_ref[...])
    o_ref[0] = _layer_norm(ALPHA * x_ref[0] + _mod_row(gate_ref, ctx_row) * out, lg_ref[...], lb_ref[...])


def _outproj1(y, u, gb, p, x, mod, ctx_row, d_skip, w_glu, b_glu, conv_w, conv_b, w_out, ln_g, ln_b, tm):
    bsz, n, d = x.shape
    tok = lambda b, i: (b, i, 0)
    const = lambda b, i: (0, 0)
    per = tm // SC_HALO
    last = n // SC_HALO - 1
    nbytes = (2 * 2 * tm * d * 4 + w_out.size * 2 + 2 * 2 * tm * (2 * S5_WIDTH + 2 * SC_WIDTH) * 2
              + 6 * tm * SC_WIDTH * 4 + 2 * tm * d * 4)
    return pl.pallas_call(
        functools.partial(_outproj1_kernel, ctx_row=ctx_row),
        grid=(bsz, n // tm),
        in_specs=[pl.BlockSpec((1, tm, S5_WIDTH), tok), pl.BlockSpec((1, tm, S5_WIDTH), tok),
                  pl.BlockSpec((1, tm, SC_WIDTH), tok), pl.BlockSpec((1, tm, SC_WIDTH), tok),
                  pl.BlockSpec((1, SC_HALO, SC_WIDTH), lambda b, i: (b, jnp.maximum(i * per - 1, 0), 0)),
                  pl.BlockSpec((1, SC_HALO, SC_WIDTH), lambda b, i: (b, jnp.minimum((i + 1) * per, last), 0)),
                  pl.BlockSpec((1, tm, d), tok), _mod_spec(mod, MOD_G1),
                  pl.BlockSpec((1, S5_WIDTH), const), _resident(w_glu.shape, const),
                  pl.BlockSpec((1, S5_WIDTH), const),
                  pl.BlockSpec((SC_CONV, SC_WIDTH), const), pl.BlockSpec((1, SC_WIDTH), const),
                  _resident(w_out.shape, const),
                  pl.BlockSpec((1, d), const), pl.BlockSpec((1, d), const)],
        out_specs=pl.BlockSpec((1, tm, d), tok),
        out_shape=jax.ShapeDtypeStruct(x.shape, F32),
        compiler_params=_params(("arbitrary", "arbitrary"), nbytes),
        name="outproj1_ln",
    )(y, u, gb, p, p, p, x, mod, d_skip, w_glu, b_glu, conv_w, conv_b, w_out, ln_g, ln_b)


def _rope_tables(n):
    t = np.arange(n)
    inv = ROPE_BASE ** (-np.arange(ROPE_FREQS, dtype=np.float64) / ROPE_FREQS)
    ang_r = (t // GRID_W)[:, None] * inv[None, :]
    ang_c = (t % GRID_W)[:, None] * inv[None, :]
    cos = np.concatenate([np.cos(ang_r), np.cos(ang_r), np.cos(ang_c), np.cos(ang_c)], axis=1)
    sin = np.concatenate([-np.sin(ang_r), np.sin(ang_r), -np.sin(ang_c), np.sin(ang_c)], axis=1)
    return jnp.asarray(cos, F32), jnp.asarray(sin, F32)


TM_LATENT = 512
TM_PROJ = 1024
TM_CONTEXT = 256
FFN_TH = 2048


def kernel(x, c, ctx, c_ctx, mod_w, mod_b, ln1_g, ln1_b, ln2_g, ln2_b, ffn_w1, ffn_b1, ffn_w2, ffn_b2,
           ab_w_in, ab_w_out, lru_conv_w, lru_conv_b, lru_w_a, lru_b_a, lru_w_x, lru_b_x, lru_lam, att_sink,
           cd_w_in, cd_w_out, s5_log_dt, s5_a_re, s5_a_im, s5_b_re, s5_b_im, s5_c_re, s5_c_im, s5_d,
           s5_w_glu, s5_b_glu, sc_conv_w, sc_conv_b):
    bsz, n, d = x.shape
    lc = ctx.shape[1]
    assert n % TM_PROJ == 0 and n % TM_LATENT == 0 and lc % TM_CONTEXT == 0 and (bsz * lc) % TM_LATENT == 0
    assert n % GRID_W == 0

    pad_rows = SUBLANES - (bsz + 1) % SUBLANES if (bsz + 1) % SUBLANES else 0
    cc = jnp.concatenate([c, c_ctx[None, :], jnp.zeros((pad_rows, d), F32)], axis=0)
    mods = {0: _modulation(cc, mod_w, mod_b, 0)}
    ctx_row = bsz

    row2 = lambda v: v.reshape(1, -1)
    rope_tabs = _rope_tables(n)

    side = {}

    def weight(stack, name, idx):
        return side.pop((name, idx)) if (name, idx) in side else _to_bf16(stack, idx)

    def hosted(call, own, jobs):
        outs = call(casts=[(stack, idx) for _, idx, stack in jobs])
        side.update({(name, idx): w for (name, idx, _), w in zip(jobs, outs[own:own + len(jobs)])})
        return outs[:own] + outs[own + len(jobs):]

    for i in range(DEPTH):
        last = i == DEPTH - 1
        j = i // 2
        mod = mods.pop(i) if i in mods else _modulation(cc, mod_w, mod_b, i)
        todo = [("w1", i, ffn_w1), ("w2", i, ffn_w2)]
        if not last:
            nj = (i + 1) // 2
            todo += [("w1", i + 1, ffn_w1), ("w2", i + 1, ffn_w2)]
            todo += ([("ab_in", nj, ab_w_in), ("ab_out", nj, ab_w_out)] if (i + 1) % 2 == 0
                     else [("cd_in", nj, cd_w_in), ("cd_out", nj, cd_w_out)])
        todo = [job for job in todo if (job[0], job[1]) not in side]
        if i % 2 == 0:
            w_in = weight(ab_w_in, "ab_in", j)
            w_out = weight(ab_w_out, "ab_out", j)
            u, gate, q, k, v = _inproj0(x, mod, None, w_in, rope_tabs, TM_PROJ)
            uc, gatec, qc, kc, vc = _inproj0(ctx, mod, ctx_row, w_in, None, TM_CONTEXT)
            w_gate = (0.5 * jnp.concatenate([lru_w_a[j, 0], lru_w_x[j, 0], lru_w_a[j, 1], lru_w_x[j, 1]],
                                            axis=-1)).astype(BF16)
            hb = lambda b: b.reshape(LRU_HEADS, 1, LRU_BLOCK)
            b_gate = 0.5 * jnp.concatenate([hb(lru_b_a[j, 0]), hb(lru_b_x[j, 0]), hb(lru_b_a[j, 1]),
                                            hb(lru_b_x[j, 1])], axis=-1)
            lam = jnp.concatenate([hb(lru_lam[j, 0]), hb(lru_lam[j, 1])], axis=-1)
            ahead_mod = [] if last else [(cc, mod_w, mod_b, i + 1)]
            *lru_out, = hosted(functools.partial(_rglru, u, gate, uc, gatec, lru_conv_w[j], row2(lru_conv_b[j]),
                                                 w_gate, b_gate, lam, mods=ahead_mod), 2, todo[0::2])
            ya, yac = lru_out[:2]
            mods.update({i + 1: m for m in lru_out[2:]})
            yb, = hosted(functools.partial(_window_attention, q, k, v, kc, vc, att_sink[j]), 1, todo[1::2])
            todo = []
            x = _outproj0(ya, yb, x, mod, None, w_out, row2(ln1_g[i]), row2(ln1_b[i]), TM_PROJ)
            if not last:
                ybc = _context_attention(qc, kc, vc, att_sink[j])
                ctx = _outproj0(yac, ybc, ctx, mod, ctx_row, w_out, row2(ln1_g[i]), row2(ln1_b[i]), TM_CONTEXT)
        else:
            assert last
            w_in = weight(cd_w_in, "cd_in", j)
            w_out = weight(cd_w_out, "cd_out", j)
            u, u_flat, gb, p = _inproj1(x, mod, None, w_in, TM_PROJ, True)
            (uc_flat,) = _inproj1(ctx, mod, ctx_row, w_in, TM_CONTEXT, False)
            wst, a16, wbig = _s5_operators(s5_log_dt[j], s5_a_re[j], s5_a_im[j], s5_b_re[j], s5_b_im[j],
                                           s5_c_re[j], s5_c_im[j])
            h_in = _s5_states(uc_flat, u_flat, wst, a16)
            y = _s5_readout(u_flat, h_in, wbig)
            x = _outproj1(y, u, gb, p, x, mod, None, row2(s5_d[j]), s5_w_glu[j].astype(BF16), row2(s5_b_glu[j]),
                          sc_conv_w[j], row2(sc_conv_b[j]), w_out, row2(ln1_g[i]), row2(ln1_b[i]), TM_LATENT)
        w1 = weight(ffn_w1, "w1", i)
        w2 = weight(ffn_w2, "w2", i)
        ahead = [job for job in todo if job[1] != i or job[0] not in ("w1", "w2")]
        x, = hosted(functools.partial(_ffn, x, mod, None, w1, row2(ffn_b1[i]), w2, row2(ffn_b2[i]),
                                      row2(ln2_g[i]), row2(ln2_b[i]), TM_LATENT, FFN_TH), 1, ahead)
        if not last:
            rows = ctx.reshape(1, bsz * lc, d)
            rows, = _ffn(rows, mod, ctx_row, w1, row2(ffn_b1[i]), w2, row2(ffn_b2[i]), row2(ln2_g[i]),
                         row2(ln2_b[i]), TM_LATENT, FFN_TH)
            ctx = rows.reshape(bsz, lc, d)
    return x
```

```python
import functools
import math

import jax
import jax.numpy as jnp
import numpy as np
from jax import lax
from jax.experimental import pallas as pl
from jax.experimental.pallas import tpu as pltpu

F32 = jnp.float32
BF16 = jnp.bfloat16

D_MODEL = 2048
DEPTH = 2
GRID_W = 64
LRU_WIDTH = D_MODEL // 2
LRU_HEADS = 8
LRU_BLOCK = LRU_WIDTH // LRU_HEADS
LRU_CONV = 4
LRU_CONV_LEFT = 2
LRU_C = 8.0
ATT_HEAD_DIM = 128
ATT_Q_HEADS = (D_MODEL // 2) // ATT_HEAD_DIM
ATT_KV_HEADS = 2
ATT_GROUP = ATT_Q_HEADS // ATT_KV_HEADS
ATT_Q_WIDTH = ATT_Q_HEADS * ATT_HEAD_DIM
ATT_KV_WIDTH = ATT_KV_HEADS * ATT_HEAD_DIM
WINDOW = 128
ATT_BLOCK = 128
ROPE_BASE = 10000.0
ROPE_FREQS = ATT_HEAD_DIM // 4
S5_WIDTH = D_MODEL // 4
S5_GROUP = 16
S5_GROUPS = S5_WIDTH // S5_GROUP
S5_STATE = 64
SC_WIDTH = D_MODEL - S5_WIDTH
SC_CONV = 3
FFN_HIDDEN = 4 * D_MODEL
ALPHA = (2.0 * DEPTH) ** 0.25
LN_EPS = 1e-5
NEG_INF = -1e30
LOG2E = math.log2(math.e)

LANES = 128
SUBLANES = 8
V7X_VMEM_BYTES = 64 * 1024 * 1024
V7X_VMEM_BUDGET = 62 * 1024 * 1024

S5_CHUNK = 16
S5_SLAB_GROUPS = LANES // S5_GROUP
S5_SLABS = S5_WIDTH // LANES
S5_SLAB_STATE = S5_SLAB_GROUPS * S5_STATE
S5_XK = S5_CHUNK * LANES
S5_HK = 4 * S5_SLAB_STATE
S5_POW_ROWS = 64


def _vmem_limit(nbytes):
    return int(min(V7X_VMEM_BUDGET, max(nbytes * 3 // 2, 16 * 1024 * 1024)))


def _resident(shape, index_map):
    return pl.BlockSpec(shape, index_map, pipeline_mode=pl.Buffered(1))


def _params(sem, nbytes):
    return pltpu.CompilerParams(dimension_semantics=sem, vmem_limit_bytes=_vmem_limit(nbytes))


def _dot(a, b):
    return jnp.dot(a, b, preferred_element_type=F32)


def _dot_nt(a, b):
    return lax.dot_general(a, b, (((1,), (1,)), ((), ())), preferred_element_type=F32)


def _layer_norm(v, g, b):
    mu = jnp.mean(v, axis=-1, keepdims=True)
    c = v - mu
    var = jnp.mean(c * c, axis=-1, keepdims=True)
    return c * lax.rsqrt(var + LN_EPS) * g + b


def _gelu_tanh(x):
    return 0.5 * x * (1.0 + jnp.tanh(math.sqrt(2.0 / math.pi) * (x + 0.044715 * (x * x * x))))


def _sigmoid(x):
    return 0.5 * (1.0 + jnp.tanh(0.5 * x))


MOD_SH1, MOD_SC1, MOD_G1, MOD_SH2, MOD_SC2, MOD_G2 = range(6)


def _mod_spec(mod, chunk):
    _, rows, width = mod.shape
    return pl.BlockSpec((1, rows, width // 6), lambda *_: (0, 0, chunk))


def _mod_row(ref, ctx_row):
    row = pl.program_id(0) if ctx_row is None else ctx_row
    return ref[0, pl.ds(row, 1), :]


def _modulate(x, sc_ref, sh_ref, ctx_row):
    return (x * (1.0 + _mod_row(sc_ref, ctx_row)) + _mod_row(sh_ref, ctx_row)).astype(BF16)


CAST_BLOCK_BYTES = 8 * 1024 * 1024


def _cast_kernel(w_ref, o_ref):
    o_ref[...] = w_ref[0].astype(BF16)


def _to_bf16(w, layer):
    _, rows, cols = w.shape
    tr = rows
    while tr * cols * 4 > CAST_BLOCK_BYTES and tr % 2 == 0 and tr // 2 >= 2 * SUBLANES:
        tr //= 2
    return pl.pallas_call(
        _cast_kernel,
        grid=(rows // tr,),
        in_specs=[pl.BlockSpec((1, tr, cols), lambda i: (layer, i, 0))],
        out_specs=pl.BlockSpec((tr, cols), lambda i: (i, 0)),
        out_shape=jax.ShapeDtypeStruct((rows, cols), BF16),
        compiler_params=_params(("arbitrary",), 2 * tr * cols * 6),
        name="weight_cast",
    )(w)


CAST_ROWS = 2 * SUBLANES


def _silu_bf16(v):
    return (v * _sigmoid(v)).astype(BF16)


def _hosted_call(body, *, grid, in_specs, out_specs, out_shape, args, sem, nbytes, casts=(), mods=(), **kwargs):
    steps = math.prod(grid)

    def flat(*ids):
        step = 0
        for extent, i in zip(grid, ids):
            step = step * extent + i
        return step

    def upto(used):
        return lambda *ids: jnp.minimum(flat(*ids), used - 1)

    side_in, side_out, side_shape, side_args, jobs = [], [], [], [], []
    for w, layer in casts:
        _, rows, cols = w.shape
        rps = max(CAST_ROWS, rows // steps)
        assert rows % rps == 0 and rows // rps <= steps
        at = upto(rows // rps)
        side_in.append(pl.BlockSpec((1, rps, cols), lambda *ids, at=at, layer=layer: (layer, at(*ids), 0)))
        side_out.append(pl.BlockSpec((rps, cols), lambda *ids, at=at: (at(*ids), 0)))
        side_shape.append(jax.ShapeDtypeStruct((rows, cols), BF16))
        side_args.append(w)
        nbytes += 2 * rps * cols * (4 + 2)
        jobs.append((1, lambda ins, out: out.__setitem__(Ellipsis, ins[0][0].astype(BF16))))
    for cc, mod_w, mod_b, layer in mods:
        depth, d, n = mod_w.shape
        width = LANES * max(1, n // LANES // steps)
        assert n % width == 0 and n // width <= steps
        at = upto(n // width)
        side_in += [pl.BlockSpec(cc.shape, lambda *ids: (0, 0)),
                    pl.BlockSpec((1, d, width), lambda *ids, at=at, layer=layer: (layer, 0, at(*ids))),
                    pl.BlockSpec((1, 1, width), lambda *ids, at=at, layer=layer: (layer, 0, at(*ids)))]
        side_out.append(pl.BlockSpec((1, cc.shape[0], width), lambda *ids, at=at: (0, 0, at(*ids))))
        side_shape.append(jax.ShapeDtypeStruct((1, cc.shape[0], n), F32))
        side_args += [cc, mod_w, mod_b.reshape(depth, 1, n)]
        nbytes += 2 * d * width * 4 + d * width * 2
        jobs.append((3, lambda ins, out: out.__setitem__(
            0, _dot(_silu_bf16(ins[0][...]), ins[1][0].astype(BF16)) + ins[2][0])))
    n_in, n_out, n_side_in, n_side_out = len(in_specs), len(out_specs), len(side_in), len(side_out)

    def kernel(*refs):
        ins = refs[:n_in]
        s_ins = refs[n_in:n_in + n_side_in]
        outs = refs[n_in + n_side_in:n_in + n_side_in + n_out]
        s_outs = refs[n_in + n_side_in + n_out:n_in + n_side_in + n_out + n_side_out]
        pos = 0
        for (arity, run), out in zip(jobs, s_outs):
            run(s_ins[pos:pos + arity], out)
            pos += arity
        body(*ins, *outs, *refs[n_in + n_side_in + n_out + n_side_out:])

    return pl.pallas_call(
        kernel, grid=grid, in_specs=list(in_specs) + side_in, out_specs=list(out_specs) + side_out,
        out_shape=list(out_shape) + side_shape, compiler_params=_params(sem, nbytes), **kwargs,
    )(*args, *side_args)


def _mod_kernel(cc_ref, w_ref, b_ref, o_ref):
    o_ref[0] = _dot(_silu_bf16(cc_ref[...]), w_ref[0].astype(BF16)) + b_ref[0]


def _modulation(cc, mod_w, mod_b, layer):
    depth, d, n = mod_w.shape
    tn = 1024
    rows = cc.shape[0]
    return pl.pallas_call(
        _mod_kernel,
        grid=(n // tn,),
        in_specs=[pl.BlockSpec((rows, d), lambda j: (0, 0)),
                  pl.BlockSpec((1, d, tn), lambda j: (layer, 0, j)),
                  pl.BlockSpec((1, 1, tn), lambda j: (layer, 0, j))],
        out_specs=pl.BlockSpec((1, rows, tn), lambda j: (0, 0, j)),
        out_shape=jax.ShapeDtypeStruct((1, rows, n), F32),
        compiler_params=_params(("arbitrary",), 2 * d * tn * 4 + d * tn * 2),
        name="modulation",
    )(cc, mod_w, mod_b.reshape(depth, 1, n))


def _rope(x, cos, sin_signed, heads):
    lane = lax.broadcasted_iota(jnp.int32, (x.shape[0], ATT_HEAD_DIM), 1)
    first = (lane % (2 * ROPE_FREQS)) < ROPE_FREQS
    out = []
    for h in range(heads):
        xs = x[:, h * ATT_HEAD_DIM:(h + 1) * ATT_HEAD_DIM]
        swapped = jnp.where(first, pltpu.roll(xs, ATT_HEAD_DIM - ROPE_FREQS, 1), pltpu.roll(xs, ROPE_FREQS, 1))
        out.append(xs * cos + swapped * sin_signed)
    return out


def _inproj0_kernel(*refs, rope, ctx_row):
    if rope:
        (x_ref, sc_ref, sh_ref, cos_ref, sin_ref, wu_ref, wg_ref, wq_ref, wk_ref, wv_ref,
         u_ref, g_ref, q_ref, k_ref, v_ref) = refs
    else:
        (x_ref, sc_ref, sh_ref, wu_ref, wg_ref, wq_ref, wk_ref, wv_ref,
         u_ref, g_ref, q_ref, k_ref, v_ref) = refs
    h = _modulate(x_ref[0], sc_ref, sh_ref, ctx_row)
    u = _dot(h, wu_ref[...])
    g = _dot(h, wg_ref[...])
    for hd in range(LRU_HEADS):
        u_ref[0, hd] = u[:, hd * LRU_BLOCK:(hd + 1) * LRU_BLOCK].astype(BF16)
        g_ref[0, hd] = g[:, hd * LRU_BLOCK:(hd + 1) * LRU_BLOCK].astype(BF16)
    q = _dot(h, wq_ref[...]) * (ATT_HEAD_DIM ** -0.5 * LOG2E)
    k = _dot(h, wk_ref[...])
    if rope:
        cos = cos_ref[...]
        sin = sin_ref[...]
        for hd, piece in enumerate(_rope(q, cos, sin, ATT_Q_HEADS)):
            q_ref[0, :, hd * ATT_HEAD_DIM:(hd + 1) * ATT_HEAD_DIM] = piece.astype(BF16)
        for hd, piece in enumerate(_rope(k, cos, sin, ATT_KV_HEADS)):
            k_ref[0, :, hd * ATT_HEAD_DIM:(hd + 1) * ATT_HEAD_DIM] = piece.astype(BF16)
    else:
        q_ref[0] = q.astype(BF16)
        k_ref[0] = k.astype(BF16)
    v_ref[0] = _dot(h, wv_ref[...]).astype(BF16)


def _inproj0(x, mod, ctx_row, w_in, rope_tabs, tm):
    bsz, n, d = x.shape
    rope = rope_tabs is not None
    tok = lambda b, i: (b, i, 0)
    in_specs = [pl.BlockSpec((1, tm, d), tok), _mod_spec(mod, MOD_SC1), _mod_spec(mod, MOD_SH1)]
    args = [x, mod, mod]
    if rope:
        in_specs += [pl.BlockSpec((tm, ATT_HEAD_DIM), lambda b, i: (i, 0))] * 2
        args += list(rope_tabs)
    kv_at = (2 * LRU_WIDTH + ATT_Q_WIDTH) // ATT_KV_WIDTH
    in_specs += [_resident((d, LRU_WIDTH), lambda b, i: (0, 0)), _resident((d, LRU_WIDTH), lambda b, i: (0, 1)),
                 _resident((d, ATT_Q_WIDTH), lambda b, i: (0, 2)),
                 _resident((d, ATT_KV_WIDTH), lambda b, i: (0, kv_at)),
                 _resident((d, ATT_KV_WIDTH), lambda b, i: (0, kv_at + 1))]
    args += [w_in] * 5
    head_major = jax.ShapeDtypeStruct((bsz, LRU_HEADS, n, LRU_BLOCK), BF16)
    head_spec = pl.BlockSpec((1, LRU_HEADS, tm, LRU_BLOCK), lambda b, i: (b, 0, i, 0))
    nbytes = 2 * tm * d * 4 + 2 * w_in.size + 2 * 2 * tm * w_in.shape[1] + tm * w_in.shape[1] * 4
    return pl.pallas_call(
        functools.partial(_inproj0_kernel, rope=rope, ctx_row=ctx_row),
        grid=(bsz, n // tm),
        in_specs=in_specs,
        out_specs=[head_spec, head_spec,
                   pl.BlockSpec((1, tm, ATT_Q_WIDTH), tok),
                   pl.BlockSpec((1, tm, ATT_KV_WIDTH), tok),
                   pl.BlockSpec((1, tm, ATT_KV_WIDTH), tok)],
        out_shape=[head_major, head_major,
                   jax.ShapeDtypeStruct((bsz, n, ATT_Q_WIDTH), BF16),
                   jax.ShapeDtypeStruct((bsz, n, ATT_KV_WIDTH), BF16),
                   jax.ShapeDtypeStruct((bsz, n, ATT_KV_WIDTH), BF16)],
        compiler_params=_params(("arbitrary", "arbitrary"), nbytes),
        name="inproj0_rope" if rope else "inproj0_ctx",
    )(*args)


LRU_PAD = SUBLANES
LRU_TILE = 256


def _scan8(a, b, row, reverse):
    for k in (1, 2, 4):
        if reverse:
            keep = row < SUBLANES - k
            shift = SUBLANES - k
        else:
            keep = row >= k
            shift = k
        a_sh = jnp.where(keep, pltpu.roll(a, shift, 0), 1.0)
        b_sh = jnp.where(keep, pltpu.roll(b, shift, 0), 0.0)
        b = a * b_sh + b
        a = a * a_sh
    return a, b


def _scan8_rows(a, b, row, reverse):
    out_a, out_b = [], []
    for g in range(a.shape[0] // SUBLANES):
        sl = slice(g * SUBLANES, (g + 1) * SUBLANES)
        ag, bg = _scan8(a[sl], b[sl], row, reverse)
        out_a.append(ag)
        out_b.append(bg)
    return jnp.concatenate(out_a, axis=0), jnp.concatenate(out_b, axis=0)


def _lru_kernel(ul_ref, gl_ref, uc_ref, gc_ref, cw_ref, cb_ref, wg_ref, bg_ref, lam_ref,
                yl_ref, yc_ref, upad, af_s, bf_s, ar_s, br_s, cin_s, *, n_lat, n_ctx):
    cw = cw_ref[...]
    cb = cb_ref[...]
    wg = wg_ref[0]
    bg = bg_ref[0]
    lam = lam_ref[0]
    neg = -lam
    softplus = jnp.maximum(neg, 0.0) + jnp.log1p(jnp.exp(-jnp.abs(neg)))
    half_rate = (-0.5 * LRU_C) * softplus
    row = lax.broadcasted_iota(jnp.int32, (SUBLANES, LRU_BLOCK), 0)
    chains = ((af_s, bf_s, False), (ar_s, br_s, True))

    def coefficients(src_ref, n_rows, row_off):
        upad[pl.ds(0, LRU_PAD), :] = jnp.zeros((LRU_PAD, LRU_BLOCK), F32)
        upad[pl.ds(LRU_PAD, n_rows), :] = src_ref[0, 0].astype(F32)
        upad[pl.ds(LRU_PAD + n_rows, LRU_PAD), :] = jnp.zeros((LRU_PAD, LRU_BLOCK), F32)

        def tile(i, carry):
            t0 = pl.multiple_of(i * LRU_TILE, LRU_TILE)
            xp = upad[pl.ds(t0, LRU_TILE + 2 * LRU_PAD), :]
            conv = cb
            for k in range(LRU_CONV):
                o = LRU_PAD - LRU_CONV_LEFT + k
                conv = conv + cw[k:k + 1, :] * xp[o:o + LRU_TILE, :]
            z = _dot(conv.astype(BF16), wg) + bg
            half_conv = 0.5 * conv
            for d, (a_ref, b_ref, reverse) in enumerate(chains):
                t_a = jnp.tanh(z[:, (2 * d) * LRU_BLOCK:(2 * d + 1) * LRU_BLOCK])
                t_x = jnp.tanh(z[:, (2 * d + 1) * LRU_BLOCK:(2 * d + 2) * LRU_BLOCK])
                rate = half_rate[:, d * LRU_BLOCK:(d + 1) * LRU_BLOCK]
                a = jnp.exp(rate + rate * t_a)
                gated = half_conv + half_conv * t_x
                b = jnp.sqrt(1.0 - a * a) * gated
                a, b = _scan8_rows(a, b, row, reverse)
                a_ref[pl.ds(row_off + t0, LRU_TILE), :] = a
                b_ref[pl.ds(row_off + t0, LRU_TILE), :] = b
            return carry

        lax.fori_loop(0, n_rows // LRU_TILE, tile, 0)

    coefficients(uc_ref, n_ctx, 0)
    coefficients(ul_ref, n_lat, n_ctx)

    groups_ctx = n_ctx // SUBLANES
    groups = (n_ctx + n_lat) // SUBLANES
    fa = af_s[pl.ds(SUBLANES - 1, groups, stride=SUBLANES), :]
    fb = bf_s[pl.ds(SUBLANES - 1, groups, stride=SUBLANES), :]
    ra = ar_s[pl.ds(0, groups, stride=SUBLANES), :]
    rb = br_s[pl.ds(0, groups, stride=SUBLANES), :]
    zero = jnp.zeros((SUBLANES, LRU_BLOCK), F32)
    carry = zero
    for v in range(groups // SUBLANES):
        sl = slice(v * SUBLANES, (v + 1) * SUBLANES)
        a, b = _scan8(fa[sl], fb[sl], row, False)
        incl = b + a * carry
        cin_s[0, sl, :] = jnp.where(row == 0, carry, pltpu.roll(incl, 1, 0))
        carry = jnp.broadcast_to(incl[SUBLANES - 1:SUBLANES, :], incl.shape)
    carry = zero
    order = list(range(groups_ctx // SUBLANES - 1, -1, -1)) + list(range(groups // SUBLANES - 1,
                                                                        groups_ctx // SUBLANES - 1, -1))
    for v in order:
        sl = slice(v * SUBLANES, (v + 1) * SUBLANES)
        a, b = _scan8(ra[sl], rb[sl], row, True)
        incl = b + a * carry
        cin_s[1, sl, :] = jnp.where(row == SUBLANES - 1, carry, pltpu.roll(incl, SUBLANES - 1, 0))
        carry = jnp.broadcast_to(incl[0:1, :], incl.shape)

    def emit(g_ref, y_ref, n_rows, row_off):
        tile_groups = LRU_TILE // SUBLANES

        def tile(i, carry):
            t0 = pl.multiple_of(i * LRU_TILE, LRU_TILE)
            g0 = pl.multiple_of(row_off // SUBLANES + i * tile_groups, tile_groups)
            rows = pl.ds(row_off + t0, LRU_TILE)
            a_f, b_f, a_r, b_r = af_s[rows, :], bf_s[rows, :], ar_s[rows, :], br_s[rows, :]
            pieces = []
            for g in range(tile_groups):
                sl = slice(g * SUBLANES, (g + 1) * SUBLANES)
                c_f = jnp.broadcast_to(cin_s[0, pl.ds(g0 + g, 1), :], (SUBLANES, LRU_BLOCK))
                c_r = jnp.broadcast_to(cin_s[1, pl.ds(g0 + g, 1), :], (SUBLANES, LRU_BLOCK))
                pieces.append((b_f[sl] + a_f[sl] * c_f) + (b_r[sl] + a_r[sl] * c_r))
            h = jnp.concatenate(pieces, axis=0)
            gate = g_ref[0, 0, pl.ds(t0, LRU_TILE), :].astype(F32)
            y_ref[0, 0, pl.ds(t0, LRU_TILE), :] = (h * _gelu_tanh(gate)).astype(BF16)
            return carry

        lax.fori_loop(0, n_rows // LRU_TILE, tile, 0)

    emit(gc_ref, yc_ref, n_ctx, 0)
    emit(gl_ref, yl_ref, n_lat, n_ctx)


def _rglru(u_lat, g_lat, u_ctx, g_ctx, conv_w, conv_b, w_gate, b_gate, lam, casts=(), mods=()):
    bsz, heads, n_lat, blk = u_lat.shape
    n_ctx = u_ctx.shape[2]
    total = n_lat + n_ctx
    assert n_ctx % (SUBLANES * SUBLANES) == 0 and n_lat % LRU_TILE == 0 and n_ctx % LRU_TILE == 0
    seq = lambda n: pl.BlockSpec((1, 1, n, blk), lambda b, h: (b, h, 0, 0))
    per_head = lambda shape: pl.BlockSpec((1,) + shape, lambda b, h: (h, 0, 0))
    nbytes =4 * total * blk * 4 + (n_lat + 2 * LRU_PAD) * blk * 4 + 2 * 3 * 2 * total * blk * 2
    return _hosted_call(
        functools.partial(_lru_kernel, n_lat=n_lat, n_ctx=n_ctx),
        casts=casts, mods=mods,
        args=(u_lat, g_lat, u_ctx, g_ctx, conv_w, conv_b, w_gate, b_gate, lam),
        grid=(bsz, heads),
        in_specs=[seq(n_lat), seq(n_lat), seq(n_ctx), seq(n_ctx),
                  pl.BlockSpec((LRU_CONV, blk), lambda b, h: (0, h)),
                  pl.BlockSpec((1, blk), lambda b, h: (0, h)),
                  per_head((blk, 4 * blk)), per_head((1, 4 * blk)), per_head((1, 2 * blk))],
        out_specs=[seq(n_lat), seq(n_ctx)],
        out_shape=[jax.ShapeDtypeStruct(u_lat.shape, BF16), jax.ShapeDtypeStruct(u_ctx.shape, BF16)],
        scratch_shapes=[pltpu.VMEM((n_lat + 2 * LRU_PAD, blk), F32)]
                       + [pltpu.VMEM((total, blk), F32)] * 4
                       + [pltpu.VMEM((2, total // SUBLANES, blk), F32)],
        sem=("arbitrary", "arbitrary"), nbytes=nbytes,
        name="rglru",
    )


ATT_TQ = 512
ATT_BAND = 3 * ATT_BLOCK


def _dot_tn(a, b):
    return lax.dot_general(a, b, (((0,), (0,)), ((), ())), preferred_element_type=F32)


def _softmax_probs(scores, sink_row):
    m = sink_row
    for s in scores:
        m = jnp.maximum(m, jnp.max(s, axis=0, keepdims=True))
    probs = [jnp.exp2(s - m).astype(BF16) for s in scores]
    pad = 2 * SUBLANES
    first = lax.broadcasted_iota(jnp.int32, (pad, m.shape[1]), 0) == 0
    p_sink = jnp.where(first, jnp.exp2(sink_row - m), 0.0).astype(BF16)
    probs[-1] = jnp.concatenate([probs[-1], p_sink], axis=0)
    return probs


def _weighted_values(probs, values):
    dh = ATT_HEAD_DIM
    acc = None
    for idx, (p, v) in enumerate(zip(probs, values)):
        v_ext = jnp.concatenate([v, jnp.ones(v.shape, BF16)], axis=1)
        if idx == len(values) - 1:
            pad = p.shape[0] - v.shape[0]
            v_sink = jnp.concatenate([jnp.zeros((pad, dh), BF16), jnp.ones((pad, dh), BF16)], axis=1)
            v_ext = jnp.concatenate([v_ext, v_sink], axis=0)
        pv = _dot_tn(p, v_ext)
        acc = pv if acc is None else acc + pv
    return acc[:, :dh] / acc[:, dh:]


def _sink_rows(sink, reps):
    return jnp.repeat(sink.reshape(ATT_KV_HEADS, 1, ATT_GROUP), reps, axis=2)


def _attn_kernel(q_ref, k_ref, v_ref, kc_ref, vc_ref, sink_ref, o_ref, *, seq):
    tile = pl.program_id(1)
    blocks = ATT_TQ // ATT_BLOCK
    qcol = lax.broadcasted_iota(jnp.int32, (1, ATT_GROUP * ATT_BLOCK), 1) % ATT_BLOCK
    krow = lax.broadcasted_iota(jnp.int32, (ATT_BAND, 1), 0)

    def scores(u):
        i, g = divmod(u, ATT_KV_HEADS)
        q0 = (tile * blocks + i) * ATT_BLOCK
        start = pl.multiple_of(jnp.clip(q0 - ATT_BLOCK, 0, seq - ATT_BAND), ATT_BLOCK)
        valid = jnp.abs((q0 + qcol) - (start + krow)) <= WINDOW
        heads = [q_ref[0, i * ATT_BLOCK:(i + 1) * ATT_BLOCK,
                       (g * ATT_GROUP + r) * ATT_HEAD_DIM:(g * ATT_GROUP + r + 1) * ATT_HEAD_DIM]
                 for r in range(ATT_GROUP)]
        qs = jnp.concatenate(heads, axis=0)
        kv = slice(g * ATT_HEAD_DIM, (g + 1) * ATT_HEAD_DIM)
        s_loc = jnp.where(valid, _dot_nt(k_ref[0, pl.ds(start, ATT_BAND), kv], qs), NEG_INF)
        return start, [s_loc, _dot_nt(kc_ref[0, :, kv], qs)]

    def emit(u, start, p):
        i, g = divmod(u, ATT_KV_HEADS)
        kv = slice(g * ATT_HEAD_DIM, (g + 1) * ATT_HEAD_DIM)
        o = _weighted_values(p, [v_ref[0, pl.ds(start, ATT_BAND), kv], vc_ref[0, :, kv]])
        for r in range(ATT_GROUP):
            col = (g * ATT_GROUP + r) * ATT_HEAD_DIM
            o_ref[0, i * ATT_BLOCK:(i + 1) * ATT_BLOCK, col:col + ATT_HEAD_DIM] = (
                o[r * ATT_BLOCK:(r + 1) * ATT_BLOCK].astype(BF16))

    n_units = blocks * ATT_KV_HEADS
    sc, pr = {}, {}
    for step in range(n_units + 2):
        if step < n_units:
            sc[step] = scores(step)
        if 0 <= step - 1 < n_units:
            start, s_list = sc.pop(step - 1)
            pr[step - 1] = (start, _softmax_probs(s_list, sink_ref[(step - 1) % ATT_KV_HEADS] * LOG2E))
        if 0 <= step - 2 < n_units:
            start, p = pr.pop(step - 2)
            emit(step - 2, start, p)


def _window_attention(q, k, v, kc, vc, sink, casts=()):
    bsz, n, _ = q.shape
    sink_rows = _sink_rows(sink, ATT_BLOCK)
    lc = kc.shape[1]
    whole = lambda rows: pl.BlockSpec((1, rows, ATT_KV_WIDTH), lambda b, i: (b, 0, 0))
    nbytes = 2 * 2 * (2 * ATT_TQ * ATT_Q_WIDTH + 2 * n * ATT_KV_WIDTH + 2 * lc * ATT_KV_WIDTH) + 8 * 1024 * 1024
    return _hosted_call(
        functools.partial(_attn_kernel, seq=n),
        casts=casts,
        args=(q, k, v, kc, vc, sink_rows),
        grid=(bsz, n // ATT_TQ),
        in_specs=[pl.BlockSpec((1, ATT_TQ, ATT_Q_WIDTH), lambda b, i: (b, i, 0)),
                  whole(n), whole(n), whole(lc), whole(lc),
                  pl.BlockSpec(sink_rows.shape, lambda b, i: (0, 0, 0))],
        out_specs=[pl.BlockSpec((1, ATT_TQ, ATT_Q_WIDTH), lambda b, i: (b, i, 0))],
        out_shape=[jax.ShapeDtypeStruct(q.shape, BF16)],
        sem=("arbitrary", "arbitrary"), nbytes=nbytes,
        name="window_attention",
    )


def _ctx_attn_kernel(q_ref, kc_ref, vc_ref, sink_ref, o_ref):
    lc = q_ref.shape[1]
    for g in range(ATT_KV_HEADS):
        heads = [q_ref[0, :, (g * ATT_GROUP + r) * ATT_HEAD_DIM:(g * ATT_GROUP + r + 1) * ATT_HEAD_DIM]
                 for r in range(ATT_GROUP)]
        qs = jnp.concatenate(heads, axis=0)
        kv = slice(g * ATT_HEAD_DIM, (g + 1) * ATT_HEAD_DIM)
        o = _weighted_values(_softmax_probs([_dot_nt(kc_ref[0, :, kv], qs)], sink_ref[g] * LOG2E),
                             [vc_ref[0, :, kv]])
        for r in range(ATT_GROUP):
            col = (g * ATT_GROUP + r) * ATT_HEAD_DIM
            o_ref[0, :, col:col + ATT_HEAD_DIM] = o[r * lc:(r + 1) * lc].astype(BF16)


def _context_attention(qc, kc, vc, sink):
    bsz, lc, _ = qc.shape
    sink_rows = _sink_rows(sink, lc)
    kv_spec = pl.BlockSpec((1, lc, ATT_KV_WIDTH), lambda b: (b, 0, 0))
    q_spec = pl.BlockSpec((1, lc, ATT_Q_WIDTH), lambda b: (b, 0, 0))
    return pl.pallas_call(
        _ctx_attn_kernel,
        grid=(bsz,),
        in_specs=[q_spec, kv_spec, kv_spec, pl.BlockSpec(sink_rows.shape, lambda b: (0, 0, 0))],
        out_specs=q_spec,
        out_shape=jax.ShapeDtypeStruct(qc.shape, BF16),
        compiler_params=_params(("arbitrary",), 8 * 1024 * 1024),
        name="context_attention",
    )(qc, kc, vc, sink_rows)


OUT_PARTS = 4


def _outproj0_kernel(ya_ref, yb_ref, x_ref, gate_ref, w_ref, lg_ref, lb_ref, o_ref, *, ctx_row):
    gate = _mod_row(gate_ref, ctx_row)
    parts = OUT_PARTS
    rows_per = x_ref.shape[1] // parts
    outs = {}
    for r in range(parts + 1):
        if r < parts:
            rows = slice(r * rows_per, (r + 1) * rows_per)
            y = jnp.concatenate([ya_ref[0, h, rows, :] for h in range(LRU_HEADS)] + [yb_ref[0, rows, :]], axis=1)
            outs[r] = _dot(y, w_ref[...])
        if r >= 1:
            rows = slice((r - 1) * rows_per, r * rows_per)
            o_ref[0, rows, :] = _layer_norm(ALPHA * x_ref[0, rows, :] + gate * outs.pop(r - 1), lg_ref[...], lb_ref[...])


def _outproj0(ya, yb, x, mod, ctx_row, w_out, ln_g, ln_b, tm):
    bsz, n, d = x.shape
    tok = lambda b, i: (b, i, 0)
    const = lambda b, i: (0, 0)
    nbytes = 2 * 2 * tm * d * 4 + w_out.size * 2 + 2 * 2 * tm * d * 2 + 2 * tm * d * 4
    return pl.pallas_call(
        functools.partial(_outproj0_kernel, ctx_row=ctx_row),
        grid=(bsz, n // tm),
        in_specs=[pl.BlockSpec((1, LRU_HEADS, tm, LRU_BLOCK), lambda b, i: (b, 0, i, 0)),
                  pl.BlockSpec((1, tm, ATT_Q_WIDTH), tok),
                  pl.BlockSpec((1, tm, d), tok),
                  _mod_spec(mod, MOD_G1),
                  _resident(w_out.shape, const),
                  pl.BlockSpec((1, d), const), pl.BlockSpec((1, d), const)],
        out_specs=pl.BlockSpec((1, tm, d), tok),
        out_shape=jax.ShapeDtypeStruct(x.shape, F32),
        compiler_params=_params(("arbitrary", "arbitrary"), nbytes),
        name="outproj0_ln",
    )(ya, yb, x, mod, w_out, ln_g, ln_b)


def _ffn_kernel(x_ref, sc_ref, sh_ref, gate_ref, w1_ref, b1_ref, w2_ref, b2_ref, lg_ref, lb_ref,
                o_ref, h_s, *, ctx_row):
    j = pl.program_id(2)

    @pl.when(j == 0)
    def _():
        h_s[...] = _modulate(x_ref[0], sc_ref, sh_ref, ctx_row)
        o_ref[0] = jnp.zeros(o_ref.shape[1:], F32)

    a = jnp.maximum(_dot(h_s[...], w1_ref[...]) + b1_ref[...], 0.0)
    o_ref[0] += _dot((a * a).astype(BF16), w2_ref[...])

    @pl.when(j == pl.num_programs(2) - 1)
    def _():
        f = o_ref[0] + b2_ref[...]
        o_ref[0] = _layer_norm(ALPHA * x_ref[0] + _mod_row(gate_ref, ctx_row) * f, lg_ref[...], lb_ref[...])


def _ffn(x, mod, ctx_row, w1, b1, w2, b2, ln_g, ln_b, tm, th, casts=()):
    bsz, n, d = x.shape
    hidden = w1.shape[1]
    tok = lambda b, i, j: (b, i, 0)
    const = lambda b, i, j: (0, 0)
    nbytes = 2 * 2 * tm * d * 4 + 2 * 2 * 2 * d * th * 2 + tm * d * 2 + 2 * tm * th * 4 + tm * d * 4
    return _hosted_call(
        functools.partial(_ffn_kernel, ctx_row=ctx_row),
        casts=casts,
        args=(x, mod, mod, mod, w1, b1, w2, b2, ln_g, ln_b),
        grid=(bsz, n // tm, hidden // th),
        in_specs=[pl.BlockSpec((1, tm, d), tok),
                  _mod_spec(mod, MOD_SC2), _mod_spec(mod, MOD_SH2), _mod_spec(mod, MOD_G2),
                  pl.BlockSpec((d, th), lambda b, i, j: (0, j)),
                  pl.BlockSpec((1, th), lambda b, i, j: (0, j)),
                  pl.BlockSpec((th, d), lambda b, i, j: (j, 0)),
                  pl.BlockSpec((1, d), const), pl.BlockSpec((1, d), const), pl.BlockSpec((1, d), const)],
        out_specs=[pl.BlockSpec((1, tm, d), tok)],
        out_shape=[jax.ShapeDtypeStruct(x.shape, F32)],
        scratch_shapes=[pltpu.VMEM((tm, d), BF16)],
        sem=("arbitrary", "arbitrary", "arbitrary"), nbytes=nbytes,
        name="ffn_ln",
    )


SC_PARTS = SC_WIDTH // S5_WIDTH


def _inproj1_kernel(*refs, full, ctx_row):
    x_ref, sc_ref, sh_ref, wu_ref = refs[:4]
    u_s = refs[-1]
    h = _modulate(x_ref[0], sc_ref, sh_ref, ctx_row)
    u = _dot(h, wu_ref[...])
    if full:
        wb_refs = refs[4:4 + SC_PARTS]
        wc_refs = refs[4 + SC_PARTS:4 + 2 * SC_PARTS]
        wx_refs = refs[4 + 2 * SC_PARTS:4 + 3 * SC_PARTS]
        u_ref, uflat_ref, gb_ref, p_ref = refs[4 + 3 * SC_PARTS:-1]
        u_ref[0] = u.astype(BF16)
        for c in range(SC_PARTS):
            cols = slice(c * S5_WIDTH, (c + 1) * S5_WIDTH)
            gb_ref[0, :, cols] = _dot(h, wb_refs[c][...]).astype(BF16)
            p_ref[0, :, cols] = (_dot(h, wc_refs[c][...]) * _dot(h, wx_refs[c][...])).astype(BF16)
    else:
        uflat_ref, = refs[4:-1]
    for k in range(S5_SLABS):
        u_s[k] = u[:, k * LANES:(k + 1) * LANES]
    chunks = u.shape[0] // S5_CHUNK
    for t in range(S5_CHUNK):
        for k in range(S5_SLABS):
            col = t * S5_WIDTH + k * LANES
            uflat_ref[0, :, col:col + LANES] = u_s[k, pl.ds(t, chunks, stride=S5_CHUNK), :].astype(BF16)


def _inproj1(x, mod, ctx_row, w_in, tm, full):
    bsz, n, d = x.shape
    tok = lambda b, i: (b, i, 0)
    n_blocks = 1 + 3 * SC_PARTS if full else 1
    flat_spec = pl.BlockSpec((1, tm // S5_CHUNK, S5_CHUNK * S5_WIDTH), tok)
    flat_shape = jax.ShapeDtypeStruct((bsz, n // S5_CHUNK, S5_CHUNK * S5_WIDTH), BF16)
    token = lambda w: (pl.BlockSpec((1, tm, w), tok), jax.ShapeDtypeStruct((bsz, n, w), BF16))
    outs = ([token(S5_WIDTH), (flat_spec, flat_shape), token(SC_WIDTH), token(SC_WIDTH)] if full
            else [(flat_spec, flat_shape)])
    ncols = n_blocks * S5_WIDTH
    nbytes = 2 * tm * d * 4 + 2 * d * ncols + 2 * 2 * tm * (ncols + S5_WIDTH) + tm * ncols * 4 + tm * S5_WIDTH * 4
    return pl.pallas_call(
        functools.partial(_inproj1_kernel, full=full, ctx_row=ctx_row),
        grid=(bsz, n // tm),
        in_specs=[pl.BlockSpec((1, tm, d), tok), _mod_spec(mod, MOD_SC1), _mod_spec(mod, MOD_SH1)]
                 + [_resident((d, S5_WIDTH), lambda b, i, c=c: (0, c)) for c in range(n_blocks)],
        out_specs=[spec for spec, _ in outs],
        out_shape=[shape for _, shape in outs],
        scratch_shapes=[pltpu.VMEM((S5_SLABS, tm, LANES), F32)],
        compiler_params=_params(("arbitrary", "arbitrary"), nbytes),
        name="inproj1" if full else "inproj1_ctx",
    )(x, mod, mod, *([w_in] * n_blocks))


def _s5_power_table(par_ref):
    rows = 2 * S5_POW_ROWS
    r = lax.broadcasted_iota(jnp.int32, (rows, S5_SLAB_STATE), 0)
    first = r < S5_POW_ROWS
    lag = (r % S5_POW_ROWS).astype(F32)
    pick = lambda i: jnp.where(first, par_ref[0, 0, i:i + 1, :], par_ref[0, 1, i:i + 1, :])
    dt = jnp.exp(pick(0))
    mag = jnp.exp(lag * dt * pick(1))
    ang = lag * dt * pick(2)
    return mag * jnp.cos(ang), mag * jnp.sin(ang)


def _s5_input_matrix(par_ref, bt_ref, d, pw_re, pw_im):
    a_re = par_ref[0, d, 1:2, :]
    a_im = par_ref[0, d, 2:3, :]
    ab_re = pw_re[d * S5_POW_ROWS + 1:d * S5_POW_ROWS + 2, :]
    ab_im = pw_im[d * S5_POW_ROWS + 1:d * S5_POW_ROWS + 2, :]
    den = a_re * a_re + a_im * a_im
    k_re = ((ab_re - 1.0) * a_re + ab_im * a_im) / den
    k_im = (ab_im * a_re - (ab_re - 1.0) * a_im) / den
    b_re = bt_ref[0, d, 0]
    b_im = bt_ref[0, d, 1]
    return k_re * b_re - k_im * b_im, k_re * b_im + k_im * b_re


def _cmul_row(x_re, x_im, p_re, p_im):
    return x_re * p_re - x_im * p_im, x_re * p_im + x_im * p_re


def _s5_state_prep_kernel(par_ref, bt_ref, wst_ref, a16_ref):
    pw_re, pw_im = _s5_power_table(par_ref)
    for d in range(2):
        bb_re, bb_im = _s5_input_matrix(par_ref, bt_ref, d, pw_re, pw_im)
        base = d * S5_POW_ROWS
        for lag in range(S5_CHUNK):
            e_re, e_im = _cmul_row(bb_re, bb_im, pw_re[base + lag:base + lag + 1, :],
                                   pw_im[base + lag:base + lag + 1, :])
            s = S5_CHUNK - 1 - lag if d == 0 else lag
            col = 2 * d * S5_SLAB_STATE
            wst_ref[0, s * LANES:(s + 1) * LANES, col:col + S5_SLAB_STATE] = e_re.astype(BF16)
            wst_ref[0, s * LANES:(s + 1) * LANES, col + S5_SLAB_STATE:col + 2 * S5_SLAB_STATE] = e_im.astype(BF16)
        row16 = base + S5_CHUNK
        a16_ref[0, :, 2 * d * S5_SLAB_STATE:(2 * d + 1) * S5_SLAB_STATE] = jnp.broadcast_to(
            pw_re[row16:row16 + 1, :], (SUBLANES, S5_SLAB_STATE))
        a16_ref[0, :, (2 * d + 1) * S5_SLAB_STATE:(2 * d + 2) * S5_SLAB_STATE] = jnp.broadcast_to(
            pw_im[row16:row16 + 1, :], (SUBLANES, S5_SLAB_STATE))


def _s5_output_prep_kernel(par_ref, bt_ref, c_ref, ct_ref, wbig_ref):
    pw_re, pw_im = _s5_power_table(par_ref)
    kern = []
    for d in range(2):
        bb_re, bb_im = _s5_input_matrix(par_ref, bt_ref, d, pw_re, pw_im)
        c_re = c_ref[0, d, 0].astype(BF16)
        c_im = c_ref[0, d, 1].astype(BF16)
        base = d * S5_POW_ROWS
        per_lag = []
        for lag in range(S5_CHUNK):
            e_re, e_im = _cmul_row(bb_re, bb_im, pw_re[base + lag:base + lag + 1, :],
                                   pw_im[base + lag:base + lag + 1, :])
            per_lag.append(_dot_nt(e_re.astype(BF16), c_re) - _dot_nt(e_im.astype(BF16), c_im))
        kern.append(per_lag)
    for s in range(S5_CHUNK):
        for t in range(S5_CHUNK):
            if s < t:
                blk = kern[0][t - s]
            elif s > t:
                blk = kern[1][s - t]
            else:
                blk = kern[0][0] + kern[1][0]
            wbig_ref[0, s * LANES:(s + 1) * LANES, t * LANES:(t + 1) * LANES] = blk.astype(BF16)
    pt_re = pw_re.T
    pt_im = pw_im.T
    for d in range(2):
        ct_re = ct_ref[0, d, 0]
        ct_im = ct_ref[0, d, 1]
        for t in range(S5_CHUNK):
            lag = t + 1 if d == 0 else S5_CHUNK - t
            col = d * S5_POW_ROWS + lag
            p_re = pt_re[:, col:col + 1]
            p_im = pt_im[:, col:col + 1]
            g_re = ct_re * p_re - ct_im * p_im
            g_im = ct_re * p_im + ct_im * p_re
            r0 = S5_XK + 2 * d * S5_SLAB_STATE
            wbig_ref[0, r0:r0 + S5_SLAB_STATE, t * LANES:(t + 1) * LANES] = g_re.astype(BF16)
            wbig_ref[0, r0 + S5_SLAB_STATE:r0 + 2 * S5_SLAB_STATE, t * LANES:(t + 1) * LANES] = (-g_im).astype(BF16)


def _s5_operators(log_dt, a_re, a_im, b_re, b_im, c_re, c_im):
    gs = S5_SLAB_GROUPS
    eye = jnp.eye(gs, dtype=F32)

    def lanes(v):
        return v.reshape(2, S5_SLABS, gs * S5_STATE)

    par = jnp.stack([lanes(jnp.broadcast_to(log_dt[:, :, None], a_re.shape)), lanes(a_re), lanes(a_im)], axis=2)
    par = jnp.pad(par, ((0, 0), (0, 0), (0, SUBLANES - 3), (0, 0))).transpose(1, 0, 2, 3)

    def embed_bt(b):
        b = b.reshape(2, S5_SLABS, gs, S5_STATE, S5_GROUP)
        e = b.transpose(0, 1, 2, 4, 3)[:, :, :, :, None, :] * eye[None, None, :, None, :, None]
        return e.reshape(2, S5_SLABS, gs * S5_GROUP, gs * S5_STATE).transpose(1, 0, 2, 3)

    def embed_c(c):
        c = c.reshape(2, S5_SLABS, gs, S5_GROUP, S5_STATE)
        e = c[:, :, :, :, None, :] * eye[None, None, :, None, :, None]
        return e.reshape(2, S5_SLABS, gs * S5_GROUP, gs * S5_STATE).transpose(1, 0, 2, 3)

    bt = jnp.stack([embed_bt(b_re), embed_bt(b_im)], axis=2)
    cm = jnp.stack([embed_c(c_re), embed_c(c_im)], axis=2)
    ct = cm.transpose(0, 1, 2, 4, 3)

    slab5 = lambda shape: pl.BlockSpec((1,) + shape, lambda k: (k, 0, 0, 0, 0))
    par_spec = pl.BlockSpec((1, 2, SUBLANES, S5_SLAB_STATE), lambda k: (k, 0, 0, 0))
    bt_spec = slab5((2, 2, LANES, S5_SLAB_STATE))
    wst, a16 = pl.pallas_call(
        _s5_state_prep_kernel,
        grid=(S5_SLABS,),
        in_specs=[par_spec, bt_spec],
        out_specs=[pl.BlockSpec((1, S5_XK, S5_HK), lambda k: (k, 0, 0)),
                   pl.BlockSpec((1, SUBLANES, S5_HK), lambda k: (k, 0, 0))],
        out_shape=[jax.ShapeDtypeStruct((S5_SLABS, S5_XK, S5_HK), BF16),
                   jax.ShapeDtypeStruct((S5_SLABS, SUBLANES, S5_HK), F32)],
        compiler_params=_params(("arbitrary",), 2 * S5_XK * S5_HK * 2 + 8 * 1024 * 1024),
        name="s5_state_operator",
    )(par, bt)
    wbig = pl.pallas_call(
        _s5_output_prep_kernel,
        grid=(S5_SLABS,),
        in_specs=[par_spec, bt_spec, bt_spec, slab5((2, 2, S5_SLAB_STATE, LANES))],
        out_specs=pl.BlockSpec((1, S5_XK + S5_HK, S5_XK), lambda k: (k, 0, 0)),
        out_shape=jax.ShapeDtypeStruct((S5_SLABS, S5_XK + S5_HK, S5_XK), BF16),
        compiler_params=_params(("arbitrary",), 2 * (S5_XK + S5_HK) * S5_XK * 2 + 8 * 1024 * 1024),
        name="s5_output_operator",
    )(par, bt, cm, ct)
    return wst, a16, wbig


def _s5_states_kernel(*refs, n_ctx, n_lat):
    xc_refs = refs[:S5_CHUNK]
    xl_refs = refs[S5_CHUNK:2 * S5_CHUNK]
    wst_ref, a16_ref, h_ref, s_s, h_s = refs[2 * S5_CHUNK:]
    x = jnp.concatenate([jnp.concatenate([r[0] for r in xc_refs], axis=1),
                         jnp.concatenate([r[0] for r in xl_refs], axis=1)], axis=0)
    s_s[...] = _dot(x, wst_ref[0])
    p = S5_SLAB_STATE
    af_re = a16_ref[0, 0:1, 0:p]
    af_im = a16_ref[0, 0:1, p:2 * p]
    ar_re = a16_ref[0, 0:1, 2 * p:3 * p]
    ar_im = a16_ref[0, 0:1, 3 * p:4 * p]
    total = n_ctx + n_lat

    def step(i, carry):
        f_re, f_im, r_re, r_im = carry
        h_s[pl.ds(i, 1), 0:p] = f_re
        h_s[pl.ds(i, 1), p:2 * p] = f_im
        s_re = s_s[pl.ds(i, 1), 0:p]
        s_im = s_s[pl.ds(i, 1), p:2 * p]
        f_re, f_im = af_re * f_re - af_im * f_im + s_re, af_re * f_im + af_im * f_re + s_im
        j = jnp.where(i < n_ctx, n_ctx - 1 - i, total + n_ctx - 1 - i)
        h_s[pl.ds(j, 1), 2 * p:3 * p] = r_re
        h_s[pl.ds(j, 1), 3 * p:4 * p] = r_im
        s_re = s_s[pl.ds(j, 1), 2 * p:3 * p]
        s_im = s_s[pl.ds(j, 1), 3 * p:4 * p]
        r_re, r_im = ar_re * r_re - ar_im * r_im + s_re, ar_re * r_im + ar_im * r_re + s_im
        return f_re, f_im, r_re, r_im

    zero = jnp.zeros((1, p), F32)
    lax.fori_loop(0, total, step, (zero, zero, zero, zero), unroll=4)
    h_ref[0, 0] = h_s[pl.ds(n_ctx, n_lat), :].astype(BF16)


def _s5_states(uc_flat, ul_flat, wst, a16):
    bsz, n_ctx, _ = uc_flat.shape
    n_lat = ul_flat.shape[1]
    piece = lambda rows, s: pl.BlockSpec((1, rows, LANES), lambda k, b, s=s: (b, 0, s * S5_SLABS + k))
    in_specs = ([piece(n_ctx, s) for s in range(S5_CHUNK)] + [piece(n_lat, s) for s in range(S5_CHUNK)]
                + [pl.BlockSpec((1, S5_XK, S5_HK), lambda k, b: (k, 0, 0)),
                   pl.BlockSpec((1, SUBLANES, S5_HK), lambda k, b: (k, 0, 0))])
    total = n_ctx + n_lat
    nbytes = 2 * S5_XK * S5_HK * 2 + 2 * total * S5_XK * 2 + 3 * total * S5_HK * 4 + 2 * n_lat * S5_HK * 2
    return pl.pallas_call(
        functools.partial(_s5_states_kernel, n_ctx=n_ctx, n_lat=n_lat),
        grid=(S5_SLABS, bsz),
        in_specs=in_specs,
        out_specs=pl.BlockSpec((1, 1, n_lat, S5_HK), lambda k, b: (b, k, 0, 0)),
        out_shape=jax.ShapeDtypeStruct((bsz, S5_SLABS, n_lat, S5_HK), BF16),
        scratch_shapes=[pltpu.VMEM((total, S5_HK), F32), pltpu.VMEM((total, S5_HK), F32)],
        compiler_params=_params(("arbitrary", "arbitrary"), nbytes),
        name="s5_states",
    )(*([uc_flat] * S5_CHUNK + [ul_flat] * S5_CHUNK + [wst, a16]))


def _s5_readout_kernel(*refs):
    x_refs = refs[:S5_CHUNK]
    h_ref, w_ref, y_ref = refs[S5_CHUNK:]
    lhs = jnp.concatenate([r[0] for r in x_refs] + [h_ref[0, 0]], axis=1)
    y = _dot(lhs, w_ref[0])
    chunks = y.shape[0]
    for t in range(S5_CHUNK):
        y_ref[0, pl.ds(t, chunks, stride=S5_CHUNK), :] = y[:, t * LANES:(t + 1) * LANES]


def _s5_readout(ul_flat, h_in, wbig):
    bsz, n_lat, _ = ul_flat.shape
    piece = lambda s: pl.BlockSpec((1, n_lat, LANES), lambda k, b, s=s: (b, 0, s * S5_SLABS + k))
    nbytes = 2 * (S5_XK + S5_HK) * S5_XK * 2 + 2 * n_lat * (S5_XK + S5_HK) * 2 * 2 + n_lat * S5_XK * 4 * 3
    return pl.pallas_call(
        _s5_readout_kernel,
        grid=(S5_SLABS, bsz),
        in_specs=[piece(s) for s in range(S5_CHUNK)]
                 + [pl.BlockSpec((1, 1, n_lat, S5_HK), lambda k, b: (b, k, 0, 0)),
                    pl.BlockSpec((1, S5_XK + S5_HK, S5_XK), lambda k, b: (k, 0, 0))],
        out_specs=pl.BlockSpec((1, n_lat * S5_CHUNK, LANES), lambda k, b: (b, 0, k)),
        out_shape=jax.ShapeDtypeStruct((bsz, n_lat * S5_CHUNK, S5_WIDTH), F32),
        compiler_params=_params(("arbitrary", "arbitrary"), nbytes),
        name="s5_readout",
    )(*([ul_flat] * S5_CHUNK + [h_in, wbig]))


SC_HALO = 16


def _outproj1_kernel(y_ref, u_ref, gb_ref, p_ref, pprev_ref, pnext_ref, x_ref, gate_ref,
                     dskip_ref, wglu_ref, bglu_ref, cw_ref, cb_ref, w_ref, lg_ref, lb_ref, o_ref, *, ctx_row):
    i = pl.program_id(1)
    tm = x_ref.shape[1]
    yc = y_ref[0] + dskip_ref[...] * u_ref[0].astype(F32)
    z = _gelu_tanh(yc)
    y_c = z * _sigmoid(_dot(z.astype(BF16), wglu_ref[...]) + bglu_ref[...])
    p = p_ref[0].astype(F32)
    row = lax.broadcasted_iota(jnp.int32, (tm, 1), 0)
    prev_row = jnp.where(i > 0, pprev_ref[0, SC_HALO - 1:SC_HALO, :].astype(F32), 0.0)
    next_row = jnp.where(i < pl.num_programs(1) - 1, pnext_ref[0, 0:1, :].astype(F32), 0.0)
    p_dn = jnp.where(row == 0, prev_row, pltpu.roll(p, 1, 0))
    p_up = jnp.where(row == tm - 1, next_row, pltpu.roll(p, tm - 1, 0))
    conv = cb_ref[...] + cw_ref[0:1, :] * p_dn + cw_ref[1:2, :] * p + cw_ref[2:3, :] * p_up
    y_d = gb_ref[0].astype(F32) * conv
    y = jnp.concatenate([y_c.astype(BF16), y_d.astype(BF16)], axis=1)
    out = _dot(y, w_ref[...])
    o_ref[0] = _layer_norm(ALPHA * x_ref[0] + _mod_row(gate_ref, ctx_row) * out, lg_ref[...], lb_ref[...])


def _outproj1(y, u, gb, p, x, mod, ctx_row, d_skip, w_glu, b_glu, conv_w, conv_b, w_out, ln_g, ln_b, tm):
    bsz, n, d = x.shape
    tok = lambda b, i: (b, i, 0)
    const = lambda b, i: (0, 0)
    per = tm // SC_HALO
    last = n // SC_HALO - 1
    nbytes = (2 * 2 * tm * d * 4 + w_out.size * 2 + 2 * 2 * tm * (2 * S5_WIDTH + 2 * SC_WIDTH) * 2
              + 6 * tm * SC_WIDTH * 4 + 2 * tm * d * 4)
    return pl.pallas_call(
        functools.partial(_outproj1_kernel, ctx_row=ctx_row),
        grid=(bsz, n // tm),
        in_specs=[pl.BlockSpec((1, tm, S5_WIDTH), tok), pl.BlockSpec((1, tm, S5_WIDTH), tok),
                  pl.BlockSpec((1, tm, SC_WIDTH), tok), pl.BlockSpec((1, tm, SC_WIDTH), tok),
                  pl.BlockSpec((1, SC_HALO, SC_WIDTH), lambda b, i: (b, jnp.maximum(i * per - 1, 0), 0)),
                  pl.BlockSpec((1, SC_HALO, SC_WIDTH), lambda b, i: (b, jnp.minimum((i + 1) * per, last), 0)),
                  pl.BlockSpec((1, tm, d), tok), _mod_spec(mod, MOD_G1),
                  pl.BlockSpec((1, S5_WIDTH), const), _resident(w_glu.shape, const),
                  pl.BlockSpec((1, S5_WIDTH), const),
                  pl.BlockSpec((SC_CONV, SC_WIDTH), const), pl.BlockSpec((1, SC_WIDTH), const),
                  _resident(w_out.shape, const),
                  pl.BlockSpec((1, d), const), pl.BlockSpec((1, d), const)],
        out_specs=pl.BlockSpec((1, tm, d), tok),
        out_shape=jax.ShapeDtypeStruct(x.shape, F32),
        compiler_params=_params(("arbitrary", "arbitrary"), nbytes),
        name="outproj1_ln",
    )(y, u, gb, p, p, p, x, mod, d_skip, w_glu, b_glu, conv_w, conv_b, w_out, ln_g, ln_b)


def _rope_tables(n):
    t = np.arange(n)
    inv = ROPE_BASE ** (-np.arange(ROPE_FREQS, dtype=np.float64) / ROPE_FREQS)
    ang_r = (t // GRID_W)[:, None] * inv[None, :]
    ang_c = (t % GRID_W)[:, None] * inv[None, :]
    cos = np.concatenate([np.cos(ang_r), np.cos(ang_r), np.cos(ang_c), np.cos(ang_c)], axis=1)
    sin = np.concatenate([-np.sin(ang_r), np.sin(ang_r), -np.sin(ang_c), np.sin(ang_c)], axis=1)
    return jnp.asarray(cos, F32), jnp.asarray(sin, F32)


TM_LATENT = 512
TM_PROJ = 1024
TM_CONTEXT = 256
FFN_TH = 2048


def kernel(x, c, ctx, c_ctx, mod_w, mod_b, ln1_g, ln1_b, ln2_g, ln2_b, ffn_w1, ffn_b1, ffn_w2, ffn_b2,
           ab_w_in, ab_w_out, lru_conv_w, lru_conv_b, lru_w_a, lru_b_a, lru_w_x, lru_b_x, lru_lam, att_sink,
           cd_w_in, cd_w_out, s5_log_dt, s5_a_re, s5_a_im, s5_b_re, s5_b_im, s5_c_re, s5_c_im, s5_d,
           s5_w_glu, s5_b_glu, sc_conv_w, sc_conv_b):
    bsz, n, d = x.shape
    lc = ctx.shape[1]
    assert n % TM_PROJ == 0 and n % TM_LATENT == 0 and lc % TM_CONTEXT == 0 and (bsz * lc) % TM_LATENT == 0
    assert n % GRID_W == 0

    pad_rows = SUBLANES - (bsz + 1) % SUBLANES if (bsz + 1) % SUBLANES else 0
    cc = jnp.concatenate([c, c_ctx[None, :], jnp.zeros((pad_rows, d), F32)], axis=0)
    mods = {0: _modulation(cc, mod_w, mod_b, 0)}
    ctx_row = bsz

    row2 = lambda v: v.reshape(1, -1)
    rope_tabs = _rope_tables(n)

    side = {}

    def weight(stack, name, idx):
        return side.pop((name, idx)) if (name, idx) in side else _to_bf16(stack, idx)

    def hosted(call, own, jobs):
        outs = call(casts=[(stack, idx) for _, idx, stack in jobs])
        side.update({(name, idx): w for (name, idx, _), w in zip(jobs, outs[own:own + len(jobs)])})
        return outs[:own] + outs[own + len(jobs):]

    for i in range(DEPTH):
        last = i == DEPTH - 1
        j = i // 2
        mod = mods.pop(i) if i in mods else _modulation(cc, mod_w, mod_b, i)
        todo = [("w1", i, ffn_w1), ("w2", i, ffn_w2)]
        if not last:
            nj = (i + 1) // 2
            todo += [("w1", i + 1, ffn_w1), ("w2", i + 1, ffn_w2)]
            todo += ([("ab_in", nj, ab_w_in), ("ab_out", nj, ab_w_out)] if (i + 1) % 2 == 0
                     else [("cd_in", nj, cd_w_in), ("cd_out", nj, cd_w_out)])
        todo = [job for job in todo if (job[0], job[1]) not in side]
        if i % 2 == 0:
            w_in = weight(ab_w_in, "ab_in", j)
            w_out = weight(ab_w_out, "ab_out", j)
            u, gate, q, k, v = _inproj0(x, mod, None, w_in, rope_tabs, TM_PROJ)
            uc, gatec, qc, kc, vc = _inproj0(ctx, mod, ctx_row, w_in, None, TM_CONTEXT)
            w_gate = (0.5 * jnp.concatenate([lru_w_a[j, 0], lru_w_x[j, 0], lru_w_a[j, 1], lru_w_x[j, 1]],
                                            axis=-1)).astype(BF16)
            hb = lambda b: b.reshape(LRU_HEADS, 1, LRU_BLOCK)
            b_gate = 0.5 * jnp.concatenate([hb(lru_b_a[j, 0]), hb(lru_b_x[j, 0]), hb(lru_b_a[j, 1]),
                                            hb(lru_b_x[j, 1])], axis=-1)
            lam = jnp.concatenate([hb(lru_lam[j, 0]), hb(lru_lam[j, 1])], axis=-1)
            ahead_mod = [] if last else [(cc, mod_w, mod_b, i + 1)]
            *lru_out, = hosted(functools.partial(_rglru, u, gate, uc, gatec, lru_conv_w[j], row2(lru_conv_b[j]),
                                                 w_gate, b_gate, lam, mods=ahead_mod), 2, todo[0::2])
            ya, yac = lru_out[:2]
            mods.update({i + 1: m for m in lru_out[2:]})
            yb, = hosted(functools.partial(_window_attention, q, k, v, kc, vc, att_sink[j]), 1, todo[1::2])
            todo = []
            x = _outproj0(ya, yb, x, mod, None, w_out, row2(ln1_g[i]), row2(ln1_b[i]), TM_PROJ)
            if not last:
                ybc = _context_attention(qc, kc, vc, att_sink[j])
                ctx = _outproj0(yac, ybc, ctx, mod, ctx_row, w_out, row2(ln1_g[i]), row2(ln1_b[i]), TM_CONTEXT)
        else:
            assert last
            w_in = weight(cd_w_in, "cd_in", j)
            w_out = weight(cd_w_out, "cd_out", j)
            u, u_flat, gb, p = _inproj1(x, mod, None, w_in, TM_PROJ, True)
            (uc_flat,) = _inproj1(ctx, mod, ctx_row, w_in, TM_CONTEXT, False)
            wst, a16, wbig = _s5_operators(s5_log_dt[j], s5_a_re[j], s5_a_im[j], s5_b_re[j], s5_b_im[j],
                                           s5_c_re[j], s5_c_im[j])
            h_in = _s5_states(uc_flat, u_flat, wst, a16)
            y = _s5_readout(u_flat, h_in, wbig)
            x = _outproj1(y, u, gb, p, x, mod, None, row2(s5_d[j]), s5_w_glu[j].astype(BF16), row2(s5_b_glu[j]),
                          sc_conv_w[j], row2(sc_conv_b[j]), w_out, row2(ln1_g[i]), row2(ln1_b[i]), TM_LATENT)
        w1 = weight(ffn_w1, "w1", i)
        w2 = weight(ffn_w2, "w2", i)
        ahead = [job for job in todo if job[1] != i or job[0] not in ("w1", "w2")]
        x, = hosted(functools.partial(_ffn, x, mod, None, w1, row2(ffn_b1[i]), w2, row2(ffn_b2[i]),
                                      row2(ln2_g[i]), row2(ln2_b[i]), TM_LATENT, FFN_TH), 1, ahead)
        if not last:
            rows = ctx.reshape(1, bsz * lc, d)
            rows, = _ffn(rows, mod, ctx_row, w1, row2(ffn_b1[i]), w2, row2(ffn_b2[i]), row2(ln2_g[i]),
                         row2(ln2_b[i]), TM_LATENT, FFN_TH)
            ctx = rows.reshape(bsz, lc, d)
    return x
```

```python
import functools
import math

import jax
import jax.numpy as jnp
import numpy as np
from jax import lax
from jax.experimental import pallas as pl
from jax.experimental.pallas import tpu as pltpu

F32 = jnp.float32
BF16 = jnp.bfloat16

D_MODEL = 2048
DEPTH = 2
GRID_W = 64
LRU_WIDTH = D_MODEL // 2
LRU_HEADS = 8
LRU_BLOCK = LRU_WIDTH // LRU_HEADS
LRU_CONV = 4
LRU_CONV_LEFT = 2
LRU_C = 8.0
ATT_HEAD_DIM = 128
ATT_Q_HEADS = (D_MODEL // 2) // ATT_HEAD_DIM
ATT_KV_HEADS = 2
ATT_GROUP = ATT_Q_HEADS // ATT_KV_HEADS
ATT_Q_WIDTH = ATT_Q_HEADS * ATT_HEAD_DIM
ATT_KV_WIDTH = ATT_KV_HEADS * ATT_HEAD_DIM
WINDOW = 128
ATT_BLOCK = 128
ROPE_BASE = 10000.0
ROPE_FREQS = ATT_HEAD_DIM // 4
S5_WIDTH = D_MODEL // 4
S5_GROUP = 16
S5_GROUPS = S5_WIDTH // S5_GROUP
S5_STATE = 64
SC_WIDTH = D_MODEL - S5_WIDTH
SC_CONV = 3
FFN_HIDDEN = 4 * D_MODEL
ALPHA = (2.0 * DEPTH) ** 0.25
LN_EPS = 1e-5
NEG_INF = -1e30
LOG2E = math.log2(math.e)

LANES = 128
SUBLANES = 8
V7X_VMEM_BYTES = 64 * 1024 * 1024
V7X_VMEM_BUDGET = 62 * 1024 * 1024

S5_CHUNK = 16
S5_SLAB_GROUPS = LANES // S5_GROUP
S5_SLABS = S5_WIDTH // LANES
S5_SLAB_STATE = S5_SLAB_GROUPS * S5_STATE
S5_XK = S5_CHUNK * LANES
S5_HK = 4 * S5_SLAB_STATE
S5_POW_ROWS = 64


def _vmem_limit(nbytes):
    return int(min(V7X_VMEM_BUDGET, max(nbytes * 3 // 2, 16 * 1024 * 1024)))


def _resident(shape, index_map):
    return pl.BlockSpec(shape, index_map, pipeline_mode=pl.Buffered(1))


def _params(sem, nbytes):
    return pltpu.CompilerParams(dimension_semantics=sem, vmem_limit_bytes=_vmem_limit(nbytes))


def _dot(a, b):
    return jnp.dot(a, b, preferred_element_type=F32)


def _dot_nt(a, b):
    return lax.dot_general(a, b, (((1,), (1,)), ((), ())), preferred_element_type=F32)


def _layer_norm(v, g, b):
    mu = jnp.mean(v, axis=-1, keepdims=True)
    c = v - mu
    var = jnp.mean(c * c, axis=-1, keepdims=True)
    return c * lax.rsqrt(var + LN_EPS) * g + b


def _gelu_tanh(x):
    return 0.5 * x * (1.0 + jnp.tanh(math.sqrt(2.0 / math.pi) * (x + 0.044715 * (x * x * x))))


def _sigmoid(x):
    return 0.5 * (1.0 + jnp.tanh(0.5 * x))


MOD_SH1, MOD_SC1, MOD_G1, MOD_SH2, MOD_SC2, MOD_G2 = range(6)


def _mod_spec(mod, chunk):
    _, rows, width = mod.shape
    return pl.BlockSpec((1, rows, width // 6), lambda *_: (0, 0, chunk))


def _mod_row(ref, ctx_row):
    row = pl.program_id(0) if ctx_row is None else ctx_row
    return ref[0, pl.ds(row, 1), :]


def _modulate(x, sc_ref, sh_ref, ctx_row):
    return (x * (1.0 + _mod_row(sc_ref, ctx_row)) + _mod_row(sh_ref, ctx_row)).astype(BF16)


CAST_BLOCK_BYTES = 8 * 1024 * 1024


def _cast_kernel(w_ref, o_ref):
    o_ref[...] = w_ref[0].astype(BF16)


def _to_bf16(w, layer):
    _, rows, cols = w.shape
    tr = rows
    while tr * cols * 4 > CAST_BLOCK_BYTES and tr % 2 == 0 and tr // 2 >= 2 * SUBLANES:
        tr //= 2
    return pl.pallas_call(
        _cast_kernel,
        grid=(rows // tr,),
        in_specs=[pl.BlockSpec((1, tr, cols), lambda i: (layer, i, 0))],
        out_specs=pl.BlockSpec((tr, cols), lambda i: (i, 0)),
        out_shape=jax.ShapeDtypeStruct((rows, cols), BF16),
        compiler_params=_params(("arbitrary",), 2 * tr * cols * 6),
        name="weight_cast",
    )(w)


CAST_ROWS = 2 * SUBLANES


def _silu_bf16(v):
    return (v * _sigmoid(v)).astype(BF16)


def _hosted_call(body, *, grid, in_specs, out_specs, out_shape, args, sem, nbytes, casts=(), mods=(), **kwargs):
    steps = math.prod(grid)

    def flat(*ids):
        step = 0
        for extent, i in zip(grid, ids):
            step = step * extent + i
        return step

    def upto(used):
        return lambda *ids: jnp.minimum(flat(*ids), used - 1)

    side_in, side_out, side_shape, side_args, jobs = [], [], [], [], []
    for w, layer in casts:
        _, rows, cols = w.shape
        rps = max(CAST_ROWS, rows // steps)
        assert rows % rps == 0 and rows // rps <= steps
        at = upto(rows // rps)
        side_in.append(pl.BlockSpec((1, rps, cols), lambda *ids, at=at, layer=layer: (layer, at(*ids), 0)))
        side_out.append(pl.BlockSpec((rps, cols), lambda *ids, at=at: (at(*ids), 0)))
        side_shape.append(jax.ShapeDtypeStruct((rows, cols), BF16))
        side_args.append(w)
        nbytes += 2 * rps * cols * (4 + 2)
        jobs.append((1, lambda ins, out: out.__setitem__(Ellipsis, ins[0][0].astype(BF16))))
    for cc, mod_w, mod_b, layer in mods:
        depth, d, n = mod_w.shape
        width = LANES * max(1, n // LANES // steps)
        assert n % width == 0 and n // width <= steps
        at = upto(n // width)
        side_in += [pl.BlockSpec(cc.shape, lambda *ids: (0, 0)),
                    pl.BlockSpec((1, d, width), lambda *ids, at=at, layer=layer: (layer, 0, at(*ids))),
                    pl.BlockSpec((1, 1, width), lambda *ids, at=at, layer=layer: (layer, 0, at(*ids)))]
        side_out.append(pl.BlockSpec((1, cc.shape[0], width), lambda *ids, at=at: (0, 0, at(*ids))))
        side_shape.append(jax.ShapeDtypeStruct((1, cc.shape[0], n), F32))
        side_args += [cc, mod_w, mod_b.reshape(depth, 1, n)]
        nbytes += 2 * d * width * 4 + d * width * 2
        jobs.append((3, lambda ins, out: out.__setitem__(
            0, _dot(_silu_bf16(ins[0][...]), ins[1][0].astype(BF16)) + ins[2][0])))
    n_in, n_out, n_side_in, n_side_out = len(in_specs), len(out_specs), len(side_in), len(side_out)

    def kernel(*refs):
        ins = refs[:n_in]
        s_ins = refs[n_in:n_in + n_side_in]
        outs = refs[n_in + n_side_in:n_in + n_side_in + n_out]
        s_outs = refs[n_in + n_side_in + n_out:n_in + n_side_in + n_out + n_side_out]
        pos = 0
        for (arity, run), out in zip(jobs, s_outs):
            run(s_ins[pos:pos + arity], out)
            pos += arity
        body(*ins, *outs, *refs[n_in + n_side_in + n_out + n_side_out:])

    return pl.pallas_call(
        kernel, grid=grid, in_specs=list(in_specs) + side_in, out_specs=list(out_specs) + side_out,
        out_shape=list(out_shape) + side_shape, compiler_params=_params(sem, nbytes), **kwargs,
    )(*args, *side_args)


def _mod_kernel(cc_ref, w_ref, b_ref, o_ref):
    o_ref[0] = _dot(_silu_bf16(cc_ref[...]), w_ref[0].astype(BF16)) + b_ref[0]


def _modulation(cc, mod_w, mod_b, layer):
    depth, d, n = mod_w.shape
    tn = 1024
    rows = cc.shape[0]
    return pl.pallas_call(
        _mod_kernel,
        grid=(n // tn,),
        in_specs=[pl.BlockSpec((rows, d), lambda j: (0, 0)),
                  pl.BlockSpec((1, d, tn), lambda j: (layer, 0, j)),
                  pl.BlockSpec((1, 1, tn), lambda j: (layer, 0, j))],
        out_specs=pl.BlockSpec((1, rows, tn), lambda j: (0, 0, j)),
        out_shape=jax.ShapeDtypeStruct((1, rows, n), F32),
        compiler_params=_params(("arbitrary",), 2 * d * tn * 4 + d * tn * 2),
        name="modulation",
    )(cc, mod_w, mod_b.reshape(depth, 1, n))


def _rope(x, cos, sin_signed, heads):
    lane = lax.broadcasted_iota(jnp.int32, (x.shape[0], ATT_HEAD_DIM), 1)
    first = (lane % (2 * ROPE_FREQS)) < ROPE_FREQS
    out = []
    for h in range(heads):
        xs = x[:, h * ATT_HEAD_DIM:(h + 1) * ATT_HEAD_DIM]
        swapped = jnp.where(first, pltpu.roll(xs, ATT_HEAD_DIM - ROPE_FREQS, 1), pltpu.roll(xs, ROPE_FREQS, 1))
        out.append(xs * cos + swapped * sin_signed)
    return out


def _inproj0_kernel(*refs, rope, ctx_row):
    if rope:
        (x_ref, sc_ref, sh_ref, cos_ref, sin_ref, wu_ref, wg_ref, wq_ref, wk_ref, wv_ref,
         u_ref, g_ref, q_ref, k_ref, v_ref) = refs
    else:
        (x_ref, sc_ref, sh_ref, wu_ref, wg_ref, wq_ref, wk_ref, wv_ref,
         u_ref, g_ref, q_ref, k_ref, v_ref) = refs
    h = _modulate(x_ref[0], sc_ref, sh_ref, ctx_row)
    q = _dot(h, wq_ref[...]) * (ATT_HEAD_DIM ** -0.5 * LOG2E)
    k = _dot(h, wk_ref[...])
    if rope:
        cos = cos_ref[...]
        sin = sin_ref[...]
        for hd, piece in enumerate(_rope(q, cos, sin, ATT_Q_HEADS)):
            q_ref[0, :, hd * ATT_HEAD_DIM:(hd + 1) * ATT_HEAD_DIM] = piece.astype(BF16)
        for hd, piece in enumerate(_rope(k, cos, sin, ATT_KV_HEADS)):
            k_ref[0, :, hd * ATT_HEAD_DIM:(hd + 1) * ATT_HEAD_DIM] = piece.astype(BF16)
    else:
        q_ref[0] = q.astype(BF16)
        k_ref[0] = k.astype(BF16)
    u = _dot(h, wu_ref[...])
    g = _dot(h, wg_ref[...])
    for hd in range(LRU_HEADS):
        u_ref[0, hd] = u[:, hd * LRU_BLOCK:(hd + 1) * LRU_BLOCK].astype(BF16)
        g_ref[0, hd] = g[:, hd * LRU_BLOCK:(hd + 1) * LRU_BLOCK].astype(BF16)
    v_ref[0] = _dot(h, wv_ref[...]).astype(BF16)


def _inproj0(x, mod, ctx_row, w_in, rope_tabs, tm):
    bsz, n, d = x.shape
    rope = rope_tabs is not None
    tok = lambda b, i: (b, i, 0)
    in_specs = [pl.BlockSpec((1, tm, d), tok), _mod_spec(mod, MOD_SC1), _mod_spec(mod, MOD_SH1)]
    args = [x, mod, mod]
    if rope:
        in_specs += [pl.BlockSpec((tm, ATT_HEAD_DIM), lambda b, i: (i, 0))] * 2
        args += list(rope_tabs)
    kv_at = (2 * LRU_WIDTH + ATT_Q_WIDTH) // ATT_KV_WIDTH
    in_specs += [_resident((d, LRU_WIDTH), lambda b, i: (0, 0)), _resident((d, LRU_WIDTH), lambda b, i: (0, 1)),
                 _resident((d, ATT_Q_WIDTH), lambda b, i: (0, 2)),
                 _resident((d, ATT_KV_WIDTH), lambda b, i: (0, kv_at)),
                 _resident((d, ATT_KV_WIDTH), lambda b, i: (0, kv_at + 1))]
    args += [w_in] * 5
    head_major = jax.ShapeDtypeStruct((bsz, LRU_HEADS, n, LRU_BLOCK), BF16)
    head_spec = pl.BlockSpec((1, LRU_HEADS, tm, LRU_BLOCK), lambda b, i: (b, 0, i, 0))
    nbytes = 2 * tm * d * 4 + 2 * w_in.size + 2 * 2 * tm * w_in.shape[1] + tm * w_in.shape[1] * 4
    return pl.pallas_call(
        functools.partial(_inproj0_kernel, rope=rope, ctx_row=ctx_row),
        grid=(bsz, n // tm),
        in_specs=in_specs,
        out_specs=[head_spec, head_spec,
                   pl.BlockSpec((1, tm, ATT_Q_WIDTH), tok),
                   pl.BlockSpec((1, tm, ATT_KV_WIDTH), tok),
                   pl.BlockSpec((1, tm, ATT_KV_WIDTH), tok)],
        out_shape=[head_major, head_major,
                   jax.ShapeDtypeStruct((bsz, n, ATT_Q_WIDTH), BF16),
                   jax.ShapeDtypeStruct((bsz, n, ATT_KV_WIDTH), BF16),
                   jax.ShapeDtypeStruct((bsz, n, ATT_KV_WIDTH), BF16)],
        compiler_params=_params(("arbitrary", "arbitrary"), nbytes),
        name="inproj0_rope" if rope else "inproj0_ctx",
    )(*args)


LRU_PAD = SUBLANES
LRU_TILE = 256


def _scan8(a, b, row, reverse):
    for k in (1, 2, 4):
        if reverse:
            keep = row < SUBLANES - k
            shift = SUBLANES - k
        else:
            keep = row >= k
            shift = k
        a_sh = jnp.where(keep, pltpu.roll(a, shift, 0), 1.0)
        b_sh = jnp.where(keep, pltpu.roll(b, shift, 0), 0.0)
        b = a * b_sh + b
        a = a * a_sh
    return a, b


def _scan8_rows(a, b, row, reverse):
    out_a, out_b = [], []
    for g in range(a.shape[0] // SUBLANES):
        sl = slice(g * SUBLANES, (g + 1) * SUBLANES)
        ag, bg = _scan8(a[sl], b[sl], row, reverse)
        out_a.append(ag)
        out_b.append(bg)
    return jnp.concatenate(out_a, axis=0), jnp.concatenate(out_b, axis=0)


def _lru_kernel(ul_ref, gl_ref, uc_ref, gc_ref, cw_ref, cb_ref, wg_ref, bg_ref, lam_ref,
                yl_ref, yc_ref, upad, af_s, bf_s, ar_s, br_s, cin_s, *, n_lat, n_ctx):
    cw = cw_ref[...]
    cb = cb_ref[...]
    wg = wg_ref[0]
    bg = bg_ref[0]
    lam = lam_ref[0]
    neg = -lam
    softplus = jnp.maximum(neg, 0.0) + jnp.log1p(jnp.exp(-jnp.abs(neg)))
    half_rate = (-0.5 * LRU_C) * softplus
    row = lax.broadcasted_iota(jnp.int32, (SUBLANES, LRU_BLOCK), 0)
    chains = ((af_s, bf_s, False), (ar_s, br_s, True))

    def coefficients(src_ref, n_rows, row_off):
        upad[pl.ds(0, LRU_PAD), :] = jnp.zeros((LRU_PAD, LRU_BLOCK), F32)
        upad[pl.ds(LRU_PAD, n_rows), :] = src_ref[0, 0].astype(F32)
        upad[pl.ds(LRU_PAD + n_rows, LRU_PAD), :] = jnp.zeros((LRU_PAD, LRU_BLOCK), F32)

        def tile(i, carry):
            t0 = pl.multiple_of(i * LRU_TILE, LRU_TILE)
            xp = upad[pl.ds(t0, LRU_TILE + 2 * LRU_PAD), :]
            conv = cb
            for k in range(LRU_CONV):
                o = LRU_PAD - LRU_CONV_LEFT + k
                conv = conv + cw[k:k + 1, :] * xp[o:o + LRU_TILE, :]
            z = _dot(conv.astype(BF16), wg) + bg
            half_conv = 0.5 * conv
            for d, (a_ref, b_ref, reverse) in enumerate(chains):
                t_a = jnp.tanh(z[:, (2 * d) * LRU_BLOCK:(2 * d + 1) * LRU_BLOCK])
                t_x = jnp.tanh(z[:, (2 * d + 1) * LRU_BLOCK:(2 * d + 2) * LRU_BLOCK])
                rate = half_rate[:, d * LRU_BLOCK:(d + 1) * LRU_BLOCK]
                a = jnp.exp(rate + rate * t_a)
                gated = half_conv + half_conv * t_x
                b = jnp.sqrt(1.0 - a * a) * gated
                a, b = _scan8_rows(a, b, row, reverse)
                a_ref[pl.ds(row_off + t0, LRU_TILE), :] = a
                b_ref[pl.ds(row_off + t0, LRU_TILE), :] = b
            return carry

        tiles = n_rows // LRU_TILE
        lax.fori_loop(0, tiles, tile, 0, unroll=2 if tiles % 2 == 0 else 1)

    coefficients(uc_ref, n_ctx, 0)
    coefficients(ul_ref, n_lat, n_ctx)

    groups_ctx = n_ctx // SUBLANES
    groups = (n_ctx + n_lat) // SUBLANES
    fa = af_s[pl.ds(SUBLANES - 1, groups, stride=SUBLANES), :]
    fb = bf_s[pl.ds(SUBLANES - 1, groups, stride=SUBLANES), :]
    ra = ar_s[pl.ds(0, groups, stride=SUBLANES), :]
    rb = br_s[pl.ds(0, groups, stride=SUBLANES), :]
    zero = jnp.zeros((SUBLANES, LRU_BLOCK), F32)
    carry = zero
    for v in range(groups // SUBLANES):
        sl = slice(v * SUBLANES, (v + 1) * SUBLANES)
        a, b = _scan8(fa[sl], fb[sl], row, False)
        incl = b + a * carry
        cin_s[0, sl, :] = jnp.where(row == 0, carry, pltpu.roll(incl, 1, 0))
        carry = jnp.broadcast_to(incl[SUBLANES - 1:SUBLANES, :], incl.shape)
    carry = zero
    order = list(range(groups_ctx // SUBLANES - 1, -1, -1)) + list(range(groups // SUBLANES - 1,
                                                                        groups_ctx // SUBLANES - 1, -1))
    for v in order:
        sl = slice(v * SUBLANES, (v + 1) * SUBLANES)
        a, b = _scan8(ra[sl], rb[sl], row, True)
        incl = b + a * carry
        cin_s[1, sl, :] = jnp.where(row == SUBLANES - 1, carry, pltpu.roll(incl, SUBLANES - 1, 0))
        carry = jnp.broadcast_to(incl[0:1, :], incl.shape)

    def emit(g_ref, y_ref, n_rows, row_off):
        tile_groups = LRU_TILE // SUBLANES

        def tile(i, carry):
            t0 = pl.multiple_of(i * LRU_TILE, LRU_TILE)
            g0 = pl.multiple_of(row_off // SUBLANES + i * tile_groups, tile_groups)
            rows = pl.ds(row_off + t0, LRU_TILE)
            a_f, b_f, a_r, b_r = af_s[rows, :], bf_s[rows, :], ar_s[rows, :], br_s[rows, :]
            pieces = []
            for g in range(tile_groups):
                sl = slice(g * SUBLANES, (g + 1) * SUBLANES)
                c_f = jnp.broadcast_to(cin_s[0, pl.ds(g0 + g, 1), :], (SUBLANES, LRU_BLOCK))
                c_r = jnp.broadcast_to(cin_s[1, pl.ds(g0 + g, 1), :], (SUBLANES, LRU_BLOCK))
                pieces.append((b_f[sl] + a_f[sl] * c_f) + (b_r[sl] + a_r[sl] * c_r))
            h = jnp.concatenate(pieces, axis=0)
            gate = g_ref[0, 0, pl.ds(t0, LRU_TILE), :].astype(F32)
            y_ref[0, 0, pl.ds(t0, LRU_TILE), :] = (h * _gelu_tanh(gate)).astype(BF16)
            return carry

        lax.fori_loop(0, n_rows // LRU_TILE, tile, 0)

    emit(gc_ref, yc_ref, n_ctx, 0)
    emit(gl_ref, yl_ref, n_lat, n_ctx)


def _rglru(u_lat, g_lat, u_ctx, g_ctx, conv_w, conv_b, w_gate, b_gate, lam, casts=(), mods=()):
    bsz, heads, n_lat, blk = u_lat.shape
    n_ctx = u_ctx.shape[2]
    total = n_lat + n_ctx
    assert n_ctx % (SUBLANES * SUBLANES) == 0 and n_lat % LRU_TILE == 0 and n_ctx % LRU_TILE == 0
    seq = lambda n: pl.BlockSpec((1, 1, n, blk), lambda b, h: (b, h, 0, 0))
    per_head = lambda shape: pl.BlockSpec((1,) + shape, lambda b, h: (h, 0, 0))
    nbytes =4 * total * blk * 4 + (n_lat + 2 * LRU_PAD) * blk * 4 + 2 * 3 * 2 * total * blk * 2
    return _hosted_call(
        functools.partial(_lru_kernel, n_lat=n_lat, n_ctx=n_ctx),
        casts=casts, mods=mods,
        args=(u_lat, g_lat, u_ctx, g_ctx, conv_w, conv_b, w_gate, b_gate, lam),
        grid=(bsz, heads),
        in_specs=[seq(n_lat), seq(n_lat), seq(n_ctx), seq(n_ctx),
                  pl.BlockSpec((LRU_CONV, blk), lambda b, h: (0, h)),
                  pl.BlockSpec((1, blk), lambda b, h: (0, h)),
                  per_head((blk, 4 * blk)), per_head((1, 4 * blk)), per_head((1, 2 * blk))],
        out_specs=[seq(n_lat), seq(n_ctx)],
        out_shape=[jax.ShapeDtypeStruct(u_lat.shape, BF16), jax.ShapeDtypeStruct(u_ctx.shape, BF16)],
        scratch_shapes=[pltpu.VMEM((n_lat + 2 * LRU_PAD, blk), F32)]
                       + [pltpu.VMEM((total, blk), F32)] * 4
                       + [pltpu.VMEM((2, total // SUBLANES, blk), F32)],
        sem=("arbitrary", "arbitrary"), nbytes=nbytes,
        name="rglru",
    )


ATT_TQ = 512
ATT_BAND = 3 * ATT_BLOCK


def _dot_tn(a, b):
    return lax.dot_general(a, b, (((0,), (0,)), ((), ())), preferred_element_type=F32)


def _softmax_probs(scores, sink_row):
    m = sink_row
    for s in scores:
        m = jnp.maximum(m, jnp.max(s, axis=0, keepdims=True))
    probs = [jnp.exp2(s - m).astype(BF16) for s in scores]
    pad = 2 * SUBLANES
    first = lax.broadcasted_iota(jnp.int32, (pad, m.shape[1]), 0) == 0
    p_sink = jnp.where(first, jnp.exp2(sink_row - m), 0.0).astype(BF16)
    probs[-1] = jnp.concatenate([probs[-1], p_sink], axis=0)
    return probs


def _weighted_values(probs, values):
    dh = ATT_HEAD_DIM
    acc = None
    for idx, (p, v) in enumerate(zip(probs, values)):
        v_ext = jnp.concatenate([v, jnp.ones(v.shape, BF16)], axis=1)
        if idx == len(values) - 1:
            pad = p.shape[0] - v.shape[0]
            v_sink = jnp.concatenate([jnp.zeros((pad, dh), BF16), jnp.ones((pad, dh), BF16)], axis=1)
            v_ext = jnp.concatenate([v_ext, v_sink], axis=0)
        pv = _dot_tn(p, v_ext)
        acc = pv if acc is None else acc + pv
    return acc[:, :dh] / acc[:, dh:]


def _sink_rows(sink, reps):
    return jnp.repeat(sink.reshape(ATT_KV_HEADS, 1, ATT_GROUP), reps, axis=2)


def _attn_kernel(q_ref, k_ref, v_ref, kc_ref, vc_ref, sink_ref, o_ref, *, seq):
    tile = pl.program_id(1)
    blocks = ATT_TQ // ATT_BLOCK
    qcol = lax.broadcasted_iota(jnp.int32, (1, ATT_GROUP * ATT_BLOCK), 1) % ATT_BLOCK
    krow = lax.broadcasted_iota(jnp.int32, (ATT_BAND, 1), 0)

    def scores(u):
        i, g = divmod(u, ATT_KV_HEADS)
        q0 = (tile * blocks + i) * ATT_BLOCK
        start = pl.multiple_of(jnp.clip(q0 - ATT_BLOCK, 0, seq - ATT_BAND), ATT_BLOCK)
        valid = jnp.abs((q0 + qcol) - (start + krow)) <= WINDOW
        heads = [q_ref[0, i * ATT_BLOCK:(i + 1) * ATT_BLOCK,
                       (g * ATT_GROUP + r) * ATT_HEAD_DIM:(g * ATT_GROUP + r + 1) * ATT_HEAD_DIM]
                 for r in range(ATT_GROUP)]
        qs = jnp.concatenate(heads, axis=0)
        kv = slice(g * ATT_HEAD_DIM, (g + 1) * ATT_HEAD_DIM)
        s_loc = jnp.where(valid, _dot_nt(k_ref[0, pl.ds(start, ATT_BAND), kv], qs), NEG_INF)
        return start, [s_loc, _dot_nt(kc_ref[0, :, kv], qs)]

    def emit(u, start, p):
        i, g = divmod(u, ATT_KV_HEADS)
        kv = slice(g * ATT_HEAD_DIM, (g + 1) * ATT_HEAD_DIM)
        o = _weighted_values(p, [v_ref[0, pl.ds(start, ATT_BAND), kv], vc_ref[0, :, kv]])
        for r in range(ATT_GROUP):
            col = (g * ATT_GROUP + r) * ATT_HEAD_DIM
            o_ref[0, i * ATT_BLOCK:(i + 1) * ATT_BLOCK, col:col + ATT_HEAD_DIM] = (
                o[r * ATT_BLOCK:(r + 1) * ATT_BLOCK].astype(BF16))

    n_units = blocks * ATT_KV_HEADS
    sc, pr = {}, {}
    for step in range(n_units + 2):
        if step < n_units:
            sc[step] = scores(step)
        if 0 <= step - 1 < n_units:
            start, s_list = sc.pop(step - 1)
            pr[step - 1] = (start, _softmax_probs(s_list, sink_ref[(step - 1) % ATT_KV_HEADS] * LOG2E))
        if 0 <= step - 2 < n_units:
            start, p = pr.pop(step - 2)
            emit(step - 2, start, p)


def _window_attention(q, k, v, kc, vc, sink, casts=()):
    bsz, n, _ = q.shape
    sink_rows = _sink_rows(sink, ATT_BLOCK)
    lc = kc.shape[1]
    whole = lambda rows: pl.BlockSpec((1, rows, ATT_KV_WIDTH), lambda b, i: (b, 0, 0))
    nbytes = 2 * 2 * (2 * ATT_TQ * ATT_Q_WIDTH + 2 * n * ATT_KV_WIDTH + 2 * lc * ATT_KV_WIDTH) + 8 * 1024 * 1024
    return _hosted_call(
        functools.partial(_attn_kernel, seq=n),
        casts=casts,
        args=(q, k, v, kc, vc, sink_rows),
        grid=(bsz, n // ATT_TQ),
        in_specs=[pl.BlockSpec((1, ATT_TQ, ATT_Q_WIDTH), lambda b, i: (b, i, 0)),
                  whole(n), whole(n), whole(lc), whole(lc),
                  pl.BlockSpec(sink_rows.shape, lambda b, i: (0, 0, 0))],
        out_specs=[pl.BlockSpec((1, ATT_TQ, ATT_Q_WIDTH), lambda b, i: (b, i, 0))],
        out_shape=[jax.ShapeDtypeStruct(q.shape, BF16)],
        sem=("arbitrary", "arbitrary"), nbytes=nbytes,
        name="window_attention",
    )


def _ctx_attn_kernel(q_ref, kc_ref, vc_ref, sink_ref, o_ref):
    lc = q_ref.shape[1]
    for g in range(ATT_KV_HEADS):
        heads = [q_ref[0, :, (g * ATT_GROUP + r) * ATT_HEAD_DIM:(g * ATT_GROUP + r + 1) * ATT_HEAD_DIM]
                 for r in range(ATT_GROUP)]
        qs = jnp.concatenate(heads, axis=0)
        kv = slice(g * ATT_HEAD_DIM, (g + 1) * ATT_HEAD_DIM)
        o = _weighted_values(_softmax_probs([_dot_nt(kc_ref[0, :, kv], qs)], sink_ref[g] * LOG2E),
                             [vc_ref[0, :, kv]])
        for r in range(ATT_GROUP):
            col = (g * ATT_GROUP + r) * ATT_HEAD_DIM
            o_ref[0, :, col:col + ATT_HEAD_DIM] = o[r * lc:(r + 1) * lc].astype(BF16)


def _context_attention(qc, kc, vc, sink):
    bsz, lc, _ = qc.shape
    sink_rows = _sink_rows(sink, lc)
    kv_spec = pl.BlockSpec((1, lc, ATT_KV_WIDTH), lambda b: (b, 0, 0))
    q_spec = pl.BlockSpec((1, lc, ATT_Q_WIDTH), lambda b: (b, 0, 0))
    return pl.pallas_call(
        _ctx_attn_kernel,
        grid=(bsz,),
        in_specs=[q_spec, kv_spec, kv_spec, pl.BlockSpec(sink_rows.shape, lambda b: (0, 0, 0))],
        out_specs=q_spec,
        out_shape=jax.ShapeDtypeStruct(qc.shape, BF16),
        compiler_params=_params(("arbitrary",), 8 * 1024 * 1024),
        name="context_attention",
    )(qc, kc, vc, sink_rows)


OUT_PARTS = 4


def _outproj0_kernel(ya_ref, yb_ref, x_ref, gate_ref, w_ref, lg_ref, lb_ref, o_ref, *, ctx_row):
    gate = _mod_row(gate_ref, ctx_row)
    parts = OUT_PARTS
    rows_per = x_ref.shape[1] // parts
    outs = {}
    for r in range(parts + 1):
        if r < parts:
            rows = slice(r * rows_per, (r + 1) * rows_per)
            y = jnp.concatenate([ya_ref[0, h, rows, :] for h in range(LRU_HEADS)] + [yb_ref[0, rows, :]], axis=1)
            outs[r] = _dot(y, w_ref[...])
        if r >= 1:
            rows = slice((r - 1) * rows_per, r * rows_per)
            o_ref[0, rows, :] = _layer_norm(ALPHA * x_ref[0, rows, :] + gate * outs.pop(r - 1), lg_ref[...], lb_ref[...])


def _outproj0(ya, yb, x, mod, ctx_row, w_out, ln_g, ln_b, tm):
    bsz, n, d = x.shape
    tok = lambda b, i: (b, i, 0)
    const = lambda b, i: (0, 0)
    nbytes = 2 * 2 * tm * d * 4 + w_out.size * 2 + 2 * 2 * tm * d * 2 + 2 * tm * d * 4
    return pl.pallas_call(
        functools.partial(_outproj0_kernel, ctx_row=ctx_row),
        grid=(bsz, n // tm),
        in_specs=[pl.BlockSpec((1, LRU_HEADS, tm, LRU_BLOCK), lambda b, i: (b, 0, i, 0)),
                  pl.BlockSpec((1, tm, ATT_Q_WIDTH), tok),
                  pl.BlockSpec((1, tm, d), tok),
                  _mod_spec(mod, MOD_G1),
                  _resident(w_out.shape, const),
                  pl.BlockSpec((1, d), const), pl.BlockSpec((1, d), const)],
        out_specs=pl.BlockSpec((1, tm, d), tok),
        out_shape=jax.ShapeDtypeStruct(x.shape, F32),
        compiler_params=_params(("arbitrary", "arbitrary"), nbytes),
        name="outproj0_ln",
    )(ya, yb, x, mod, w_out, ln_g, ln_b)


def _ffn_kernel(x_ref, sc_ref, sh_ref, gate_ref, w1_ref, b1_ref, w2_ref, b2_ref, lg_ref, lb_ref,
                o_ref, h_s, *, ctx_row):
    j = pl.program_id(2)

    @pl.when(j == 0)
    def _():
        h_s[...] = _modulate(x_ref[0], sc_ref, sh_ref, ctx_row)
        o_ref[0] = jnp.zeros(o_ref.shape[1:], F32)

    a = jnp.maximum(_dot(h_s[...], w1_ref[...]) + b1_ref[...], 0.0)
    o_ref[0] += _dot((a * a).astype(BF16), w2_ref[...])

    @pl.when(j == pl.num_programs(2) - 1)
    def _():
        f = o_ref[0] + b2_ref[...]
        o_ref[0] = _layer_norm(ALPHA * x_ref[0] + _mod_row(gate_ref, ctx_row) * f, lg_ref[...], lb_ref[...])


def _ffn(x, mod, ctx_row, w1, b1, w2, b2, ln_g, ln_b, tm, th, casts=()):
    bsz, n, d = x.shape
    hidden = w1.shape[1]
    tok = lambda b, i, j: (b, i, 0)
    const = lambda b, i, j: (0, 0)
    nbytes = 2 * 2 * tm * d * 4 + 2 * 2 * 2 * d * th * 2 + tm * d * 2 + 2 * tm * th * 4 + tm * d * 4
    return _hosted_call(
        functools.partial(_ffn_kernel, ctx_row=ctx_row),
        casts=casts,
        args=(x, mod, mod, mod, w1, b1, w2, b2, ln_g, ln_b),
        grid=(bsz, n // tm, hidden // th),
        in_specs=[pl.BlockSpec((1, tm, d), tok),
                  _mod_spec(mod, MOD_SC2), _mod_spec(mod, MOD_SH2), _mod_spec(mod, MOD_G2),
                  pl.BlockSpec((d, th), lambda b, i, j: (0, j)),
                  pl.BlockSpec((1, th), lambda b, i, j: (0, j)),
                  pl.BlockSpec((th, d), lambda b, i, j: (j, 0)),
                  pl.BlockSpec((1, d), const), pl.BlockSpec((1, d), const), pl.BlockSpec((1, d), const)],
        out_specs=[pl.BlockSpec((1, tm, d), tok)],
        out_shape=[jax.ShapeDtypeStruct(x.shape, F32)],
        scratch_shapes=[pltpu.VMEM((tm, d), BF16)],
        sem=("arbitrary", "arbitrary", "arbitrary"), nbytes=nbytes,
        name="ffn_ln",
    )


SC_PARTS = SC_WIDTH // S5_WIDTH


def _inproj1_kernel(*refs, full, ctx_row):
    x_ref, sc_ref, sh_ref, wu_ref = refs[:4]
    u_s = refs[-1]
    h = _modulate(x_ref[0], sc_ref, sh_ref, ctx_row)
    u = _dot(h, wu_ref[...])
    if full:
        wb_refs = refs[4:4 + SC_PARTS]
        wc_refs = refs[4 + SC_PARTS:4 + 2 * SC_PARTS]
        wx_refs = refs[4 + 2 * SC_PARTS:4 + 3 * SC_PARTS]
        u_ref, uflat_ref, gb_ref, p_ref = refs[4 + 3 * SC_PARTS:-1]
        u_ref[0] = u.astype(BF16)
        for c in range(SC_PARTS):
            cols = slice(c * S5_WIDTH, (c + 1) * S5_WIDTH)
            gb_ref[0, :, cols] = _dot(h, wb_refs[c][...]).astype(BF16)
            p_ref[0, :, cols] = (_dot(h, wc_refs[c][...]) * _dot(h, wx_refs[c][...])).astype(BF16)
    else:
        uflat_ref, = refs[4:-1]
    for k in range(S5_SLABS):
        u_s[k] = u[:, k * LANES:(k + 1) * LANES]
    chunks = u.shape[0] // S5_CHUNK
    for t in range(S5_CHUNK):
        for k in range(S5_SLABS):
            col = t * S5_WIDTH + k * LANES
            uflat_ref[0, :, col:col + LANES] = u_s[k, pl.ds(t, chunks, stride=S5_CHUNK), :].astype(BF16)


def _inproj1(x, mod, ctx_row, w_in, tm, full):
    bsz, n, d = x.shape
    tok = lambda b, i: (b, i, 0)
    n_blocks = 1 + 3 * SC_PARTS if full else 1
    flat_spec = pl.BlockSpec((1, tm // S5_CHUNK, S5_CHUNK * S5_WIDTH), tok)
    flat_shape = jax.ShapeDtypeStruct((bsz, n // S5_CHUNK, S5_CHUNK * S5_WIDTH), BF16)
    token = lambda w: (pl.BlockSpec((1, tm, w), tok), jax.ShapeDtypeStruct((bsz, n, w), BF16))
    outs = ([token(S5_WIDTH), (flat_spec, flat_shape), token(SC_WIDTH), token(SC_WIDTH)] if full
            else [(flat_spec, flat_shape)])
    ncols = n_blocks * S5_WIDTH
    nbytes = 2 * tm * d * 4 + 2 * d * ncols + 2 * 2 * tm * (ncols + S5_WIDTH) + tm * ncols * 4 + tm * S5_WIDTH * 4
    return pl.pallas_call(
        functools.partial(_inproj1_kernel, full=full, ctx_row=ctx_row),
        grid=(bsz, n // tm),
        in_specs=[pl.BlockSpec((1, tm, d), tok), _mod_spec(mod, MOD_SC1), _mod_spec(mod, MOD_SH1)]
                 + [_resident((d, S5_WIDTH), lambda b, i, c=c: (0, c)) for c in range(n_blocks)],
        out_specs=[spec for spec, _ in outs],
        out_shape=[shape for _, shape in outs],
        scratch_shapes=[pltpu.VMEM((S5_SLABS, tm, LANES), F32)],
        compiler_params=_params(("arbitrary", "arbitrary"), nbytes),
        name="inproj1" if full else "inproj1_ctx",
    )(x, mod, mod, *([w_in] * n_blocks))


def _s5_power_table(par_ref):
    rows = 2 * S5_POW_ROWS
    r = lax.broadcasted_iota(jnp.int32, (rows, S5_SLAB_STATE), 0)
    first = r < S5_POW_ROWS
    lag = (r % S5_POW_ROWS).astype(F32)
    pick = lambda i: jnp.where(first, par_ref[0, 0, i:i + 1, :], par_ref[0, 1, i:i + 1, :])
    dt = jnp.exp(pick(0))
    mag = jnp.exp(lag * dt * pick(1))
    ang = lag * dt * pick(2)
    return mag * jnp.cos(ang), mag * jnp.sin(ang)


def _s5_input_matrix(par_ref, bt_ref, d, pw_re, pw_im):
    a_re = par_ref[0, d, 1:2, :]
    a_im = par_ref[0, d, 2:3, :]
    ab_re = pw_re[d * S5_POW_ROWS + 1:d * S5_POW_ROWS + 2, :]
    ab_im = pw_im[d * S5_POW_ROWS + 1:d * S5_POW_ROWS + 2, :]
    den = a_re * a_re + a_im * a_im
    k_re = ((ab_re - 1.0) * a_re + ab_im * a_im) / den
    k_im = (ab_im * a_re - (ab_re - 1.0) * a_im) / den
    b_re = bt_ref[0, d, 0]
    b_im = bt_ref[0, d, 1]
    return k_re * b_re - k_im * b_im, k_re * b_im + k_im * b_re


def _cmul_row(x_re, x_im, p_re, p_im):
    return x_re * p_re - x_im * p_im, x_re * p_im + x_im * p_re


def _s5_state_prep_kernel(par_ref, bt_ref, wst_ref, a16_ref):
    pw_re, pw_im = _s5_power_table(par_ref)
    for d in range(2):
        bb_re, bb_im = _s5_input_matrix(par_ref, bt_ref, d, pw_re, pw_im)
        base = d * S5_POW_ROWS
        for lag in range(S5_CHUNK):
            e_re, e_im = _cmul_row(bb_re, bb_im, pw_re[base + lag:base + lag + 1, :],
                                   pw_im[base + lag:base + lag + 1, :])
            s = S5_CHUNK - 1 - lag if d == 0 else lag
            col = 2 * d * S5_SLAB_STATE
            wst_ref[0, s * LANES:(s + 1) * LANES, col:col + S5_SLAB_STATE] = e_re.astype(BF16)
            wst_ref[0, s * LANES:(s + 1) * LANES, col + S5_SLAB_STATE:col + 2 * S5_SLAB_STATE] = e_im.astype(BF16)
        row16 = base + S5_CHUNK
        a16_ref[0, :, 2 * d * S5_SLAB_STATE:(2 * d + 1) * S5_SLAB_STATE] = jnp.broadcast_to(
            pw_re[row16:row16 + 1, :], (SUBLANES, S5_SLAB_STATE))
        a16_ref[0, :, (2 * d + 1) * S5_SLAB_STATE:(2 * d + 2) * S5_SLAB_STATE] = jnp.broadcast_to(
            pw_im[row16:row16 + 1, :], (SUBLANES, S5_SLAB_STATE))


def _s5_output_prep_kernel(par_ref, bt_ref, c_ref, ct_ref, wbig_ref):
    pw_re, pw_im = _s5_power_table(par_ref)
    kern = []
    for d in range(2):
        bb_re, bb_im = _s5_input_matrix(par_ref, bt_ref, d, pw_re, pw_im)
        c_re = c_ref[0, d, 0].astype(BF16)
        c_im = c_ref[0, d, 1].astype(BF16)
        base = d * S5_POW_ROWS
        per_lag = []
        for lag in range(S5_CHUNK):
            e_re, e_im = _cmul_row(bb_re, bb_im, pw_re[base + lag:base + lag + 1, :],
                                   pw_im[base + lag:base + lag + 1, :])
            per_lag.append(_dot_nt(e_re.astype(BF16), c_re) - _dot_nt(e_im.astype(BF16), c_im))
        kern.append(per_lag)
    for s in range(S5_CHUNK):
        for t in range(S5_CHUNK):
            if s < t:
                blk = kern[0][t - s]
            elif s > t:
                blk = kern[1][s - t]
            else:
                blk = kern[0][0] + kern[1][0]
            wbig_ref[0, s * LANES:(s + 1) * LANES, t * LANES:(t + 1) * LANES] = blk.astype(BF16)
    pt_re = pw_re.T
    pt_im = pw_im.T
    for d in range(2):
        ct_re = ct_ref[0, d, 0]
        ct_im = ct_ref[0, d, 1]
        for t in range(S5_CHUNK):
            lag = t + 1 if d == 0 else S5_CHUNK - t
            col = d * S5_POW_ROWS + lag
            p_re = pt_re[:, col:col + 1]
            p_im = pt_im[:, col:col + 1]
            g_re = ct_re * p_re - ct_im * p_im
            g_im = ct_re * p_im + ct_im * p_re
            r0 = S5_XK + 2 * d * S5_SLAB_STATE
            wbig_ref[0, r0:r0 + S5_SLAB_STATE, t * LANES:(t + 1) * LANES] = g_re.astype(BF16)
            wbig_ref[0, r0 + S5_SLAB_STATE:r0 + 2 * S5_SLAB_STATE, t * LANES:(t + 1) * LANES] = (-g_im).astype(BF16)


def _s5_operators(log_dt, a_re, a_im, b_re, b_im, c_re, c_im):
    gs = S5_SLAB_GROUPS
    eye = jnp.eye(gs, dtype=F32)

    def lanes(v):
        return v.reshape(2, S5_SLABS, gs * S5_STATE)

    par = jnp.stack([lanes(jnp.broadcast_to(log_dt[:, :, None], a_re.shape)), lanes(a_re), lanes(a_im)], axis=2)
    par = jnp.pad(par, ((0, 0), (0, 0), (0, SUBLANES - 3), (0, 0))).transpose(1, 0, 2, 3)

    def embed_bt(b):
        b = b.reshape(2, S5_SLABS, gs, S5_STATE, S5_GROUP)
        e = b.transpose(0, 1, 2, 4, 3)[:, :, :, :, None, :] * eye[None, None, :, None, :, None]
        return e.reshape(2, S5_SLABS, gs * S5_GROUP, gs * S5_STATE).transpose(1, 0, 2, 3)

    def embed_c(c):
        c = c.reshape(2, S5_SLABS, gs, S5_GROUP, S5_STATE)
        e = c[:, :, :, :, None, :] * eye[None, None, :, None, :, None]
        return e.reshape(2, S5_SLABS, gs * S5_GROUP, gs * S5_STATE).transpose(1, 0, 2, 3)

    bt = jnp.stack([embed_bt(b_re), embed_bt(b_im)], axis=2)
    cm = jnp.stack([embed_c(c_re), embed_c(c_im)], axis=2)
    ct = cm.transpose(0, 1, 2, 4, 3)

    slab5 = lambda shape: pl.BlockSpec((1,) + shape, lambda k: (k, 0, 0, 0, 0))
    par_spec = pl.BlockSpec((1, 2, SUBLANES, S5_SLAB_STATE), lambda k: (k, 0, 0, 0))
    bt_spec = slab5((2, 2, LANES, S5_SLAB_STATE))
    wst, a16 = pl.pallas_call(
        _s5_state_prep_kernel,
        grid=(S5_SLABS,),
        in_specs=[par_spec, bt_spec],
        out_specs=[pl.BlockSpec((1, S5_XK, S5_HK), lambda k: (k, 0, 0)),
                   pl.BlockSpec((1, SUBLANES, S5_HK), lambda k: (k, 0, 0))],
        out_shape=[jax.ShapeDtypeStruct((S5_SLABS, S5_XK, S5_HK), BF16),
                   jax.ShapeDtypeStruct((S5_SLABS, SUBLANES, S5_HK), F32)],
        compiler_params=_params(("arbitrary",), 2 * S5_XK * S5_HK * 2 + 8 * 1024 * 1024),
        name="s5_state_operator",
    )(par, bt)
    wbig = pl.pallas_call(
        _s5_output_prep_kernel,
        grid=(S5_SLABS,),
        in_specs=[par_spec, bt_spec, bt_spec, slab5((2, 2, S5_SLAB_STATE, LANES))],
        out_specs=pl.BlockSpec((1, S5_XK + S5_HK, S5_XK), lambda k: (k, 0, 0)),
        out_shape=jax.ShapeDtypeStruct((S5_SLABS, S5_XK + S5_HK, S5_XK), BF16),
        compiler_params=_params(("arbitrary",), 2 * (S5_XK + S5_HK) * S5_XK * 2 + 8 * 1024 * 1024),
        name="s5_output_operator",
    )(par, bt, cm, ct)
    return wst, a16, wbig


def _s5_states_kernel(*refs, n_ctx, n_lat):
    xc_refs = refs[:S5_CHUNK]
    xl_refs = refs[S5_CHUNK:2 * S5_CHUNK]
    wst_ref, a16_ref, h_ref, s_s, h_s = refs[2 * S5_CHUNK:]
    x = jnp.concatenate([jnp.concatenate([r[0] for r in xc_refs], axis=1),
                         jnp.concatenate([r[0] for r in xl_refs], axis=1)], axis=0)
    s_s[...] = _dot(x, wst_ref[0])
    p = S5_SLAB_STATE
    af_re = a16_ref[0, 0:1, 0:p]
    af_im = a16_ref[0, 0:1, p:2 * p]
    ar_re = a16_ref[0, 0:1, 2 * p:3 * p]
    ar_im = a16_ref[0, 0:1, 3 * p:4 * p]
    total = n_ctx + n_lat

    def step(i, carry):
        f_re, f_im, r_re, r_im = carry
        h_s[pl.ds(i, 1), 0:p] = f_re
        h_s[pl.ds(i, 1), p:2 * p] = f_im
        s_re = s_s[pl.ds(i, 1), 0:p]
        s_im = s_s[pl.ds(i, 1), p:2 * p]
        f_re, f_im = af_re * f_re - af_im * f_im + s_re, af_re * f_im + af_im * f_re + s_im
        j = jnp.where(i < n_ctx, n_ctx - 1 - i, total + n_ctx - 1 - i)
        h_s[pl.ds(j, 1), 2 * p:3 * p] = r_re
        h_s[pl.ds(j, 1), 3 * p:4 * p] = r_im
        s_re = s_s[pl.ds(j, 1), 2 * p:3 * p]
        s_im = s_s[pl.ds(j, 1), 3 * p:4 * p]
        r_re, r_im = ar_re * r_re - ar_im * r_im + s_re, ar_re * r_im + ar_im * r_re + s_im
        return f_re, f_im, r_re, r_im

    zero = jnp.zeros((1, p), F32)
    lax.fori_loop(0, total, step, (zero, zero, zero, zero), unroll=4)
    h_ref[0, 0] = h_s[pl.ds(n_ctx, n_lat), :].astype(BF16)


def _s5_states(uc_flat, ul_flat, wst, a16):
    bsz, n_ctx, _ = uc_flat.shape
    n_lat = ul_flat.shape[1]
    piece = lambda rows, s: pl.BlockSpec((1, rows, LANES), lambda k, b, s=s: (b, 0, s * S5_SLABS + k))
    in_specs = ([piece(n_ctx, s) for s in range(S5_CHUNK)] + [piece(n_lat, s) for s in range(S5_CHUNK)]
                + [pl.BlockSpec((1, S5_XK, S5_HK), lambda k, b: (k, 0, 0)),
                   pl.BlockSpec((1, SUBLANES, S5_HK), lambda k, b: (k, 0, 0))])
    total = n_ctx + n_lat
    nbytes = 2 * S5_XK * S5_HK * 2 + 2 * total * S5_XK * 2 + 3 * total * S5_HK * 4 + 2 * n_lat * S5_HK * 2
    return pl.pallas_call(
        functools.partial(_s5_states_kernel, n_ctx=n_ctx, n_lat=n_lat),
        grid=(S5_SLABS, bsz),
        in_specs=in_specs,
        out_specs=pl.BlockSpec((1, 1, n_lat, S5_HK), lambda k, b: (b, k, 0, 0)),
        out_shape=jax.ShapeDtypeStruct((bsz, S5_SLABS, n_lat, S5_HK), BF16),
        scratch_shapes=[pltpu.VMEM((total, S5_HK), F32), pltpu.VMEM((total, S5_HK), F32)],
        compiler_params=_params(("arbitrary", "arbitrary"), nbytes),
        name="s5_states",
    )(*([uc_flat] * S5_CHUNK + [ul_flat] * S5_CHUNK + [wst, a16]))


def _s5_readout_kernel(*refs):
    x_refs = refs[:S5_CHUNK]
    h_ref, w_ref, y_ref = refs[S5_CHUNK:]
    lhs = jnp.concatenate([r[0] for r in x_refs] + [h_ref[0, 0]], axis=1)
    y = _dot(lhs, w_ref[0])
    chunks = y.shape[0]
    for t in range(S5_CHUNK):
        y_ref[0, pl.ds(t, chunks, stride=S5_CHUNK), :] = y[:, t * LANES:(t + 1) * LANES]


def _s5_readout(ul_flat, h_in, wbig):
    bsz, n_lat, _ = ul_flat.shape
    piece = lambda s: pl.BlockSpec((1, n_lat, LANES), lambda k, b, s=s: (b, 0, s * S5_SLABS + k))
    nbytes = 2 * (S5_XK + S5_HK) * S5_XK * 2 + 2 * n_lat * (S5_XK + S5_HK) * 2 * 2 + n_lat * S5_XK * 4 * 3
    return pl.pallas_call(
        _s5_readout_kernel,
        grid=(S5_SLABS, bsz),
        in_specs=[piece(s) for s in range(S5_CHUNK)]
                 + [pl.BlockSpec((1, 1, n_lat, S5_HK), lambda k, b: (b, k, 0, 0)),
                    pl.BlockSpec((1, S5_XK + S5_HK, S5_XK), lambda k, b: (k, 0, 0))],
        out_specs=pl.BlockSpec((1, n_lat * S5_CHUNK, LANES), lambda k, b: (b, 0, k)),
        out_shape=jax.ShapeDtypeStruct((bsz, n_lat * S5_CHUNK, S5_WIDTH), F32),
        compiler_params=_params(("arbitrary", "arbitrary"), nbytes),
        name="s5_readout",
    )(*([ul_flat] * S5_CHUNK + [h_in, wbig]))


SC_HALO = 16


def _outproj1_kernel(y_ref, u_ref, gb_ref, p_ref, pprev_ref, pnext_ref, x_ref, gate_ref,
                     dskip_ref, wglu_ref, bglu_ref, cw_ref, cb_ref, w_ref, lg_ref, lb_ref, o_ref, *, ctx_row):
    i = pl.program_id(1)
    tm = x_ref.shape[1]
    yc = y_ref[0] + dskip_ref[...] * u_ref[0].astype(F32)
    z = _gelu_tanh(yc)
    y_c = z * _sigmoid(_dot(z.astype(BF16), wglu_ref[...]) + bglu_ref[...])
    p = p_ref[0].astype(F32)
    row = lax.broadcasted_iota(jnp.int32, (tm, 1), 0)
    prev_row = jnp.where(i > 0, pprev_ref[0, SC_HALO - 1:SC_HALO, :].astype(F32), 0.0)
    next_row = jnp.where(i < pl.num_programs(1) - 1, pnext_ref[0, 0:1, :].astype(F32), 0.0)
    p_dn = jnp.where(row == 0, prev_row, pltpu.roll(p, 1, 0))
    p_up = jnp.where(row == tm - 1, next_row, pltpu.roll(p, tm - 1, 0))
    conv = cb_ref[...] + cw_ref[0:1, :] * p_dn + cw_ref[1:2, :] * p + cw_ref[2:3, :] * p_up
    y_d = gb_ref[0].astype(F32) * conv
    y = jnp.concatenate([y_c.astype(BF16), y_d.astype(BF16)], axis=1)
    out = _dot(y, w_ref[...])
    o_ref[0] = _layer_norm(ALPHA * x_ref[0] + _mod_row(gate_ref, ctx_row) * out, lg_ref[...], lb_ref[...])


def _outproj1(y, u, gb, p, x, mod, ctx_row, d_skip, w_glu, b_glu, conv_w, conv_b, w_out, ln_g, ln_b, tm):
    bsz, n, d = x.shape
    tok = lambda b, i: (b, i, 0)
    const = lambda b, i: (0, 0)
    per = tm // SC_HALO
    last = n // SC_HALO - 1
    nbytes = (2 * 2 * tm * d * 4 + w_out.size * 2 + 2 * 2 * tm * (2 * S5_WIDTH + 2 * SC_WIDTH) * 2
              + 6 * tm * SC_WIDTH * 4 + 2 * tm * d * 4)
    return pl.pallas_call(
        functools.partial(_outproj1_kernel, ctx_row=ctx_row),
        grid=(bsz, n // tm),
        in_specs=[pl.BlockSpec((1, tm, S5_WIDTH), tok), pl.BlockSpec((1, tm, S5_WIDTH), tok),
                  pl.BlockSpec((1, tm, SC_WIDTH), tok), pl.BlockSpec((1, tm, SC_WIDTH), tok),
                  pl.BlockSpec((1, SC_HALO, SC_WIDTH), lambda b, i: (b, jnp.maximum(i * per - 1, 0), 0)),
                  pl.BlockSpec((1, SC_HALO, SC_WIDTH), lambda b, i: (b, jnp.minimum((i + 1) * per, last), 0)),
                  pl.BlockSpec((1, tm, d), tok), _mod_spec(mod, MOD_G1),
                  pl.BlockSpec((1, S5_WIDTH), const), _resident(w_glu.shape, const),
                  pl.BlockSpec((1, S5_WIDTH), const),
                  pl.BlockSpec((SC_CONV, SC_WIDTH), const), pl.BlockSpec((1, SC_WIDTH), const),
                  _resident(w_out.shape, const),
                  pl.BlockSpec((1, d), const), pl.BlockSpec((1, d), const)],
        out_specs=pl.BlockSpec((1, tm, d), tok),
        out_shape=jax.ShapeDtypeStruct(x.shape, F32),
        compiler_params=_params(("arbitrary", "arbitrary"), nbytes),
        name="outproj1_ln",
    )(y, u, gb, p, p, p, x, mod, d_skip, w_glu, b_glu, conv_w, conv_b, w_out, ln_g, ln_b)


def _rope_tables(n):
    t = np.arange(n)
    inv = ROPE_BASE ** (-np.arange(ROPE_FREQS, dtype=np.float64) / ROPE_FREQS)
    ang_r = (t // GRID_W)[:, None] * inv[None, :]
    ang_c = (t % GRID_W)[:, None] * inv[None, :]
    cos = np.concatenate([np.cos(ang_r), np.cos(ang_r), np.cos(ang_c), np.cos(ang_c)], axis=1)
    sin = np.concatenate([-np.sin(ang_r), np.sin(ang_r), -np.sin(ang_c), np.sin(ang_c)], axis=1)
    return jnp.asarray(cos, F32), jnp.asarray(sin, F32)


TM_LATENT = 512
TM_PROJ = 1024
TM_CONTEXT = 256
FFN_TH = 2048


def kernel(x, c, ctx, c_ctx, mod_w, mod_b, ln1_g, ln1_b, ln2_g, ln2_b, ffn_w1, ffn_b1, ffn_w2, ffn_b2,
           ab_w_in, ab_w_out, lru_conv_w, lru_conv_b, lru_w_a, lru_b_a, lru_w_x, lru_b_x, lru_lam, att_sink,
           cd_w_in, cd_w_out, s5_log_dt, s5_a_re, s5_a_im, s5_b_re, s5_b_im, s5_c_re, s5_c_im, s5_d,
           s5_w_glu, s5_b_glu, sc_conv_w, sc_conv_b):
    bsz, n, d = x.shape
    lc = ctx.shape[1]
    assert n % TM_PROJ == 0 and n % TM_LATENT == 0 and lc % TM_CONTEXT == 0 and (bsz * lc) % TM_LATENT == 0
    assert n % GRID_W == 0

    pad_rows = SUBLANES - (bsz + 1) % SUBLANES if (bsz + 1) % SUBLANES else 0
    cc = jnp.concatenate([c, c_ctx[None, :], jnp.zeros((pad_rows, d), F32)], axis=0)
    mods = {0: _modulation(cc, mod_w, mod_b, 0)}
    ctx_row = bsz

    row2 = lambda v: v.reshape(1, -1)
    rope_tabs = _rope_tables(n)

    side = {}

    def weight(stack, name, idx):
        return side.pop((name, idx)) if (name, idx) in side else _to_bf16(stack, idx)

    def hosted(call, own, jobs):
        outs = call(casts=[(stack, idx) for _, idx, stack in jobs])
        side.update({(name, idx): w for (name, idx, _), w in zip(jobs, outs[own:own + len(jobs)])})
        return outs[:own] + outs[own + len(jobs):]

    for i in range(DEPTH):
        last = i == DEPTH - 1
        j = i // 2
        mod = mods.pop(i) if i in mods else _modulation(cc, mod_w, mod_b, i)
        todo = [("w1", i, ffn_w1), ("w2", i, ffn_w2)]
        if not last:
            nj = (i + 1) // 2
            todo += [("w1", i + 1, ffn_w1), ("w2", i + 1, ffn_w2)]
            todo += ([("ab_in", nj, ab_w_in), ("ab_out", nj, ab_w_out)] if (i + 1) % 2 == 0
                     else [("cd_in", nj, cd_w_in), ("cd_out", nj, cd_w_out)])
        todo = [job for job in todo if (job[0], job[1]) not in side]
        if i % 2 == 0:
            w_in = weight(ab_w_in, "ab_in", j)
            w_out = weight(ab_w_out, "ab_out", j)
            u, gate, q, k, v = _inproj0(x, mod, None, w_in, rope_tabs, TM_PROJ)
            uc, gatec, qc, kc, vc = _inproj0(ctx, mod, ctx_row, w_in, None, TM_CONTEXT)
            w_gate = (0.5 * jnp.concatenate([lru_w_a[j, 0], lru_w_x[j, 0], lru_w_a[j, 1], lru_w_x[j, 1]],
                                            axis=-1)).astype(BF16)
            hb = lambda b: b.reshape(LRU_HEADS, 1, LRU_BLOCK)
            b_gate = 0.5 * jnp.concatenate([hb(lru_b_a[j, 0]), hb(lru_b_x[j, 0]), hb(lru_b_a[j, 1]),
                                            hb(lru_b_x[j, 1])], axis=-1)
            lam = jnp.concatenate([hb(lru_lam[j, 0]), hb(lru_lam[j, 1])], axis=-1)
            ahead_mod = [] if last else [(cc, mod_w, mod_b, i + 1)]
            *lru_out, = hosted(functools.partial(_rglru, u, gate, uc, gatec, lru_conv_w[j], row2(lru_conv_b[j]),
                                                 w_gate, b_gate, lam, mods=ahead_mod), 2, todo[0::2])
            ya, yac = lru_out[:2]
            mods.update({i + 1: m for m in lru_out[2:]})
            yb, = hosted(functools.partial(_window_attention, q, k, v, kc, vc, att_sink[j]), 1, todo[1::2])
            todo = []
            x = _outproj0(ya, yb, x, mod, None, w_out, row2(ln1_g[i]), row2(ln1_b[i]), TM_PROJ)
            if not last:
                ybc = _context_attention(qc, kc, vc, att_sink[j])
                ctx = _outproj0(yac, ybc, ctx, mod, ctx_row, w_out, row2(ln1_g[i]), row2(ln1_b[i]), TM_CONTEXT)
        else:
            assert last
            w_in = weight(cd_w_in, "cd_in", j)
            w_out = weight(cd_w_out, "cd_out", j)
            u, u_flat, gb, p = _inproj1(x, mod, None, w_in, TM_PROJ, True)
            (uc_flat,) = _inproj1(ctx, mod, ctx_row, w_in, TM_CONTEXT, False)
            wst, a16, wbig = _s5_operators(s5_log_dt[j], s5_a_re[j], s5_a_im[j], s5_b_re[j], s5_b_im[j],
                                           s5_c_re[j], s5_c_im[j])
            h_in = _s5_states(uc_flat, u_flat, wst, a16)
            y = _s5_readout(u_flat, h_in, wbig)
            x = _outproj1(y, u, gb, p, x, mod, None, row2(s5_d[j]), s5_w_glu[j].astype(BF16), row2(s5_b_glu[j]),
                          sc_conv_w[j], row2(sc_conv_b[j]), w_out, row2(ln1_g[i]), row2(ln1_b[i]), TM_LATENT)
        w1 = weight(ffn_w1, "w1", i)
        w2 = weight(ffn_w2, "w2", i)
        ahead = [job for job in todo if job[1] != i or job[0] not in ("w1", "w2")]
        x, = hosted(functools.partial(_ffn, x, mod, None, w1, row2(ffn_b1[i]), w2, row2(ffn_b2[i]),
                                      row2(ln2_g[i]), row2(ln2_b[i]), TM_LATENT, FFN_TH), 1, ahead)
        if not last:
            rows = ctx.reshape(1, bsz * lc, d)
            rows, = _ffn(rows, mod, ctx_row, w1, row2(ffn_b1[i]), w2, row2(ffn_b2[i]), row2(ln2_g[i]),
                         row2(ln2_b[i]), TM_LATENT, FFN_TH)
            ctx = rows.reshape(bsz, lc, d)
    return x
```

```python
import functools
import math

import jax
import jax.numpy as jnp
import numpy as np
from jax import lax
from jax.experimental import pallas as pl
from jax.experimental.pallas import tpu as pltpu

F32 = jnp.float32
BF16 = jnp.bfloat16

D_MODEL = 2048
DEPTH = 2
GRID_W = 64
LRU_WIDTH = D_MODEL // 2
LRU_HEADS = 8
LRU_BLOCK = LRU_WIDTH // LRU_HEADS
LRU_CONV = 4
LRU_CONV_LEFT = 2
LRU_C = 8.0
ATT_HEAD_DIM = 128
ATT_Q_HEADS = (D_MODEL // 2) // ATT_HEAD_DIM
ATT_KV_HEADS = 2
ATT_GROUP = ATT_Q_HEADS // ATT_KV_HEADS
ATT_Q_WIDTH = ATT_Q_HEADS * ATT_HEAD_DIM
ATT_KV_WIDTH = ATT_KV_HEADS * ATT_HEAD_DIM
WINDOW = 128
ATT_BLOCK = 128
ROPE_BASE = 10000.0
ROPE_FREQS = ATT_HEAD_DIM // 4
S5_WIDTH = D_MODEL // 4
S5_GROUP = 16
S5_STATE = 64
SC_WIDTH = D_MODEL - S5_WIDTH
SC_CONV = 3
ALPHA = (2.0 * DEPTH) ** 0.25
LN_EPS = 1e-5
NEG_INF = -1e30
LOG2E = math.log2(math.e)

LANES = 128
SUBLANES = 8
V7X_VMEM_BYTES = 64 * 1024 * 1024
V7X_VMEM_BUDGET = V7X_VMEM_BYTES - 2 * 1024 * 1024

S5_CHUNK = 16
S5_SLAB_GROUPS = LANES // S5_GROUP
S5_SLABS = S5_WIDTH // LANES
S5_SLAB_STATE = S5_SLAB_GROUPS * S5_STATE
S5_XK = S5_CHUNK * LANES
S5_HK = 4 * S5_SLAB_STATE
S5_POW_ROWS = 64


def _vmem_limit(nbytes):
    return int(min(V7X_VMEM_BUDGET, max(nbytes * 3 // 2, 16 * 1024 * 1024)))


def _resident(shape, index_map):
    return pl.BlockSpec(shape, index_map, pipeline_mode=pl.Buffered(1))


def _params(sem, nbytes):
    return pltpu.CompilerParams(dimension_semantics=sem, vmem_limit_bytes=_vmem_limit(nbytes))


def _dot(a, b):
    return jnp.dot(a, b, preferred_element_type=F32)


def _dot_nt(a, b):
    return lax.dot_general(a, b, (((1,), (1,)), ((), ())), preferred_element_type=F32)


def _layer_norm(v, g, b):
    mu = jnp.mean(v, axis=-1, keepdims=True)
    c = v - mu
    var = jnp.mean(c * c, axis=-1, keepdims=True)
    return c * lax.rsqrt(var + LN_EPS) * g + b


def _gelu_tanh(x):
    return 0.5 * x * (1.0 + jnp.tanh(math.sqrt(2.0 / math.pi) * (x + 0.044715 * (x * x * x))))


def _sigmoid(x):
    return 0.5 * (1.0 + jnp.tanh(0.5 * x))


MOD_SH1, MOD_SC1, MOD_G1, MOD_SH2, MOD_SC2, MOD_G2 = range(6)


def _mod_spec(mod, chunk):
    _, rows, width = mod.shape
    return pl.BlockSpec((1, rows, width // 6), lambda *_: (0, 0, chunk))


def _mod_row(ref, ctx_row):
    row = pl.program_id(0) if ctx_row is None else ctx_row
    return ref[0, pl.ds(row, 1), :]


def _modulate(x, sc_ref, sh_ref, ctx_row):
    return (x * (1.0 + _mod_row(sc_ref, ctx_row)) + _mod_row(sh_ref, ctx_row)).astype(BF16)


CAST_BLOCK_BYTES = 8 * 1024 * 1024


def _cast_kernel(w_ref, o_ref):
    o_ref[...] = w_ref[0].astype(BF16)


def _to_bf16(w, layer):
    _, rows, cols = w.shape
    tr = rows
    while tr * cols * 4 > CAST_BLOCK_BYTES and tr % 2 == 0 and tr // 2 >= 2 * SUBLANES:
        tr //= 2
    return pl.pallas_call(
        _cast_kernel,
        grid=(rows // tr,),
        in_specs=[pl.BlockSpec((1, tr, cols), lambda i: (layer, i, 0))],
        out_specs=pl.BlockSpec((tr, cols), lambda i: (i, 0)),
        out_shape=jax.ShapeDtypeStruct((rows, cols), BF16),
        compiler_params=_params(("arbitrary",), 2 * tr * cols * 6),
        name="weight_cast",
    )(w)


CAST_ROWS = 2 * SUBLANES


def _silu_bf16(v):
    return (v * _sigmoid(v)).astype(BF16)


def _hosted_call(body, *, grid, in_specs, out_specs, out_shape, args, sem, nbytes, casts=(), mods=(), **kwargs):
    steps = math.prod(grid)

    def flat(*ids):
        step = 0
        for extent, i in zip(grid, ids):
            step = step * extent + i
        return step

    def upto(used):
        return lambda *ids: jnp.minimum(flat(*ids), used - 1)

    def run_cast(ins, out):
        out[...] = ins[0][0].astype(BF16)

    def run_mod(ins, out):
        cc_ref, w_ref, b_ref = ins
        out[0] = _dot(_silu_bf16(cc_ref[...]), w_ref[0].astype(BF16)) + b_ref[0]

    side_in, side_out, side_shape, side_args, jobs = [], [], [], [], []
    for w, layer in casts:
        _, rows, cols = w.shape
        rps = max(CAST_ROWS, rows // steps)
        assert rows % rps == 0 and rows // rps <= steps
        at = upto(rows // rps)
        side_in.append(pl.BlockSpec((1, rps, cols), lambda *ids, at=at, layer=layer: (layer, at(*ids), 0)))
        side_out.append(pl.BlockSpec((rps, cols), lambda *ids, at=at: (at(*ids), 0)))
        side_shape.append(jax.ShapeDtypeStruct((rows, cols), BF16))
        side_args.append(w)
        nbytes += 2 * rps * cols * (4 + 2)
        jobs.append((1, run_cast))
    for cc, mod_w, mod_b, layer in mods:
        depth, d, n = mod_w.shape
        width = LANES * max(1, n // LANES // steps)
        assert n % width == 0 and n // width <= steps
        at = upto(n // width)
        side_in += [pl.BlockSpec(cc.shape, lambda *ids: (0, 0)),
                    pl.BlockSpec((1, d, width), lambda *ids, at=at, layer=layer: (layer, 0, at(*ids))),
                    pl.BlockSpec((1, 1, width), lambda *ids, at=at, layer=layer: (layer, 0, at(*ids)))]
        side_out.append(pl.BlockSpec((1, cc.shape[0], width), lambda *ids, at=at: (0, 0, at(*ids))))
        side_shape.append(jax.ShapeDtypeStruct((1, cc.shape[0], n), F32))
        side_args += [cc, mod_w, mod_b.reshape(depth, 1, n)]
        nbytes += 2 * d * width * 4 + d * width * 2
        jobs.append((3, run_mod))
    n_in, n_out, n_side_in, n_side_out = len(in_specs), len(out_specs), len(side_in), len(side_out)

    def kernel(*refs):
        ins = refs[:n_in]
        s_ins = refs[n_in:n_in + n_side_in]
        outs = refs[n_in + n_side_in:n_in + n_side_in + n_out]
        s_outs = refs[n_in + n_side_in + n_out:n_in + n_side_in + n_out + n_side_out]
        pos = 0
        for (arity, run), out in zip(jobs, s_outs):
            run(s_ins[pos:pos + arity], out)
            pos += arity
        body(*ins, *outs, *refs[n_in + n_side_in + n_out + n_side_out:])

    return pl.pallas_call(
        kernel, grid=grid, in_specs=list(in_specs) + side_in, out_specs=list(out_specs) + side_out,
        out_shape=list(out_shape) + side_shape, compiler_params=_params(sem, nbytes), **kwargs,
    )(*args, *side_args)


def _mod_kernel(cc_ref, w_ref, b_ref, o_ref):
    o_ref[0] = _dot(_silu_bf16(cc_ref[...]), w_ref[0].astype(BF16)) + b_ref[0]


def _modulation(cc, mod_w, mod_b, layer):
    depth, d, n = mod_w.shape
    tn = 1024
    rows = cc.shape[0]
    return pl.pallas_call(
        _mod_kernel,
        grid=(n // tn,),
        in_specs=[pl.BlockSpec((rows, d), lambda j: (0, 0)),
                  pl.BlockSpec((1, d, tn), lambda j: (layer, 0, j)),
                  pl.BlockSpec((1, 1, tn), lambda j: (layer, 0, j))],
        out_specs=pl.BlockSpec((1, rows, tn), lambda j: (0, 0, j)),
        out_shape=jax.ShapeDtypeStruct((1, rows, n), F32),
        compiler_params=_params(("arbitrary",), 2 * d * tn * 4 + d * tn * 2),
        name="modulation",
    )(cc, mod_w, mod_b.reshape(depth, 1, n))


def _rope(x, cos, sin_signed, heads):
    lane = lax.broadcasted_iota(jnp.int32, (x.shape[0], ATT_HEAD_DIM), 1)
    first = (lane % (2 * ROPE_FREQS)) < ROPE_FREQS
    out = []
    for h in range(heads):
        xs = x[:, h * ATT_HEAD_DIM:(h + 1) * ATT_HEAD_DIM]
        swapped = jnp.where(first, pltpu.roll(xs, ATT_HEAD_DIM - ROPE_FREQS, 1), pltpu.roll(xs, ROPE_FREQS, 1))
        out.append(xs * cos + swapped * sin_signed)
    return out


def _inproj0_kernel(*refs, rope, ctx_row):
    if rope:
        (x_ref, sc_ref, sh_ref, cos_ref, sin_ref, wu_ref, wg_ref, wq_ref, wk_ref, wv_ref,
         u_ref, g_ref, q_ref, k_ref, v_ref) = refs
    else:
        (x_ref, sc_ref, sh_ref, wu_ref, wg_ref, wq_ref, wk_ref, wv_ref,
         u_ref, g_ref, q_ref, k_ref, v_ref) = refs
    h = _modulate(x_ref[0], sc_ref, sh_ref, ctx_row)
    q = _dot(h, wq_ref[...]) * (ATT_HEAD_DIM ** -0.5 * LOG2E)
    k = _dot(h, wk_ref[...])
    if rope:
        cos = cos_ref[...]
        sin = sin_ref[...]
        for hd, piece in enumerate(_rope(q, cos, sin, ATT_Q_HEADS)):
            q_ref[0, :, hd * ATT_HEAD_DIM:(hd + 1) * ATT_HEAD_DIM] = piece.astype(BF16)
        for hd, piece in enumerate(_rope(k, cos, sin, ATT_KV_HEADS)):
            k_ref[0, :, hd * ATT_HEAD_DIM:(hd + 1) * ATT_HEAD_DIM] = piece.astype(BF16)
    else:
        q_ref[0] = q.astype(BF16)
        k_ref[0] = k.astype(BF16)
    u = _dot(h, wu_ref[...])
    g = _dot(h, wg_ref[...])
    for hd in range(LRU_HEADS):
        u_ref[0, hd] = u[:, hd * LRU_BLOCK:(hd + 1) * LRU_BLOCK].astype(BF16)
        g_ref[0, hd] = g[:, hd * LRU_BLOCK:(hd + 1) * LRU_BLOCK].astype(BF16)
    v_ref[0] = _dot(h, wv_ref[...]).astype(BF16)


def _inproj0(x, mod, ctx_row, w_in, rope_tabs, tm):
    bsz, n, d = x.shape
    rope = rope_tabs is not None
    tok = lambda b, i: (b, i, 0)
    in_specs = [pl.BlockSpec((1, tm, d), tok), _mod_spec(mod, MOD_SC1), _mod_spec(mod, MOD_SH1)]
    args = [x, mod, mod]
    if rope:
        in_specs += [pl.BlockSpec((tm, ATT_HEAD_DIM), lambda b, i: (i, 0))] * 2
        args += list(rope_tabs)
    kv_at = (2 * LRU_WIDTH + ATT_Q_WIDTH) // ATT_KV_WIDTH
    in_specs += [_resident((d, LRU_WIDTH), lambda b, i: (0, 0)), _resident((d, LRU_WIDTH), lambda b, i: (0, 1)),
                 _resident((d, ATT_Q_WIDTH), lambda b, i: (0, 2)),
                 _resident((d, ATT_KV_WIDTH), lambda b, i: (0, kv_at)),
                 _resident((d, ATT_KV_WIDTH), lambda b, i: (0, kv_at + 1))]
    args += [w_in] * 5
    head_major = jax.ShapeDtypeStruct((bsz, LRU_HEADS, n, LRU_BLOCK), BF16)
    head_spec = pl.BlockSpec((1, LRU_HEADS, tm, LRU_BLOCK), lambda b, i: (b, 0, i, 0))
    nbytes = 2 * tm * d * 4 + 2 * w_in.size + 2 * 2 * tm * w_in.shape[1] + tm * w_in.shape[1] * 4
    return pl.pallas_call(
        functools.partial(_inproj0_kernel, rope=rope, ctx_row=ctx_row),
        grid=(bsz, n // tm),
        in_specs=in_specs,
        out_specs=[head_spec, head_spec,
                   pl.BlockSpec((1, tm, ATT_Q_WIDTH), tok),
                   pl.BlockSpec((1, tm, ATT_KV_WIDTH), tok),
                   pl.BlockSpec((1, tm, ATT_KV_WIDTH), tok)],
        out_shape=[head_major, head_major,
                   jax.ShapeDtypeStruct((bsz, n, ATT_Q_WIDTH), BF16),
                   jax.ShapeDtypeStruct((bsz, n, ATT_KV_WIDTH), BF16),
                   jax.ShapeDtypeStruct((bsz, n, ATT_KV_WIDTH), BF16)],
        compiler_params=_params(("arbitrary", "arbitrary"), nbytes),
        name="inproj0_rope" if rope else "inproj0_ctx",
    )(*args)


LRU_PAD = SUBLANES
LRU_TILE = 256


def _scan8(a, b, row, reverse):
    for k in (1, 2, 4):
        if reverse:
            keep = row < SUBLANES - k
            shift = SUBLANES - k
        else:
            keep = row >= k
            shift = k
        a_sh = jnp.where(keep, pltpu.roll(a, shift, 0), 1.0)
        b_sh = jnp.where(keep, pltpu.roll(b, shift, 0), 0.0)
        b = a * b_sh + b
        a = a * a_sh
    return a, b


def _scan8_rows(a, b, row, reverse):
    out_a, out_b = [], []
    for g in range(a.shape[0] // SUBLANES):
        sl = slice(g * SUBLANES, (g + 1) * SUBLANES)
        ag, bg = _scan8(a[sl], b[sl], row, reverse)
        out_a.append(ag)
        out_b.append(bg)
    return jnp.concatenate(out_a, axis=0), jnp.concatenate(out_b, axis=0)


def _lru_kernel(ul_ref, gl_ref, uc_ref, gc_ref, cw_ref, cb_ref, wg_ref, bg_ref, lam_ref,
                yl_ref, yc_ref, upad, af_s, bf_s, ar_s, br_s, cin_s, *, n_lat, n_ctx):
    cw = cw_ref[...]
    cb = cb_ref[...]
    wg = wg_ref[0]
    bg = bg_ref[0]
    lam = lam_ref[0]
    neg = -lam
    softplus = jnp.maximum(neg, 0.0) + jnp.log1p(jnp.exp(-jnp.abs(neg)))
    half_rate = (-0.5 * LRU_C) * softplus
    row = lax.broadcasted_iota(jnp.int32, (SUBLANES, LRU_BLOCK), 0)
    chains = ((af_s, bf_s, False), (ar_s, br_s, True))

    def coefficients(src_ref, n_rows, row_off):
        upad[pl.ds(0, LRU_PAD), :] = jnp.zeros((LRU_PAD, LRU_BLOCK), F32)
        upad[pl.ds(LRU_PAD, n_rows), :] = src_ref[0, 0].astype(F32)
        upad[pl.ds(LRU_PAD + n_rows, LRU_PAD), :] = jnp.zeros((LRU_PAD, LRU_BLOCK), F32)

        def tile(i, carry):
            t0 = pl.multiple_of(i * LRU_TILE, LRU_TILE)
            xp = upad[pl.ds(t0, LRU_TILE + 2 * LRU_PAD), :]
            conv = cb
            for k in range(LRU_CONV):
                o = LRU_PAD - LRU_CONV_LEFT + k
                conv = conv + cw[k:k + 1, :] * xp[o:o + LRU_TILE, :]
            z = _dot(conv.astype(BF16), wg) + bg
            half_conv = 0.5 * conv
            for d, (a_ref, b_ref, reverse) in enumerate(chains):
                t_a = jnp.tanh(z[:, (2 * d) * LRU_BLOCK:(2 * d + 1) * LRU_BLOCK])
                t_x = jnp.tanh(z[:, (2 * d + 1) * LRU_BLOCK:(2 * d + 2) * LRU_BLOCK])
                rate = half_rate[:, d * LRU_BLOCK:(d + 1) * LRU_BLOCK]
                a = jnp.exp(rate + rate * t_a)
                gated = half_conv + half_conv * t_x
                b = jnp.sqrt(1.0 - a * a) * gated
                a, b = _scan8_rows(a, b, row, reverse)
                a_ref[pl.ds(row_off + t0, LRU_TILE), :] = a
                b_ref[pl.ds(row_off + t0, LRU_TILE), :] = b
            return carry

        tiles = n_rows // LRU_TILE
        lax.fori_loop(0, tiles, tile, 0, unroll=2 if tiles % 2 == 0 else 1)

    coefficients(uc_ref, n_ctx, 0)
    coefficients(ul_ref, n_lat, n_ctx)

    groups_ctx = n_ctx // SUBLANES
    groups = (n_ctx + n_lat) // SUBLANES
    fa = af_s[pl.ds(SUBLANES - 1, groups, stride=SUBLANES), :]
    fb = bf_s[pl.ds(SUBLANES - 1, groups, stride=SUBLANES), :]
    ra = ar_s[pl.ds(0, groups, stride=SUBLANES), :]
    rb = br_s[pl.ds(0, groups, stride=SUBLANES), :]
    zero = jnp.zeros((SUBLANES, LRU_BLOCK), F32)
    carry = zero
    for v in range(groups // SUBLANES):
        sl = slice(v * SUBLANES, (v + 1) * SUBLANES)
        a, b = _scan8(fa[sl], fb[sl], row, False)
        incl = b + a * carry
        cin_s[0, sl, :] = jnp.where(row == 0, carry, pltpu.roll(incl, 1, 0))
        carry = jnp.broadcast_to(incl[SUBLANES - 1:SUBLANES, :], incl.shape)
    carry = zero
    order = list(range(groups_ctx // SUBLANES - 1, -1, -1)) + list(range(groups // SUBLANES - 1,
                                                                        groups_ctx // SUBLANES - 1, -1))
    for v in order:
        sl = slice(v * SUBLANES, (v + 1) * SUBLANES)
        a, b = _scan8(ra[sl], rb[sl], row, True)
        incl = b + a * carry
        cin_s[1, sl, :] = jnp.where(row == SUBLANES - 1, carry, pltpu.roll(incl, SUBLANES - 1, 0))
        carry = jnp.broadcast_to(incl[0:1, :], incl.shape)

    def emit(g_ref, y_ref, n_rows, row_off):
        tile_groups = LRU_TILE // SUBLANES

        def tile(i, carry):
            t0 = pl.multiple_of(i * LRU_TILE, LRU_TILE)
            g0 = pl.multiple_of(row_off // SUBLANES + i * tile_groups, tile_groups)
            rows = pl.ds(row_off + t0, LRU_TILE)
            a_f, b_f, a_r, b_r = af_s[rows, :], bf_s[rows, :], ar_s[rows, :], br_s[rows, :]
            pieces = []
            for g in range(tile_groups):
                sl = slice(g * SUBLANES, (g + 1) * SUBLANES)
                c_f = jnp.broadcast_to(cin_s[0, pl.ds(g0 + g, 1), :], (SUBLANES, LRU_BLOCK))
                c_r = jnp.broadcast_to(cin_s[1, pl.ds(g0 + g, 1), :], (SUBLANES, LRU_BLOCK))
                pieces.append((b_f[sl] + a_f[sl] * c_f) + (b_r[sl] + a_r[sl] * c_r))
            h = jnp.concatenate(pieces, axis=0)
            gate = g_ref[0, 0, pl.ds(t0, LRU_TILE), :].astype(F32)
            y_ref[0, 0, pl.ds(t0, LRU_TILE), :] = (h * _gelu_tanh(gate)).astype(BF16)
            return carry

        lax.fori_loop(0, n_rows // LRU_TILE, tile, 0)

    emit(gc_ref, yc_ref, n_ctx, 0)
    emit(gl_ref, yl_ref, n_lat, n_ctx)


def _rglru(u_lat, g_lat, u_ctx, g_ctx, conv_w, conv_b, w_gate, b_gate, lam, casts=(), mods=()):
    bsz, heads, n_lat, blk = u_lat.shape
    n_ctx = u_ctx.shape[2]
    total = n_lat + n_ctx
    assert n_ctx % (SUBLANES * SUBLANES) == 0 and n_lat % LRU_TILE == 0 and n_ctx % LRU_TILE == 0
    seq = lambda n: pl.BlockSpec((1, 1, n, blk), lambda b, h: (b, h, 0, 0))
    per_head = lambda shape: pl.BlockSpec((1,) + shape, lambda b, h: (h, 0, 0))
    nbytes =4 * total * blk * 4 + (n_lat + 2 * LRU_PAD) * blk * 4 + 2 * 3 * 2 * total * blk * 2
    return _hosted_call(
        functools.partial(_lru_kernel, n_lat=n_lat, n_ctx=n_ctx),
        casts=casts, mods=mods,
        args=(u_lat, g_lat, u_ctx, g_ctx, conv_w, conv_b, w_gate, b_gate, lam),
        grid=(bsz, heads),
        in_specs=[seq(n_lat), seq(n_lat), seq(n_ctx), seq(n_ctx),
                  pl.BlockSpec((LRU_CONV, blk), lambda b, h: (0, h)),
                  pl.BlockSpec((1, blk), lambda b, h: (0, h)),
                  per_head((blk, 4 * blk)), per_head((1, 4 * blk)), per_head((1, 2 * blk))],
        out_specs=[seq(n_lat), seq(n_ctx)],
        out_shape=[jax.ShapeDtypeStruct(u_lat.shape, BF16), jax.ShapeDtypeStruct(u_ctx.shape, BF16)],
        scratch_shapes=[pltpu.VMEM((n_lat + 2 * LRU_PAD, blk), F32)]
                       + [pltpu.VMEM((total, blk), F32)] * 4
                       + [pltpu.VMEM((2, total // SUBLANES, blk), F32)],
        sem=("arbitrary", "arbitrary"), nbytes=nbytes,
        name="rglru",
    )


ATT_TQ = 512
ATT_BAND = 3 * ATT_BLOCK


def _dot_tn(a, b):
    return lax.dot_general(a, b, (((0,), (0,)), ((), ())), preferred_element_type=F32)


def _softmax_probs(scores, sink_row):
    m = sink_row
    for s in scores:
        m = jnp.maximum(m, jnp.max(s, axis=0, keepdims=True))
    probs = [jnp.exp2(s - m).astype(BF16) for s in scores]
    pad = 2 * SUBLANES
    first = lax.broadcasted_iota(jnp.int32, (pad, m.shape[1]), 0) == 0
    p_sink = jnp.where(first, jnp.exp2(sink_row - m), 0.0).astype(BF16)
    probs[-1] = jnp.concatenate([probs[-1], p_sink], axis=0)
    return probs


def _weighted_values(probs, values):
    dh = ATT_HEAD_DIM
    acc = None
    for idx, (p, v) in enumerate(zip(probs, values)):
        v_ext = jnp.concatenate([v, jnp.ones(v.shape, BF16)], axis=1)
        if idx == len(values) - 1:
            pad = p.shape[0] - v.shape[0]
            v_sink = jnp.concatenate([jnp.zeros((pad, dh), BF16), jnp.ones((pad, dh), BF16)], axis=1)
            v_ext = jnp.concatenate([v_ext, v_sink], axis=0)
        pv = _dot_tn(p, v_ext)
        acc = pv if acc is None else acc + pv
    return acc[:, :dh] / acc[:, dh:]


def _sink_rows(sink, reps):
    return jnp.repeat(sink.reshape(ATT_KV_HEADS, 1, ATT_GROUP), reps, axis=2)


def _attn_kernel(q_ref, k_ref, v_ref, kc_ref, vc_ref, sink_ref, o_ref, *, seq):
    tile = pl.program_id(1)
    blocks = ATT_TQ // ATT_BLOCK
    qcol = lax.broadcasted_iota(jnp.int32, (1, ATT_GROUP * ATT_BLOCK), 1) % ATT_BLOCK
    krow = lax.broadcasted_iota(jnp.int32, (ATT_BAND, 1), 0)

    def scores(u):
        i, g = divmod(u, ATT_KV_HEADS)
        q0 = (tile * blocks + i) * ATT_BLOCK
        start = pl.multiple_of(jnp.clip(q0 - ATT_BLOCK, 0, seq - ATT_BAND), ATT_BLOCK)
        valid = jnp.abs((q0 + qcol) - (start + krow)) <= WINDOW
        heads = [q_ref[0, i * ATT_BLOCK:(i + 1) * ATT_BLOCK,
                       (g * ATT_GROUP + r) * ATT_HEAD_DIM:(g * ATT_GROUP + r + 1) * ATT_HEAD_DIM]
                 for r in range(ATT_GROUP)]
        qs = jnp.concatenate(heads, axis=0)
        kv = slice(g * ATT_HEAD_DIM, (g + 1) * ATT_HEAD_DIM)
        s_loc = jnp.where(valid, _dot_nt(k_ref[0, pl.ds(start, ATT_BAND), kv], qs), NEG_INF)
        return start, [s_loc, _dot_nt(kc_ref[0, :, kv], qs)]

    def emit(u, start, p):
        i, g = divmod(u, ATT_KV_HEADS)
        kv = slice(g * ATT_HEAD_DIM, (g + 1) * ATT_HEAD_DIM)
        o = _weighted_values(p, [v_ref[0, pl.ds(start, ATT_BAND), kv], vc_ref[0, :, kv]])
        for r in range(ATT_GROUP):
            col = (g * ATT_GROUP + r) * ATT_HEAD_DIM
            o_ref[0, i * ATT_BLOCK:(i + 1) * ATT_BLOCK, col:col + ATT_HEAD_DIM] = (
                o[r * ATT_BLOCK:(r + 1) * ATT_BLOCK].astype(BF16))

    n_units = blocks * ATT_KV_HEADS
    sc, pr = {}, {}
    for step in range(n_units + 2):
        if step < n_units:
            sc[step] = scores(step)
        if 0 <= step - 1 < n_units:
            start, s_list = sc.pop(step - 1)
            pr[step - 1] = (start, _softmax_probs(s_list, sink_ref[(step - 1) % ATT_KV_HEADS] * LOG2E))
        if 0 <= step - 2 < n_units:
            start, p = pr.pop(step - 2)
            emit(step - 2, start, p)


def _window_attention(q, k, v, kc, vc, sink, casts=()):
    bsz, n, _ = q.shape
    sink_rows = _sink_rows(sink, ATT_BLOCK)
    lc = kc.shape[1]
    whole = lambda rows: pl.BlockSpec((1, rows, ATT_KV_WIDTH), lambda b, i: (b, 0, 0))
    nbytes = 2 * 2 * (2 * ATT_TQ * ATT_Q_WIDTH + 2 * n * ATT_KV_WIDTH + 2 * lc * ATT_KV_WIDTH) + 8 * 1024 * 1024
    return _hosted_call(
        functools.partial(_attn_kernel, seq=n),
        casts=casts,
        args=(q, k, v, kc, vc, sink_rows),
        grid=(bsz, n // ATT_TQ),
        in_specs=[pl.BlockSpec((1, ATT_TQ, ATT_Q_WIDTH), lambda b, i: (b, i, 0)),
                  whole(n), whole(n), whole(lc), whole(lc),
                  pl.BlockSpec(sink_rows.shape, lambda b, i: (0, 0, 0))],
        out_specs=[pl.BlockSpec((1, ATT_TQ, ATT_Q_WIDTH), lambda b, i: (b, i, 0))],
        out_shape=[jax.ShapeDtypeStruct(q.shape, BF16)],
        sem=("arbitrary", "arbitrary"), nbytes=nbytes,
        name="window_attention",
    )


def _ctx_attn_kernel(q_ref, kc_ref, vc_ref, sink_ref, o_ref):
    lc = q_ref.shape[1]
    for g in range(ATT_KV_HEADS):
        heads = [q_ref[0, :, (g * ATT_GROUP + r) * ATT_HEAD_DIM:(g * ATT_GROUP + r + 1) * ATT_HEAD_DIM]
                 for r in range(ATT_GROUP)]
        qs = jnp.concatenate(heads, axis=0)
        kv = slice(g * ATT_HEAD_DIM, (g + 1) * ATT_HEAD_DIM)
        o = _weighted_values(_softmax_probs([_dot_nt(kc_ref[0, :, kv], qs)], sink_ref[g] * LOG2E),
                             [vc_ref[0, :, kv]])
        for r in range(ATT_GROUP):
            col = (g * ATT_GROUP + r) * ATT_HEAD_DIM
            o_ref[0, :, col:col + ATT_HEAD_DIM] = o[r * lc:(r + 1) * lc].astype(BF16)


def _context_attention(qc, kc, vc, sink):
    bsz, lc, _ = qc.shape
    sink_rows = _sink_rows(sink, lc)
    kv_spec = pl.BlockSpec((1, lc, ATT_KV_WIDTH), lambda b: (b, 0, 0))
    q_spec = pl.BlockSpec((1, lc, ATT_Q_WIDTH), lambda b: (b, 0, 0))
    return pl.pallas_call(
        _ctx_attn_kernel,
        grid=(bsz,),
        in_specs=[q_spec, kv_spec, kv_spec, pl.BlockSpec(sink_rows.shape, lambda b: (0, 0, 0))],
        out_specs=q_spec,
        out_shape=jax.ShapeDtypeStruct(qc.shape, BF16),
        compiler_params=_params(("arbitrary",), 8 * 1024 * 1024),
        name="context_attention",
    )(qc, kc, vc, sink_rows)


OUT_PARTS = 4


def _outproj0_kernel(ya_ref, yb_ref, x_ref, gate_ref, w_ref, lg_ref, lb_ref, o_ref, *, ctx_row):
    gate = _mod_row(gate_ref, ctx_row)
    parts = OUT_PARTS
    rows_per = x_ref.shape[1] // parts
    outs = {}
    for r in range(parts + 1):
        if r < parts:
            rows = slice(r * rows_per, (r + 1) * rows_per)
            y = jnp.concatenate([ya_ref[0, h, rows, :] for h in range(LRU_HEADS)] + [yb_ref[0, rows, :]], axis=1)
            outs[r] = _dot(y, w_ref[...])
        if r >= 1:
            rows = slice((r - 1) * rows_per, r * rows_per)
            o_ref[0, rows, :] = _layer_norm(ALPHA * x_ref[0, rows, :] + gate * outs.pop(r - 1), lg_ref[...], lb_ref[...])


def _outproj0(ya, yb, x, mod, ctx_row, w_out, ln_g, ln_b, tm):
    bsz, n, d = x.shape
    tok = lambda b, i: (b, i, 0)
    const = lambda b, i: (0, 0)
    nbytes = 2 * 2 * tm * d * 4 + w_out.size * 2 + 2 * 2 * tm * d * 2 + 2 * tm * d * 4
    return pl.pallas_call(
        functools.partial(_outproj0_kernel, ctx_row=ctx_row),
        grid=(bsz, n // tm),
        in_specs=[pl.BlockSpec((1, LRU_HEADS, tm, LRU_BLOCK), lambda b, i: (b, 0, i, 0)),
                  pl.BlockSpec((1, tm, ATT_Q_WIDTH), tok),
                  pl.BlockSpec((1, tm, d), tok),
                  _mod_spec(mod, MOD_G1),
                  _resident(w_out.shape, const),
                  pl.BlockSpec((1, d), const), pl.BlockSpec((1, d), const)],
        out_specs=pl.BlockSpec((1, tm, d), tok),
        out_shape=jax.ShapeDtypeStruct(x.shape, F32),
        compiler_params=_params(("arbitrary", "arbitrary"), nbytes),
        name="outproj0_ln",
    )(ya, yb, x, mod, w_out, ln_g, ln_b)


def _ffn_kernel(x_ref, sc_ref, sh_ref, gate_ref, w1_ref, b1_ref, w2_ref, b2_ref, lg_ref, lb_ref,
                o_ref, h_s, *, ctx_row):
    j = pl.program_id(2)

    @pl.when(j == 0)
    def _():
        h_s[...] = _modulate(x_ref[0], sc_ref, sh_ref, ctx_row)
        o_ref[0] = jnp.zeros(o_ref.shape[1:], F32)

    a = jnp.maximum(_dot(h_s[...], w1_ref[...]) + b1_ref[...], 0.0)
    o_ref[0] += _dot((a * a).astype(BF16), w2_ref[...])

    @pl.when(j == pl.num_programs(2) - 1)
    def _():
        f = o_ref[0] + b2_ref[...]
        o_ref[0] = _layer_norm(ALPHA * x_ref[0] + _mod_row(gate_ref, ctx_row) * f, lg_ref[...], lb_ref[...])


def _ffn(x, mod, ctx_row, w1, b1, w2, b2, ln_g, ln_b, tm, th, casts=()):
    bsz, n, d = x.shape
    hidden = w1.shape[1]
    tok = lambda b, i, j: (b, i, 0)
    const = lambda b, i, j: (0, 0)
    nbytes = 2 * 2 * tm * d * 4 + 2 * 2 * 2 * d * th * 2 + tm * d * 2 + 2 * tm * th * 4 + tm * d * 4
    return _hosted_call(
        functools.partial(_ffn_kernel, ctx_row=ctx_row),
        casts=casts,
        args=(x, mod, mod, mod, w1, b1, w2, b2, ln_g, ln_b),
        grid=(bsz, n // tm, hidden // th),
        in_specs=[pl.BlockSpec((1, tm, d), tok),
                  _mod_spec(mod, MOD_SC2), _mod_spec(mod, MOD_SH2), _mod_spec(mod, MOD_G2),
                  pl.BlockSpec((d, th), lambda b, i, j: (0, j)),
                  pl.BlockSpec((1, th), lambda b, i, j: (0, j)),
                  pl.BlockSpec((th, d), lambda b, i, j: (j, 0)),
                  pl.BlockSpec((1, d), const), pl.BlockSpec((1, d), const), pl.BlockSpec((1, d), const)],
        out_specs=[pl.BlockSpec((1, tm, d), tok)],
        out_shape=[jax.ShapeDtypeStruct(x.shape, F32)],
        scratch_shapes=[pltpu.VMEM((tm, d), BF16)],
        sem=("arbitrary", "arbitrary", "arbitrary"), nbytes=nbytes,
        name="ffn_ln",
    )


SC_PARTS = SC_WIDTH // S5_WIDTH


def _inproj1_kernel(*refs, full, ctx_row):
    x_ref, sc_ref, sh_ref, wu_ref = refs[:4]
    u_s = refs[-1]
    h = _modulate(x_ref[0], sc_ref, sh_ref, ctx_row)
    u = _dot(h, wu_ref[...])
    if full:
        wb_refs = refs[4:4 + SC_PARTS]
        wc_refs = refs[4 + SC_PARTS:4 + 2 * SC_PARTS]
        wx_refs = refs[4 + 2 * SC_PARTS:4 + 3 * SC_PARTS]
        u_ref, uflat_ref, gb_ref, p_ref = refs[4 + 3 * SC_PARTS:-1]
        u_ref[0] = u.astype(BF16)
        for c in range(SC_PARTS):
            cols = slice(c * S5_WIDTH, (c + 1) * S5_WIDTH)
            gb_ref[0, :, cols] = _dot(h, wb_refs[c][...]).astype(BF16)
            p_ref[0, :, cols] = (_dot(h, wc_refs[c][...]) * _dot(h, wx_refs[c][...])).astype(BF16)
    else:
        uflat_ref, = refs[4:-1]
    for k in range(S5_SLABS):
        u_s[k] = u[:, k * LANES:(k + 1) * LANES]
    chunks = u.shape[0] // S5_CHUNK
    for t in range(S5_CHUNK):
        for k in range(S5_SLABS):
            col = t * S5_WIDTH + k * LANES
            uflat_ref[0, :, col:col + LANES] = u_s[k, pl.ds(t, chunks, stride=S5_CHUNK), :].astype(BF16)


def _inproj1(x, mod, ctx_row, w_in, tm, full):
    bsz, n, d = x.shape
    tok = lambda b, i: (b, i, 0)
    n_blocks = 1 + 3 * SC_PARTS if full else 1
    flat_spec = pl.BlockSpec((1, tm // S5_CHUNK, S5_CHUNK * S5_WIDTH), tok)
    flat_shape = jax.ShapeDtypeStruct((bsz, n // S5_CHUNK, S5_CHUNK * S5_WIDTH), BF16)
    token = lambda w: (pl.BlockSpec((1, tm, w), tok), jax.ShapeDtypeStruct((bsz, n, w), BF16))
    outs = ([token(S5_WIDTH), (flat_spec, flat_shape), token(SC_WIDTH), token(SC_WIDTH)] if full
            else [(flat_spec, flat_shape)])
    ncols = n_blocks * S5_WIDTH
    nbytes = 2 * tm * d * 4 + 2 * d * ncols + 2 * 2 * tm * (ncols + S5_WIDTH) + tm * ncols * 4 + tm * S5_WIDTH * 4
    return pl.pallas_call(
        functools.partial(_inproj1_kernel, full=full, ctx_row=ctx_row),
        grid=(bsz, n // tm),
        in_specs=[pl.BlockSpec((1, tm, d), tok), _mod_spec(mod, MOD_SC1), _mod_spec(mod, MOD_SH1)]
                 + [_resident((d, S5_WIDTH), lambda b, i, c=c: (0, c)) for c in range(n_blocks)],
        out_specs=[spec for spec, _ in outs],
        out_shape=[shape for _, shape in outs],
        scratch_shapes=[pltpu.VMEM((S5_SLABS, tm, LANES), F32)],
        compiler_params=_params(("arbitrary", "arbitrary"), nbytes),
        name="inproj1" if full else "inproj1_ctx",
    )(x, mod, mod, *([w_in] * n_blocks))


def _s5_power_table(par_ref):
    rows = 2 * S5_POW_ROWS
    r = lax.broadcasted_iota(jnp.int32, (rows, S5_SLAB_STATE), 0)
    first = r < S5_POW_ROWS
    lag = (r % S5_POW_ROWS).astype(F32)
    pick = lambda i: jnp.where(first, par_ref[0, 0, i:i + 1, :], par_ref[0, 1, i:i + 1, :])
    dt = jnp.exp(pick(0))
    mag = jnp.exp(lag * dt * pick(1))
    ang = lag * dt * pick(2)
    return mag * jnp.cos(ang), mag * jnp.sin(ang)


def _s5_input_matrix(par_ref, bt_ref, d, pw_re, pw_im):
    a_re = par_ref[0, d, 1:2, :]
    a_im = par_ref[0, d, 2:3, :]
    ab_re = pw_re[d * S5_POW_ROWS + 1:d * S5_POW_ROWS + 2, :]
    ab_im = pw_im[d * S5_POW_ROWS + 1:d * S5_POW_ROWS + 2, :]
    den = a_re * a_re + a_im * a_im
    k_re = ((ab_re - 1.0) * a_re + ab_im * a_im) / den
    k_im = (ab_im * a_re - (ab_re - 1.0) * a_im) / den
    b_re = bt_ref[0, d, 0]
    b_im = bt_ref[0, d, 1]
    return k_re * b_re - k_im * b_im, k_re * b_im + k_im * b_re


def _cmul_row(x_re, x_im, p_re, p_im):
    return x_re * p_re - x_im * p_im, x_re * p_im + x_im * p_re


def _s5_state_prep_kernel(par_ref, bt_ref, wst_ref, a16_ref):
    pw_re, pw_im = _s5_power_table(par_ref)
    for d in range(2):
        bb_re, bb_im = _s5_input_matrix(par_ref, bt_ref, d, pw_re, pw_im)
        base = d * S5_POW_ROWS
        for lag in range(S5_CHUNK):
            e_re, e_im = _cmul_row(bb_re, bb_im, pw_re[base + lag:base + lag + 1, :],
                                   pw_im[base + lag:base + lag + 1, :])
            s = S5_CHUNK - 1 - lag if d == 0 else lag
            col = 2 * d * S5_SLAB_STATE
            wst_ref[0, s * LANES:(s + 1) * LANES, col:col + S5_SLAB_STATE] = e_re.astype(BF16)
            wst_ref[0, s * LANES:(s + 1) * LANES, col + S5_SLAB_STATE:col + 2 * S5_SLAB_STATE] = e_im.astype(BF16)
        row16 = base + S5_CHUNK
        a16_ref[0, :, 2 * d * S5_SLAB_STATE:(2 * d + 1) * S5_SLAB_STATE] = jnp.broadcast_to(
            pw_re[row16:row16 + 1, :], (SUBLANES, S5_SLAB_STATE))
        a16_ref[0, :, (2 * d + 1) * S5_SLAB_STATE:(2 * d + 2) * S5_SLAB_STATE] = jnp.broadcast_to(
            pw_im[row16:row16 + 1, :], (SUBLANES, S5_SLAB_STATE))


def _s5_output_prep_kernel(par_ref, bt_ref, c_ref, ct_ref, wbig_ref):
    pw_re, pw_im = _s5_power_table(par_ref)
    kern = []
    for d in range(2):
        bb_re, bb_im = _s5_input_matrix(par_ref, bt_ref, d, pw_re, pw_im)
        c_re = c_ref[0, d, 0].astype(BF16)
        c_im = c_ref[0, d, 1].astype(BF16)
        base = d * S5_POW_ROWS
        per_lag = []
        for lag in range(S5_CHUNK):
            e_re, e_im = _cmul_row(bb_re, bb_im, pw_re[base + lag:base + lag + 1, :],
                                   pw_im[base + lag:base + lag + 1, :])
            per_lag.append(_dot_nt(e_re.astype(BF16), c_re) - _dot_nt(e_im.astype(BF16), c_im))
        kern.append(per_lag)
    for s in range(S5_CHUNK):
        for t in range(S5_CHUNK):
            if s < t:
                blk = kern[0][t - s]
            elif s > t:
                blk = kern[1][s - t]
            else:
                blk = kern[0][0] + kern[1][0]
            wbig_ref[0, s * LANES:(s + 1) * LANES, t * LANES:(t + 1) * LANES] = blk.astype(BF16)
    pt_re = pw_re.T
    pt_im = pw_im.T
    for d in range(2):
        ct_re = ct_ref[0, d, 0]
        ct_im = ct_ref[0, d, 1]
        for t in range(S5_CHUNK):
            lag = t + 1 if d == 0 else S5_CHUNK - t
            col = d * S5_POW_ROWS + lag
            p_re = pt_re[:, col:col + 1]
            p_im = pt_im[:, col:col + 1]
            g_re = ct_re * p_re - ct_im * p_im
            g_im = ct_re * p_im + ct_im * p_re
            r0 = S5_XK + 2 * d * S5_SLAB_STATE
            wbig_ref[0, r0:r0 + S5_SLAB_STATE, t * LANES:(t + 1) * LANES] = g_re.astype(BF16)
            wbig_ref[0, r0 + S5_SLAB_STATE:r0 + 2 * S5_SLAB_STATE, t * LANES:(t + 1) * LANES] = (-g_im).astype(BF16)


def _s5_operators(log_dt, a_re, a_im, b_re, b_im, c_re, c_im):
    gs = S5_SLAB_GROUPS
    eye = jnp.eye(gs, dtype=F32)

    def lanes(v):
        return v.reshape(2, S5_SLABS, gs * S5_STATE)

    par = jnp.stack([lanes(jnp.broadcast_to(log_dt[:, :, None], a_re.shape)), lanes(a_re), lanes(a_im)], axis=2)
    par = jnp.pad(par, ((0, 0), (0, 0), (0, SUBLANES - 3), (0, 0))).transpose(1, 0, 2, 3)

    def embed_bt(b):
        b = b.reshape(2, S5_SLABS, gs, S5_STATE, S5_GROUP)
        e = b.transpose(0, 1, 2, 4, 3)[:, :, :, :, None, :] * eye[None, None, :, None, :, None]
        return e.reshape(2, S5_SLABS, gs * S5_GROUP, gs * S5_STATE).transpose(1, 0, 2, 3)

    def embed_c(c):
        c = c.reshape(2, S5_SLABS, gs, S5_GROUP, S5_STATE)
        e = c[:, :, :, :, None, :] * eye[None, None, :, None, :, None]
        return e.reshape(2, S5_SLABS, gs * S5_GROUP, gs * S5_STATE).transpose(1, 0, 2, 3)

    bt = jnp.stack([embed_bt(b_re), embed_bt(b_im)], axis=2)
    cm = jnp.stack([embed_c(c_re), embed_c(c_im)], axis=2)
    ct = cm.transpose(0, 1, 2, 4, 3)

    slab5 = lambda shape: pl.BlockSpec((1,) + shape, lambda k: (k, 0, 0, 0, 0))
    par_spec = pl.BlockSpec((1, 2, SUBLANES, S5_SLAB_STATE), lambda k: (k, 0, 0, 0))
    bt_spec = slab5((2, 2, LANES, S5_SLAB_STATE))
    wst, a16 = pl.pallas_call(
        _s5_state_prep_kernel,
        grid=(S5_SLABS,),
        in_specs=[par_spec, bt_spec],
        out_specs=[pl.BlockSpec((1, S5_XK, S5_HK), lambda k: (k, 0, 0)),
                   pl.BlockSpec((1, SUBLANES, S5_HK), lambda k: (k, 0, 0))],
        out_shape=[jax.ShapeDtypeStruct((S5_SLABS, S5_XK, S5_HK), BF16),
                   jax.ShapeDtypeStruct((S5_SLABS, SUBLANES, S5_HK), F32)],
        compiler_params=_params(("arbitrary",), 2 * S5_XK * S5_HK * 2 + 8 * 1024 * 1024),
        name="s5_state_operator",
    )(par, bt)
    wbig = pl.pallas_call(
        _s5_output_prep_kernel,
        grid=(S5_SLABS,),
        in_specs=[par_spec, bt_spec, bt_spec, slab5((2, 2, S5_SLAB_STATE, LANES))],
        out_specs=pl.BlockSpec((1, S5_XK + S5_HK, S5_XK), lambda k: (k, 0, 0)),
        out_shape=jax.ShapeDtypeStruct((S5_SLABS, S5_XK + S5_HK, S5_XK), BF16),
        compiler_params=_params(("arbitrary",), 2 * (S5_XK + S5_HK) * S5_XK * 2 + 8 * 1024 * 1024),
        name="s5_output_operator",
    )(par, bt, cm, ct)
    return wst, a16, wbig


def _s5_states_kernel(*refs, n_ctx, n_lat):
    xc_refs = refs[:S5_CHUNK]
    xl_refs = refs[S5_CHUNK:2 * S5_CHUNK]
    wst_ref, a16_ref, h_ref, s_s, h_s = refs[2 * S5_CHUNK:]
    x = jnp.concatenate([jnp.concatenate([r[0] for r in xc_refs], axis=1),
                         jnp.concatenate([r[0] for r in xl_refs], axis=1)], axis=0)
    s_s[...] = _dot(x, wst_ref[0])
    p = S5_SLAB_STATE
    af_re = a16_ref[0, 0:1, 0:p]
    af_im = a16_ref[0, 0:1, p:2 * p]
    ar_re = a16_ref[0, 0:1, 2 * p:3 * p]
    ar_im = a16_ref[0, 0:1, 3 * p:4 * p]
    total = n_ctx + n_lat

    def step(i, carry):
        f_re, f_im, r_re, r_im = carry
        h_s[pl.ds(i, 1), 0:p] = f_re
        h_s[pl.ds(i, 1), p:2 * p] = f_im
        s_re = s_s[pl.ds(i, 1), 0:p]
        s_im = s_s[pl.ds(i, 1), p:2 * p]
        f_re, f_im = af_re * f_re - af_im * f_im + s_re, af_re * f_im + af_im * f_re + s_im
        j = jnp.where(i < n_ctx, n_ctx - 1 - i, total + n_ctx - 1 - i)
        h_s[pl.ds(j, 1), 2 * p:3 * p] = r_re
        h_s[pl.ds(j, 1), 3 * p:4 * p] = r_im
        s_re = s_s[pl.ds(j, 1), 2 * p:3 * p]
        s_im = s_s[pl.ds(j, 1), 3 * p:4 * p]
        r_re, r_im = ar_re * r_re - ar_im * r_im + s_re, ar_re * r_im + ar_im * r_re + s_im
        return f_re, f_im, r_re, r_im

    zero = jnp.zeros((1, p), F32)
    lax.fori_loop(0, total, step, (zero, zero, zero, zero), unroll=8)
    h_ref[0, 0] = h_s[pl.ds(n_ctx, n_lat), :].astype(BF16)


def _s5_states(uc_flat, ul_flat, wst, a16):
    bsz, n_ctx, _ = uc_flat.shape
    n_lat = ul_flat.shape[1]
    piece = lambda rows, s: pl.BlockSpec((1, rows, LANES), lambda k, b, s=s: (b, 0, s * S5_SLABS + k))
    in_specs = ([piece(n_ctx, s) for s in range(S5_CHUNK)] + [piece(n_lat, s) for s in range(S5_CHUNK)]
                + [pl.BlockSpec((1, S5_XK, S5_HK), lambda k, b: (k, 0, 0)),
                   pl.BlockSpec((1, SUBLANES, S5_HK), lambda k, b: (k, 0, 0))])
    total = n_ctx + n_lat
    nbytes = 2 * S5_XK * S5_HK * 2 + 2 * total * S5_XK * 2 + 3 * total * S5_HK * 4 + 2 * n_lat * S5_HK * 2
    return pl.pallas_call(
        functools.partial(_s5_states_kernel, n_ctx=n_ctx, n_lat=n_lat),
        grid=(S5_SLABS, bsz),
        in_specs=in_specs,
        out_specs=pl.BlockSpec((1, 1, n_lat, S5_HK), lambda k, b: (b, k, 0, 0)),
        out_shape=jax.ShapeDtypeStruct((bsz, S5_SLABS, n_lat, S5_HK), BF16),
        scratch_shapes=[pltpu.VMEM((total, S5_HK), F32), pltpu.VMEM((total, S5_HK), F32)],
        compiler_params=_params(("arbitrary", "arbitrary"), nbytes),
        name="s5_states",
    )(*([uc_flat] * S5_CHUNK + [ul_flat] * S5_CHUNK + [wst, a16]))


def _s5_readout_kernel(*refs):
    x_refs = refs[:S5_CHUNK]
    h_ref, w_ref, y_ref = refs[S5_CHUNK:]
    lhs = jnp.concatenate([r[0] for r in x_refs] + [h_ref[0, 0]], axis=1)
    y = _dot(lhs, w_ref[0])
    chunks = y.shape[0]
    for t in range(S5_CHUNK):
        y_ref[0, pl.ds(t, chunks, stride=S5_CHUNK), :] = y[:, t * LANES:(t + 1) * LANES]


def _s5_readout(ul_flat, h_in, wbig):
    bsz, n_lat, _ = ul_flat.shape
    piece = lambda s: pl.BlockSpec((1, n_lat, LANES), lambda k, b, s=s: (b, 0, s * S5_SLABS + k))
    nbytes = 2 * (S5_XK + S5_HK) * S5_XK * 2 + 2 * n_lat * (S5_XK + S5_HK) * 2 * 2 + n_lat * S5_XK * 4 * 3
    return pl.pallas_call(
        _s5_readout_kernel,
        grid=(S5_SLABS, bsz),
        in_specs=[piece(s) for s in range(S5_CHUNK)]
                 + [pl.BlockSpec((1, 1, n_lat, S5_HK), lambda k, b: (b, k, 0, 0)),
                    pl.BlockSpec((1, S5_XK + S5_HK, S5_XK), lambda k, b: (k, 0, 0))],
        out_specs=pl.BlockSpec((1, n_lat * S5_CHUNK, LANES), lambda k, b: (b, 0, k)),
        out_shape=jax.ShapeDtypeStruct((bsz, n_lat * S5_CHUNK, S5_WIDTH), F32),
        compiler_params=_params(("arbitrary", "arbitrary"), nbytes),
        name="s5_readout",
    )(*([ul_flat] * S5_CHUNK + [h_in, wbig]))


SC_HALO = 16


def _outproj1_kernel(y_ref, u_ref, gb_ref, p_ref, pprev_ref, pnext_ref, x_ref, gate_ref,
                     dskip_ref, wglu_ref, bglu_ref, cw_ref, cb_ref, w_ref, lg_ref, lb_ref, o_ref, *, ctx_row):
    i = pl.program_id(1)
    tm = x_ref.shape[1]
    yc = y_ref[0] + dskip_ref[...] * u_ref[0].astype(F32)
    z = _gelu_tanh(yc)
    y_c = z * _sigmoid(_dot(z.astype(BF16), wglu_ref[...]) + bglu_ref[...])
    p = p_ref[0].astype(F32)
    row = lax.broadcasted_iota(jnp.int32, (tm, 1), 0)
    prev_row = jnp.where(i > 0, pprev_ref[0, SC_HALO - 1:SC_HALO, :].astype(F32), 0.0)
    next_row = jnp.where(i < pl.num_programs(1) - 1, pnext_ref[0, 0:1, :].astype(F32), 0.0)
    p_dn = jnp.where(row == 0, prev_row, pltpu.roll(p, 1, 0))
    p_up = jnp.where(row == tm - 1, next_row, pltpu.roll(p, tm - 1, 0))
    conv = cb_ref[...] + cw_ref[0:1, :] * p_dn + cw_ref[1:2, :] * p + cw_ref[2:3, :] * p_up
    y_d = gb_ref[0].astype(F32) * conv
    y = jnp.concatenate([y_c.astype(BF16), y_d.astype(BF16)], axis=1)
    out = _dot(y, w_ref[...])
    o_ref[0] = _layer_norm(ALPHA * x_ref[0] + _mod_row(gate_ref, ctx_row) * out, lg_ref[...], lb_ref[...])


def _outproj1(y, u, gb, p, x, mod, ctx_row, d_skip, w_glu, b_glu, conv_w, conv_b, w_out, ln_g, ln_b, tm):
    bsz, n, d = x.shape
    tok = lambda b, i: (b, i, 0)
    const = lambda b, i: (0, 0)
    per = tm // SC_HALO
    last = n // SC_HALO - 1
    nbytes = (2 * 2 * tm * d * 4 + w_out.size * 2 + 2 * 2 * tm * (2 * S5_WIDTH + 2 * SC_WIDTH) * 2
              + 6 * tm * SC_WIDTH * 4 + 2 * tm * d * 4)
    return pl.pallas_call(
        functools.partial(_outproj1_kernel, ctx_row=ctx_row),
        grid=(bsz, n // tm),
        in_specs=[pl.BlockSpec((1, tm, S5_WIDTH), tok), pl.BlockSpec((1, tm, S5_WIDTH), tok),
                  pl.BlockSpec((1, tm, SC_WIDTH), tok), pl.BlockSpec((1, tm, SC_WIDTH), tok),
                  pl.BlockSpec((1, SC_HALO, SC_WIDTH), lambda b, i: (b, jnp.maximum(i * per - 1, 0), 0)),
                  pl.BlockSpec((1, SC_HALO, SC_WIDTH), lambda b, i: (b, jnp.minimum((i + 1) * per, last), 0)),
                  pl.BlockSpec((1, tm, d), tok), _mod_spec(mod, MOD_G1),
                  pl.BlockSpec((1, S5_WIDTH), const), _resident(w_glu.shape, const),
                  pl.BlockSpec((1, S5_WIDTH), const),
                  pl.BlockSpec((SC_CONV, SC_WIDTH), const), pl.BlockSpec((1, SC_WIDTH), const),
                  _resident(w_out.shape, const),
                  pl.BlockSpec((1, d), const), pl.BlockSpec((1, d), const)],
        out_specs=pl.BlockSpec((1, tm, d), tok),
        out_shape=jax.ShapeDtypeStruct(x.shape, F32),
        compiler_params=_params(("arbitrary", "arbitrary"), nbytes),
        name="outproj1_ln",
    )(y, u, gb, p, p, p, x, mod, d_skip, w_glu, b_glu, conv_w, conv_b, w_out, ln_g, ln_b)


def _rope_tables(n):
    t = np.arange(n)
    inv = ROPE_BASE ** (-np.arange(ROPE_FREQS, dtype=np.float64) / ROPE_FREQS)
    ang_r = (t // GRID_W)[:, None] * inv[None, :]
    ang_c = (t % GRID_W)[:, None] * inv[None, :]
    cos = np.concatenate([np.cos(ang_r), np.cos(ang_r), np.cos(ang_c), np.cos(ang_c)], axis=1)
    sin = np.concatenate([-np.sin(ang_r), np.sin(ang_r), -np.sin(ang_c), np.sin(ang_c)], axis=1)
    return jnp.asarray(cos, F32), jnp.asarray(sin, F32)


TM_LATENT = 512
TM_PROJ = 1024
TM_CONTEXT = 256
FFN_TH = 2048


def kernel(x, c, ctx, c_ctx, mod_w, mod_b, ln1_g, ln1_b, ln2_g, ln2_b, ffn_w1, ffn_b1, ffn_w2, ffn_b2,
           ab_w_in, ab_w_out, lru_conv_w, lru_conv_b, lru_w_a, lru_b_a, lru_w_x, lru_b_x, lru_lam, att_sink,
           cd_w_in, cd_w_out, s5_log_dt, s5_a_re, s5_a_im, s5_b_re, s5_b_im, s5_c_re, s5_c_im, s5_d,
           s5_w_glu, s5_b_glu, sc_conv_w, sc_conv_b):
    bsz, n, d = x.shape
    lc = ctx.shape[1]
    assert n % TM_PROJ == 0 and n % TM_LATENT == 0 and lc % TM_CONTEXT == 0 and (bsz * lc) % TM_LATENT == 0
    assert n % GRID_W == 0

    pad_rows = SUBLANES - (bsz + 1) % SUBLANES if (bsz + 1) % SUBLANES else 0
    cc = jnp.concatenate([c, c_ctx[None, :], jnp.zeros((pad_rows, d), F32)], axis=0)
    mods = {0: _modulation(cc, mod_w, mod_b, 0)}
    ctx_row = bsz

    row2 = lambda v: v.reshape(1, -1)
    rope_tabs = _rope_tables(n)

    side = {}

    def weight(stack, name, idx):
        return side.pop((name, idx)) if (name, idx) in side else _to_bf16(stack, idx)

    def hosted(call, own, jobs):
        outs = call(casts=[(stack, idx) for _, idx, stack in jobs])
        side.update({(name, idx): w for (name, idx, _), w in zip(jobs, outs[own:own + len(jobs)])})
        return outs[:own] + outs[own + len(jobs):]

    for i in range(DEPTH):
        last = i == DEPTH - 1
        j = i // 2
        mod = mods.pop(i) if i in mods else _modulation(cc, mod_w, mod_b, i)
        todo = [("w1", i, ffn_w1), ("w2", i, ffn_w2)]
        if not last:
            nj = (i + 1) // 2
            todo += [("w1", i + 1, ffn_w1), ("w2", i + 1, ffn_w2)]
            todo += ([("ab_in", nj, ab_w_in), ("ab_out", nj, ab_w_out)] if (i + 1) % 2 == 0
                     else [("cd_in", nj, cd_w_in), ("cd_out", nj, cd_w_out)])
        todo = [job for job in todo if (job[0], job[1]) not in side]
        if i % 2 == 0:
            w_in = weight(ab_w_in, "ab_in", j)
            w_out = weight(ab_w_out, "ab_out", j)
            u, gate, q, k, v = _inproj0(x, mod, None, w_in, rope_tabs, TM_PROJ)
            uc, gatec, qc, kc, vc = _inproj0(ctx, mod, ctx_row, w_in, None, TM_CONTEXT)
            w_gate = (0.5 * jnp.concatenate([lru_w_a[j, 0], lru_w_x[j, 0], lru_w_a[j, 1], lru_w_x[j, 1]],
                                            axis=-1)).astype(BF16)
            hb = lambda b: b.reshape(LRU_HEADS, 1, LRU_BLOCK)
            b_gate = 0.5 * jnp.concatenate([hb(lru_b_a[j, 0]), hb(lru_b_x[j, 0]), hb(lru_b_a[j, 1]),
                                            hb(lru_b_x[j, 1])], axis=-1)
            lam = jnp.concatenate([hb(lru_lam[j, 0]), hb(lru_lam[j, 1])], axis=-1)
            ahead_mod = [] if last else [(cc, mod_w, mod_b, i + 1)]
            *lru_out, = hosted(functools.partial(_rglru, u, gate, uc, gatec, lru_conv_w[j], row2(lru_conv_b[j]),
                                                 w_gate, b_gate, lam, mods=ahead_mod), 2, todo[0::2])
            ya, yac = lru_out[:2]
            mods.update({i + 1: m for m in lru_out[2:]})
            yb, = hosted(functools.partial(_window_attention, q, k, v, kc, vc, att_sink[j]), 1, todo[1::2])
            todo = []
            x = _outproj0(ya, yb, x, mod, None, w_out, row2(ln1_g[i]), row2(ln1_b[i]), TM_PROJ)
            if not last:
                ybc = _context_attention(qc, kc, vc, att_sink[j])
                ctx = _outproj0(yac, ybc, ctx, mod, ctx_row, w_out, row2(ln1_g[i]), row2(ln1_b[i]), TM_CONTEXT)
        else:
            assert last
            w_in = weight(cd_w_in, "cd_in", j)
            w_out = weight(cd_w_out, "cd_out", j)
            u, u_flat, gb, p = _inproj1(x, mod, None, w_in, TM_PROJ, True)
            (uc_flat,) = _inproj1(ctx, mod, ctx_row, w_in, TM_CONTEXT, False)
            wst, a16, wbig = _s5_operators(s5_log_dt[j], s5_a_re[j], s5_a_im[j], s5_b_re[j], s5_b_im[j],
                                           s5_c_re[j], s5_c_im[j])
            h_in = _s5_states(uc_flat, u_flat, wst, a16)
            y = _s5_readout(u_flat, h_in, wbig)
            x = _outproj1(y, u, gb, p, x, mod, None, row2(s5_d[j]), s5_w_glu[j].astype(BF16), row2(s5_b_glu[j]),
                          sc_conv_w[j], row2(sc_conv_b[j]), w_out, row2(ln1_g[i]), row2(ln1_b[i]), TM_LATENT)
        w1 = weight(ffn_w1, "w1", i)
        w2 = weight(ffn_w2, "w2", i)
        ahead = [job for job in todo if job[1] != i or job[0] not in ("w1", "w2")]
        x, = hosted(functools.partial(_ffn, x, mod, None, w1, row2(ffn_b1[i]), w2, row2(ffn_b2[i]),
                                      row2(ln2_g[i]), row2(ln2_b[i]), TM_LATENT, FFN_TH), 1, ahead)
        if not last:
            rows = ctx.reshape(1, bsz * lc, d)
            rows, = _ffn(rows, mod, ctx_row, w1, row2(ffn_b1[i]), w2, row2(ffn_b2[i]), row2(ln2_g[i]),
                         row2(ln2_b[i]), TM_LATENT, FFN_TH)
            ctx = rows.reshape(bsz, lc, d)
    return x
```

```python
import functools
import math

import jax
import jax.numpy as jnp
import numpy as np
from jax import lax
from jax.experimental import pallas as pl
from jax.experimental.pallas import tpu as pltpu

F32 = jnp.float32
BF16 = jnp.bfloat16

D_MODEL = 2048
DEPTH = 2
GRID_W = 64
LRU_WIDTH = D_MODEL // 2
LRU_HEADS = 8
LRU_BLOCK = LRU_WIDTH // LRU_HEADS
LRU_CONV = 4
LRU_CONV_LEFT = 2
LRU_C = 8.0
ATT_HEAD_DIM = 128
ATT_Q_HEADS = (D_MODEL // 2) // ATT_HEAD_DIM
ATT_KV_HEADS = 2
ATT_GROUP = ATT_Q_HEADS // ATT_KV_HEADS
ATT_Q_WIDTH = ATT_Q_HEADS * ATT_HEAD_DIM
ATT_KV_WIDTH = ATT_KV_HEADS * ATT_HEAD_DIM
WINDOW = 128
ATT_BLOCK = 128
ROPE_BASE = 10000.0
ROPE_FREQS = ATT_HEAD_DIM // 4
S5_WIDTH = D_MODEL // 4
S5_GROUP = 16
S5_STATE = 64
SC_WIDTH = D_MODEL - S5_WIDTH
SC_CONV = 3
ALPHA = (2.0 * DEPTH) ** 0.25
LN_EPS = 1e-5
NEG_INF = -1e30
LOG2E = math.log2(math.e)

LANES = 128
SUBLANES = 8
V7X_VMEM_BYTES = 64 * 1024 * 1024
V7X_VMEM_BUDGET = V7X_VMEM_BYTES - 2 * 1024 * 1024

S5_CHUNK = 16
S5_SLAB_GROUPS = LANES // S5_GROUP
S5_SLABS = S5_WIDTH // LANES
S5_SLAB_STATE = S5_SLAB_GROUPS * S5_STATE
S5_XK = S5_CHUNK * LANES
S5_HK = 4 * S5_SLAB_STATE
S5_POW_ROWS = 64


def _vmem_limit(nbytes):
    return int(min(V7X_VMEM_BUDGET, max(nbytes * 3 // 2, 16 * 1024 * 1024)))


def _resident(shape, index_map):
    return pl.BlockSpec(shape, index_map, pipeline_mode=pl.Buffered(1))


def _params(sem, nbytes):
    return pltpu.CompilerParams(dimension_semantics=sem, vmem_limit_bytes=_vmem_limit(nbytes))


def _dot(a, b):
    return jnp.dot(a, b, preferred_element_type=F32)


def _dot_nt(a, b):
    return lax.dot_general(a, b, (((1,), (1,)), ((), ())), preferred_element_type=F32)


def _layer_norm(v, g, b):
    mu = jnp.mean(v, axis=-1, keepdims=True)
    c = v - mu
    var = jnp.mean(c * c, axis=-1, keepdims=True)
    return c * lax.rsqrt(var + LN_EPS) * g + b


def _gelu_tanh(x):
    return 0.5 * x * (1.0 + jnp.tanh(math.sqrt(2.0 / math.pi) * (x + 0.044715 * (x * x * x))))


def _sigmoid(x):
    return 0.5 * (1.0 + jnp.tanh(0.5 * x))


MOD_SH1, MOD_SC1, MOD_G1, MOD_SH2, MOD_SC2, MOD_G2 = range(6)


def _mod_spec(mod, chunk):
    _, rows, width = mod.shape
    return pl.BlockSpec((1, rows, width // 6), lambda *_: (0, 0, chunk))


def _mod_row(ref, ctx_row):
    row = pl.program_id(0) if ctx_row is None else ctx_row
    return ref[0, pl.ds(row, 1), :]


def _modulate(x, sc_ref, sh_ref, ctx_row):
    return (x * (1.0 + _mod_row(sc_ref, ctx_row)) + _mod_row(sh_ref, ctx_row)).astype(BF16)


CAST_BLOCK_BYTES = 8 * 1024 * 1024


def _cast_kernel(w_ref, o_ref):
    o_ref[...] = w_ref[0].astype(BF16)


def _to_bf16(w, layer):
    _, rows, cols = w.shape
    tr = rows
    while tr * cols * 4 > CAST_BLOCK_BYTES and tr % 2 == 0 and tr // 2 >= 2 * SUBLANES:
        tr //= 2
    return pl.pallas_call(
        _cast_kernel,
        grid=(rows // tr,),
        in_specs=[pl.BlockSpec((1, tr, cols), lambda i: (layer, i, 0))],
        out_specs=pl.BlockSpec((tr, cols), lambda i: (i, 0)),
        out_shape=jax.ShapeDtypeStruct((rows, cols), BF16),
        compiler_params=_params(("arbitrary",), 2 * tr * cols * 6),
        name="weight_cast",
    )(w)


CAST_ROWS = 2 * SUBLANES


def _silu_bf16(v):
    return (v * _sigmoid(v)).astype(BF16)


def _hosted_call(body, *, grid, in_specs, out_specs, out_shape, args, sem, nbytes, casts=(), mods=(), **kwargs):
    steps = math.prod(grid)

    def flat(*ids):
        step = 0
        for extent, i in zip(grid, ids):
            step = step * extent + i
        return step

    def upto(used):
        return lambda *ids: jnp.minimum(flat(*ids), used - 1)

    def run_cast(ins, out):
        out[...] = ins[0][0].astype(BF16)

    def run_mod(ins, out):
        cc_ref, w_ref, b_ref = ins
        out[0] = _dot(_silu_bf16(cc_ref[...]), w_ref[0].astype(BF16)) + b_ref[0]

    side_in, side_out, side_shape, side_args, jobs = [], [], [], [], []
    for w, layer in casts:
        _, rows, cols = w.shape
        rps = max(CAST_ROWS, rows // steps)
        assert rows % rps == 0 and rows // rps <= steps
        at = upto(rows // rps)
        side_in.append(pl.BlockSpec((1, rps, cols), lambda *ids, at=at, layer=layer: (layer, at(*ids), 0)))
        side_out.append(pl.BlockSpec((rps, cols), lambda *ids, at=at: (at(*ids), 0)))
        side_shape.append(jax.ShapeDtypeStruct((rows, cols), BF16))
        side_args.append(w)
        nbytes += 2 * rps * cols * (4 + 2)
        jobs.append((1, run_cast))
    for cc, mod_w, mod_b, layer in mods:
        depth, d, n = mod_w.shape
        width = LANES * max(1, n // LANES // steps)
        assert n % width == 0 and n // width <= steps
        at = upto(n // width)
        side_in += [pl.BlockSpec(cc.shape, lambda *ids: (0, 0)),
                    pl.BlockSpec((1, d, width), lambda *ids, at=at, layer=layer: (layer, 0, at(*ids))),
                    pl.BlockSpec((1, 1, width), lambda *ids, at=at, layer=layer: (layer, 0, at(*ids)))]
        side_out.append(pl.BlockSpec((1, cc.shape[0], width), lambda *ids, at=at: (0, 0, at(*ids))))
        side_shape.append(jax.ShapeDtypeStruct((1, cc.shape[0], n), F32))
        side_args += [cc, mod_w, mod_b.reshape(depth, 1, n)]
        nbytes += 2 * d * width * 4 + d * width * 2
        jobs.append((3, run_mod))
    n_in, n_out, n_side_in, n_side_out = len(in_specs), len(out_specs), len(side_in), len(side_out)

    def kernel(*refs):
        ins = refs[:n_in]
        s_ins = refs[n_in:n_in + n_side_in]
        outs = refs[n_in + n_side_in:n_in + n_side_in + n_out]
        s_outs = refs[n_in + n_side_in + n_out:n_in + n_side_in + n_out + n_side_out]
        pos = 0
        for (arity, run), out in zip(jobs, s_outs):
            run(s_ins[pos:pos + arity], out)
            pos += arity
        body(*ins, *outs, *refs[n_in + n_side_in + n_out + n_side_out:])

    return pl.pallas_call(
        kernel, grid=grid, in_specs=list(in_specs) + side_in, out_specs=list(out_specs) + side_out,
        out_shape=list(out_shape) + side_shape, compiler_params=_params(sem, nbytes), **kwargs,
    )(*args, *side_args)


def _mod_kernel(cc_ref, w_ref, b_ref, o_ref):
    o_ref[0] = _dot(_silu_bf16(cc_ref[...]), w_ref[0].astype(BF16)) + b_ref[0]


def _modulation(cc, mod_w, mod_b, layer):
    depth, d, n = mod_w.shape
    tn = 1024
    rows = cc.shape[0]
    return pl.pallas_call(
        _mod_kernel,
        grid=(n // tn,),
        in_specs=[pl.BlockSpec((rows, d), lambda j: (0, 0)),
                  pl.BlockSpec((1, d, tn), lambda j: (layer, 0, j)),
                  pl.BlockSpec((1, 1, tn), lambda j: (layer, 0, j))],
        out_specs=pl.BlockSpec((1, rows, tn), lambda j: (0, 0, j)),
        out_shape=jax.ShapeDtypeStruct((1, rows, n), F32),
        compiler_params=_params(("arbitrary",), 2 * d * tn * 4 + d * tn * 2),
        name="modulation",
    )(cc, mod_w, mod_b.reshape(depth, 1, n))


def _rope(x, cos, sin_signed, heads):
    lane = lax.broadcasted_iota(jnp.int32, (x.shape[0], ATT_HEAD_DIM), 1)
    first = (lane % (2 * ROPE_FREQS)) < ROPE_FREQS
    out = []
    for h in range(heads):
        xs = x[:, h * ATT_HEAD_DIM:(h + 1) * ATT_HEAD_DIM]
        swapped = jnp.where(first, pltpu.roll(xs, ATT_HEAD_DIM - ROPE_FREQS, 1), pltpu.roll(xs, ROPE_FREQS, 1))
        out.append(xs * cos + swapped * sin_signed)
    return out


def _inproj0_kernel(*refs, rope, ctx_row):
    if rope:
        (x_ref, sc_ref, sh_ref, cos_ref, sin_ref, wu_ref, wg_ref, wq_ref, wk_ref, wv_ref,
         u_ref, g_ref, q_ref, k_ref, v_ref) = refs
    else:
        (x_ref, sc_ref, sh_ref, wu_ref, wg_ref, wq_ref, wk_ref, wv_ref,
         u_ref, g_ref, q_ref, k_ref, v_ref) = refs
    h = _modulate(x_ref[0], sc_ref, sh_ref, ctx_row)
    q = _dot(h, wq_ref[...]) * (ATT_HEAD_DIM ** -0.5 * LOG2E)
    k = _dot(h, wk_ref[...])
    if rope:
        cos = cos_ref[...]
        sin = sin_ref[...]
        for hd, piece in enumerate(_rope(q, cos, sin, ATT_Q_HEADS)):
            q_ref[0, :, hd * ATT_HEAD_DIM:(hd + 1) * ATT_HEAD_DIM] = piece.astype(BF16)
        for hd, piece in enumerate(_rope(k, cos, sin, ATT_KV_HEADS)):
            k_ref[0, :, hd * ATT_HEAD_DIM:(hd + 1) * ATT_HEAD_DIM] = piece.astype(BF16)
    else:
        q_ref[0] = q.astype(BF16)
        k_ref[0] = k.astype(BF16)
    u = _dot(h, wu_ref[...])
    g = _dot(h, wg_ref[...])
    for hd in range(LRU_HEADS):
        u_ref[0, hd] = u[:, hd * LRU_BLOCK:(hd + 1) * LRU_BLOCK].astype(BF16)
        g_ref[0, hd] = g[:, hd * LRU_BLOCK:(hd + 1) * LRU_BLOCK].astype(BF16)
    v_ref[0] = _dot(h, wv_ref[...]).astype(BF16)


def _inproj0(x, mod, ctx_row, w_in, rope_tabs, tm):
    bsz, n, d = x.shape
    rope = rope_tabs is not None
    tok = lambda b, i: (b, i, 0)
    in_specs = [pl.BlockSpec((1, tm, d), tok), _mod_spec(mod, MOD_SC1), _mod_spec(mod, MOD_SH1)]
    args = [x, mod, mod]
    if rope:
        in_specs += [pl.BlockSpec((tm, ATT_HEAD_DIM), lambda b, i: (i, 0))] * 2
        args += list(rope_tabs)
    kv_at = (2 * LRU_WIDTH + ATT_Q_WIDTH) // ATT_KV_WIDTH
    in_specs += [_resident((d, LRU_WIDTH), lambda b, i: (0, 0)), _resident((d, LRU_WIDTH), lambda b, i: (0, 1)),
                 _resident((d, ATT_Q_WIDTH), lambda b, i: (0, 2)),
                 _resident((d, ATT_KV_WIDTH), lambda b, i: (0, kv_at)),
                 _resident((d, ATT_KV_WIDTH), lambda b, i: (0, kv_at + 1))]
    args += [w_in] * 5
    head_major = jax.ShapeDtypeStruct((bsz, LRU_HEADS, n, LRU_BLOCK), BF16)
    head_spec = pl.BlockSpec((1, LRU_HEADS, tm, LRU_BLOCK), lambda b, i: (b, 0, i, 0))
    nbytes = 2 * tm * d * 4 + 2 * w_in.size + 2 * 2 * tm * w_in.shape[1] + tm * w_in.shape[1] * 4
    return pl.pallas_call(
        functools.partial(_inproj0_kernel, rope=rope, ctx_row=ctx_row),
        grid=(bsz, n // tm),
        in_specs=in_specs,
        out_specs=[head_spec, head_spec,
                   pl.BlockSpec((1, tm, ATT_Q_WIDTH), tok),
                   pl.BlockSpec((1, tm, ATT_KV_WIDTH), tok),
                   pl.BlockSpec((1, tm, ATT_KV_WIDTH), tok)],
        out_shape=[head_major, head_major,
                   jax.ShapeDtypeStruct((bsz, n, ATT_Q_WIDTH), BF16),
                   jax.ShapeDtypeStruct((bsz, n, ATT_KV_WIDTH), BF16),
                   jax.ShapeDtypeStruct((bsz, n, ATT_KV_WIDTH), BF16)],
        compiler_params=_params(("arbitrary", "arbitrary"), nbytes),
        name="inproj0_rope" if rope else "inproj0_ctx",
    )(*args)


LRU_PAD = SUBLANES
LRU_TILE = 256


def _scan8(a, b, row, reverse):
    for k in (1, 2, 4):
        if reverse:
            keep = row < SUBLANES - k
            shift = SUBLANES - k
        else:
            keep = row >= k
            shift = k
        a_sh = jnp.where(keep, pltpu.roll(a, shift, 0), 1.0)
        b_sh = jnp.where(keep, pltpu.roll(b, shift, 0), 0.0)
        b = a * b_sh + b
        a = a * a_sh
    return a, b


def _scan8_rows(a, b, row, reverse):
    out_a, out_b = [], []
    for g in range(a.shape[0] // SUBLANES):
        sl = slice(g * SUBLANES, (g + 1) * SUBLANES)
        ag, bg = _scan8(a[sl], b[sl], row, reverse)
        out_a.append(ag)
        out_b.append(bg)
    return jnp.concatenate(out_a, axis=0), jnp.concatenate(out_b, axis=0)


def _lru_kernel(ul_ref, gl_ref, uc_ref, gc_ref, cw_ref, cb_ref, wg_ref, bg_ref, lam_ref,
                yl_ref, yc_ref, upad, af_s, bf_s, ar_s, br_s, cin_s, *, n_lat, n_ctx):
    cw = cw_ref[...]
    cb = cb_ref[...]
    wg = wg_ref[0]
    bg = bg_ref[0]
    lam = lam_ref[0]
    neg = -lam
    softplus = jnp.maximum(neg, 0.0) + jnp.log1p(jnp.exp(-jnp.abs(neg)))
    half_rate = (-0.5 * LRU_C) * softplus
    row = lax.broadcasted_iota(jnp.int32, (SUBLANES, LRU_BLOCK), 0)
    chains = ((af_s, bf_s, False), (ar_s, br_s, True))

    def coefficients(src_ref, n_rows, row_off):
        upad[pl.ds(0, LRU_PAD), :] = jnp.zeros((LRU_PAD, LRU_BLOCK), F32)
        upad[pl.ds(LRU_PAD, n_rows), :] = src_ref[0, 0].astype(F32)
        upad[pl.ds(LRU_PAD + n_rows, LRU_PAD), :] = jnp.zeros((LRU_PAD, LRU_BLOCK), F32)

        def tile(i, carry):
            t0 = pl.multiple_of(i * LRU_TILE, LRU_TILE)
            xp = upad[pl.ds(t0, LRU_TILE + 2 * LRU_PAD), :]
            conv = cb
            for k in range(LRU_CONV):
                o = LRU_PAD - LRU_CONV_LEFT + k
                conv = conv + cw[k:k + 1, :] * xp[o:o + LRU_TILE, :]
            z = _dot(conv.astype(BF16), wg) + bg
            half_conv = 0.5 * conv
            for d, (a_ref, b_ref, reverse) in enumerate(chains):
                t_a = jnp.tanh(z[:, (2 * d) * LRU_BLOCK:(2 * d + 1) * LRU_BLOCK])
                t_x = jnp.tanh(z[:, (2 * d + 1) * LRU_BLOCK:(2 * d + 2) * LRU_BLOCK])
                rate = half_rate[:, d * LRU_BLOCK:(d + 1) * LRU_BLOCK]
                a = jnp.exp(rate + rate * t_a)
                gated = half_conv + half_conv * t_x
                b = jnp.sqrt(1.0 - a * a) * gated
                a, b = _scan8_rows(a, b, row, reverse)
                a_ref[pl.ds(row_off + t0, LRU_TILE), :] = a
                b_ref[pl.ds(row_off + t0, LRU_TILE), :] = b
            return carry

        tiles = n_rows // LRU_TILE
        lax.fori_loop(0, tiles, tile, 0, unroll=2 if tiles % 2 == 0 else 1)

    coefficients(uc_ref, n_ctx, 0)
    coefficients(ul_ref, n_lat, n_ctx)

    groups_ctx = n_ctx // SUBLANES
    groups = (n_ctx + n_lat) // SUBLANES
    fa = af_s[pl.ds(SUBLANES - 1, groups, stride=SUBLANES), :]
    fb = bf_s[pl.ds(SUBLANES - 1, groups, stride=SUBLANES), :]
    ra = ar_s[pl.ds(0, groups, stride=SUBLANES), :]
    rb = br_s[pl.ds(0, groups, stride=SUBLANES), :]
    zero = jnp.zeros((SUBLANES, LRU_BLOCK), F32)
    carry = zero
    for v in range(groups // SUBLANES):
        sl = slice(v * SUBLANES, (v + 1) * SUBLANES)
        a, b = _scan8(fa[sl], fb[sl], row, False)
        incl = b + a * carry
        cin_s[0, sl, :] = jnp.where(row == 0, carry, pltpu.roll(incl, 1, 0))
        carry = jnp.broadcast_to(incl[SUBLANES - 1:SUBLANES, :], incl.shape)
    carry = zero
    order = list(range(groups_ctx // SUBLANES - 1, -1, -1)) + list(range(groups // SUBLANES - 1,
                                                                        groups_ctx // SUBLANES - 1, -1))
    for v in order:
        sl = slice(v * SUBLANES, (v + 1) * SUBLANES)
        a, b = _scan8(ra[sl], rb[sl], row, True)
        incl = b + a * carry
        cin_s[1, sl, :] = jnp.where(row == SUBLANES - 1, carry, pltpu.roll(incl, SUBLANES - 1, 0))
        carry = jnp.broadcast_to(incl[0:1, :], incl.shape)

    def emit(g_ref, y_ref, n_rows, row_off):
        tile_groups = LRU_TILE // SUBLANES

        def tile(i, carry):
            t0 = pl.multiple_of(i * LRU_TILE, LRU_TILE)
            g0 = pl.multiple_of(row_off // SUBLANES + i * tile_groups, tile_groups)
            rows = pl.ds(row_off + t0, LRU_TILE)
            a_f, b_f, a_r, b_r = af_s[rows, :], bf_s[rows, :], ar_s[rows, :], br_s[rows, :]
            pieces = []
            for g in range(tile_groups):
                sl = slice(g * SUBLANES, (g + 1) * SUBLANES)
                c_f = jnp.broadcast_to(cin_s[0, pl.ds(g0 + g, 1), :], (SUBLANES, LRU_BLOCK))
                c_r = jnp.broadcast_to(cin_s[1, pl.ds(g0 + g, 1), :], (SUBLANES, LRU_BLOCK))
                pieces.append((b_f[sl] + a_f[sl] * c_f) + (b_r[sl] + a_r[sl] * c_r))
            h = jnp.concatenate(pieces, axis=0)
            gate = g_ref[0, 0, pl.ds(t0, LRU_TILE), :].astype(F32)
            y_ref[0, 0, pl.ds(t0, LRU_TILE), :] = (h * _gelu_tanh(gate)).astype(BF16)
            return carry

        lax.fori_loop(0, n_rows // LRU_TILE, tile, 0)

    emit(gc_ref, yc_ref, n_ctx, 0)
    emit(gl_ref, yl_ref, n_lat, n_ctx)


def _rglru(u_lat, g_lat, u_ctx, g_ctx, conv_w, conv_b, w_gate, b_gate, lam, casts=(), mods=()):
    bsz, heads, n_lat, blk = u_lat.shape
    n_ctx = u_ctx.shape[2]
    total = n_lat + n_ctx
    assert n_ctx % (SUBLANES * SUBLANES) == 0 and n_lat % LRU_TILE == 0 and n_ctx % LRU_TILE == 0
    seq = lambda n: pl.BlockSpec((1, 1, n, blk), lambda b, h: (b, h, 0, 0))
    per_head = lambda shape: pl.BlockSpec((1,) + shape, lambda b, h: (h, 0, 0))
    nbytes =4 * total * blk * 4 + (n_lat + 2 * LRU_PAD) * blk * 4 + 2 * 3 * 2 * total * blk * 2
    return _hosted_call(
        functools.partial(_lru_kernel, n_lat=n_lat, n_ctx=n_ctx),
        casts=casts, mods=mods,
        args=(u_lat, g_lat, u_ctx, g_ctx, conv_w, conv_b, w_gate, b_gate, lam),
        grid=(bsz, heads),
        in_specs=[seq(n_lat), seq(n_lat), seq(n_ctx), seq(n_ctx),
                  pl.BlockSpec((LRU_CONV, blk), lambda b, h: (0, h)),
                  pl.BlockSpec((1, blk), lambda b, h: (0, h)),
                  per_head((blk, 4 * blk)), per_head((1, 4 * blk)), per_head((1, 2 * blk))],
        out_specs=[seq(n_lat), seq(n_ctx)],
        out_shape=[jax.ShapeDtypeStruct(u_lat.shape, BF16), jax.ShapeDtypeStruct(u_ctx.shape, BF16)],
        scratch_shapes=[pltpu.VMEM((n_lat + 2 * LRU_PAD, blk), F32)]
                       + [pltpu.VMEM((total, blk), F32)] * 4
                       + [pltpu.VMEM((2, total // SUBLANES, blk), F32)],
        sem=("arbitrary", "arbitrary"), nbytes=nbytes,
        name="rglru",
    )


ATT_TQ = 512
ATT_BAND = 3 * ATT_BLOCK


def _dot_tn(a, b):
    return lax.dot_general(a, b, (((0,), (0,)), ((), ())), preferred_element_type=F32)


def _softmax_probs(scores, sink_row):
    m = sink_row
    for s in scores:
        m = jnp.maximum(m, jnp.max(s, axis=0, keepdims=True))
    probs = [jnp.exp2(s - m).astype(BF16) for s in scores]
    pad = 2 * SUBLANES
    first = lax.broadcasted_iota(jnp.int32, (pad, m.shape[1]), 0) == 0
    p_sink = jnp.where(first, jnp.exp2(sink_row - m), 0.0).astype(BF16)
    probs[-1] = jnp.concatenate([probs[-1], p_sink], axis=0)
    return probs


def _weighted_values(probs, values):
    dh = ATT_HEAD_DIM
    acc = None
    for idx, (p, v) in enumerate(zip(probs, values)):
        v_ext = jnp.concatenate([v, jnp.ones(v.shape, BF16)], axis=1)
        if idx == len(values) - 1:
            pad = p.shape[0] - v.shape[0]
            v_sink = jnp.concatenate([jnp.zeros((pad, dh), BF16), jnp.ones((pad, dh), BF16)], axis=1)
            v_ext = jnp.concatenate([v_ext, v_sink], axis=0)
        pv = _dot_tn(p, v_ext)
        acc = pv if acc is None else acc + pv
    return acc[:, :dh] / acc[:, dh:]


def _sink_rows(sink, reps):
    return jnp.repeat(sink.reshape(ATT_KV_HEADS, 1, ATT_GROUP), reps, axis=2)


def _attn_kernel(q_ref, k_ref, v_ref, kc_ref, vc_ref, sink_ref, o_ref, *, seq):
    tile = pl.program_id(1)
    blocks = ATT_TQ // ATT_BLOCK
    qcol = lax.broadcasted_iota(jnp.int32, (1, ATT_GROUP * ATT_BLOCK), 1) % ATT_BLOCK
    krow = lax.broadcasted_iota(jnp.int32, (ATT_BAND, 1), 0)

    def scores(u):
        i, g = divmod(u, ATT_KV_HEADS)
        q0 = (tile * blocks + i) * ATT_BLOCK
        start = pl.multiple_of(jnp.clip(q0 - ATT_BLOCK, 0, seq - ATT_BAND), ATT_BLOCK)
        valid = jnp.abs((q0 + qcol) - (start + krow)) <= WINDOW
        heads = [q_ref[0, i * ATT_BLOCK:(i + 1) * ATT_BLOCK,
                       (g * ATT_GROUP + r) * ATT_HEAD_DIM:(g * ATT_GROUP + r + 1) * ATT_HEAD_DIM]
                 for r in range(ATT_GROUP)]
        qs = jnp.concatenate(heads, axis=0)
        kv = slice(g * ATT_HEAD_DIM, (g + 1) * ATT_HEAD_DIM)
        s_loc = jnp.where(valid, _dot_nt(k_ref[0, pl.ds(start, ATT_BAND), kv], qs), NEG_INF)
        return start, [s_loc, _dot_nt(kc_ref[0, :, kv], qs)]

    def emit(u, start, p):
        i, g = divmod(u, ATT_KV_HEADS)
        kv = slice(g * ATT_HEAD_DIM, (g + 1) * ATT_HEAD_DIM)
        o = _weighted_values(p, [v_ref[0, pl.ds(start, ATT_BAND), kv], vc_ref[0, :, kv]])
        for r in range(ATT_GROUP):
            col = (g * ATT_GROUP + r) * ATT_HEAD_DIM
            o_ref[0, i * ATT_BLOCK:(i + 1) * ATT_BLOCK, col:col + ATT_HEAD_DIM] = (
                o[r * ATT_BLOCK:(r + 1) * ATT_BLOCK].astype(BF16))

    n_units = blocks * ATT_KV_HEADS
    sc, pr = {}, {}
    for step in range(n_units + 2):
        if step < n_units:
            sc[step] = scores(step)
        if 0 <= step - 1 < n_units:
            start, s_list = sc.pop(step - 1)
            pr[step - 1] = (start, _softmax_probs(s_list, sink_ref[(step - 1) % ATT_KV_HEADS] * LOG2E))
        if 0 <= step - 2 < n_units:
            start, p = pr.pop(step - 2)
            emit(step - 2, start, p)


def _window_attention(q, k, v, kc, vc, sink, casts=()):
    bsz, n, _ = q.shape
    sink_rows = _sink_rows(sink, ATT_BLOCK)
    lc = kc.shape[1]
    whole = lambda rows: pl.BlockSpec((1, rows, ATT_KV_WIDTH), lambda b, i: (b, 0, 0))
    nbytes = 2 * 2 * (2 * ATT_TQ * ATT_Q_WIDTH + 2 * n * ATT_KV_WIDTH + 2 * lc * ATT_KV_WIDTH) + 8 * 1024 * 1024
    return _hosted_call(
        functools.partial(_attn_kernel, seq=n),
        casts=casts,
        args=(q, k, v, kc, vc, sink_rows),
        grid=(bsz, n // ATT_TQ),
        in_specs=[pl.BlockSpec((1, ATT_TQ, ATT_Q_WIDTH), lambda b, i: (b, i, 0)),
                  whole(n), whole(n), whole(lc), whole(lc),
                  pl.BlockSpec(sink_rows.shape, lambda b, i: (0, 0, 0))],
        out_specs=[pl.BlockSpec((1, ATT_TQ, ATT_Q_WIDTH), lambda b, i: (b, i, 0))],
        out_shape=[jax.ShapeDtypeStruct(q.shape, BF16)],
        sem=("arbitrary", "arbitrary"), nbytes=nbytes,
        name="window_attention",
    )


def _ctx_attn_kernel(q_ref, kc_ref, vc_ref, sink_ref, o_ref):
    lc = q_ref.shape[1]
    for g in range(ATT_KV_HEADS):
        heads = [q_ref[0, :, (g * ATT_GROUP + r) * ATT_HEAD_DIM:(g * ATT_GROUP + r + 1) * ATT_HEAD_DIM]
                 for r in range(ATT_GROUP)]
        qs = jnp.concatenate(heads, axis=0)
        kv = slice(g * ATT_HEAD_DIM, (g + 1) * ATT_HEAD_DIM)
        o = _weighted_values(_softmax_probs([_dot_nt(kc_ref[0, :, kv], qs)], sink_ref[g] * LOG2E),
                             [vc_ref[0, :, kv]])
        for r in range(ATT_GROUP):
            col = (g * ATT_GROUP + r) * ATT_HEAD_DIM
            o_ref[0, :, col:col + ATT_HEAD_DIM] = o[r * lc:(r + 1) * lc].astype(BF16)


def _context_attention(qc, kc, vc, sink):
    bsz, lc, _ = qc.shape
    sink_rows = _sink_rows(sink, lc)
    kv_spec = pl.BlockSpec((1, lc, ATT_KV_WIDTH), lambda b: (b, 0, 0))
    q_spec = pl.BlockSpec((1, lc, ATT_Q_WIDTH), lambda b: (b, 0, 0))
    return pl.pallas_call(
        _ctx_attn_kernel,
        grid=(bsz,),
        in_specs=[q_spec, kv_spec, kv_spec, pl.BlockSpec(sink_rows.shape, lambda b: (0, 0, 0))],
        out_specs=q_spec,
        out_shape=jax.ShapeDtypeStruct(qc.shape, BF16),
        compiler_params=_params(("arbitrary",), 8 * 1024 * 1024),
        name="context_attention",
    )(qc, kc, vc, sink_rows)


OUT_PART_ROWS = 256


def _outproj0_kernel(ya_ref, yb_ref, x_ref, gate_ref, w_ref, lg_ref, lb_ref, o_ref, *, ctx_row):
    gate = _mod_row(gate_ref, ctx_row)
    rows_per = min(OUT_PART_ROWS, x_ref.shape[1])
    parts = x_ref.shape[1] // rows_per
    outs = {}
    for r in range(parts + 1):
        if r < parts:
            rows = slice(r * rows_per, (r + 1) * rows_per)
            y = jnp.concatenate([ya_ref[0, h, rows, :] for h in range(LRU_HEADS)] + [yb_ref[0, rows, :]], axis=1)
            outs[r] = _dot(y, w_ref[...])
        if r >= 1:
            rows = slice((r - 1) * rows_per, r * rows_per)
            o_ref[0, rows, :] = _layer_norm(ALPHA * x_ref[0, rows, :] + gate * outs.pop(r - 1), lg_ref[...], lb_ref[...])


def _outproj0(ya, yb, x, mod, ctx_row, w_out, ln_g, ln_b, tm):
    bsz, n, d = x.shape
    tok = lambda b, i: (b, i, 0)
    const = lambda b, i: (0, 0)
    nbytes = 2 * 2 * tm * d * 4 + w_out.size * 2 + 2 * 2 * tm * d * 2 + 2 * tm * d * 4
    return pl.pallas_call(
        functools.partial(_outproj0_kernel, ctx_row=ctx_row),
        grid=(bsz, n // tm),
        in_specs=[pl.BlockSpec((1, LRU_HEADS, tm, LRU_BLOCK), lambda b, i: (b, 0, i, 0)),
                  pl.BlockSpec((1, tm, ATT_Q_WIDTH), tok),
                  pl.BlockSpec((1, tm, d), tok),
                  _mod_spec(mod, MOD_G1),
                  _resident(w_out.shape, const),
                  pl.BlockSpec((1, d), const), pl.BlockSpec((1, d), const)],
        out_specs=pl.BlockSpec((1, tm, d), tok),
        out_shape=jax.ShapeDtypeStruct(x.shape, F32),
        compiler_params=_params(("arbitrary", "arbitrary"), nbytes),
        name="outproj0_ln",
    )(ya, yb, x, mod, w_out, ln_g, ln_b)


def _ffn_kernel(x_ref, sc_ref, sh_ref, gate_ref, w1_ref, b1_ref, w2_ref, b2_ref, lg_ref, lb_ref,
                o_ref, h_s, *, ctx_row):
    j = pl.program_id(2)

    @pl.when(j == 0)
    def _():
        h_s[...] = _modulate(x_ref[0], sc_ref, sh_ref, ctx_row)
        o_ref[0] = jnp.zeros(o_ref.shape[1:], F32)

    a = jnp.maximum(_dot(h_s[...], w1_ref[...]) + b1_ref[...], 0.0)
    o_ref[0] += _dot((a * a).astype(BF16), w2_ref[...])

    @pl.when(j == pl.num_programs(2) - 1)
    def _():
        f = o_ref[0] + b2_ref[...]
        o_ref[0] = _layer_norm(ALPHA * x_ref[0] + _mod_row(gate_ref, ctx_row) * f, lg_ref[...], lb_ref[...])


def _ffn(x, mod, ctx_row, w1, b1, w2, b2, ln_g, ln_b, tm, th, casts=()):
    bsz, n, d = x.shape
    hidden = w1.shape[1]
    tok = lambda b, i, j: (b, i, 0)
    const = lambda b, i, j: (0, 0)
    nbytes = 2 * 2 * tm * d * 4 + 2 * 2 * 2 * d * th * 2 + tm * d * 2 + 2 * tm * th * 4 + tm * d * 4
    return _hosted_call(
        functools.partial(_ffn_kernel, ctx_row=ctx_row),
        casts=casts,
        args=(x, mod, mod, mod, w1, b1, w2, b2, ln_g, ln_b),
        grid=(bsz, n // tm, hidden // th),
        in_specs=[pl.BlockSpec((1, tm, d), tok),
                  _mod_spec(mod, MOD_SC2), _mod_spec(mod, MOD_SH2), _mod_spec(mod, MOD_G2),
                  pl.BlockSpec((d, th), lambda b, i, j: (0, j)),
                  pl.BlockSpec((1, th), lambda b, i, j: (0, j)),
                  pl.BlockSpec((th, d), lambda b, i, j: (j, 0)),
                  pl.BlockSpec((1, d), const), pl.BlockSpec((1, d), const), pl.BlockSpec((1, d), const)],
        out_specs=[pl.BlockSpec((1, tm, d), tok)],
        out_shape=[jax.ShapeDtypeStruct(x.shape, F32)],
        scratch_shapes=[pltpu.VMEM((tm, d), BF16)],
        sem=("arbitrary", "arbitrary", "arbitrary"), nbytes=nbytes,
        name="ffn_ln",
    )


SC_PARTS = SC_WIDTH // S5_WIDTH


def _inproj1_kernel(*refs, full, ctx_row):
    x_ref, sc_ref, sh_ref, wu_ref = refs[:4]
    u_s = refs[-1]
    h = _modulate(x_ref[0], sc_ref, sh_ref, ctx_row)
    u = _dot(h, wu_ref[...])
    if full:
        wb_refs = refs[4:4 + SC_PARTS]
        wc_refs = refs[4 + SC_PARTS:4 + 2 * SC_PARTS]
        wx_refs = refs[4 + 2 * SC_PARTS:4 + 3 * SC_PARTS]
        u_ref, uflat_ref, gb_ref, p_ref = refs[4 + 3 * SC_PARTS:-1]
        u_ref[0] = u.astype(BF16)
        for c in range(SC_PARTS):
            cols = slice(c * S5_WIDTH, (c + 1) * S5_WIDTH)
            gb_ref[0, :, cols] = _dot(h, wb_refs[c][...]).astype(BF16)
            p_ref[0, :, cols] = (_dot(h, wc_refs[c][...]) * _dot(h, wx_refs[c][...])).astype(BF16)
    else:
        uflat_ref, = refs[4:-1]
    for k in range(S5_SLABS):
        u_s[k] = u[:, k * LANES:(k + 1) * LANES]
    chunks = u.shape[0] // S5_CHUNK
    for t in range(S5_CHUNK):
        for k in range(S5_SLABS):
            col = t * S5_WIDTH + k * LANES
            uflat_ref[0, :, col:col + LANES] = u_s[k, pl.ds(t, chunks, stride=S5_CHUNK), :].astype(BF16)


def _inproj1(x, mod, ctx_row, w_in, tm, full):
    bsz, n, d = x.shape
    tok = lambda b, i: (b, i, 0)
    n_blocks = 1 + 3 * SC_PARTS if full else 1
    flat_spec = pl.BlockSpec((1, tm // S5_CHUNK, S5_CHUNK * S5_WIDTH), tok)
    flat_shape = jax.ShapeDtypeStruct((bsz, n // S5_CHUNK, S5_CHUNK * S5_WIDTH), BF16)
    token = lambda w: (pl.BlockSpec((1, tm, w), tok), jax.ShapeDtypeStruct((bsz, n, w), BF16))
    outs = ([token(S5_WIDTH), (flat_spec, flat_shape), token(SC_WIDTH), token(SC_WIDTH)] if full
            else [(flat_spec, flat_shape)])
    ncols = n_blocks * S5_WIDTH
    nbytes = 2 * tm * d * 4 + 2 * d * ncols + 2 * 2 * tm * (ncols + S5_WIDTH) + tm * ncols * 4 + tm * S5_WIDTH * 4
    return pl.pallas_call(
        functools.partial(_inproj1_kernel, full=full, ctx_row=ctx_row),
        grid=(bsz, n // tm),
        in_specs=[pl.BlockSpec((1, tm, d), tok), _mod_spec(mod, MOD_SC1), _mod_spec(mod, MOD_SH1)]
                 + [_resident((d, S5_WIDTH), lambda b, i, c=c: (0, c)) for c in range(n_blocks)],
        out_specs=[spec for spec, _ in outs],
        out_shape=[shape for _, shape in outs],
        scratch_shapes=[pltpu.VMEM((S5_SLABS, tm, LANES), F32)],
        compiler_params=_params(("arbitrary", "arbitrary"), nbytes),
        name="inproj1" if full else "inproj1_ctx",
    )(x, mod, mod, *([w_in] * n_blocks))


def _s5_power_table(par_ref):
    rows = 2 * S5_POW_ROWS
    r = lax.broadcasted_iota(jnp.int32, (rows, S5_SLAB_STATE), 0)
    first = r < S5_POW_ROWS
    lag = (r % S5_POW_ROWS).astype(F32)
    pick = lambda i: jnp.where(first, par_ref[0, 0, i:i + 1, :], par_ref[0, 1, i:i + 1, :])
    dt = jnp.exp(pick(0))
    mag = jnp.exp(lag * dt * pick(1))
    ang = lag * dt * pick(2)
    return mag * jnp.cos(ang), mag * jnp.sin(ang)


def _s5_input_matrix(par_ref, bt_ref, d, pw_re, pw_im):
    a_re = par_ref[0, d, 1:2, :]
    a_im = par_ref[0, d, 2:3, :]
    ab_re = pw_re[d * S5_POW_ROWS + 1:d * S5_POW_ROWS + 2, :]
    ab_im = pw_im[d * S5_POW_ROWS + 1:d * S5_POW_ROWS + 2, :]
    den = a_re * a_re + a_im * a_im
    k_re = ((ab_re - 1.0) * a_re + ab_im * a_im) / den
    k_im = (ab_im * a_re - (ab_re - 1.0) * a_im) / den
    b_re = bt_ref[0, d, 0]
    b_im = bt_ref[0, d, 1]
    return k_re * b_re - k_im * b_im, k_re * b_im + k_im * b_re


def _cmul_row(x_re, x_im, p_re, p_im):
    return x_re * p_re - x_im * p_im, x_re * p_im + x_im * p_re


def _s5_state_prep_kernel(par_ref, bt_ref, wst_ref, a16_ref):
    pw_re, pw_im = _s5_power_table(par_ref)
    for d in range(2):
        bb_re, bb_im = _s5_input_matrix(par_ref, bt_ref, d, pw_re, pw_im)
        base = d * S5_POW_ROWS
        for lag in range(S5_CHUNK):
            e_re, e_im = _cmul_row(bb_re, bb_im, pw_re[base + lag:base + lag + 1, :],
                                   pw_im[base + lag:base + lag + 1, :])
            s = S5_CHUNK - 1 - lag if d == 0 else lag
            col = 2 * d * S5_SLAB_STATE
            wst_ref[0, s * LANES:(s + 1) * LANES, col:col + S5_SLAB_STATE] = e_re.astype(BF16)
            wst_ref[0, s * LANES:(s + 1) * LANES, col + S5_SLAB_STATE:col + 2 * S5_SLAB_STATE] = e_im.astype(BF16)
        row16 = base + S5_CHUNK
        a16_ref[0, :, 2 * d * S5_SLAB_STATE:(2 * d + 1) * S5_SLAB_STATE] = jnp.broadcast_to(
            pw_re[row16:row16 + 1, :], (SUBLANES, S5_SLAB_STATE))
        a16_ref[0, :, (2 * d + 1) * S5_SLAB_STATE:(2 * d + 2) * S5_SLAB_STATE] = jnp.broadcast_to(
            pw_im[row16:row16 + 1, :], (SUBLANES, S5_SLAB_STATE))


def _s5_output_prep_kernel(par_ref, bt_ref, c_ref, ct_ref, wbig_ref):
    pw_re, pw_im = _s5_power_table(par_ref)
    kern = []
    for d in range(2):
        bb_re, bb_im = _s5_input_matrix(par_ref, bt_ref, d, pw_re, pw_im)
        c_re = c_ref[0, d, 0].astype(BF16)
        c_im = c_ref[0, d, 1].astype(BF16)
        base = d * S5_POW_ROWS
        per_lag = []
        for lag in range(S5_CHUNK):
            e_re, e_im = _cmul_row(bb_re, bb_im, pw_re[base + lag:base + lag + 1, :],
                                   pw_im[base + lag:base + lag + 1, :])
            per_lag.append(_dot_nt(e_re.astype(BF16), c_re) - _dot_nt(e_im.astype(BF16), c_im))
        kern.append(per_lag)
    for s in range(S5_CHUNK):
        for t in range(S5_CHUNK):
            if s < t:
                blk = kern[0][t - s]
            elif s > t:
                blk = kern[1][s - t]
            else:
                blk = kern[0][0] + kern[1][0]
            wbig_ref[0, s * LANES:(s + 1) * LANES, t * LANES:(t + 1) * LANES] = blk.astype(BF16)
    pt_re = pw_re.T
    pt_im = pw_im.T
    for d in range(2):
        ct_re = ct_ref[0, d, 0]
        ct_im = ct_ref[0, d, 1]
        for t in range(S5_CHUNK):
            lag = t + 1 if d == 0 else S5_CHUNK - t
            col = d * S5_POW_ROWS + lag
            p_re = pt_re[:, col:col + 1]
            p_im = pt_im[:, col:col + 1]
            g_re = ct_re * p_re - ct_im * p_im
            g_im = ct_re * p_im + ct_im * p_re
            r0 = S5_XK + 2 * d * S5_SLAB_STATE
            wbig_ref[0, r0:r0 + S5_SLAB_STATE, t * LANES:(t + 1) * LANES] = g_re.astype(BF16)
            wbig_ref[0, r0 + S5_SLAB_STATE:r0 + 2 * S5_SLAB_STATE, t * LANES:(t + 1) * LANES] = (-g_im).astype(BF16)


def _s5_operators(log_dt, a_re, a_im, b_re, b_im, c_re, c_im):
    gs = S5_SLAB_GROUPS
    eye = jnp.eye(gs, dtype=F32)

    def lanes(v):
        return v.reshape(2, S5_SLABS, gs * S5_STATE)

    par = jnp.stack([lanes(jnp.broadcast_to(log_dt[:, :, None], a_re.shape)), lanes(a_re), lanes(a_im)], axis=2)
    par = jnp.pad(par, ((0, 0), (0, 0), (0, SUBLANES - 3), (0, 0))).transpose(1, 0, 2, 3)

    def embed_bt(b):
        b = b.reshape(2, S5_SLABS, gs, S5_STATE, S5_GROUP)
        e = b.transpose(0, 1, 2, 4, 3)[:, :, :, :, None, :] * eye[None, None, :, None, :, None]
        return e.reshape(2, S5_SLABS, gs * S5_GROUP, gs * S5_STATE).transpose(1, 0, 2, 3)

    def embed_c(c):
        c = c.reshape(2, S5_SLABS, gs, S5_GROUP, S5_STATE)
        e = c[:, :, :, :, None, :] * eye[None, None, :, None, :, None]
        return e.reshape(2, S5_SLABS, gs * S5_GROUP, gs * S5_STATE).transpose(1, 0, 2, 3)

    bt = jnp.stack([embed_bt(b_re), embed_bt(b_im)], axis=2)
    cm = jnp.stack([embed_c(c_re), embed_c(c_im)], axis=2)
    ct = cm.transpose(0, 1, 2, 4, 3)

    slab5 = lambda shape: pl.BlockSpec((1,) + shape, lambda k: (k, 0, 0, 0, 0))
    par_spec = pl.BlockSpec((1, 2, SUBLANES, S5_SLAB_STATE), lambda k: (k, 0, 0, 0))
    bt_spec = slab5((2, 2, LANES, S5_SLAB_STATE))
    wst, a16 = pl.pallas_call(
        _s5_state_prep_kernel,
        grid=(S5_SLABS,),
        in_specs=[par_spec, bt_spec],
        out_specs=[pl.BlockSpec((1, S5_XK, S5_HK), lambda k: (k, 0, 0)),
                   pl.BlockSpec((1, SUBLANES, S5_HK), lambda k: (k, 0, 0))],
        out_shape=[jax.ShapeDtypeStruct((S5_SLABS, S5_XK, S5_HK), BF16),
                   jax.ShapeDtypeStruct((S5_SLABS, SUBLANES, S5_HK), F32)],
        compiler_params=_params(("arbitrary",), 2 * S5_XK * S5_HK * 2 + 8 * 1024 * 1024),
        name="s5_state_operator",
    )(par, bt)
    wbig = pl.pallas_call(
        _s5_output_prep_kernel,
        grid=(S5_SLABS,),
        in_specs=[par_spec, bt_spec, bt_spec, slab5((2, 2, S5_SLAB_STATE, LANES))],
        out_specs=pl.BlockSpec((1, S5_XK + S5_HK, S5_XK), lambda k: (k, 0, 0)),
        out_shape=jax.ShapeDtypeStruct((S5_SLABS, S5_XK + S5_HK, S5_XK), BF16),
        compiler_params=_params(("arbitrary",), 2 * (S5_XK + S5_HK) * S5_XK * 2 + 8 * 1024 * 1024),
        name="s5_output_operator",
    )(par, bt, cm, ct)
    return wst, a16, wbig


def _s5_states_kernel(*refs, n_ctx, n_lat):
    xc_refs = refs[:S5_CHUNK]
    xl_refs = refs[S5_CHUNK:2 * S5_CHUNK]
    wst_ref, a16_ref, h_ref, s_s, h_s = refs[2 * S5_CHUNK:]
    x = jnp.concatenate([jnp.concatenate([r[0] for r in xc_refs], axis=1),
                         jnp.concatenate([r[0] for r in xl_refs], axis=1)], axis=0)
    s_s[...] = _dot(x, wst_ref[0])
    p = S5_SLAB_STATE
    af_re = a16_ref[0, 0:1, 0:p]
    af_im = a16_ref[0, 0:1, p:2 * p]
    ar_re = a16_ref[0, 0:1, 2 * p:3 * p]
    ar_im = a16_ref[0, 0:1, 3 * p:4 * p]
    total = n_ctx + n_lat

    def step(i, carry):
        f_re, f_im, r_re, r_im = carry
        h_s[pl.ds(i, 1), 0:p] = f_re
        h_s[pl.ds(i, 1), p:2 * p] = f_im
        s_re = s_s[pl.ds(i, 1), 0:p]
        s_im = s_s[pl.ds(i, 1), p:2 * p]
        f_re, f_im = af_re * f_re - af_im * f_im + s_re, af_re * f_im + af_im * f_re + s_im
        j = jnp.where(i < n_ctx, n_ctx - 1 - i, total + n_ctx - 1 - i)
        h_s[pl.ds(j, 1), 2 * p:3 * p] = r_re
        h_s[pl.ds(j, 1), 3 * p:4 * p] = r_im
        s_re = s_s[pl.ds(j, 1), 2 * p:3 * p]
        s_im = s_s[pl.ds(j, 1), 3 * p:4 * p]
        r_re, r_im = ar_re * r_re - ar_im * r_im + s_re, ar_re * r_im + ar_im * r_re + s_im
        return f_re, f_im, r_re, r_im

    zero = jnp.zeros((1, p), F32)
    lax.fori_loop(0, total, step, (zero, zero, zero, zero), unroll=8)
    h_ref[0, 0] = h_s[pl.ds(n_ctx, n_lat), :].astype(BF16)


def _s5_states(uc_flat, ul_flat, wst, a16):
    bsz, n_ctx, _ = uc_flat.shape
    n_lat = ul_flat.shape[1]
    piece = lambda rows, s: pl.BlockSpec((1, rows, LANES), lambda k, b, s=s: (b, 0, s * S5_SLABS + k))
    in_specs = ([piece(n_ctx, s) for s in range(S5_CHUNK)] + [piece(n_lat, s) for s in range(S5_CHUNK)]
                + [pl.BlockSpec((1, S5_XK, S5_HK), lambda k, b: (k, 0, 0)),
                   pl.BlockSpec((1, SUBLANES, S5_HK), lambda k, b: (k, 0, 0))])
    total = n_ctx + n_lat
    nbytes = 2 * S5_XK * S5_HK * 2 + 2 * total * S5_XK * 2 + 3 * total * S5_HK * 4 + 2 * n_lat * S5_HK * 2
    return pl.pallas_call(
        functools.partial(_s5_states_kernel, n_ctx=n_ctx, n_lat=n_lat),
        grid=(S5_SLABS, bsz),
        in_specs=in_specs,
        out_specs=pl.BlockSpec((1, 1, n_lat, S5_HK), lambda k, b: (b, k, 0, 0)),
        out_shape=jax.ShapeDtypeStruct((bsz, S5_SLABS, n_lat, S5_HK), BF16),
        scratch_shapes=[pltpu.VMEM((total, S5_HK), F32), pltpu.VMEM((total, S5_HK), F32)],
        compiler_params=_params(("arbitrary", "arbitrary"), nbytes),
        name="s5_states",
    )(*([uc_flat] * S5_CHUNK + [ul_flat] * S5_CHUNK + [wst, a16]))


def _s5_readout_kernel(*refs):
    x_refs = refs[:S5_CHUNK]
    h_ref, w_ref, y_ref = refs[S5_CHUNK:]
    lhs = jnp.concatenate([r[0] for r in x_refs] + [h_ref[0, 0]], axis=1)
    y = _dot(lhs, w_ref[0])
    chunks = y.shape[0]
    for t in range(S5_CHUNK):
        y_ref[0, pl.ds(t, chunks, stride=S5_CHUNK), :] = y[:, t * LANES:(t + 1) * LANES]


def _s5_readout(ul_flat, h_in, wbig):
    bsz, n_lat, _ = ul_flat.shape
    piece = lambda s: pl.BlockSpec((1, n_lat, LANES), lambda k, b, s=s: (b, 0, s * S5_SLABS + k))
    nbytes = 2 * (S5_XK + S5_HK) * S5_XK * 2 + 2 * n_lat * (S5_XK + S5_HK) * 2 * 2 + n_lat * S5_XK * 4 * 3
    return pl.pallas_call(
        _s5_readout_kernel,
        grid=(S5_SLABS, bsz),
        in_specs=[piece(s) for s in range(S5_CHUNK)]
                 + [pl.BlockSpec((1, 1, n_lat, S5_HK), lambda k, b: (b, k, 0, 0)),
                    pl.BlockSpec((1, S5_XK + S5_HK, S5_XK), lambda k, b: (k, 0, 0))],
        out_specs=pl.BlockSpec((1, n_lat * S5_CHUNK, LANES), lambda k, b: (b, 0, k)),
        out_shape=jax.ShapeDtypeStruct((bsz, n_lat * S5_CHUNK, S5_WIDTH), F32),
        compiler_params=_params(("arbitrary", "arbitrary"), nbytes),
        name="s5_readout",
    )(*([ul_flat] * S5_CHUNK + [h_in, wbig]))


SC_HALO = 16


def _outproj1_kernel(y_ref, u_ref, gb_ref, p_ref, pprev_ref, pnext_ref, x_ref, gate_ref,
                     dskip_ref, wglu_ref, bglu_ref, cw_ref, cb_ref, w_ref, lg_ref, lb_ref, o_ref, *, ctx_row):
    i = pl.program_id(1)
    tm = x_ref.shape[1]
    yc = y_ref[0] + dskip_ref[...] * u_ref[0].astype(F32)
    z = _gelu_tanh(yc)
    y_c = z * _sigmoid(_dot(z.astype(BF16), wglu_ref[...]) + bglu_ref[...])
    p = p_ref[0].astype(F32)
    row = lax.broadcasted_iota(jnp.int32, (tm, 1), 0)
    prev_row = jnp.where(i > 0, pprev_ref[0, SC_HALO - 1:SC_HALO, :].astype(F32), 0.0)
    next_row = jnp.where(i < pl.num_programs(1) - 1, pnext_ref[0, 0:1, :].astype(F32), 0.0)
    p_dn = jnp.where(row == 0, prev_row, pltpu.roll(p, 1, 0))
    p_up = jnp.where(row == tm - 1, next_row, pltpu.roll(p, tm - 1, 0))
    conv = cb_ref[...] + cw_ref[0:1, :] * p_dn + cw_ref[1:2, :] * p + cw_ref[2:3, :] * p_up
    y_d = gb_ref[0].astype(F32) * conv
    y = jnp.concatenate([y_c.astype(BF16), y_d.astype(BF16)], axis=1)
    out = _dot(y, w_ref[...])
    o_ref[0] = _layer_norm(ALPHA * x_ref[0] + _mod_row(gate_ref, ctx_row) * out, lg_ref[...], lb_ref[...])


def _outproj1(y, u, gb, p, x, mod, ctx_row, d_skip, w_glu, b_glu, conv_w, conv_b, w_out, ln_g, ln_b, tm):
    bsz, n, d = x.shape
    tok = lambda b, i: (b, i, 0)
    const = lambda b, i: (0, 0)
    per = tm // SC_HALO
    last = n // SC_HALO - 1
    nbytes = (2 * 2 * tm * d * 4 + w_out.size * 2 + 2 * 2 * tm * (2 * S5_WIDTH + 2 * SC_WIDTH) * 2
              + 6 * tm * SC_WIDTH * 4 + 2 * tm * d * 4)
    return pl.pallas_call(
        functools.partial(_outproj1_kernel, ctx_row=ctx_row),
        grid=(bsz, n // tm),
        in_specs=[pl.BlockSpec((1, tm, S5_WIDTH), tok), pl.BlockSpec((1, tm, S5_WIDTH), tok),
                  pl.BlockSpec((1, tm, SC_WIDTH), tok), pl.BlockSpec((1, tm, SC_WIDTH), tok),
                  pl.BlockSpec((1, SC_HALO, SC_WIDTH), lambda b, i: (b, jnp.maximum(i * per - 1, 0), 0)),
                  pl.BlockSpec((1, SC_HALO, SC_WIDTH), lambda b, i: (b, jnp.minimum((i + 1) * per, last), 0)),
                  pl.BlockSpec((1, tm, d), tok), _mod_spec(mod, MOD_G1),
                  pl.BlockSpec((1, S5_WIDTH), const), _resident(w_glu.shape, const),
                  pl.BlockSpec((1, S5_WIDTH), const),
                  pl.BlockSpec((SC_CONV, SC_WIDTH), const), pl.BlockSpec((1, SC_WIDTH), const),
                  _resident(w_out.shape, const),
                  pl.BlockSpec((1, d), const), pl.BlockSpec((1, d), const)],
        out_specs=pl.BlockSpec((1, tm, d), tok),
        out_shape=jax.ShapeDtypeStruct(x.shape, F32),
        compiler_params=_params(("arbitrary", "arbitrary"), nbytes),
        name="outproj1_ln",
    )(y, u, gb, p, p, p, x, mod, d_skip, w_glu, b_glu, conv_w, conv_b, w_out, ln_g, ln_b)


def _rope_tables(n):
    t = np.arange(n)
    inv = ROPE_BASE ** (-np.arange(ROPE_FREQS, dtype=np.float64) / ROPE_FREQS)
    ang_r = (t // GRID_W)[:, None] * inv[None, :]
    ang_c = (t % GRID_W)[:, None] * inv[None, :]
    cos = np.concatenate([np.cos(ang_r), np.cos(ang_r), np.cos(ang_c), np.cos(ang_c)], axis=1)
    sin = np.concatenate([-np.sin(ang_r), np.sin(ang_r), -np.sin(ang_c), np.sin(ang_c)], axis=1)
    return jnp.asarray(cos, F32), jnp.asarray(sin, F32)


TM_LATENT = 512
TM_PROJ = 1024
TM_CONTEXT = 256
FFN_TH = 2048


def kernel(x, c, ctx, c_ctx, mod_w, mod_b, ln1_g, ln1_b, ln2_g, ln2_b, ffn_w1, ffn_b1, ffn_w2, ffn_b2,
           ab_w_in, ab_w_out, lru_conv_w, lru_conv_b, lru_w_a, lru_b_a, lru_w_x, lru_b_x, lru_lam, att_sink,
           cd_w_in, cd_w_out, s5_log_dt, s5_a_re, s5_a_im, s5_b_re, s5_b_im, s5_c_re, s5_c_im, s5_d,
           s5_w_glu, s5_b_glu, sc_conv_w, sc_conv_b):
    bsz, n, d = x.shape
    lc = ctx.shape[1]
    assert n % TM_PROJ == 0 and n % TM_LATENT == 0 and lc % TM_CONTEXT == 0 and (bsz * lc) % TM_LATENT == 0
    assert n % GRID_W == 0

    pad_rows = SUBLANES - (bsz + 1) % SUBLANES if (bsz + 1) % SUBLANES else 0
    cc = jnp.concatenate([c, c_ctx[None, :], jnp.zeros((pad_rows, d), F32)], axis=0)
    mods = {0: _modulation(cc, mod_w, mod_b, 0)}
    ctx_row = bsz

    row2 = lambda v: v.reshape(1, -1)
    rope_tabs = _rope_tables(n)

    side = {}

    def weight(stack, name, idx):
        return side.pop((name, idx)) if (name, idx) in side else _to_bf16(stack, idx)

    def hosted(call, own, jobs):
        outs = call(casts=[(stack, idx) for _, idx, stack in jobs])
        side.update({(name, idx): w for (name, idx, _), w in zip(jobs, outs[own:own + len(jobs)])})
        return outs[:own] + outs[own + len(jobs):]

    for i in range(DEPTH):
        last = i == DEPTH - 1
        j = i // 2
        mod = mods.pop(i) if i in mods else _modulation(cc, mod_w, mod_b, i)
        todo = [("w1", i, ffn_w1), ("w2", i, ffn_w2)]
        if not last:
            nj = (i + 1) // 2
            todo += [("w1", i + 1, ffn_w1), ("w2", i + 1, ffn_w2)]
            todo += ([("ab_in", nj, ab_w_in), ("ab_out", nj, ab_w_out)] if (i + 1) % 2 == 0
                     else [("cd_in", nj, cd_w_in), ("cd_out", nj, cd_w_out)])
        todo = [job for job in todo if (job[0], job[1]) not in side]
        if i % 2 == 0:
            w_in = weight(ab_w_in, "ab_in", j)
            w_out = weight(ab_w_out, "ab_out", j)
            u, gate, q, k, v = _inproj0(x, mod, None, w_in, rope_tabs, TM_PROJ)
            uc, gatec, qc, kc, vc = _inproj0(ctx, mod, ctx_row, w_in, None, TM_CONTEXT)
            w_gate = (0.5 * jnp.concatenate([lru_w_a[j, 0], lru_w_x[j, 0], lru_w_a[j, 1], lru_w_x[j, 1]],
                                            axis=-1)).astype(BF16)
            hb = lambda b: b.reshape(LRU_HEADS, 1, LRU_BLOCK)
            b_gate = 0.5 * jnp.concatenate([hb(lru_b_a[j, 0]), hb(lru_b_x[j, 0]), hb(lru_b_a[j, 1]),
                                            hb(lru_b_x[j, 1])], axis=-1)
            lam = jnp.concatenate([hb(lru_lam[j, 0]), hb(lru_lam[j, 1])], axis=-1)
            ahead_mod = [] if last else [(cc, mod_w, mod_b, i + 1)]
            *lru_out, = hosted(functools.partial(_rglru, u, gate, uc, gatec, lru_conv_w[j], row2(lru_conv_b[j]),
                                                 w_gate, b_gate, lam, mods=ahead_mod), 2, todo[0::2])
            ya, yac = lru_out[:2]
            mods.update({i + 1: m for m in lru_out[2:]})
            yb, = hosted(functools.partial(_window_attention, q, k, v, kc, vc, att_sink[j]), 1, todo[1::2])
            todo = []
            x = _outproj0(ya, yb, x, mod, None, w_out, row2(ln1_g[i]), row2(ln1_b[i]), TM_PROJ)
            if not last:
                ybc = _context_attention(qc, kc, vc, att_sink[j])
                ctx = _outproj0(yac, ybc, ctx, mod, ctx_row, w_out, row2(ln1_g[i]), row2(ln1_b[i]), TM_CONTEXT)
        else:
            assert last
            w_in = weight(cd_w_in, "cd_in", j)
            w_out = weight(cd_w_out, "cd_out", j)
            u, u_flat, gb, p = _inproj1(x, mod, None, w_in, TM_PROJ, True)
            (uc_flat,) = _inproj1(ctx, mod, ctx_row, w_in, TM_CONTEXT, False)
            wst, a16, wbig = _s5_operators(s5_log_dt[j], s5_a_re[j], s5_a_im[j], s5_b_re[j], s5_b_im[j],
                                           s5_c_re[j], s5_c_im[j])
            h_in = _s5_states(uc_flat, u_flat, wst, a16)
            y = _s5_readout(u_flat, h_in, wbig)
            x = _outproj1(y, u, gb, p, x, mod, None, row2(s5_d[j]), s5_w_glu[j].astype(BF16), row2(s5_b_glu[j]),
                          sc_conv_w[j], row2(sc_conv_b[j]), w_out, row2(ln1_g[i]), row2(ln1_b[i]), TM_LATENT)
        w1 = weight(ffn_w1, "w1", i)
        w2 = weight(ffn_w2, "w2", i)
        ahead = [job for job in todo if job[1] != i or job[0] not in ("w1", "w2")]
        x, = hosted(functools.partial(_ffn, x, mod, None, w1, row2(ffn_b1[i]), w2, row2(ffn_b2[i]),
                                      row2(ln2_g[i]), row2(ln2_b[i]), TM_LATENT, FFN_TH), 1, ahead)
        if not last:
            rows = ctx.reshape(1, bsz * lc, d)
            rows, = _ffn(rows, mod, ctx_row, w1, row2(ffn_b1[i]), w2, row2(ffn_b2[i]), row2(ln2_g[i]),
                         row2(ln2_b[i]), TM_LATENT, FFN_TH)
            ctx = rows.reshape(bsz, lc, d)
    return x
```

```python
import functools
import math

import jax
import jax.numpy as jnp
import numpy as np
from jax import lax
from jax.experimental import pallas as pl
from jax.experimental.pallas import tpu as pltpu

F32 = jnp.float32
BF16 = jnp.bfloat16

D_MODEL = 2048
DEPTH = 2
GRID_W = 64
LRU_WIDTH = D_MODEL // 2
LRU_HEADS = 8
LRU_BLOCK = LRU_WIDTH // LRU_HEADS
LRU_CONV = 4
LRU_CONV_LEFT = 2
LRU_C = 8.0
ATT_HEAD_DIM = 128
ATT_Q_HEADS = (D_MODEL // 2) // ATT_HEAD_DIM
ATT_KV_HEADS = 2
ATT_GROUP = ATT_Q_HEADS // ATT_KV_HEADS
ATT_Q_WIDTH = ATT_Q_HEADS * ATT_HEAD_DIM
ATT_KV_WIDTH = ATT_KV_HEADS * ATT_HEAD_DIM
WINDOW = 128
ATT_BLOCK = 128
ROPE_BASE = 10000.0
ROPE_FREQS = ATT_HEAD_DIM // 4
S5_WIDTH = D_MODEL // 4
S5_GROUP = 16
S5_STATE = 64
SC_WIDTH = D_MODEL - S5_WIDTH
SC_CONV = 3
ALPHA = (2.0 * DEPTH) ** 0.25
LN_EPS = 1e-5
NEG_INF = -1e30
LOG2E = math.log2(math.e)

LANES = 128
SUBLANES = 8
V7X_VMEM_BYTES = 64 * 1024 * 1024
V7X_VMEM_BUDGET = V7X_VMEM_BYTES - 2 * 1024 * 1024

S5_CHUNK = 16
S5_SLAB_GROUPS = LANES // S5_GROUP
S5_SLABS = S5_WIDTH // LANES
S5_SLAB_STATE = S5_SLAB_GROUPS * S5_STATE
S5_XK = S5_CHUNK * LANES
S5_HK = 4 * S5_SLAB_STATE
S5_POW_ROWS = 64


def _vmem_limit(nbytes):
    return int(min(V7X_VMEM_BUDGET, max(nbytes * 3 // 2, 16 * 1024 * 1024)))


def _resident(shape, index_map):
    return pl.BlockSpec(shape, index_map, pipeline_mode=pl.Buffered(1))


def _params(sem, nbytes):
    return pltpu.CompilerParams(dimension_semantics=sem, vmem_limit_bytes=_vmem_limit(nbytes))


def _dot(a, b):
    return jnp.dot(a, b, preferred_element_type=F32)


def _dot_nt(a, b):
    return lax.dot_general(a, b, (((1,), (1,)), ((), ())), preferred_element_type=F32)


def _layer_norm(v, g, b):
    mu = jnp.mean(v, axis=-1, keepdims=True)
    c = v - mu
    var = jnp.mean(c * c, axis=-1, keepdims=True)
    return c * lax.rsqrt(var + LN_EPS) * g + b


def _gelu_tanh(x):
    return 0.5 * x * (1.0 + jnp.tanh(math.sqrt(2.0 / math.pi) * (x + 0.044715 * (x * x * x))))


def _sigmoid(x):
    return 0.5 * (1.0 + jnp.tanh(0.5 * x))


MOD_SH1, MOD_SC1, MOD_G1, MOD_SH2, MOD_SC2, MOD_G2 = range(6)


def _mod_spec(mod, chunk):
    _, rows, width = mod.shape
    return pl.BlockSpec((1, rows, width // 6), lambda *_: (0, 0, chunk))


def _mod_row(ref, ctx_row):
    row = pl.program_id(0) if ctx_row is None else ctx_row
    return ref[0, pl.ds(row, 1), :]


def _modulate(x, sc_ref, sh_ref, ctx_row):
    return (x * (1.0 + _mod_row(sc_ref, ctx_row)) + _mod_row(sh_ref, ctx_row)).astype(BF16)


CAST_BLOCK_BYTES = 8 * 1024 * 1024


def _cast_kernel(w_ref, o_ref):
    o_ref[...] = w_ref[0].astype(BF16)


def _to_bf16(w, layer):
    _, rows, cols = w.shape
    tr = rows
    while tr * cols * 4 > CAST_BLOCK_BYTES and tr % 2 == 0 and tr // 2 >= 2 * SUBLANES:
        tr //= 2
    return pl.pallas_call(
        _cast_kernel,
        grid=(rows // tr,),
        in_specs=[pl.BlockSpec((1, tr, cols), lambda i: (layer, i, 0))],
        out_specs=pl.BlockSpec((tr, cols), lambda i: (i, 0)),
        out_shape=jax.ShapeDtypeStruct((rows, cols), BF16),
        compiler_params=_params(("arbitrary",), 2 * tr * cols * 6),
        name="weight_cast",
    )(w)


CAST_ROWS = 2 * SUBLANES


def _silu_bf16(v):
    return (v * _sigmoid(v)).astype(BF16)


def _hosted_call(body, *, grid, in_specs, out_specs, out_shape, args, sem, nbytes, casts=(), mods=(), **kwargs):
    steps = math.prod(grid)

    def flat(*ids):
        step = 0
        for extent, i in zip(grid, ids):
            step = step * extent + i
        return step

    def upto(used):
        return lambda *ids: jnp.minimum(flat(*ids), used - 1)

    def run_cast(ins, out):
        out[...] = ins[0][0].astype(BF16)

    def run_mod(ins, out):
        cc_ref, w_ref, b_ref = ins
        out[0] = _dot(_silu_bf16(cc_ref[...]), w_ref[0].astype(BF16)) + b_ref[0]

    side_in, side_out, side_shape, side_args, jobs = [], [], [], [], []
    for w, layer in casts:
        _, rows, cols = w.shape
        rps = max(CAST_ROWS, rows // steps)
        assert rows % rps == 0 and rows // rps <= steps
        at = upto(rows // rps)
        side_in.append(pl.BlockSpec((1, rps, cols), lambda *ids, at=at, layer=layer: (layer, at(*ids), 0)))
        side_out.append(pl.BlockSpec((rps, cols), lambda *ids, at=at: (at(*ids), 0)))
        side_shape.append(jax.ShapeDtypeStruct((rows, cols), BF16))
        side_args.append(w)
        nbytes += 2 * rps * cols * (4 + 2)
        jobs.append((1, run_cast))
    for cc, mod_w, mod_b, layer in mods:
        depth, d, n = mod_w.shape
        width = LANES * max(1, n // LANES // steps)
        assert n % width == 0 and n // width <= steps
        at = upto(n // width)
        side_in += [pl.BlockSpec(cc.shape, lambda *ids: (0, 0)),
                    pl.BlockSpec((1, d, width), lambda *ids, at=at, layer=layer: (layer, 0, at(*ids))),
                    pl.BlockSpec((1, 1, width), lambda *ids, at=at, layer=layer: (layer, 0, at(*ids)))]
        side_out.append(pl.BlockSpec((1, cc.shape[0], width), lambda *ids, at=at: (0, 0, at(*ids))))
        side_shape.append(jax.ShapeDtypeStruct((1, cc.shape[0], n), F32))
        side_args += [cc, mod_w, mod_b.reshape(depth, 1, n)]
        nbytes += 2 * d * width * 4 + d * width * 2
        jobs.append((3, run_mod))
    n_in, n_out, n_side_in, n_side_out = len(in_specs), len(out_specs), len(side_in), len(side_out)

    def kernel(*refs):
        ins = refs[:n_in]
        s_ins = refs[n_in:n_in + n_side_in]
        outs = refs[n_in + n_side_in:n_in + n_side_in + n_out]
        s_outs = refs[n_in + n_side_in + n_out:n_in + n_side_in + n_out + n_side_out]
        pos = 0
        for (arity, run), out in zip(jobs, s_outs):
            run(s_ins[pos:pos + arity], out)
            pos += arity
        body(*ins, *outs, *refs[n_in + n_side_in + n_out + n_side_out:])

    return pl.pallas_call(
        kernel, grid=grid, in_specs=list(in_specs) + side_in, out_specs=list(out_specs) + side_out,
        out_shape=list(out_shape) + side_shape, compiler_params=_params(sem, nbytes), **kwargs,
    )(*args, *side_args)


def _mod_kernel(cc_ref, w_ref, b_ref, o_ref):
    o_ref[0] = _dot(_silu_bf16(cc_ref[...]), w_ref[0].astype(BF16)) + b_ref[0]


def _modulation(cc, mod_w, mod_b, layer):
    depth, d, n = mod_w.shape
    tn = 1024
    rows = cc.shape[0]
    return pl.pallas_call(
        _mod_kernel,
        grid=(n // tn,),
        in_specs=[pl.BlockSpec((rows, d), lambda j: (0, 0)),
                  pl.BlockSpec((1, d, tn), lambda j: (layer, 0, j)),
                  pl.BlockSpec((1, 1, tn), lambda j: (layer, 0, j))],
        out_specs=pl.BlockSpec((1, rows, tn), lambda j: (0, 0, j)),
        out_shape=jax.ShapeDtypeStruct((1, rows, n), F32),
        compiler_params=_params(("arbitrary",), 2 * d * tn * 4 + d * tn * 2),
        name="modulation",
    )(cc, mod_w, mod_b.reshape(depth, 1, n))


def _rope(x, cos, sin_signed, heads):
    lane = lax.broadcasted_iota(jnp.int32, (x.shape[0], ATT_HEAD_DIM), 1)
    first = (lane % (2 * ROPE_FREQS)) < ROPE_FREQS
    out = []
    for h in range(heads):
        xs = x[:, h * ATT_HEAD_DIM:(h + 1) * ATT_HEAD_DIM]
        swapped = jnp.where(first, pltpu.roll(xs, ATT_HEAD_DIM - ROPE_FREQS, 1), pltpu.roll(xs, ROPE_FREQS, 1))
        out.append(xs * cos + swapped * sin_signed)
    return out


def _inproj0_kernel(*refs, rope, ctx_row):
    if rope:
        (x_ref, sc_ref, sh_ref, cos_ref, sin_ref, wu_ref, wg_ref, wq_ref, wk_ref, wv_ref,
         u_ref, g_ref, q_ref, k_ref, v_ref) = refs
    else:
        (x_ref, sc_ref, sh_ref, wu_ref, wg_ref, wq_ref, wk_ref, wv_ref,
         u_ref, g_ref, q_ref, k_ref, v_ref) = refs
    h = _modulate(x_ref[0], sc_ref, sh_ref, ctx_row)
    q = _dot(h, wq_ref[...]) * (ATT_HEAD_DIM ** -0.5 * LOG2E)
    k = _dot(h, wk_ref[...])
    if rope:
        cos = cos_ref[...]
        sin = sin_ref[...]
        for hd, piece in enumerate(_rope(q, cos, sin, ATT_Q_HEADS)):
            q_ref[0, :, hd * ATT_HEAD_DIM:(hd + 1) * ATT_HEAD_DIM] = piece.astype(BF16)
        for hd, piece in enumerate(_rope(k, cos, sin, ATT_KV_HEADS)):
            k_ref[0, :, hd * ATT_HEAD_DIM:(hd + 1) * ATT_HEAD_DIM] = piece.astype(BF16)
    else:
        q_ref[0] = q.astype(BF16)
        k_ref[0] = k.astype(BF16)
    u = _dot(h, wu_ref[...])
    g = _dot(h, wg_ref[...])
    for hd in range(LRU_HEADS):
        u_ref[0, hd] = u[:, hd * LRU_BLOCK:(hd + 1) * LRU_BLOCK].astype(BF16)
        g_ref[0, hd] = _gelu_tanh(g[:, hd * LRU_BLOCK:(hd + 1) * LRU_BLOCK]).astype(BF16)
    v_ref[0] = _dot(h, wv_ref[...]).astype(BF16)


def _inproj0(x, mod, ctx_row, w_in, rope_tabs, tm):
    bsz, n, d = x.shape
    rope = rope_tabs is not None
    tok = lambda b, i: (b, i, 0)
    in_specs = [pl.BlockSpec((1, tm, d), tok), _mod_spec(mod, MOD_SC1), _mod_spec(mod, MOD_SH1)]
    args = [x, mod, mod]
    if rope:
        in_specs += [pl.BlockSpec((tm, ATT_HEAD_DIM), lambda b, i: (i, 0))] * 2
        args += list(rope_tabs)
    kv_at = (2 * LRU_WIDTH + ATT_Q_WIDTH) // ATT_KV_WIDTH
    in_specs += [_resident((d, LRU_WIDTH), lambda b, i: (0, 0)), _resident((d, LRU_WIDTH), lambda b, i: (0, 1)),
                 _resident((d, ATT_Q_WIDTH), lambda b, i: (0, 2)),
                 _resident((d, ATT_KV_WIDTH), lambda b, i: (0, kv_at)),
                 _resident((d, ATT_KV_WIDTH), lambda b, i: (0, kv_at + 1))]
    args += [w_in] * 5
    head_major = jax.ShapeDtypeStruct((bsz, LRU_HEADS, n, LRU_BLOCK), BF16)
    head_spec = pl.BlockSpec((1, LRU_HEADS, tm, LRU_BLOCK), lambda b, i: (b, 0, i, 0))
    nbytes = 2 * tm * d * 4 + 2 * w_in.size + 2 * 2 * tm * w_in.shape[1] + tm * w_in.shape[1] * 4
    return pl.pallas_call(
        functools.partial(_inproj0_kernel, rope=rope, ctx_row=ctx_row),
        grid=(bsz, n // tm),
        in_specs=in_specs,
        out_specs=[head_spec, head_spec,
                   pl.BlockSpec((1, tm, ATT_Q_WIDTH), tok),
                   pl.BlockSpec((1, tm, ATT_KV_WIDTH), tok),
                   pl.BlockSpec((1, tm, ATT_KV_WIDTH), tok)],
        out_shape=[head_major, head_major,
                   jax.ShapeDtypeStruct((bsz, n, ATT_Q_WIDTH), BF16),
                   jax.ShapeDtypeStruct((bsz, n, ATT_KV_WIDTH), BF16),
                   jax.ShapeDtypeStruct((bsz, n, ATT_KV_WIDTH), BF16)],
        compiler_params=_params(("arbitrary", "arbitrary"), nbytes),
        name="inproj0_rope" if rope else "inproj0_ctx",
    )(*args)


LRU_PAD = SUBLANES
LRU_TILE = 256


def _scan8(a, b, row, reverse):
    for k in (1, 2, 4):
        if reverse:
            keep = row < SUBLANES - k
            shift = SUBLANES - k
        else:
            keep = row >= k
            shift = k
        a_sh = jnp.where(keep, pltpu.roll(a, shift, 0), 1.0)
        b_sh = jnp.where(keep, pltpu.roll(b, shift, 0), 0.0)
        b = a * b_sh + b
        a = a * a_sh
    return a, b


def _scan8_rows(a, b, row, reverse):
    out_a, out_b = [], []
    for g in range(a.shape[0] // SUBLANES):
        sl = slice(g * SUBLANES, (g + 1) * SUBLANES)
        ag, bg = _scan8(a[sl], b[sl], row, reverse)
        out_a.append(ag)
        out_b.append(bg)
    return jnp.concatenate(out_a, axis=0), jnp.concatenate(out_b, axis=0)


def _lru_kernel(ul_ref, gl_ref, uc_ref, gc_ref, cw_ref, cb_ref, wg_ref, bg_ref, lam_ref,
                yl_ref, yc_ref, upad, af_s, bf_s, ar_s, br_s, cin_s, *, n_lat, n_ctx):
    cw = cw_ref[...]
    cb = cb_ref[...]
    wg = wg_ref[0]
    bg = bg_ref[0]
    lam = lam_ref[0]
    neg = -lam
    softplus = jnp.maximum(neg, 0.0) + jnp.log1p(jnp.exp(-jnp.abs(neg)))
    half_rate = (-0.5 * LRU_C) * softplus
    row = lax.broadcasted_iota(jnp.int32, (SUBLANES, LRU_BLOCK), 0)
    chains = ((af_s, bf_s, False), (ar_s, br_s, True))

    def coefficients(src_ref, n_rows, row_off):
        upad[pl.ds(0, LRU_PAD), :] = jnp.zeros((LRU_PAD, LRU_BLOCK), F32)
        upad[pl.ds(LRU_PAD, n_rows), :] = src_ref[0, 0].astype(F32)
        upad[pl.ds(LRU_PAD + n_rows, LRU_PAD), :] = jnp.zeros((LRU_PAD, LRU_BLOCK), F32)

        def tile(i, carry):
            t0 = pl.multiple_of(i * LRU_TILE, LRU_TILE)
            xp = upad[pl.ds(t0, LRU_TILE + 2 * LRU_PAD), :]
            conv = cb
            for k in range(LRU_CONV):
                o = LRU_PAD - LRU_CONV_LEFT + k
                conv = conv + cw[k:k + 1, :] * xp[o:o + LRU_TILE, :]
            z = _dot(conv.astype(BF16), wg) + bg
            half_conv = 0.5 * conv
            for d, (a_ref, b_ref, reverse) in enumerate(chains):
                t_a = jnp.tanh(z[:, (2 * d) * LRU_BLOCK:(2 * d + 1) * LRU_BLOCK])
                t_x = jnp.tanh(z[:, (2 * d + 1) * LRU_BLOCK:(2 * d + 2) * LRU_BLOCK])
                rate = half_rate[:, d * LRU_BLOCK:(d + 1) * LRU_BLOCK]
                a = jnp.exp(rate + rate * t_a)
                gated = half_conv + half_conv * t_x
                b = jnp.sqrt(1.0 - a * a) * gated
                a, b = _scan8_rows(a, b, row, reverse)
                a_ref[pl.ds(row_off + t0, LRU_TILE), :] = a
                b_ref[pl.ds(row_off + t0, LRU_TILE), :] = b
            return carry

        tiles = n_rows // LRU_TILE
        lax.fori_loop(0, tiles, tile, 0, unroll=2 if tiles % 2 == 0 else 1)

    coefficients(uc_ref, n_ctx, 0)
    coefficients(ul_ref, n_lat, n_ctx)

    groups_ctx = n_ctx // SUBLANES
    groups = (n_ctx + n_lat) // SUBLANES
    fa = af_s[pl.ds(SUBLANES - 1, groups, stride=SUBLANES), :]
    fb = bf_s[pl.ds(SUBLANES - 1, groups, stride=SUBLANES), :]
    ra = ar_s[pl.ds(0, groups, stride=SUBLANES), :]
    rb = br_s[pl.ds(0, groups, stride=SUBLANES), :]
    zero = jnp.zeros((SUBLANES, LRU_BLOCK), F32)
    carry = zero
    for v in range(groups // SUBLANES):
        sl = slice(v * SUBLANES, (v + 1) * SUBLANES)
        a, b = _scan8(fa[sl], fb[sl], row, False)
        incl = b + a * carry
        cin_s[0, sl, :] = jnp.where(row == 0, carry, pltpu.roll(incl, 1, 0))
        carry = jnp.broadcast_to(incl[SUBLANES - 1:SUBLANES, :], incl.shape)
    carry = zero
    order = list(range(groups_ctx // SUBLANES - 1, -1, -1)) + list(range(groups // SUBLANES - 1,
                                                                        groups_ctx // SUBLANES - 1, -1))
    for v in order:
        sl = slice(v * SUBLANES, (v + 1) * SUBLANES)
        a, b = _scan8(ra[sl], rb[sl], row, True)
        incl = b + a * carry
        cin_s[1, sl, :] = jnp.where(row == SUBLANES - 1, carry, pltpu.roll(incl, SUBLANES - 1, 0))
        carry = jnp.broadcast_to(incl[0:1, :], incl.shape)

    def emit(g_ref, y_ref, n_rows, row_off):
        tile_groups = LRU_TILE // SUBLANES

        def tile(i, carry):
            t0 = pl.multiple_of(i * LRU_TILE, LRU_TILE)
            g0 = pl.multiple_of(row_off // SUBLANES + i * tile_groups, tile_groups)
            rows = pl.ds(row_off + t0, LRU_TILE)
            a_f, b_f, a_r, b_r = af_s[rows, :], bf_s[rows, :], ar_s[rows, :], br_s[rows, :]
            pieces = []
            for g in range(tile_groups):
                sl = slice(g * SUBLANES, (g + 1) * SUBLANES)
                c_f = jnp.broadcast_to(cin_s[0, pl.ds(g0 + g, 1), :], (SUBLANES, LRU_BLOCK))
                c_r = jnp.broadcast_to(cin_s[1, pl.ds(g0 + g, 1), :], (SUBLANES, LRU_BLOCK))
                pieces.append((b_f[sl] + a_f[sl] * c_f) + (b_r[sl] + a_r[sl] * c_r))
            h = jnp.concatenate(pieces, axis=0)
            gate = g_ref[0, 0, pl.ds(t0, LRU_TILE), :].astype(F32)
            y_ref[0, 0, pl.ds(t0, LRU_TILE), :] = (h * gate).astype(BF16)
            return carry

        lax.fori_loop(0, n_rows // LRU_TILE, tile, 0)

    emit(gc_ref, yc_ref, n_ctx, 0)
    emit(gl_ref, yl_ref, n_lat, n_ctx)


def _rglru(u_lat, g_lat, u_ctx, g_ctx, conv_w, conv_b, w_gate, b_gate, lam, casts=(), mods=()):
    bsz, heads, n_lat, blk = u_lat.shape
    n_ctx = u_ctx.shape[2]
    total = n_lat + n_ctx
    assert n_ctx % (SUBLANES * SUBLANES) == 0 and n_lat % LRU_TILE == 0 and n_ctx % LRU_TILE == 0
    seq = lambda n: pl.BlockSpec((1, 1, n, blk), lambda b, h: (b, h, 0, 0))
    per_head = lambda shape: pl.BlockSpec((1,) + shape, lambda b, h: (h, 0, 0))
    nbytes =4 * total * blk * 4 + (n_lat + 2 * LRU_PAD) * blk * 4 + 2 * 3 * 2 * total * blk * 2
    return _hosted_call(
        functools.partial(_lru_kernel, n_lat=n_lat, n_ctx=n_ctx),
        casts=casts, mods=mods,
        args=(u_lat, g_lat, u_ctx, g_ctx, conv_w, conv_b, w_gate, b_gate, lam),
        grid=(bsz, heads),
        in_specs=[seq(n_lat), seq(n_lat), seq(n_ctx), seq(n_ctx),
                  pl.BlockSpec((LRU_CONV, blk), lambda b, h: (0, h)),
                  pl.BlockSpec((1, blk), lambda b, h: (0, h)),
                  per_head((blk, 4 * blk)), per_head((1, 4 * blk)), per_head((1, 2 * blk))],
        out_specs=[seq(n_lat), seq(n_ctx)],
        out_shape=[jax.ShapeDtypeStruct(u_lat.shape, BF16), jax.ShapeDtypeStruct(u_ctx.shape, BF16)],
        scratch_shapes=[pltpu.VMEM((n_lat + 2 * LRU_PAD, blk), F32)]
                       + [pltpu.VMEM((total, blk), F32)] * 4
                       + [pltpu.VMEM((2, total // SUBLANES, blk), F32)],
        sem=("arbitrary", "arbitrary"), nbytes=nbytes,
        name="rglru",
    )


ATT_TQ = 512
ATT_BAND = 3 * ATT_BLOCK


def _dot_tn(a, b):
    return lax.dot_general(a, b, (((0,), (0,)), ((), ())), preferred_element_type=F32)


def _softmax_probs(scores, sink_row):
    m = sink_row
    for s in scores:
        m = jnp.maximum(m, jnp.max(s, axis=0, keepdims=True))
    probs = [jnp.exp2(s - m).astype(BF16) for s in scores]
    pad = 2 * SUBLANES
    first = lax.broadcasted_iota(jnp.int32, (pad, m.shape[1]), 0) == 0
    p_sink = jnp.where(first, jnp.exp2(sink_row - m), 0.0).astype(BF16)
    probs[-1] = jnp.concatenate([probs[-1], p_sink], axis=0)
    return probs


def _weighted_values(probs, values):
    dh = ATT_HEAD_DIM
    acc = None
    for idx, (p, v) in enumerate(zip(probs, values)):
        v_ext = jnp.concatenate([v, jnp.ones(v.shape, BF16)], axis=1)
        if idx == len(values) - 1:
            pad = p.shape[0] - v.shape[0]
            v_sink = jnp.concatenate([jnp.zeros((pad, dh), BF16), jnp.ones((pad, dh), BF16)], axis=1)
            v_ext = jnp.concatenate([v_ext, v_sink], axis=0)
        pv = _dot_tn(p, v_ext)
        acc = pv if acc is None else acc + pv
    return acc[:, :dh] / acc[:, dh:]


def _sink_rows(sink, reps):
    return jnp.repeat(sink.reshape(ATT_KV_HEADS, 1, ATT_GROUP), reps, axis=2)


def _attn_kernel(q_ref, k_ref, v_ref, kc_ref, vc_ref, sink_ref, o_ref, *, seq):
    tile = pl.program_id(1)
    blocks = ATT_TQ // ATT_BLOCK
    qcol = lax.broadcasted_iota(jnp.int32, (1, ATT_GROUP * ATT_BLOCK), 1) % ATT_BLOCK
    krow = lax.broadcasted_iota(jnp.int32, (ATT_BAND, 1), 0)

    def scores(u):
        i, g = divmod(u, ATT_KV_HEADS)
        q0 = (tile * blocks + i) * ATT_BLOCK
        start = pl.multiple_of(jnp.clip(q0 - ATT_BLOCK, 0, seq - ATT_BAND), ATT_BLOCK)
        valid = jnp.abs((q0 + qcol) - (start + krow)) <= WINDOW
        heads = [q_ref[0, i * ATT_BLOCK:(i + 1) * ATT_BLOCK,
                       (g * ATT_GROUP + r) * ATT_HEAD_DIM:(g * ATT_GROUP + r + 1) * ATT_HEAD_DIM]
                 for r in range(ATT_GROUP)]
        qs = jnp.concatenate(heads, axis=0)
        kv = slice(g * ATT_HEAD_DIM, (g + 1) * ATT_HEAD_DIM)
        s_loc = jnp.where(valid, _dot_nt(k_ref[0, pl.ds(start, ATT_BAND), kv], qs), NEG_INF)
        return start, [s_loc, _dot_nt(kc_ref[0, :, kv], qs)]

    def emit(u, start, p):
        i, g = divmod(u, ATT_KV_HEADS)
        kv = slice(g * ATT_HEAD_DIM, (g + 1) * ATT_HEAD_DIM)
        o = _weighted_values(p, [v_ref[0, pl.ds(start, ATT_BAND), kv], vc_ref[0, :, kv]])
        for r in range(ATT_GROUP):
            col = (g * ATT_GROUP + r) * ATT_HEAD_DIM
            o_ref[0, i * ATT_BLOCK:(i + 1) * ATT_BLOCK, col:col + ATT_HEAD_DIM] = (
                o[r * ATT_BLOCK:(r + 1) * ATT_BLOCK].astype(BF16))

    n_units = blocks * ATT_KV_HEADS
    sc, pr = {}, {}
    for step in range(n_units + 2):
        if step < n_units:
            sc[step] = scores(step)
        if 0 <= step - 1 < n_units:
            start, s_list = sc.pop(step - 1)
            pr[step - 1] = (start, _softmax_probs(s_list, sink_ref[(step - 1) % ATT_KV_HEADS] * LOG2E))
        if 0 <= step - 2 < n_units:
            start, p = pr.pop(step - 2)
            emit(step - 2, start, p)


def _window_attention(q, k, v, kc, vc, sink, casts=()):
    bsz, n, _ = q.shape
    sink_rows = _sink_rows(sink, ATT_BLOCK)
    lc = kc.shape[1]
    whole = lambda rows: pl.BlockSpec((1, rows, ATT_KV_WIDTH), lambda b, i: (b, 0, 0))
    nbytes = 2 * 2 * (2 * ATT_TQ * ATT_Q_WIDTH + 2 * n * ATT_KV_WIDTH + 2 * lc * ATT_KV_WIDTH) + 8 * 1024 * 1024
    return _hosted_call(
        functools.partial(_attn_kernel, seq=n),
        casts=casts,
        args=(q, k, v, kc, vc, sink_rows),
        grid=(bsz, n // ATT_TQ),
        in_specs=[pl.BlockSpec((1, ATT_TQ, ATT_Q_WIDTH), lambda b, i: (b, i, 0)),
                  whole(n), whole(n), whole(lc), whole(lc),
                  pl.BlockSpec(sink_rows.shape, lambda b, i: (0, 0, 0))],
        out_specs=[pl.BlockSpec((1, ATT_TQ, ATT_Q_WIDTH), lambda b, i: (b, i, 0))],
        out_shape=[jax.ShapeDtypeStruct(q.shape, BF16)],
        sem=("arbitrary", "arbitrary"), nbytes=nbytes,
        name="window_attention",
    )


def _ctx_attn_kernel(q_ref, kc_ref, vc_ref, sink_ref, o_ref):
    lc = q_ref.shape[1]
    for g in range(ATT_KV_HEADS):
        heads = [q_ref[0, :, (g * ATT_GROUP + r) * ATT_HEAD_DIM:(g * ATT_GROUP + r + 1) * ATT_HEAD_DIM]
                 for r in range(ATT_GROUP)]
        qs = jnp.concatenate(heads, axis=0)
        kv = slice(g * ATT_HEAD_DIM, (g + 1) * ATT_HEAD_DIM)
        o = _weighted_values(_softmax_probs([_dot_nt(kc_ref[0, :, kv], qs)], sink_ref[g] * LOG2E),
                             [vc_ref[0, :, kv]])
        for r in range(ATT_GROUP):
            col = (g * ATT_GROUP + r) * ATT_HEAD_DIM
            o_ref[0, :, col:col + ATT_HEAD_DIM] = o[r * lc:(r + 1) * lc].astype(BF16)


def _context_attention(qc, kc, vc, sink):
    bsz, lc, _ = qc.shape
    sink_rows = _sink_rows(sink, lc)
    kv_spec = pl.BlockSpec((1, lc, ATT_KV_WIDTH), lambda b: (b, 0, 0))
    q_spec = pl.BlockSpec((1, lc, ATT_Q_WIDTH), lambda b: (b, 0, 0))
    return pl.pallas_call(
        _ctx_attn_kernel,
        grid=(bsz,),
        in_specs=[q_spec, kv_spec, kv_spec, pl.BlockSpec(sink_rows.shape, lambda b: (0, 0, 0))],
        out_specs=q_spec,
        out_shape=jax.ShapeDtypeStruct(qc.shape, BF16),
        compiler_params=_params(("arbitrary",), 8 * 1024 * 1024),
        name="context_attention",
    )(qc, kc, vc, sink_rows)


OUT_PART_ROWS = 256


def _outproj0_kernel(ya_ref, yb_ref, x_ref, gate_ref, w_ref, lg_ref, lb_ref, o_ref, *, ctx_row):
    gate = _mod_row(gate_ref, ctx_row)
    rows_per = min(OUT_PART_ROWS, x_ref.shape[1])
    parts = x_ref.shape[1] // rows_per
    outs = {}
    for r in range(parts + 1):
        if r < parts:
            rows = slice(r * rows_per, (r + 1) * rows_per)
            y = jnp.concatenate([ya_ref[0, h, rows, :] for h in range(LRU_HEADS)] + [yb_ref[0, rows, :]], axis=1)
            outs[r] = _dot(y, w_ref[...])
        if r >= 1:
            rows = slice((r - 1) * rows_per, r * rows_per)
            o_ref[0, rows, :] = _layer_norm(ALPHA * x_ref[0, rows, :] + gate * outs.pop(r - 1), lg_ref[...], lb_ref[...])


def _outproj0(ya, yb, x, mod, ctx_row, w_out, ln_g, ln_b, tm):
    bsz, n, d = x.shape
    tok = lambda b, i: (b, i, 0)
    const = lambda b, i: (0, 0)
    nbytes = 2 * 2 * tm * d * 4 + w_out.size * 2 + 2 * 2 * tm * d * 2 + 2 * tm * d * 4
    return pl.pallas_call(
        functools.partial(_outproj0_kernel, ctx_row=ctx_row),
        grid=(bsz, n // tm),
        in_specs=[pl.BlockSpec((1, LRU_HEADS, tm, LRU_BLOCK), lambda b, i: (b, 0, i, 0)),
                  pl.BlockSpec((1, tm, ATT_Q_WIDTH), tok),
                  pl.BlockSpec((1, tm, d), tok),
                  _mod_spec(mod, MOD_G1),
                  _resident(w_out.shape, const),
                  pl.BlockSpec((1, d), const), pl.BlockSpec((1, d), const)],
        out_specs=pl.BlockSpec((1, tm, d), tok),
        out_shape=jax.ShapeDtypeStruct(x.shape, F32),
        compiler_params=_params(("arbitrary", "arbitrary"), nbytes),
        name="outproj0_ln",
    )(ya, yb, x, mod, w_out, ln_g, ln_b)


def _ffn_kernel(x_ref, sc_ref, sh_ref, gate_ref, w1_ref, b1_ref, w2_ref, b2_ref, lg_ref, lb_ref,
                o_ref, h_s, *, ctx_row):
    j = pl.program_id(2)

    @pl.when(j == 0)
    def _():
        h_s[...] = _modulate(x_ref[0], sc_ref, sh_ref, ctx_row)
        o_ref[0] = jnp.zeros(o_ref.shape[1:], F32)

    a = jnp.maximum(_dot(h_s[...], w1_ref[...]) + b1_ref[...], 0.0)
    o_ref[0] += _dot((a * a).astype(BF16), w2_ref[...])

    @pl.when(j == pl.num_programs(2) - 1)
    def _():
        f = o_ref[0] + b2_ref[...]
        o_ref[0] = _layer_norm(ALPHA * x_ref[0] + _mod_row(gate_ref, ctx_row) * f, lg_ref[...], lb_ref[...])


def _ffn(x, mod, ctx_row, w1, b1, w2, b2, ln_g, ln_b, tm, th, casts=()):
    bsz, n, d = x.shape
    hidden = w1.shape[1]
    tok = lambda b, i, j: (b, i, 0)
    const = lambda b, i, j: (0, 0)
    nbytes = 2 * 2 * tm * d * 4 + 2 * 2 * 2 * d * th * 2 + tm * d * 2 + 2 * tm * th * 4 + tm * d * 4
    return _hosted_call(
        functools.partial(_ffn_kernel, ctx_row=ctx_row),
        casts=casts,
        args=(x, mod, mod, mod, w1, b1, w2, b2, ln_g, ln_b),
        grid=(bsz, n // tm, hidden // th),
        in_specs=[pl.BlockSpec((1, tm, d), tok),
                  _mod_spec(mod, MOD_SC2), _mod_spec(mod, MOD_SH2), _mod_spec(mod, MOD_G2),
                  pl.BlockSpec((d, th), lambda b, i, j: (0, j)),
                  pl.BlockSpec((1, th), lambda b, i, j: (0, j)),
                  pl.BlockSpec((th, d), lambda b, i, j: (j, 0)),
                  pl.BlockSpec((1, d), const), pl.BlockSpec((1, d), const), pl.BlockSpec((1, d), const)],
        out_specs=[pl.BlockSpec((1, tm, d), tok)],
        out_shape=[jax.ShapeDtypeStruct(x.shape, F32)],
        scratch_shapes=[pltpu.VMEM((tm, d), BF16)],
        sem=("arbitrary", "arbitrary", "arbitrary"), nbytes=nbytes,
        name="ffn_ln",
    )


SC_PARTS = SC_WIDTH // S5_WIDTH


def _inproj1_kernel(*refs, full, ctx_row):
    x_ref, sc_ref, sh_ref, wu_ref = refs[:4]
    u_s = refs[-1]
    h = _modulate(x_ref[0], sc_ref, sh_ref, ctx_row)
    u = _dot(h, wu_ref[...])
    if full:
        wb_refs = refs[4:4 + SC_PARTS]
        wc_refs = refs[4 + SC_PARTS:4 + 2 * SC_PARTS]
        wx_refs = refs[4 + 2 * SC_PARTS:4 + 3 * SC_PARTS]
        u_ref, uflat_ref, gb_ref, p_ref = refs[4 + 3 * SC_PARTS:-1]
        u_ref[0] = u.astype(BF16)
        for c in range(SC_PARTS):
            cols = slice(c * S5_WIDTH, (c + 1) * S5_WIDTH)
            gb_ref[0, :, cols] = _dot(h, wb_refs[c][...]).astype(BF16)
            p_ref[0, :, cols] = (_dot(h, wc_refs[c][...]) * _dot(h, wx_refs[c][...])).astype(BF16)
    else:
        uflat_ref, = refs[4:-1]
    for k in range(S5_SLABS):
        u_s[k] = u[:, k * LANES:(k + 1) * LANES]
    chunks = u.shape[0] // S5_CHUNK
    for t in range(S5_CHUNK):
        for k in range(S5_SLABS):
            col = t * S5_WIDTH + k * LANES
            uflat_ref[0, :, col:col + LANES] = u_s[k, pl.ds(t, chunks, stride=S5_CHUNK), :].astype(BF16)


def _inproj1(x, mod, ctx_row, w_in, tm, full):
    bsz, n, d = x.shape
    tok = lambda b, i: (b, i, 0)
    n_blocks = 1 + 3 * SC_PARTS if full else 1
    flat_spec = pl.BlockSpec((1, tm // S5_CHUNK, S5_CHUNK * S5_WIDTH), tok)
    flat_shape = jax.ShapeDtypeStruct((bsz, n // S5_CHUNK, S5_CHUNK * S5_WIDTH), BF16)
    token = lambda w: (pl.BlockSpec((1, tm, w), tok), jax.ShapeDtypeStruct((bsz, n, w), BF16))
    outs = ([token(S5_WIDTH), (flat_spec, flat_shape), token(SC_WIDTH), token(SC_WIDTH)] if full
            else [(flat_spec, flat_shape)])
    ncols = n_blocks * S5_WIDTH
    nbytes = 2 * tm * d * 4 + 2 * d * ncols + 2 * 2 * tm * (ncols + S5_WIDTH) + tm * ncols * 4 + tm * S5_WIDTH * 4
    return pl.pallas_call(
        functools.partial(_inproj1_kernel, full=full, ctx_row=ctx_row),
        grid=(bsz, n // tm),
        in_specs=[pl.BlockSpec((1, tm, d), tok), _mod_spec(mod, MOD_SC1), _mod_spec(mod, MOD_SH1)]
                 + [_resident((d, S5_WIDTH), lambda b, i, c=c: (0, c)) for c in range(n_blocks)],
        out_specs=[spec for spec, _ in outs],
        out_shape=[shape for _, shape in outs],
        scratch_shapes=[pltpu.VMEM((S5_SLABS, tm, LANES), F32)],
        compiler_params=_params(("arbitrary", "arbitrary"), nbytes),
        name="inproj1" if full else "inproj1_ctx",
    )(x, mod, mod, *([w_in] * n_blocks))


def _s5_power_table(par_ref):
    rows = 2 * S5_POW_ROWS
    r = lax.broadcasted_iota(jnp.int32, (rows, S5_SLAB_STATE), 0)
    first = r < S5_POW_ROWS
    lag = (r % S5_POW_ROWS).astype(F32)
    pick = lambda i: jnp.where(first, par_ref[0, 0, i:i + 1, :], par_ref[0, 1, i:i + 1, :])
    dt = jnp.exp(pick(0))
    mag = jnp.exp(lag * dt * pick(1))
    ang = lag * dt * pick(2)
    return mag * jnp.cos(ang), mag * jnp.sin(ang)


def _s5_input_matrix(par_ref, bt_ref, d, pw_re, pw_im):
    a_re = par_ref[0, d, 1:2, :]
    a_im = par_ref[0, d, 2:3, :]
    ab_re = pw_re[d * S5_POW_ROWS + 1:d * S5_POW_ROWS + 2, :]
    ab_im = pw_im[d * S5_POW_ROWS + 1:d * S5_POW_ROWS + 2, :]
    den = a_re * a_re + a_im * a_im
    k_re = ((ab_re - 1.0) * a_re + ab_im * a_im) / den
    k_im = (ab_im * a_re - (ab_re - 1.0) * a_im) / den
    b_re = bt_ref[0, d, 0]
    b_im = bt_ref[0, d, 1]
    return k_re * b_re - k_im * b_im, k_re * b_im + k_im * b_re


def _cmul_row(x_re, x_im, p_re, p_im):
    return x_re * p_re - x_im * p_im, x_re * p_im + x_im * p_re


def _s5_state_prep_kernel(par_ref, bt_ref, wst_ref, a16_ref):
    pw_re, pw_im = _s5_power_table(par_ref)
    for d in range(2):
        bb_re, bb_im = _s5_input_matrix(par_ref, bt_ref, d, pw_re, pw_im)
        base = d * S5_POW_ROWS
        for lag in range(S5_CHUNK):
            e_re, e_im = _cmul_row(bb_re, bb_im, pw_re[base + lag:base + lag + 1, :],
                                   pw_im[base + lag:base + lag + 1, :])
            s = S5_CHUNK - 1 - lag if d == 0 else lag
            col = 2 * d * S5_SLAB_STATE
            wst_ref[0, s * LANES:(s + 1) * LANES, col:col + S5_SLAB_STATE] = e_re.astype(BF16)
            wst_ref[0, s * LANES:(s + 1) * LANES, col + S5_SLAB_STATE:col + 2 * S5_SLAB_STATE] = e_im.astype(BF16)
        row16 = base + S5_CHUNK
        a16_ref[0, :, 2 * d * S5_SLAB_STATE:(2 * d + 1) * S5_SLAB_STATE] = jnp.broadcast_to(
            pw_re[row16:row16 + 1, :], (SUBLANES, S5_SLAB_STATE))
        a16_ref[0, :, (2 * d + 1) * S5_SLAB_STATE:(2 * d + 2) * S5_SLAB_STATE] = jnp.broadcast_to(
            pw_im[row16:row16 + 1, :], (SUBLANES, S5_SLAB_STATE))


def _s5_output_prep_kernel(par_ref, bt_ref, c_ref, ct_ref, wbig_ref):
    pw_re, pw_im = _s5_power_table(par_ref)
    kern = []
    for d in range(2):
        bb_re, bb_im = _s5_input_matrix(par_ref, bt_ref, d, pw_re, pw_im)
        c_re = c_ref[0, d, 0].astype(BF16)
        c_im = c_ref[0, d, 1].astype(BF16)
        base = d * S5_POW_ROWS
        per_lag = []
        for lag in range(S5_CHUNK):
            e_re, e_im = _cmul_row(bb_re, bb_im, pw_re[base + lag:base + lag + 1, :],
                                   pw_im[base + lag:base + lag + 1, :])
            per_lag.append(_dot_nt(e_re.astype(BF16), c_re) - _dot_nt(e_im.astype(BF16), c_im))
        kern.append(per_lag)
    for s in range(S5_CHUNK):
        for t in range(S5_CHUNK):
            if s < t:
                blk = kern[0][t - s]
            elif s > t:
                blk = kern[1][s - t]
            else:
                blk = kern[0][0] + kern[1][0]
            wbig_ref[0, s * LANES:(s + 1) * LANES, t * LANES:(t + 1) * LANES] = blk.astype(BF16)
    pt_re = pw_re.T
    pt_im = pw_im.T
    for d in range(2):
        ct_re = ct_ref[0, d, 0]
        ct_im = ct_ref[0, d, 1]
        for t in range(S5_CHUNK):
            lag = t + 1 if d == 0 else S5_CHUNK - t
            col = d * S5_POW_ROWS + lag
            p_re = pt_re[:, col:col + 1]
            p_im = pt_im[:, col:col + 1]
            g_re = ct_re * p_re - ct_im * p_im
            g_im = ct_re * p_im + ct_im * p_re
            r0 = S5_XK + 2 * d * S5_SLAB_STATE
            wbig_ref[0, r0:r0 + S5_SLAB_STATE, t * LANES:(t + 1) * LANES] = g_re.astype(BF16)
            wbig_ref[0, r0 + S5_SLAB_STATE:r0 + 2 * S5_SLAB_STATE, t * LANES:(t + 1) * LANES] = (-g_im).astype(BF16)


def _s5_operators(log_dt, a_re, a_im, b_re, b_im, c_re, c_im):
    gs = S5_SLAB_GROUPS
    eye = jnp.eye(gs, dtype=F32)

    def lanes(v):
        return v.reshape(2, S5_SLABS, gs * S5_STATE)

    par = jnp.stack([lanes(jnp.broadcast_to(log_dt[:, :, None], a_re.shape)), lanes(a_re), lanes(a_im)], axis=2)
    par = jnp.pad(par, ((0, 0), (0, 0), (0, SUBLANES - 3), (0, 0))).transpose(1, 0, 2, 3)

    def embed_bt(b):
        b = b.reshape(2, S5_SLABS, gs, S5_STATE, S5_GROUP)
        e = b.transpose(0, 1, 2, 4, 3)[:, :, :, :, None, :] * eye[None, None, :, None, :, None]
        return e.reshape(2, S5_SLABS, gs * S5_GROUP, gs * S5_STATE).transpose(1, 0, 2, 3)

    def embed_c(c):
        c = c.reshape(2, S5_SLABS, gs, S5_GROUP, S5_STATE)
        e = c[:, :, :, :, None, :] * eye[None, None, :, None, :, None]
        return e.reshape(2, S5_SLABS, gs * S5_GROUP, gs * S5_STATE).transpose(1, 0, 2, 3)

    bt = jnp.stack([embed_bt(b_re), embed_bt(b_im)], axis=2)
    cm = jnp.stack([embed_c(c_re), embed_c(c_im)], axis=2)
    ct = cm.transpose(0, 1, 2, 4, 3)

    slab5 = lambda shape: pl.BlockSpec((1,) + shape, lambda k: (k, 0, 0, 0, 0))
    par_spec = pl.BlockSpec((1, 2, SUBLANES, S5_SLAB_STATE), lambda k: (k, 0, 0, 0))
    bt_spec = slab5((2, 2, LANES, S5_SLAB_STATE))
    wst, a16 = pl.pallas_call(
        _s5_state_prep_kernel,
        grid=(S5_SLABS,),
        in_specs=[par_spec, bt_spec],
        out_specs=[pl.BlockSpec((1, S5_XK, S5_HK), lambda k: (k, 0, 0)),
                   pl.BlockSpec((1, SUBLANES, S5_HK), lambda k: (k, 0, 0))],
        out_shape=[jax.ShapeDtypeStruct((S5_SLABS, S5_XK, S5_HK), BF16),
                   jax.ShapeDtypeStruct((S5_SLABS, SUBLANES, S5_HK), F32)],
        compiler_params=_params(("arbitrary",), 2 * S5_XK * S5_HK * 2 + 8 * 1024 * 1024),
        name="s5_state_operator",
    )(par, bt)
    wbig = pl.pallas_call(
        _s5_output_prep_kernel,
        grid=(S5_SLABS,),
        in_specs=[par_spec, bt_spec, bt_spec, slab5((2, 2, S5_SLAB_STATE, LANES))],
        out_specs=pl.BlockSpec((1, S5_XK + S5_HK, S5_XK), lambda k: (k, 0, 0)),
        out_shape=jax.ShapeDtypeStruct((S5_SLABS, S5_XK + S5_HK, S5_XK), BF16),
        compiler_params=_params(("arbitrary",), 2 * (S5_XK + S5_HK) * S5_XK * 2 + 8 * 1024 * 1024),
        name="s5_output_operator",
    )(par, bt, cm, ct)
    return wst, a16, wbig


def _s5_states_kernel(*refs, n_ctx, n_lat):
    xc_refs = refs[:S5_CHUNK]
    xl_refs = refs[S5_CHUNK:2 * S5_CHUNK]
    wst_ref, a16_ref, h_ref, s_s, h_s = refs[2 * S5_CHUNK:]
    x = jnp.concatenate([jnp.concatenate([r[0] for r in xc_refs], axis=1),
                         jnp.concatenate([r[0] for r in xl_refs], axis=1)], axis=0)
    s_s[...] = _dot(x, wst_ref[0])
    p = S5_SLAB_STATE
    af_re = a16_ref[0, 0:1, 0:p]
    af_im = a16_ref[0, 0:1, p:2 * p]
    ar_re = a16_ref[0, 0:1, 2 * p:3 * p]
    ar_im = a16_ref[0, 0:1, 3 * p:4 * p]
    total = n_ctx + n_lat

    def step(i, carry):
        f_re, f_im, r_re, r_im = carry
        h_s[pl.ds(i, 1), 0:p] = f_re
        h_s[pl.ds(i, 1), p:2 * p] = f_im
        s_re = s_s[pl.ds(i, 1), 0:p]
        s_im = s_s[pl.ds(i, 1), p:2 * p]
        f_re, f_im = af_re * f_re - af_im * f_im + s_re, af_re * f_im + af_im * f_re + s_im
        j = jnp.where(i < n_ctx, n_ctx - 1 - i, total + n_ctx - 1 - i)
        h_s[pl.ds(j, 1), 2 * p:3 * p] = r_re
        h_s[pl.ds(j, 1), 3 * p:4 * p] = r_im
        s_re = s_s[pl.ds(j, 1), 2 * p:3 * p]
        s_im = s_s[pl.ds(j, 1), 3 * p:4 * p]
        r_re, r_im = ar_re * r_re - ar_im * r_im + s_re, ar_re * r_im + ar_im * r_re + s_im
        return f_re, f_im, r_re, r_im

    zero = jnp.zeros((1, p), F32)
    lax.fori_loop(0, total, step, (zero, zero, zero, zero), unroll=8)
    h_ref[0, 0] = h_s[pl.ds(n_ctx, n_lat), :].astype(BF16)


def _s5_states(uc_flat, ul_flat, wst, a16):
    bsz, n_ctx, _ = uc_flat.shape
    n_lat = ul_flat.shape[1]
    piece = lambda rows, s: pl.BlockSpec((1, rows, LANES), lambda k, b, s=s: (b, 0, s * S5_SLABS + k))
    in_specs = ([piece(n_ctx, s) for s in range(S5_CHUNK)] + [piece(n_lat, s) for s in range(S5_CHUNK)]
                + [pl.BlockSpec((1, S5_XK, S5_HK), lambda k, b: (k, 0, 0)),
                   pl.BlockSpec((1, SUBLANES, S5_HK), lambda k, b: (k, 0, 0))])
    total = n_ctx + n_lat
    nbytes = 2 * S5_XK * S5_HK * 2 + 2 * total * S5_XK * 2 + 3 * total * S5_HK * 4 + 2 * n_lat * S5_HK * 2
    return pl.pallas_call(
        functools.partial(_s5_states_kernel, n_ctx=n_ctx, n_lat=n_lat),
        grid=(S5_SLABS, bsz),
        in_specs=in_specs,
        out_specs=pl.BlockSpec((1, 1, n_lat, S5_HK), lambda k, b: (b, k, 0, 0)),
        out_shape=jax.ShapeDtypeStruct((bsz, S5_SLABS, n_lat, S5_HK), BF16),
        scratch_shapes=[pltpu.VMEM((total, S5_HK), F32), pltpu.VMEM((total, S5_HK), F32)],
        compiler_params=_params(("arbitrary", "arbitrary"), nbytes),
        name="s5_states",
    )(*([uc_flat] * S5_CHUNK + [ul_flat] * S5_CHUNK + [wst, a16]))


def _s5_readout_kernel(*refs):
    x_refs = refs[:S5_CHUNK]
    h_ref, w_ref, y_ref = refs[S5_CHUNK:]
    lhs = jnp.concatenate([r[0] for r in x_refs] + [h_ref[0, 0]], axis=1)
    y = _dot(lhs, w_ref[0])
    chunks = y.shape[0]
    for t in range(S5_CHUNK):
        y_ref[0, pl.ds(t, chunks, stride=S5_CHUNK), :] = y[:, t * LANES:(t + 1) * LANES]


def _s5_readout(ul_flat, h_in, wbig):
    bsz, n_lat, _ = ul_flat.shape
    piece = lambda s: pl.BlockSpec((1, n_lat, LANES), lambda k, b, s=s: (b, 0, s * S5_SLABS + k))
    nbytes = 2 * (S5_XK + S5_HK) * S5_XK * 2 + 2 * n_lat * (S5_XK + S5_HK) * 2 * 2 + n_lat * S5_XK * 4 * 3
    return pl.pallas_call(
        _s5_readout_kernel,
        grid=(S5_SLABS, bsz),
        in_specs=[piece(s) for s in range(S5_CHUNK)]
                 + [pl.BlockSpec((1, 1, n_lat, S5_HK), lambda k, b: (b, k, 0, 0)),
                    pl.BlockSpec((1, S5_XK + S5_HK, S5_XK), lambda k, b: (k, 0, 0))],
        out_specs=pl.BlockSpec((1, n_lat * S5_CHUNK, LANES), lambda k, b: (b, 0, k)),
        out_shape=jax.ShapeDtypeStruct((bsz, n_lat * S5_CHUNK, S5_WIDTH), F32),
        compiler_params=_params(("arbitrary", "arbitrary"), nbytes),
        name="s5_readout",
    )(*([ul_flat] * S5_CHUNK + [h_in, wbig]))


SC_HALO = 16


def _outproj1_kernel(y_ref, u_ref, gb_ref, p_ref, pprev_ref, pnext_ref, x_ref, gate_ref,
                     dskip_ref, wglu_ref, bglu_ref, cw_ref, cb_ref, w_ref, lg_ref, lb_ref, o_ref, *, ctx_row):
    i = pl.program_id(1)
    tm = x_ref.shape[1]
    yc = y_ref[0] + dskip_ref[...] * u_ref[0].astype(F32)
    z = _gelu_tanh(yc)
    y_c = z * _sigmoid(_dot(z.astype(BF16), wglu_ref[...]) + bglu_ref[...])
    p = p_ref[0].astype(F32)
    row = lax.broadcasted_iota(jnp.int32, (tm, 1), 0)
    prev_row = jnp.where(i > 0, pprev_ref[0, SC_HALO - 1:SC_HALO, :].astype(F32), 0.0)
    next_row = jnp.where(i < pl.num_programs(1) - 1, pnext_ref[0, 0:1, :].astype(F32), 0.0)
    p_dn = jnp.where(row == 0, prev_row, pltpu.roll(p, 1, 0))
    p_up = jnp.where(row == tm - 1, next_row, pltpu.roll(p, tm - 1, 0))
    conv = cb_ref[...] + cw_ref[0:1, :] * p_dn + cw_ref[1:2, :] * p + cw_ref[2:3, :] * p_up
    y_d = gb_ref[0].astype(F32) * conv
    y = jnp.concatenate([y_c.astype(BF16), y_d.astype(BF16)], axis=1)
    out = _dot(y, w_ref[...])
    o_ref[0] = _layer_norm(ALPHA * x_ref[0] + _mod_row(gate_ref, ctx_row) * out, lg_ref[...], lb_ref[...])


def _outproj1(y, u, gb, p, x, mod, ctx_row, d_skip, w_glu, b_glu, conv_w, conv_b, w_out, ln_g, ln_b, tm):
    bsz, n, d = x.shape
    tok = lambda b, i: (b, i, 0)
    const = lambda b, i: (0, 0)
    per = tm // SC_HALO
    last = n // SC_HALO - 1
    nbytes = (2 * 2 * tm * d * 4 + w_out.size * 2 + 2 * 2 * tm * (2 * S5_WIDTH + 2 * SC_WIDTH) * 2
              + 6 * tm * SC_WIDTH * 4 + 2 * tm * d * 4)
    return pl.pallas_call(
        functools.partial(_outproj1_kernel, ctx_row=ctx_row),
        grid=(bsz, n // tm),
        in_specs=[pl.BlockSpec((1, tm, S5_WIDTH), tok), pl.BlockSpec((1, tm, S5_WIDTH), tok),
                  pl.BlockSpec((1, tm, SC_WIDTH), tok), pl.BlockSpec((1, tm, SC_WIDTH), tok),
                  pl.BlockSpec((1, SC_HALO, SC_WIDTH), lambda b, i: (b, jnp.maximum(i * per - 1, 0), 0)),
                  pl.BlockSpec((1, SC_HALO, SC_WIDTH), lambda b, i: (b, jnp.minimum((i + 1) * per, last), 0)),
                  pl.BlockSpec((1, tm, d), tok), _mod_spec(mod, MOD_G1),
                  pl.BlockSpec((1, S5_WIDTH), const), _resident(w_glu.shape, const),
                  pl.BlockSpec((1, S5_WIDTH), const),
                  pl.BlockSpec((SC_CONV, SC_WIDTH), const), pl.BlockSpec((1, SC_WIDTH), const),
                  _resident(w_out.shape, const),
                  pl.BlockSpec((1, d), const), pl.BlockSpec((1, d), const)],
        out_specs=pl.BlockSpec((1, tm, d), tok),
        out_shape=jax.ShapeDtypeStruct(x.shape, F32),
        compiler_params=_params(("arbitrary", "arbitrary"), nbytes),
        name="outproj1_ln",
    )(y, u, gb, p, p, p, x, mod, d_skip, w_glu, b_glu, conv_w, conv_b, w_out, ln_g, ln_b)


def _rope_tables(n):
    t = np.arange(n)
    inv = ROPE_BASE ** (-np.arange(ROPE_FREQS, dtype=np.float64) / ROPE_FREQS)
    ang_r = (t // GRID_W)[:, None] * inv[None, :]
    ang_c = (t % GRID_W)[:, None] * inv[None, :]
    cos = np.concatenate([np.cos(ang_r), np.cos(ang_r), np.cos(ang_c), np.cos(ang_c)], axis=1)
    sin = np.concatenate([-np.sin(ang_r), np.sin(ang_r), -np.sin(ang_c), np.sin(ang_c)], axis=1)
    return jnp.asarray(cos, F32), jnp.asarray(sin, F32)


TM_LATENT = 512
TM_PROJ = 1024
TM_CONTEXT = 256
FFN_TH = 2048


def kernel(x, c, ctx, c_ctx, mod_w, mod_b, ln1_g, ln1_b, ln2_g, ln2_b, ffn_w1, ffn_b1, ffn_w2, ffn_b2,
           ab_w_in, ab_w_out, lru_conv_w, lru_conv_b, lru_w_a, lru_b_a, lru_w_x, lru_b_x, lru_lam, att_sink,
           cd_w_in, cd_w_out, s5_log_dt, s5_a_re, s5_a_im, s5_b_re, s5_b_im, s5_c_re, s5_c_im, s5_d,
           s5_w_glu, s5_b_glu, sc_conv_w, sc_conv_b):
    bsz, n, d = x.shape
    lc = ctx.shape[1]
    assert n % TM_PROJ == 0 and n % TM_LATENT == 0 and lc % TM_CONTEXT == 0 and (bsz * lc) % TM_LATENT == 0
    assert n % GRID_W == 0

    pad_rows = SUBLANES - (bsz + 1) % SUBLANES if (bsz + 1) % SUBLANES else 0
    cc = jnp.concatenate([c, c_ctx[None, :], jnp.zeros((pad_rows, d), F32)], axis=0)
    mods = {0: _modulation(cc, mod_w, mod_b, 0)}
    ctx_row = bsz

    row2 = lambda v: v.reshape(1, -1)
    rope_tabs = _rope_tables(n)

    side = {}

    def weight(stack, name, idx):
        return side.pop((name, idx)) if (name, idx) in side else _to_bf16(stack, idx)

    def hosted(call, own, jobs):
        outs = call(casts=[(stack, idx) for _, idx, stack in jobs])
        side.update({(name, idx): w for (name, idx, _), w in zip(jobs, outs[own:own + len(jobs)])})
        return outs[:own] + outs[own + len(jobs):]

    for i in range(DEPTH):
        last = i == DEPTH - 1
        j = i // 2
        mod = mods.pop(i) if i in mods else _modulation(cc, mod_w, mod_b, i)
        todo = [("w1", i, ffn_w1), ("w2", i, ffn_w2)]
        if not last:
            nj = (i + 1) // 2
            todo += [("w1", i + 1, ffn_w1), ("w2", i + 1, ffn_w2)]
            todo += ([("ab_in", nj, ab_w_in), ("ab_out", nj, ab_w_out)] if (i + 1) % 2 == 0
                     else [("cd_in", nj, cd_w_in), ("cd_out", nj, cd_w_out)])
        todo = [job for job in todo if (job[0], job[1]) not in side]
        if i % 2 == 0:
            w_in = weight(ab_w_in, "ab_in", j)
            w_out = weight(ab_w_out, "ab_out", j)
            u, gate, q, k, v = _inproj0(x, mod, None, w_in, rope_tabs, TM_PROJ)
            uc, gatec, qc, kc, vc = _inproj0(ctx, mod, ctx_row, w_in, None, TM_CONTEXT)
            w_gate = (0.5 * jnp.concatenate([lru_w_a[j, 0], lru_w_x[j, 0], lru_w_a[j, 1], lru_w_x[j, 1]],
                                            axis=-1)).astype(BF16)
            hb = lambda b: b.reshape(LRU_HEADS, 1, LRU_BLOCK)
            b_gate = 0.5 * jnp.concatenate([hb(lru_b_a[j, 0]), hb(lru_b_x[j, 0]), hb(lru_b_a[j, 1]),
                                            hb(lru_b_x[j, 1])], axis=-1)
            lam = jnp.concatenate([hb(lru_lam[j, 0]), hb(lru_lam[j, 1])], axis=-1)
            ahead_mod = [] if last else [(cc, mod_w, mod_b, i + 1)]
            *lru_out, = hosted(functools.partial(_rglru, u, gate, uc, gatec, lru_conv_w[j], row2(lru_conv_b[j]),
                                                 w_gate, b_gate, lam, mods=ahead_mod), 2, todo[0::2])
            ya, yac = lru_out[:2]
            mods.update({i + 1: m for m in lru_out[2:]})
            yb, = hosted(functools.partial(_window_attention, q, k, v, kc, vc, att_sink[j]), 1, todo[1::2])
            todo = []
            x = _outproj0(ya, yb, x, mod, None, w_out, row2(ln1_g[i]), row2(ln1_b[i]), TM_PROJ)
            if not last:
                ybc = _context_attention(qc, kc, vc, att_sink[j])
                ctx = _outproj0(yac, ybc, ctx, mod, ctx_row, w_out, row2(ln1_g[i]), row2(ln1_b[i]), TM_CONTEXT)
        else:
            assert last
            w_in = weight(cd_w_in, "cd_in", j)
            w_out = weight(cd_w_out, "cd_out", j)
            u, u_flat, gb, p = _inproj1(x, mod, None, w_in, TM_PROJ, True)
            (uc_flat,) = _inproj1(ctx, mod, ctx_row, w_in, TM_CONTEXT, False)
            wst, a16, wbig = _s5_operators(s5_log_dt[j], s5_a_re[j], s5_a_im[j], s5_b_re[j], s5_b_im[j],
                                           s5_c_re[j], s5_c_im[j])
            h_in = _s5_states(uc_flat, u_flat, wst, a16)
            y = _s5_readout(u_flat, h_in, wbig)
            x = _outproj1(y, u, gb, p, x, mod, None, row2(s5_d[j]), s5_w_glu[j].astype(BF16), row2(s5_b_glu[j]),
                          sc_conv_w[j], row2(sc_conv_b[j]), w_out, row2(ln1_g[i]), row2(ln1_b[i]), TM_LATENT)
        w1 = weight(ffn_w1, "w1", i)
        w2 = weight(ffn_w2, "w2", i)
        ahead = [job for job in todo if job[1] != i or job[0] not in ("w1", "w2")]
        x, = hosted(functools.partial(_ffn, x, mod, None, w1, row2(ffn_b1[i]), w2, row2(ffn_b2[i]),
                                      row2(ln2_g[i]), row2(ln2_b[i]), TM_LATENT, FFN_TH), 1, ahead)
        if not last:
            rows = ctx.reshape(1, bsz * lc, d)
            rows, = _ffn(rows, mod, ctx_row, w1, row2(ffn_b1[i]), w2, row2(ffn_b2[i]), row2(ln2_g[i]),
                         row2(ln2_b[i]), TM_LATENT, FFN_TH)
            ctx = rows.reshape(bsz, lc, d)
    return x
```
